```python
import math
import jax, jax.numpy as jnp
from jax import lax
import numpy as np

D_MODEL = 1024
BATCH = 16
SEQ = 2048
DEPTH = 1

D_ATTN = D_MODEL // 2
HEAD_DIM_A = 64
N_HEADS_A = D_ATTN // HEAD_DIM_A
DILATED_BRANCHES = ((128, 1), (512, 4), (2048, 16))
BAND_BLOCK = 128

D_MLSTM = D_MODEL // 2
N_HEADS_M = 4
HEAD_DIM_M = D_MLSTM // N_HEADS_M
CONV_WIDTH = 4
MLSTM_CHUNK = 128

D_MIX = D_ATTN + D_MLSTM
PROJ_WIDTHS = (D_ATTN, D_ATTN, D_ATTN, D_MLSTM, D_MLSTM, D_MLSTM, N_HEADS_M, N_HEADS_M)

N_EXPERTS = 32
TOP_K = 4
D_EXPERT = D_MODEL
SWIGLU_LIMIT = 7.0
SWIGLU_ALPHA = 1.702
MOE_BLOCK = 128

DEEPNORM_ALPHA = (2 * DEPTH) ** 0.25
DEEPNORM_BETA = (8 * DEPTH) ** -0.25
LN_EPS = 1e-5
RMS_EPS = 1e-6

kernel_name = "hymba_dilated_mlstm_moe_deepnorm"


def _layer_norm(x, g, b):
    xf = x.astype(jnp.float32)
    mu = jnp.mean(xf, axis=-1, keepdims=True)
    var = jnp.mean(jnp.square(xf - mu), axis=-1, keepdims=True)
    return ((xf - mu) * lax.rsqrt(var + LN_EPS) * g + b).astype(x.dtype)


def _split_heads(t, n_heads):
    b, s, _ = t.shape
    return t.reshape(b, s, n_heads, -1).transpose(0, 2, 1, 3)


def _dilated_branch(q, k, v, window, dilation):
    b, h, s, e = q.shape
    n = s // dilation
    span = window // dilation
    L = BAND_BLOCK
    n_pad = -(-n // L) * L
    nb = n_pad // L

    def to_sub(t):
        t = t.reshape(b, h, n, dilation, e).transpose(0, 1, 3, 2, 4)
        t = jnp.pad(t, ((0, 0), (0, 0), (0, 0), (0, n_pad - n), (0, 0)))
        return t.reshape(b, h, dilation, nb, L, e)

    def with_prev(t):
        prev = jnp.pad(t[:, :, :, :-1], ((0, 0), (0, 0), (0, 0), (1, 0), (0, 0), (0, 0)))
        return jnp.concatenate([prev, t], axis=4)

    qs = to_sub(q)
    kb = with_prev(to_sub(k))
    vb = with_prev(to_sub(v))
    scores = jnp.einsum('bhrnqe,bhrnke->bhrnqk', qs, kb).astype(jnp.float32) * (e ** -0.5)
    qi = jnp.arange(L)[:, None]
    kj = jnp.arange(2 * L)[None, :]
    dist = qi - kj + L
    key_pos = jnp.arange(nb)[:, None, None] * L + kj[None] - L
    mask = (dist >= 0) & (dist <= span) & (key_pos >= 0)
    scores = jnp.where(mask, scores, -jnp.inf)
    m = jnp.max(scores, axis=-1, keepdims=True)
    p = jnp.exp(scores - m)
    l = jnp.sum(p, axis=-1, keepdims=True)
    o = jnp.einsum('bhrnqk,bhrnke->bhrnqe', (p / l).astype(v.dtype), vb)
    lse = (m + jnp.log(l))[..., 0]
    o = o.reshape(b, h, dilation, n_pad, e)[:, :, :, :n]
    o = o.transpose(0, 1, 3, 2, 4).reshape(b, h, s, e)
    lse = lse.reshape(b, h, dilation, n_pad)[:, :, :, :n]
    lse = lse.transpose(0, 1, 3, 2).reshape(b, h, s)
    return o, lse


def _dilated_attention(q, k, v):
    outs, lses = [], []
    for window, dilation in DILATED_BRANCHES:
        o, lse = _dilated_branch(q, k, v, window, dilation)
        outs.append(o)
        lses.append(lse)
    wts = jax.nn.softmax(jnp.stack(lses, axis=0), axis=0)
    out = jnp.einsum('cbhs,cbhse->bhse', wts, jnp.stack(outs, axis=0).astype(jnp.float32))
    return out.astype(q.dtype)


def _causal_depthwise_conv(x, w, b):
    width, c = w.shape
    y = lax.conv_general_dilated(x, w[:, None, :].astype(x.dtype), window_strides=(1,),
                                 padding=((width - 1, 0),),
                                 dimension_numbers=('NWC', 'WIO', 'NWC'),
                                 feature_group_count=c)
    return y + b


def _mlstm_chunkwise(q, k, v, i_pre, f_pre):
    b, h, s, e = q.shape
    L = MLSTM_CHUNK
    nc = s // L
    q = q.astype(jnp.float32) * (e ** -0.5)
    k = k.astype(jnp.float32)
    v = v.astype(jnp.float32)
    i_pre = i_pre.astype(jnp.float32)
    log_f = jax.nn.log_sigmoid(f_pre.astype(jnp.float32))

    def chunks(t):
        return jnp.moveaxis(t.reshape(b, h, nc, L, *t.shape[3:]), 2, 0)

    causal = jnp.tril(jnp.ones((L, L), dtype=bool))

    def step(carry, inp):
        C, n, m = carry
        qc, kc, vc, ic, fc = inp
        g = jnp.cumsum(fc, axis=-1)
        D = g[..., :, None] - g[..., None, :] + ic[..., None, :]
        D = jnp.where(causal, D, -jnp.inf)
        inter = g + m[..., None]
        m_t = jnp.maximum(inter, jnp.max(D, axis=-1))
        w_intra = jnp.exp(D - m_t[..., None])
        w_inter = jnp.exp(inter - m_t)
        qk = jnp.einsum('bhqe,bhke->bhqk', qc, kc) * w_intra
        num = jnp.einsum('bhqk,bhkf->bhqf', qk, vc) + w_inter[..., None] * jnp.einsum('bhqe,bhef->bhqf', qc, C)
        den = jnp.sum(qk, axis=-1) + w_inter * jnp.einsum('bhqe,bhe->bhq', qc, n)
        hc = num / jnp.maximum(jnp.abs(den), jnp.exp(-m_t))[..., None]
        g_last = g[..., -1]
        a = g_last[..., None] - g + ic
        m_new = jnp.maximum(g_last + m, jnp.max(a, axis=-1))
        decay = jnp.exp(g_last + m - m_new)
        wk = jnp.exp(a - m_new[..., None])
        C_new = decay[..., None, None] * C + jnp.einsum('bhl,bhle,bhlf->bhef', wk, kc, vc)
        n_new = decay[..., None] * n + jnp.einsum('bhl,bhle->bhe', wk, kc)
        return (C_new, n_new, m_new), hc

    init = (jnp.zeros((b, h, e, e), jnp.float32), jnp.zeros((b, h, e), jnp.float32),
            jnp.zeros((b, h), jnp.float32))
    _, hs = lax.scan(step, init, (chunks(q), chunks(k), chunks(v), chunks(i_pre), chunks(log_f)))
    return jnp.moveaxis(hs, 0, 2).reshape(b, h, s, e)


def _mixer(x, w_in, conv_w, conv_b, w_mq, w_mk, b_igate, b_fgate, mnorm_g, w_out):
    b, s, _ = x.shape
    proj = x @ w_in
    split_at = np.cumsum(PROJ_WIDTHS)[:-1].tolist()
    qa, ka, va, xm, vm, om, ip, fp = jnp.split(proj, split_at, axis=-1)
    attn = _dilated_attention(_split_heads(qa, N_HEADS_A), _split_heads(ka, N_HEADS_A),
                              _split_heads(va, N_HEADS_A))
    attn = attn.transpose(0, 2, 1, 3).reshape(b, s, D_ATTN)
    xc = jax.nn.silu(_causal_depthwise_conv(xm, conv_w, conv_b))
    xc = _split_heads(xc, N_HEADS_M)
    qm = jnp.einsum('bhse,hef->bhsf', xc, w_mq)
    km = jnp.einsum('bhse,hef->bhsf', xc, w_mk)
    i_pre = (ip + b_igate).transpose(0, 2, 1)
    f_pre = (fp + b_fgate).transpose(0, 2, 1)
    hm = _mlstm_chunkwise(qm, km, _split_heads(vm, N_HEADS_M), i_pre, f_pre)
    hm = hm * lax.rsqrt(jnp.mean(jnp.square(hm), axis=-1, keepdims=True) + RMS_EPS)
    hm = hm * mnorm_g.reshape(N_HEADS_M, 1, HEAD_DIM_M)
    hm = hm.transpose(0, 2, 1, 3).reshape(b, s, D_MLSTM)
    hm = (jax.nn.sigmoid(om.astype(jnp.float32)) * hm).astype(x.dtype)
    return jnp.concatenate([attn, hm], axis=-1) @ w_out


def _moe(x2d, w_router, b_router, w_gate, b_gate, w_up, b_up, w_down, b_down):
    t, d = x2d.shape
    logits = (x2d @ w_router + b_router).astype(jnp.float32)
    top_val, top_idx = lax.top_k(logits, TOP_K)
    gate = jax.nn.softmax(top_val, axis=-1)
    n_assign = t * TOP_K
    flat_e = top_idx.reshape(-1)
    flat_tok = jnp.arange(n_assign, dtype=jnp.int32) // TOP_K
    flat_gate = gate.reshape(-1)
    order = jnp.argsort(flat_e)
    sorted_e = flat_e[order]
    counts = jnp.bincount(flat_e, length=N_EXPERTS)
    starts = jnp.cumsum(counts) - counts
    padded = (counts + MOE_BLOCK - 1) // MOE_BLOCK * MOE_BLOCK
    pends = jnp.cumsum(padded)
    pstarts = pends - padded
    dest = pstarts[sorted_e] + (jnp.arange(n_assign) - starts[sorted_e])
    n_rows = (-(-n_assign // MOE_BLOCK) + N_EXPERTS) * MOE_BLOCK
    n_blocks = n_rows // MOE_BLOCK
    row_tok = jnp.zeros((n_rows,), jnp.int32).at[dest].set(flat_tok[order])
    row_gate = jnp.zeros((n_rows,), jnp.float32).at[dest].set(flat_gate[order])
    block_e = jnp.minimum(jnp.searchsorted(pends, jnp.arange(n_blocks) * MOE_BLOCK, side='right'),
                          N_EXPERTS - 1)

    def expert_block(args):
        tok, e = args
        xb = x2d[tok]
        g = jnp.minimum(xb @ w_gate[e] + b_gate[e], SWIGLU_LIMIT)
        u = jnp.clip(xb @ w_up[e] + b_up[e], -SWIGLU_LIMIT, SWIGLU_LIMIT)
        hdn = g * jax.nn.sigmoid(SWIGLU_ALPHA * g) * (u + 1)
        return hdn @ w_down[e] + b_down[e]

    y = lax.map(expert_block, (row_tok.reshape(n_blocks, MOE_BLOCK), block_e))
    y = y.reshape(n_rows, d) * row_gate[:, None].astype(y.dtype)
    return jnp.zeros_like(x2d).at[row_tok].add(y)


def setup_inputs(seed: int = 0) -> dict:
    key = jax.random.key(seed)
    ks = jax.random.split(key, 23)

    def nrm(k, shape, scale):
        return jax.random.normal(k, shape, jnp.float32) * scale

    col_scale = jnp.concatenate([
        jnp.ones((2 * D_ATTN,), jnp.float32),
        jnp.full((D_ATTN,), DEEPNORM_BETA, jnp.float32),
        jnp.ones((D_MLSTM,), jnp.float32),
        jnp.full((D_MLSTM,), DEEPNORM_BETA, jnp.float32),
        jnp.ones((D_MLSTM + 2 * N_HEADS_M,), jnp.float32)])
    p_total = sum(PROJ_WIDTHS)
    return {
        "x": nrm(ks[0], (BATCH, SEQ, D_MODEL), 1.0),
        "w_in": nrm(ks[1], (DEPTH, D_MODEL, p_total), D_MODEL ** -0.5) * col_scale,
        "conv_w": nrm(ks[2], (DEPTH, CONV_WIDTH, D_MLSTM), CONV_WIDTH ** -0.5),
        "conv_b": nrm(ks[3], (DEPTH, D_MLSTM), 0.01),
        "w_mq": nrm(ks[4], (DEPTH, N_HEADS_M, HEAD_DIM_M, HEAD_DIM_M), HEAD_DIM_M ** -0.5),
        "w_mk": nrm(ks[5], (DEPTH, N_HEADS_M, HEAD_DIM_M, HEAD_DIM_M), HEAD_DIM_M ** -0.5),
        "b_igate": nrm(ks[6], (DEPTH, N_HEADS_M), 0.1),
        "b_fgate": jnp.linspace(3.0, 6.0, N_HEADS_M, dtype=jnp.float32)[None] + nrm(ks[7], (DEPTH, N_HEADS_M), 0.1),
        "mnorm_g": 1.0 + nrm(ks[8], (DEPTH, D_MLSTM), 0.02),
        "w_out": nrm(ks[9], (DEPTH, D_MIX, D_MODEL), D_MIX ** -0.5 * DEEPNORM_BETA),
        "ln1_g": 1.0 + nrm(ks[10], (DEPTH, D_MODEL), 0.02),
        "ln1_b": nrm(ks[11], (DEPTH, D_MODEL), 0.02),
        "w_router": nrm(ks[12], (DEPTH, D_MODEL, N_EXPERTS), D_MODEL ** -0.5),
        "b_router": nrm(ks[13], (DEPTH, N_EXPERTS), 0.01),
        "w_gate": nrm(ks[14], (DEPTH, N_EXPERTS, D_MODEL, D_EXPERT), D_MODEL ** -0.5),
        "b_gate": nrm(ks[15], (DEPTH, N_EXPERTS, D_EXPERT), 0.01),
        "w_up": nrm(ks[16], (DEPTH, N_EXPERTS, D_MODEL, D_EXPERT), D_MODEL ** -0.5 * DEEPNORM_BETA),
        "b_up": nrm(ks[17], (DEPTH, N_EXPERTS, D_EXPERT), 0.01),
        "w_down": nrm(ks[18], (DEPTH, N_EXPERTS, D_EXPERT, D_MODEL), D_EXPERT ** -0.5 * DEEPNORM_BETA),
        "b_down": nrm(ks[19], (DEPTH, N_EXPERTS, D_MODEL), 0.01),
        "ln2_g": 1.0 + nrm(ks[20], (DEPTH, D_MODEL), 0.02),
        "ln2_b": nrm(ks[21], (DEPTH, D_MODEL), 0.02),
    }


def reference(x, w_in, conv_w, conv_b, w_mq, w_mk, b_igate, b_fgate, mnorm_g, w_out,
              ln1_g, ln1_b, w_router, b_router, w_gate, b_gate, w_up, b_up, w_down, b_down,
              ln2_g, ln2_b):
    h = x
    for l in range(DEPTH):
        y = _mixer(h, w_in[l], conv_w[l], conv_b[l], w_mq[l], w_mk[l], b_igate[l], b_fgate[l],
                   mnorm_g[l], w_out[l])
        h = _layer_norm(DEEPNORM_ALPHA * h + y, ln1_g[l], ln1_b[l])
        y = _moe(h.reshape(-1, D_MODEL), w_router[l], b_router[l], w_gate[l], b_gate[l],
                 w_up[l], b_up[l], w_down[l], b_down[l]).reshape(h.shape)
        h = _layer_norm(DEEPNORM_ALPHA * h + y, ln2_g[l], ln2_b[l])
    return h
```

```python
import functools
import math

import jax
import jax.numpy as jnp
from jax import lax
from jax.experimental import pallas as pl
from jax.experimental.pallas import tpu as pltpu

F32 = jnp.float32
BF16 = jnp.bfloat16
NEG_INF = float("-inf")

V7X_LANES = 128
V7X_SUBLANES = 8
V7X_VMEM_LIMIT_BYTES = 56 * 1024 * 1024

D_MODEL = 1024
D_ATTN = 512
HEAD_DIM_A = 64
D_MLSTM = 512
N_HEADS_M = 4
HEAD_DIM_M = 128
CONV_WIDTH = 4
CHUNK = 128
DILATED_BRANCHES = ((128, 1), (512, 4), (2048, 16))
N_EXPERTS = 32
TOP_K = 4
SWIGLU_LIMIT = 7.0
SWIGLU_ALPHA = 1.702
DEEPNORM_ALPHA = 2.0 ** 0.25
LN_EPS = 1e-5
RMS_EPS = 1e-6

PROJ_MAIN = 3 * D_ATTN + 3 * D_MLSTM
TOK_TILE = 256
RUN_ALIGN = V7X_SUBLANES
ROWS_LOCAL = TOK_TILE * TOP_K + N_EXPERTS * RUN_ALIGN
ROW_BLOCK = 256


def _dot(a, b):
    return jnp.dot(a, b, preferred_element_type=F32)


def _dot_nt(a, b):
    return lax.dot_general(a, b, (((1,), (1,)), ((), ())), preferred_element_type=F32)


def _split3(x):
    hi = x.astype(BF16)
    r1 = x - hi.astype(F32)
    mid = r1.astype(BF16)
    lo = (r1 - mid.astype(F32)).astype(BF16)
    return hi, mid, lo


def _compiler_params(sem):
    return pltpu.CompilerParams(dimension_semantics=sem, vmem_limit_bytes=V7X_VMEM_LIMIT_BYTES)


def _inproj_kernel(x_ref, w_ref, wg_ref, proj_ref, gcol_ref, grow_ref):
    xb = x_ref[...].astype(BF16)
    step = 512
    for n in range(0, PROJ_MAIN, step):
        proj_ref[:, n:n + step] = _dot(xb, w_ref[:, n:n + step])
    g = _dot(xb, wg_ref[...])
    gcol_ref[...] = g
    grow_ref[0] = g.T[:V7X_SUBLANES, :]


def _inproj(x2d, w_main, w_gates, batch, seq):
    t = x2d.shape[0]
    tm = 512
    per_b = seq // tm
    return pl.pallas_call(
        _inproj_kernel,
        grid=(t // tm,),
        in_specs=[
            pl.BlockSpec((tm, D_MODEL), lambda i: (i, 0)),
            pl.BlockSpec((D_MODEL, PROJ_MAIN), lambda i: (0, 0)),
            pl.BlockSpec((D_MODEL, V7X_LANES), lambda i: (0, 0)),
        ],
        out_specs=[
            pl.BlockSpec((tm, PROJ_MAIN), lambda i: (i, 0)),
            pl.BlockSpec((tm, V7X_LANES), lambda i: (i, 0)),
            pl.BlockSpec((1, V7X_SUBLANES, tm), lambda i: (i // per_b, 0, i % per_b)),
        ],
        out_shape=[
            jax.ShapeDtypeStruct((t, PROJ_MAIN), F32),
            jax.ShapeDtypeStruct((t, V7X_LANES), F32),
            jax.ShapeDtypeStruct((batch, V7X_SUBLANES, seq), F32),
        ],
        compiler_params=_compiler_params(("arbitrary",)),
        name="inproj",
    )(x2d, w_main, w_gates)


def _attn_kernel(q_ref, k_ref, v_ref, o_ref, o0, o1, o2, l0, l1, l2, *, seq):
    obufs = (o0, o1, o2)
    lbufs = (l0, l1, l2)
    lane = lax.broadcasted_iota(jnp.int32, (CHUNK, V7X_LANES), 1)
    head0 = lane < HEAD_DIM_A
    qi = lax.broadcasted_iota(jnp.int32, (CHUNK, 2 * CHUNK), 0)
    kj = lax.broadcasted_iota(jnp.int32, (CHUNK, 2 * CHUNK), 1)
    band = (kj >= qi) & (kj <= qi + CHUNK)
    scale = HEAD_DIM_A ** -0.5

    for c, (window, dil) in enumerate(DILATED_BRANCHES):
        assert window // dil == CHUNK
        n_steps = seq // CHUNK

        def rows(start, dil=dil):
            if dil == 1:
                return pl.ds(start, CHUNK)
            return pl.ds(start, CHUNK, stride=dil)

        def step(idx, carry, c=c, dil=dil, rows=rows):
            r = idx % dil
            blk = idx // dil
            start = r + dil * CHUNK * blk
            pstart = jnp.maximum(start - dil * CHUNK, 0)
            q = q_ref[0, rows(start), :] * scale
            kk = jnp.concatenate([k_ref[0, rows(pstart), :], k_ref[0, rows(start), :]], axis=0).astype(BF16)
            vv = jnp.concatenate([v_ref[0, rows(pstart), :], v_ref[0, rows(start), :]], axis=0).astype(BF16)
            valid = band & (kj >= jnp.where(blk > 0, 0, CHUNK))
            outs, lses = [], []
            for hh in range(2):
                hmask = head0 if hh == 0 else jnp.logical_not(head0)
                qh = jnp.where(hmask, q, 0.0).astype(BF16)
                s = _dot_nt(qh, kk)
                s = jnp.where(valid, s, NEG_INF)
                m = jnp.max(s, axis=1, keepdims=True)
                p = jnp.exp(s - m)
                l = jnp.sum(p, axis=1, keepdims=True)
                o = _dot(p.astype(BF16), vv) / l
                outs.append(o)
                lses.append(jnp.broadcast_to(m + jnp.log(l), (CHUNK, V7X_LANES)))
            obufs[c][rows(start), :] = jnp.where(head0, outs[0], outs[1])
            lbufs[c][rows(start), :] = jnp.where(head0, lses[0], lses[1])
            return carry

        lax.fori_loop(0, n_steps, step, 0)

    def combine(i, carry):
        sl = pl.ds(pl.multiple_of(i * 256, 256), 256)
        la, lb, lc = l0[sl, :], l1[sl, :], l2[sl, :]
        mx = jnp.maximum(jnp.maximum(la, lb), lc)
        wa, wb, wc = jnp.exp(la - mx), jnp.exp(lb - mx), jnp.exp(lc - mx)
        out = (wa * o0[sl, :] + wb * o1[sl, :] + wc * o2[sl, :]) / (wa + wb + wc)
        o_ref[0, sl, :] = out.astype(o_ref.dtype)
        return carry

    lax.fori_loop(0, seq // 256, combine, 0)


def _attention(proj3d):
    batch, seq, _ = proj3d.shape
    nblk = D_ATTN // V7X_LANES
    blk = (1, seq, V7X_LANES)
    scratch = [pltpu.VMEM((seq, V7X_LANES), F32) for _ in range(6)]
    return pl.pallas_call(
        functools.partial(_attn_kernel, seq=seq),
        grid=(batch, nblk),
        in_specs=[
            pl.BlockSpec(blk, lambda b, g: (b, 0, g)),
            pl.BlockSpec(blk, lambda b, g: (b, 0, nblk + g)),
            pl.BlockSpec(blk, lambda b, g: (b, 0, 2 * nblk + g)),
        ],
        out_specs=pl.BlockSpec(blk, lambda b, g: (b, 0, g)),
        out_shape=jax.ShapeDtypeStruct((batch, seq, D_ATTN), BF16),
        scratch_shapes=scratch,
        compiler_params=_compiler_params(("arbitrary", "arbitrary")),
        name="dilated_attention",
    )(proj3d, proj3d, proj3d)


def _log_sigmoid(x):
    return jnp.minimum(x, 0.0) - jnp.log1p(jnp.exp(-jnp.abs(x)))


def _mlstm_kernel(xm_ref, vm_ref, om_ref, gcol_ref, grow_ref, cw_ref, cb_ref, wq_ref, wk_ref,
                  bcol_ref, brow_ref, mg_ref, o_ref,
                  xpad, q_s, k_s, icol_s, lfcol_s, rows_s, *, seq):
    h = pl.program_id(1)
    nchunk = seq // CHUNK
    lane = lax.broadcasted_iota(jnp.int32, (CHUNK, V7X_LANES), 1)
    ri = lax.broadcasted_iota(jnp.int32, (CHUNK, CHUNK), 0)
    ci = lax.broadcasted_iota(jnp.int32, (CHUNK, CHUNK), 1)
    tril = ri >= ci
    tril_b = jnp.where(tril, 1.0, 0.0).astype(BF16)
    triu_b = jnp.where(ri <= ci, 1.0, 0.0).astype(BF16)

    pad = V7X_SUBLANES
    xpad[0:pad, :] = jnp.zeros((pad, V7X_LANES), F32)
    xpad[pad:pad + seq, :] = xm_ref[0]
    wq = wq_ref[0]
    wk = wk_ref[0]
    qscale = HEAD_DIM_M ** -0.5

    def conv_step(i, carry):
        base = pl.multiple_of(i * 256, 256)
        y = jnp.broadcast_to(cb_ref[...], (256, V7X_LANES))
        xw = xpad[pl.ds(base, 256 + pad), :]
        for j in range(CONV_WIDTH):
            off = pad - (CONV_WIDTH - 1) + j
            y = y + cw_ref[j:j + 1, :] * xw[off:off + 256, :]
        xc = (y * jax.nn.sigmoid(y)).astype(BF16)
        q_s[pl.ds(base, 256), :] = _dot(xc, wq) * qscale
        k_s[pl.ds(base, 256), :] = _dot(xc, wk)
        gc = gcol_ref[0, pl.ds(base, 256), :] + bcol_ref[...]
        lane256 = lax.broadcasted_iota(jnp.int32, (256, V7X_LANES), 1)
        ic = jnp.sum(jnp.where(lane256 == h, gc, 0.0), axis=1, keepdims=True)
        fc = jnp.sum(jnp.where(lane256 == h + N_HEADS_M, gc, 0.0), axis=1, keepdims=True)
        icol_s[pl.ds(base, 256), :] = jnp.broadcast_to(ic, (256, V7X_LANES))
        lfcol_s[pl.ds(base, 256), :] = jnp.broadcast_to(_log_sigmoid(fc), (256, V7X_LANES))
        return carry

    lax.fori_loop(0, seq // 256, conv_step, 0)

    gr = grow_ref[0] + brow_ref[...]
    sub = lax.broadcasted_iota(jnp.int32, (V7X_SUBLANES, seq), 0)
    irow = jnp.sum(jnp.where(sub == h, gr, 0.0), axis=0, keepdims=True)
    frow = jnp.sum(jnp.where(sub == h + N_HEADS_M, gr, 0.0), axis=0, keepdims=True)
    rows_s[...] = jnp.zeros((V7X_SUBLANES, seq), F32)
    rows_s[0:1, :] = irow
    rows_s[1:2, :] = _log_sigmoid(frow)

    mg = mg_ref[...]

    def chunk_step(c, carry):
        cmat, nrow, m = carry
        base = pl.multiple_of(c * CHUNK, CHUNK)
        sl = pl.ds(base, CHUNK)
        qc = q_s[sl, :]
        kc = k_s[sl, :]
        vc = vm_ref[0, sl, :].astype(BF16)
        icol = icol_s[sl, :]
        lf_col = lfcol_s[sl, :]
        gcol = sum(_dot(tril_b, part) for part in _split3(lf_col))
        rows = rows_s[:, sl]
        grows = sum(_dot(part, triu_b) for part in _split3(rows))
        irow_c = rows[0:1, :]
        grow_c = grows[1:2, :]
        g_last = grow_c[:, CHUNK - 1:CHUNK]

        d = gcol - grow_c + irow_c
        d = jnp.where(tril, d, NEG_INF)
        inter = gcol + m
        m_t = jnp.maximum(inter, jnp.max(d, axis=1, keepdims=True))
        w_intra = jnp.exp(d - m_t)
        w_inter = jnp.exp(inter - m_t)
        qb = qc.astype(BF16)
        kb = kc.astype(BF16)
        qk = _dot_nt(qb, kb) * w_intra
        num = _dot(qk.astype(BF16), vc) + w_inter * _dot(qb, cmat.astype(BF16))
        den = jnp.sum(qk, axis=1, keepdims=True) + w_inter * jnp.sum(qc * nrow, axis=1, keepdims=True)
        hc = num / jnp.maximum(jnp.abs(den), jnp.exp(-m_t))

        a_col = g_last - gcol + icol
        a_row = g_last - grow_c + irow_c
        m_new = jnp.maximum(g_last + m, jnp.max(a_row, axis=1, keepdims=True))
        decay = jnp.exp(g_last + m - m_new)
        wk_col = jnp.exp(a_col - m_new)
        kw = wk_col * kc
        c_new = decay * cmat + _dot(kw.T.astype(BF16), vc)
        n_new = decay * nrow + jnp.sum(kw, axis=0, keepdims=True)

        hn = hc * lax.rsqrt(jnp.mean(hc * hc, axis=1, keepdims=True) + RMS_EPS) * mg
        hn = jax.nn.sigmoid(om_ref[0, sl, :]) * hn
        o_ref[0, sl, :] = hn.astype(o_ref.dtype)
        return c_new, n_new, m_new

    init = (jnp.zeros((HEAD_DIM_M, HEAD_DIM_M), F32), jnp.zeros((1, HEAD_DIM_M), F32), jnp.zeros((1, 1), F32))
    lax.fori_loop(0, nchunk, chunk_step, init)


def _mlstm(proj3d, gcol3d, grow3d, conv_w, conv_b, wq, wk, bias_col, bias_row, mnorm):
    batch, seq, _ = proj3d.shape
    blk = (1, seq, V7X_LANES)
    a0 = 3 * D_ATTN // V7X_LANES
    nh = N_HEADS_M
    return pl.pallas_call(
        functools.partial(_mlstm_kernel, seq=seq),
        grid=(batch, nh),
        in_specs=[
            pl.BlockSpec(blk, lambda b, h: (b, 0, a0 + h)),
            pl.BlockSpec(blk, lambda b, h: (b, 0, a0 + nh + h)),
            pl.BlockSpec(blk, lambda b, h: (b, 0, a0 + 2 * nh + h)),
            pl.BlockSpec(blk, lambda b, h: (b, 0, 0)),
            pl.BlockSpec((1, V7X_SUBLANES, seq), lambda b, h: (b, 0, 0)),
            pl.BlockSpec((CONV_WIDTH, V7X_LANES), lambda b, h: (0, h)),
            pl.BlockSpec((1, V7X_LANES), lambda b, h: (0, h)),
            pl.BlockSpec((1, HEAD_DIM_M, HEAD_DIM_M), lambda b, h: (h, 0, 0)),
            pl.BlockSpec((1, HEAD_DIM_M, HEAD_DIM_M), lambda b, h: (h, 0, 0)),
            pl.BlockSpec((1, V7X_LANES), lambda b, h: (0, 0)),
            pl.BlockSpec((V7X_SUBLANES, seq), lambda b, h: (0, 0)),
            pl.BlockSpec((1, V7X_LANES), lambda b, h: (0, h)),
        ],
        out_specs=pl.BlockSpec(blk, lambda b, h: (b, 0, h)),
        out_shape=jax.ShapeDtypeStruct((batch, seq, D_MLSTM), BF16),
        scratch_shapes=[
            pltpu.VMEM((seq + V7X_SUBLANES, V7X_LANES), F32),
            pltpu.VMEM((seq, V7X_LANES), F32),
            pltpu.VMEM((seq, V7X_LANES), F32),
            pltpu.VMEM((seq, V7X_LANES), F32),
            pltpu.VMEM((seq, V7X_LANES), F32),
            pltpu.VMEM((V7X_SUBLANES, seq), F32),
        ],
        compiler_params=_compiler_params(("arbitrary", "arbitrary")),
        name="mlstm",
    )(proj3d, proj3d, proj3d, gcol3d, grow3d, conv_w, conv_b, wq, wk, bias_col, bias_row, mnorm)


def _layer_norm(z, g, b):
    mu = jnp.mean(z, axis=1, keepdims=True)
    zc = z - mu
    var = jnp.mean(zc * zc, axis=1, keepdims=True)
    return zc * lax.rsqrt(var + LN_EPS) * g + b


def _router_kernel(attn_ref, hm_ref, x_ref, woa_ref, wom_ref, g_ref, b_ref, wr_ref, br_ref,
                   h_ref, post_ref, col_ref, tab_ref, tot_ref, carry):
    i = pl.program_id(0)
    tm = TOK_TILE

    @pl.when(i == 0)
    def _():
        carry[...] = jnp.zeros_like(carry)

    y = _dot(attn_ref[...], woa_ref[...]) + _dot(hm_ref[...], wom_ref[...])
    hval = _layer_norm(DEEPNORM_ALPHA * x_ref[...] + y, g_ref[...], b_ref[...])
    h_ref[...] = hval

    lane = lax.broadcasted_iota(jnp.int32, (tm, V7X_LANES), 1)
    lane_f = lane.astype(F32)
    logits = _dot(hval.astype(BF16), wr_ref[...]) + br_ref[...]
    logits = jnp.where(lane < N_EXPERTS, logits, NEG_INF)

    sel, vals = [], []
    for _ in range(TOP_K):
        mx = jnp.max(logits, axis=1, keepdims=True)
        idx = jnp.min(jnp.where(logits == mx, lane_f, float(V7X_LANES)), axis=1, keepdims=True)
        hit = lane_f == idx
        logits = jnp.where(hit, NEG_INF, logits)
        sel.append(hit)
        vals.append(mx)
    exps = [jnp.exp(v - vals[0]) for v in vals]
    den = exps[0] + exps[1] + exps[2] + exps[3]
    gates = [e / den for e in exps]

    onehot = jnp.zeros((tm, V7X_LANES), F32)
    for hit in sel:
        onehot = jnp.where(hit, 1.0, onehot)
    ri = lax.broadcasted_iota(jnp.int32, (tm, tm), 0)
    ci = lax.broadcasted_iota(jnp.int32, (tm, tm), 1)
    strict_lower = jnp.where(ri > ci, 1.0, 0.0).astype(BF16)
    rank = _dot(strict_lower, onehot.astype(BF16))
    cnt = jnp.sum(onehot, axis=0, keepdims=True)
    cnt_al = jnp.floor((cnt + (RUN_ALIGN - 1)) * (1.0 / RUN_ALIGN))
    er = lax.broadcasted_iota(jnp.int32, (V7X_LANES, V7X_LANES), 0)
    ec = lax.broadcasted_iota(jnp.int32, (V7X_LANES, V7X_LANES), 1)
    before = jnp.where(er < ec, 1.0, 0.0).astype(BF16)
    cnt8 = jnp.broadcast_to(cnt_al, (V7X_SUBLANES, V7X_LANES)).astype(BF16)
    slot = _dot(cnt8, before)[0:1, :] * float(RUN_ALIGN)
    cnt_al = cnt_al * float(RUN_ALIGN)

    col = jnp.zeros((tm, V7X_LANES), F32)
    for k in range(TOP_K):
        lpos = jnp.sum(jnp.where(sel[k], slot + rank, 0.0), axis=1, keepdims=True)
        col = jnp.where(lane == k, lpos, col)
        col = jnp.where(lane == TOP_K + k, gates[k], col)
    col_ref[...] = col
    post_ref[0] = col.T[:V7X_SUBLANES, :]

    base = carry[...]
    tab = jnp.zeros((V7X_SUBLANES, V7X_LANES), F32)
    sub = lax.broadcasted_iota(jnp.int32, (V7X_SUBLANES, V7X_LANES), 0)
    tab = jnp.where(sub == 0, cnt_al, tab)
    tab = jnp.where(sub == 1, base, tab)
    tab = jnp.where(sub == 2, slot, tab)
    tab_ref[0] = tab.astype(jnp.int32)
    carry[...] = base + cnt_al
    tot_ref[...] = jnp.broadcast_to(base + cnt_al, (V7X_SUBLANES, V7X_LANES)).astype(jnp.int32)


def _outproj_router(attn2d, hm2d, x2d, wo_a, wo_m, ln_g, ln_b, w_r, b_r):
    t = x2d.shape[0]
    tm = TOK_TILE
    nt = t // tm
    const = lambda i: (0, 0)
    return pl.pallas_call(
        _router_kernel,
        grid=(nt,),
        in_specs=[
            pl.BlockSpec((tm, D_ATTN), lambda i: (i, 0)),
            pl.BlockSpec((tm, D_MLSTM), lambda i: (i, 0)),
            pl.BlockSpec((tm, D_MODEL), lambda i: (i, 0)),
            pl.BlockSpec((D_ATTN, D_MODEL), const),
            pl.BlockSpec((D_MLSTM, D_MODEL), const),
            pl.BlockSpec((1, D_MODEL), const),
            pl.BlockSpec((1, D_MODEL), const),
            pl.BlockSpec((D_MODEL, V7X_LANES), const),
            pl.BlockSpec((1, V7X_LANES), const),
        ],
        out_specs=[
            pl.BlockSpec((tm, D_MODEL), lambda i: (i, 0)),
            pl.BlockSpec((1, V7X_SUBLANES, tm), lambda i: (i, 0, 0)),
            pl.BlockSpec((tm, V7X_LANES), lambda i: (i, 0)),
            pl.BlockSpec((1, V7X_SUBLANES, V7X_LANES), lambda i: (i, 0, 0)),
            pl.BlockSpec((V7X_SUBLANES, V7X_LANES), const),
        ],
        out_shape=[
            jax.ShapeDtypeStruct((t, D_MODEL), F32),
            jax.ShapeDtypeStruct((nt, V7X_SUBLANES, tm), F32),
            jax.ShapeDtypeStruct((t, V7X_LANES), F32),
            jax.ShapeDtypeStruct((nt, V7X_SUBLANES, V7X_LANES), jnp.int32),
            jax.ShapeDtypeStruct((V7X_SUBLANES, V7X_LANES), jnp.int32),
        ],
        scratch_shapes=[pltpu.VMEM((1, V7X_LANES), F32)],
        compiler_params=_compiler_params(("arbitrary",)),
        name="outproj_router",
    )(attn2d, hm2d, x2d, wo_a, wo_m, ln_g, ln_b, w_r, b_r)


def _run_copy(src, dst, start_src, start_dst, n, sem):
    n = pl.multiple_of(n, RUN_ALIGN)
    return pltpu.make_async_copy(
        src.at[pl.ds(pl.multiple_of(start_src, RUN_ALIGN), n)],
        dst.at[pl.ds(pl.multiple_of(start_dst, RUN_ALIGN), n)],
        sem)


def _dispatch_kernel(cnt_sm, slot_sm, dst_sm, tail_sm, post_ref, h_ref, xs_hbm, ybuf, zbuf, sem, zsem):
    i = pl.program_id(0)
    nt = pl.num_programs(0)
    cur = i % 2
    hb = h_ref[...].astype(BF16)
    pos = post_ref[0]
    chunk = 256
    for rc in range(ROWS_LOCAL // chunk):
        rows = (lax.broadcasted_iota(jnp.int32, (chunk, TOK_TILE), 0) + rc * chunk).astype(F32)
        p = jnp.zeros((chunk, TOK_TILE), F32)
        for k in range(TOP_K):
            p = jnp.where(rows == pos[k:k + 1, :], 1.0, p)
        ybuf[cur, rc * chunk:(rc + 1) * chunk, :] = _dot(p.astype(BF16), hb)

    def total_rows(tile):
        return lax.fori_loop(0, N_EXPERTS, lambda e, acc: acc + cnt_sm[tile * N_EXPERTS + e], 0)

    @pl.when(i > 0)
    def _():
        n_prev = total_rows(i - 1)
        _run_copy(ybuf.at[1 - cur], xs_hbm, 0, 0, n_prev, sem).wait()

    def issue(e, carry):
        n = cnt_sm[i * N_EXPERTS + e]

        @pl.when(n > 0)
        def _():
            _run_copy(ybuf.at[cur], xs_hbm, slot_sm[i * N_EXPERTS + e], dst_sm[i * N_EXPERTS + e], n, sem).start()
        return carry

    lax.fori_loop(0, N_EXPERTS, issue, 0)

    @pl.when(i == nt - 1)
    def _():
        n_cur = total_rows(i)
        _run_copy(ybuf.at[cur], xs_hbm, 0, 0, n_cur, sem).wait()
        zbuf[...] = jnp.zeros_like(zbuf)

        def ztail(wait):
            def body(e, carry):
                n = tail_sm[N_EXPERTS + e]

                @pl.when(n > 0)
                def _():
                    cp = _run_copy(zbuf, xs_hbm, 0, tail_sm[e], n, zsem)
                    if wait:
                        cp.wait()
                    else:
                        cp.start()
                return carry
            return body

        lax.fori_loop(0, N_EXPERTS, ztail(False), 0)
        lax.fori_loop(0, N_EXPERTS, ztail(True), 0)

        def zblock(wait):
            def body(j, carry):
                cp = _run_copy(zbuf, xs_hbm, 0, j * ROW_BLOCK, ROW_BLOCK, zsem)
                if wait:
                    cp.wait()
                else:
                    cp.start()
                return carry
            return body

        n_blocks = xs_hbm.shape[0] // ROW_BLOCK
        lax.fori_loop(tail_sm[2 * N_EXPERTS], n_blocks, zblock(False), 0)
        lax.fori_loop(tail_sm[2 * N_EXPERTS], n_blocks, zblock(True), 0)


def _dispatch(cnt, slot, dst, tail, post, h2d, n_rows):
    t = h2d.shape[0]
    nt = t // TOK_TILE
    return pl.pallas_call(
        _dispatch_kernel,
        grid_spec=pltpu.PrefetchScalarGridSpec(
            num_scalar_prefetch=4,
            grid=(nt,),
            in_specs=[
                pl.BlockSpec((1, V7X_SUBLANES, TOK_TILE), lambda i, *_: (i, 0, 0)),
                pl.BlockSpec((TOK_TILE, D_MODEL), lambda i, *_: (i, 0)),
            ],
            out_specs=pl.BlockSpec(memory_space=pl.ANY),
            scratch_shapes=[
                pltpu.VMEM((2, ROWS_LOCAL, D_MODEL), F32),
                pltpu.VMEM((ROW_BLOCK, D_MODEL), F32),
                pltpu.SemaphoreType.DMA(()),
                pltpu.SemaphoreType.DMA(()),
            ],
        ),
        out_shape=jax.ShapeDtypeStruct((n_rows, D_MODEL), F32),
        compiler_params=_compiler_params(("arbitrary",)),
        name="moe_dispatch",
    )(cnt, slot, dst, tail, post, h2d)


def _expert_kernel(bexp_sm, nused_sm, xs_ref, wg_ref, bg_ref, wu_ref, bu_ref, wd_ref, bd_ref, ys_ref, hbuf):
    j = pl.program_id(0)

    @pl.when(j < nused_sm[0])
    def _():
        xb = xs_ref[...].astype(BF16)
        step = 512
        for n in range(0, D_MODEL, step):
            g = _dot(xb, wg_ref[0, :, n:n + step]) + bg_ref[0, :, n:n + step]
            u = _dot(xb, wu_ref[0, :, n:n + step]) + bu_ref[0, :, n:n + step]
            g = jnp.minimum(g, SWIGLU_LIMIT)
            u = jnp.clip(u, -SWIGLU_LIMIT, SWIGLU_LIMIT)
            hbuf[:, n:n + step] = (g * jax.nn.sigmoid(SWIGLU_ALPHA * g) * (u + 1.0)).astype(BF16)
        ys_ref[...] = _dot(hbuf[...], wd_ref[0]) + bd_ref[0]

    @pl.when(j >= nused_sm[0])
    def _():
        ys_ref[...] = jnp.zeros_like(ys_ref)


def _experts(bexp, nused, xs, wg, bg, wu, bu, wd, bd):
    n_rows = xs.shape[0]
    nb = n_rows // ROW_BLOCK

    def row_map(j, be, nu):
        return (jnp.minimum(j, nu[0] - 1), 0)

    def w_map(j, be, nu):
        return (be[jnp.minimum(j, nu[0] - 1)], 0, 0)

    wspec = pl.BlockSpec((1, D_MODEL, D_MODEL), w_map)
    bspec = pl.BlockSpec((1, 1, D_MODEL), w_map)
    return pl.pallas_call(
        _expert_kernel,
        grid_spec=pltpu.PrefetchScalarGridSpec(
            num_scalar_prefetch=2,
            grid=(nb,),
            in_specs=[pl.BlockSpec((ROW_BLOCK, D_MODEL), row_map), wspec, bspec, wspec, bspec, wspec, bspec],
            out_specs=pl.BlockSpec((ROW_BLOCK, D_MODEL), lambda j, be, nu: (j, 0)),
            scratch_shapes=[pltpu.VMEM((ROW_BLOCK, D_MODEL), BF16)],
        ),
        out_shape=jax.ShapeDtypeStruct((n_rows, D_MODEL), F32),
        compiler_params=_compiler_params(("arbitrary",)),
        name="moe_experts",
    )(bexp, nused, xs, wg, bg, wu, bu, wd, bd)


def _combine_kernel(cnt_sm, slot_sm, dst_sm, col_ref, h_ref, g_ref, b_ref, ys_hbm, o_ref, ybuf, sem):
    i = pl.program_id(0)
    nt = pl.num_programs(0)
    cur = i % 2

    def fetch(tile, buf, wait):
        def body(e, carry):
            n = cnt_sm[tile * N_EXPERTS + e]

            @pl.when(n > 0)
            def _():
                cp = _run_copy(ys_hbm, ybuf.at[buf], dst_sm[tile * N_EXPERTS + e], slot_sm[tile * N_EXPERTS + e],
                               n, sem.at[buf])
                if wait:
                    cp.wait()
                else:
                    cp.start()
            return carry
        lax.fori_loop(0, N_EXPERTS, body, 0)

    @pl.when(i == 0)
    def _():
        ybuf[...] = jnp.zeros_like(ybuf)
        fetch(0, 0, False)

    @pl.when(i + 1 < nt)
    def _():
        fetch(i + 1, 1 - cur, False)

    fetch(i, cur, True)

    col = col_ref[...]
    moe = jnp.zeros((TOK_TILE, D_MODEL), F32)
    chunk = 256
    for rc in range(ROWS_LOCAL // chunk):
        rows = (lax.broadcasted_iota(jnp.int32, (TOK_TILE, chunk), 1) + rc * chunk).astype(F32)
        s = jnp.zeros((TOK_TILE, chunk), F32)
        for k in range(TOP_K):
            s = jnp.where(rows == col[:, k:k + 1], col[:, TOP_K + k:TOP_K + k + 1], s)
        moe = moe + _dot(s.astype(BF16), ybuf[cur, rc * chunk:(rc + 1) * chunk, :].astype(BF16))
    o_ref[...] = _layer_norm(DEEPNORM_ALPHA * h_ref[...] + moe, g_ref[...], b_ref[...])


def _combine(cnt, slot, dst, col, h2d, ln_g, ln_b, ys):
    t = h2d.shape[0]
    nt = t // TOK_TILE
    return pl.pallas_call(
        _combine_kernel,
        grid_spec=pltpu.PrefetchScalarGridSpec(
            num_scalar_prefetch=3,
            grid=(nt,),
            in_specs=[
                pl.BlockSpec((TOK_TILE, V7X_LANES), lambda i, *_: (i, 0)),
                pl.BlockSpec((TOK_TILE, D_MODEL), lambda i, *_: (i, 0)),
                pl.BlockSpec((1, D_MODEL), lambda i, *_: (0, 0)),
                pl.BlockSpec((1, D_MODEL), lambda i, *_: (0, 0)),
                pl.BlockSpec(memory_space=pl.ANY),
            ],
            out_specs=pl.BlockSpec((TOK_TILE, D_MODEL), lambda i, *_: (i, 0)),
            scratch_shapes=[
                pltpu.VMEM((2, ROWS_LOCAL, D_MODEL), F32),
                pltpu.SemaphoreType.DMA((2,)),
            ],
        ),
        out_shape=jax.ShapeDtypeStruct((t, D_MODEL), F32),
        compiler_params=_compiler_params(("arbitrary",)),
        name="moe_combine",
    )(cnt, slot, dst, col, h2d, ln_g, ln_b, ys)


def _pad_lanes(a, width=V7X_LANES):
    return jnp.pad(a, ((0, 0), (0, width - a.shape[1])))


def _layer(h3d, w_in, conv_w, conv_b, w_mq, w_mk, b_igate, b_fgate, mnorm_g, w_out,
           ln1_g, ln1_b, w_router, b_router, w_gate, b_gate, w_up, b_up, w_down, b_down, ln2_g, ln2_b):
    batch, seq, _ = h3d.shape
    t = batch * seq
    x2d = h3d.reshape(t, D_MODEL)

    w_main = w_in[:, :PROJ_MAIN].astype(BF16)
    w_gates = _pad_lanes(w_in[:, PROJ_MAIN:]).astype(BF16)
    proj, gcol, grow = _inproj(x2d, w_main, w_gates, batch, seq)
    proj3d = proj.reshape(batch, seq, PROJ_MAIN)
    attn = _attention(proj3d)
    gate_bias = jnp.concatenate([b_igate, b_fgate]).astype(F32)
    bias_col = _pad_lanes(gate_bias[None, :])
    bias_row = jnp.broadcast_to(gate_bias[:, None], (V7X_SUBLANES, seq))
    hm = _mlstm(proj3d, gcol.reshape(batch, seq, V7X_LANES), grow, conv_w, conv_b[None, :],
                w_mq.astype(BF16), w_mk.astype(BF16), bias_col, bias_row, mnorm_g[None, :])

    wo = w_out.astype(BF16)
    h2d, post, col, tab, tot = _outproj_router(
        attn.reshape(t, D_ATTN), hm.reshape(t, D_MLSTM), x2d, wo[:D_ATTN], wo[D_ATTN:],
        ln1_g[None, :], ln1_b[None, :], _pad_lanes(w_router).astype(BF16), _pad_lanes(b_router[None, :]))

    nt = t // TOK_TILE
    total = tot[0, :N_EXPERTS]
    region = (total + ROW_BLOCK - 1) // ROW_BLOCK * ROW_BLOCK
    region_end = jnp.cumsum(region)
    region_start = region_end - region
    n_rows = (t * TOP_K + nt * N_EXPERTS * (RUN_ALIGN - 1)) // ROW_BLOCK * ROW_BLOCK + (N_EXPERTS + 1) * ROW_BLOCK
    nb = n_rows // ROW_BLOCK
    cnt = tab[:, 0, :N_EXPERTS].reshape(-1)
    slot = tab[:, 2, :N_EXPERTS].reshape(-1)
    dst = (tab[:, 1, :N_EXPERTS] + region_start[None, :]).reshape(-1)
    tail = jnp.concatenate([region_start + total, region - total, region_end[-1:] // ROW_BLOCK]).astype(jnp.int32)
    block_row = jnp.arange(nb, dtype=jnp.int32) * ROW_BLOCK
    bexp = jnp.minimum(jnp.sum(region_end[None, :] <= block_row[:, None], axis=1), N_EXPERTS - 1).astype(jnp.int32)
    nused = (region_end[-1:] // ROW_BLOCK).astype(jnp.int32)

    xs = _dispatch(cnt, slot, dst, tail, post, h2d, n_rows)
    ys = _experts(bexp, nused, xs, w_gate.astype(BF16), b_gate[:, None, :], w_up.astype(BF16), b_up[:, None, :],
                  w_down.astype(BF16), b_down[:, None, :])
    out = _combine(cnt, slot, dst, col, h2d, ln2_g[None, :], ln2_b[None, :], ys)
    return out.reshape(batch, seq, D_MODEL)


def kernel(x, w_in, conv_w, conv_b, w_mq, w_mk, b_igate, b_fgate, mnorm_g, w_out, ln1_g, ln1_b, w_router, b_router, w_gate, b_gate, w_up, b_up, w_down, b_down, ln2_g, ln2_b):
    h = x
    for l in range(w_in.shape[0]):
        h = _layer(h, w_in[l], conv_w[l], conv_b[l], w_mq[l], w_mk[l], b_igate[l], b_fgate[l], mnorm_g[l],
                   w_out[l], ln1_g[l], ln1_b[l], w_router[l], b_router[l], w_gate[l], b_gate[l], w_up[l],
                   b_up[l], w_down[l], b_down[l], ln2_g[l], ln2_b[l])
    return h
```

```python
import functools
import math

import jax
import jax.numpy as jnp
from jax import lax
from jax.experimental import pallas as pl
from jax.experimental.pallas import tpu as pltpu

F32 = jnp.float32
BF16 = jnp.bfloat16
NEG_INF = float("-inf")

V7X_LANES = 128
V7X_SUBLANES = 8
V7X_VMEM_LIMIT_BYTES = 56 * 1024 * 1024

D_MODEL = 1024
D_ATTN = 512
HEAD_DIM_A = 64
D_MLSTM = 512
N_HEADS_M = 4
HEAD_DIM_M = 128
CONV_WIDTH = 4
CHUNK = 128
DILATED_BRANCHES = ((128, 1), (512, 4), (2048, 16))
N_EXPERTS = 32
TOP_K = 4
SWIGLU_LIMIT = 7.0
SWIGLU_ALPHA = 1.702
DEEPNORM_ALPHA = 2.0 ** 0.25
LN_EPS = 1e-5
RMS_EPS = 1e-6

PROJ_MAIN = 3 * D_ATTN + 3 * D_MLSTM
TOK_TILE = 256
RUN_ALIGN = V7X_SUBLANES
ROWS_LOCAL = TOK_TILE * TOP_K + N_EXPERTS * RUN_ALIGN
ROW_BLOCK = 256


def _dot(a, b):
    return jnp.dot(a, b, preferred_element_type=F32)


def _dot_nt(a, b):
    return lax.dot_general(a, b, (((1,), (1,)), ((), ())), preferred_element_type=F32)


def _split3(x):
    hi = x.astype(BF16)
    r1 = x - hi.astype(F32)
    mid = r1.astype(BF16)
    lo = (r1 - mid.astype(F32)).astype(BF16)
    return hi, mid, lo


def _compiler_params(sem):
    return pltpu.CompilerParams(dimension_semantics=sem, vmem_limit_bytes=V7X_VMEM_LIMIT_BYTES)


def _inproj_kernel(x_ref, w_ref, wg_ref, proj_ref, gcol_ref, grow_ref):
    xb = x_ref[...].astype(BF16)
    step = 512
    for n in range(0, PROJ_MAIN, step):
        proj_ref[:, n:n + step] = _dot(xb, w_ref[:, n:n + step])
    g = _dot(xb, wg_ref[...])
    gcol_ref[...] = g
    grow_ref[0] = g.T[:V7X_SUBLANES, :]


def _inproj(x2d, w_main, w_gates, batch, seq):
    t = x2d.shape[0]
    tm = 512
    per_b = seq // tm
    return pl.pallas_call(
        _inproj_kernel,
        grid=(t // tm,),
        in_specs=[
            pl.BlockSpec((tm, D_MODEL), lambda i: (i, 0)),
            pl.BlockSpec((D_MODEL, PROJ_MAIN), lambda i: (0, 0)),
            pl.BlockSpec((D_MODEL, V7X_LANES), lambda i: (0, 0)),
        ],
        out_specs=[
            pl.BlockSpec((tm, PROJ_MAIN), lambda i: (i, 0)),
            pl.BlockSpec((tm, V7X_LANES), lambda i: (i, 0)),
            pl.BlockSpec((1, V7X_SUBLANES, tm), lambda i: (i // per_b, 0, i % per_b)),
        ],
        out_shape=[
            jax.ShapeDtypeStruct((t, PROJ_MAIN), F32),
            jax.ShapeDtypeStruct((t, V7X_LANES), F32),
            jax.ShapeDtypeStruct((batch, V7X_SUBLANES, seq), F32),
        ],
        compiler_params=_compiler_params(("arbitrary",)),
        name="inproj",
    )(x2d, w_main, w_gates)


def _attn_kernel(q_ref, k_ref, v_ref, o_ref, q0s, q1s, bias_b, bias_f, o0, o1, o2, l0, l1, l2, *, seq):
    obufs = (o0, o1, o2)
    lbufs = (l0, l1, l2)
    two = 2 * CHUNK
    head0 = lax.broadcasted_iota(jnp.int32, (CHUNK, V7X_LANES), 1) < HEAD_DIM_A
    qscale = HEAD_DIM_A ** -0.5 * math.log2(math.e)

    def prep(i, carry):
        sl = pl.ds(pl.multiple_of(i * two, two), two)
        h0 = lax.broadcasted_iota(jnp.int32, (two, V7X_LANES), 1) < HEAD_DIM_A
        q = q_ref[0, sl, :] * qscale
        q0s[sl, :] = jnp.where(h0, q, 0.0)
        q1s[sl, :] = jnp.where(h0, 0.0, q)
        return carry

    lax.fori_loop(0, seq // two, prep, 0)

    qi = lax.broadcasted_iota(jnp.int32, (two, two), 0) % CHUNK
    kj = lax.broadcasted_iota(jnp.int32, (two, two), 1)
    bias_b[...] = jnp.where((kj >= qi) & (kj <= qi + CHUNK), 0.0, NEG_INF)
    qf = lax.broadcasted_iota(jnp.int32, (two, CHUNK), 0) % CHUNK
    kf = lax.broadcasted_iota(jnp.int32, (two, CHUNK), 1)
    bias_f[...] = jnp.where(kf <= qf, 0.0, NEG_INF)

    def block(c, dil, start, has_prev):
        def rows(s0):
            return pl.ds(s0, CHUNK) if dil == 1 else pl.ds(s0, CHUNK, stride=dil)

        q2 = jnp.concatenate([q0s[rows(start), :], q1s[rows(start), :]], axis=0).astype(BF16)
        if has_prev:
            prev = start - dil * CHUNK
            kk = jnp.concatenate([k_ref[0, rows(prev), :], k_ref[0, rows(start), :]], axis=0).astype(BF16)
            vv = jnp.concatenate([v_ref[0, rows(prev), :], v_ref[0, rows(start), :]], axis=0).astype(BF16)
            bias = bias_b[...]
        else:
            kk = k_ref[0, rows(start), :].astype(BF16)
            vv = v_ref[0, rows(start), :].astype(BF16)
            bias = bias_f[...]
        s = _dot_nt(q2, kk) + bias
        m = jnp.max(s, axis=1, keepdims=True)
        p = jnp.exp2(s - m)
        l = jnp.sum(p, axis=1, keepdims=True)
        o = _dot(p.astype(BF16), vv) / l
        lse = m + jnp.log2(l)
        obufs[c][rows(start), :] = jnp.where(head0, o[:CHUNK], o[CHUNK:])
        lbufs[c][rows(start), :] = jnp.where(head0, jnp.broadcast_to(lse[:CHUNK], (CHUNK, V7X_LANES)),
                                             jnp.broadcast_to(lse[CHUNK:], (CHUNK, V7X_LANES)))

    for c, (window, dil) in enumerate(DILATED_BRANCHES):
        assert window // dil == CHUNK
        nb = seq // (dil * CHUNK)

        def residue(r, carry, c=c, dil=dil, nb=nb):
            block(c, dil, r, False)
            if nb > 1:
                def later(blk, carry2):
                    block(c, dil, r + dil * CHUNK * blk, True)
                    return carry2
                lax.fori_loop(1, nb, later, 0, unroll=min(nb - 1, 5))
            return carry

        if dil == 1:
            residue(0, 0)
        else:
            lax.fori_loop(0, dil, residue, 0, unroll=(1 if nb > 1 else 4))

    def combine(i, carry):
        sl = pl.ds(pl.multiple_of(i * 256, 256), 256)
        la, lb, lc = l0[sl, :], l1[sl, :], l2[sl, :]
        mx = jnp.maximum(jnp.maximum(la, lb), lc)
        wa, wb, wc = jnp.exp2(la - mx), jnp.exp2(lb - mx), jnp.exp2(lc - mx)
        out = (wa * o0[sl, :] + wb * o1[sl, :] + wc * o2[sl, :]) / (wa + wb + wc)
        o_ref[0, sl, :] = out.astype(o_ref.dtype)
        return carry

    lax.fori_loop(0, seq // 256, combine, 0)


def _attention(proj3d):
    batch, seq, _ = proj3d.shape
    nblk = D_ATTN // V7X_LANES
    blk = (1, seq, V7X_LANES)
    scratch = ([pltpu.VMEM((seq, V7X_LANES), F32), pltpu.VMEM((seq, V7X_LANES), F32),
                pltpu.VMEM((2 * CHUNK, 2 * CHUNK), F32), pltpu.VMEM((2 * CHUNK, CHUNK), F32)]
               + [pltpu.VMEM((seq, V7X_LANES), F32) for _ in range(6)])
    return pl.pallas_call(
        functools.partial(_attn_kernel, seq=seq),
        grid=(batch, nblk),
        in_specs=[
            pl.BlockSpec(blk, lambda b, g: (b, 0, g)),
            pl.BlockSpec(blk, lambda b, g: (b, 0, nblk + g)),
            pl.BlockSpec(blk, lambda b, g: (b, 0, 2 * nblk + g)),
        ],
        out_specs=pl.BlockSpec(blk, lambda b, g: (b, 0, g)),
        out_shape=jax.ShapeDtypeStruct((batch, seq, D_ATTN), BF16),
        scratch_shapes=scratch,
        compiler_params=_compiler_params(("arbitrary", "arbitrary")),
        name="dilated_attention",
    )(proj3d, proj3d, proj3d)


def _log_sigmoid(x):
    return jnp.minimum(x, 0.0) - jnp.log1p(jnp.exp(-jnp.abs(x)))


def _mlstm_kernel(xm_ref, vm_ref, om_ref, gcol_ref, grow_ref, cw_ref, cb_ref, wq_ref, wk_ref,
                  bcol_ref, brow_ref, mg_ref, o_ref,
                  xpad, q_s, k_s, icol_s, lfcol_s, gcol_s, ri_s, rf_s, gf_s, mi_s, mo_s, gl_s, *, seq):
    h = pl.program_id(1)
    nchunk = seq // CHUNK
    lane = lax.broadcasted_iota(jnp.int32, (CHUNK, V7X_LANES), 1)
    ri = lax.broadcasted_iota(jnp.int32, (CHUNK, CHUNK), 0)
    ci = lax.broadcasted_iota(jnp.int32, (CHUNK, CHUNK), 1)
    tril = ri >= ci
    tril_b = jnp.where(tril, 1.0, 0.0).astype(BF16)
    triu_b = jnp.where(ri <= ci, 1.0, 0.0).astype(BF16)

    pad = V7X_SUBLANES
    xpad[0:pad, :] = jnp.zeros((pad, V7X_LANES), F32)
    xpad[pad:pad + seq, :] = xm_ref[0]
    wq = wq_ref[0]
    wk = wk_ref[0]
    qscale = HEAD_DIM_M ** -0.5

    def conv_step(i, carry):
        base = pl.multiple_of(i * 256, 256)
        y = jnp.broadcast_to(cb_ref[...], (256, V7X_LANES))
        xw = xpad[pl.ds(base, 256 + pad), :]
        for j in range(CONV_WIDTH):
            off = pad - (CONV_WIDTH - 1) + j
            y = y + cw_ref[j:j + 1, :] * xw[off:off + 256, :]
        xc = (y * jax.nn.sigmoid(y)).astype(BF16)
        q_s[pl.ds(base, 256), :] = _dot(xc, wq) * qscale
        k_s[pl.ds(base, 256), :] = _dot(xc, wk)
        gc = gcol_ref[0, pl.ds(base, 256), :] + bcol_ref[...]
        lane256 = lax.broadcasted_iota(jnp.int32, (256, V7X_LANES), 1)
        ic = jnp.sum(jnp.where(lane256 == h, gc, 0.0), axis=1, keepdims=True)
        fc = jnp.sum(jnp.where(lane256 == h + N_HEADS_M, gc, 0.0), axis=1, keepdims=True)
        icol_s[pl.ds(base, 256), :] = jnp.broadcast_to(ic, (256, V7X_LANES))
        lfcol_s[pl.ds(base, 256), :] = jnp.broadcast_to(_log_sigmoid(fc), (256, V7X_LANES))
        return carry

    lax.fori_loop(0, seq // 256, conv_step, 0)

    gr = grow_ref[0] + brow_ref[...]
    sub = lax.broadcasted_iota(jnp.int32, (V7X_SUBLANES, seq), 0)
    irow = jnp.sum(jnp.where(sub == h, gr, 0.0), axis=0, keepdims=True)
    frow = jnp.sum(jnp.where(sub == h + N_HEADS_M, gr, 0.0), axis=0, keepdims=True)
    lfrow = _log_sigmoid(frow)
    for c in range(nchunk):
        ri_s[c:c + 1, :] = irow[:, c * CHUNK:(c + 1) * CHUNK]
        rf_s[c:c + 1, :] = lfrow[:, c * CHUNK:(c + 1) * CHUNK]

    def cum_step(c, carry):
        sl = pl.ds(pl.multiple_of(c * CHUNK, CHUNK), CHUNK)
        gcol_s[sl, :] = sum(_dot(tril_b, part) for part in _split3(lfcol_s[sl, :]))
        return carry

    lax.fori_loop(0, nchunk, cum_step, 0, unroll=4)
    irows = ri_s[...]
    grows = sum(_dot(part, triu_b) for part in _split3(rf_s[...]))
    gf_s[...] = grows
    g_last = grows[:, CHUNK - 1:CHUNK]
    a_max = jnp.max(g_last - grows + irows, axis=1, keepdims=True)

    crow = lax.broadcasted_iota(jnp.int32, (nchunk, V7X_LANES), 0)
    m = jnp.zeros((1, 1), F32)
    m_in = jnp.zeros((nchunk, V7X_LANES), F32)
    m_out = jnp.zeros((nchunk, V7X_LANES), F32)
    for c in range(nchunk):
        m_in = jnp.where(crow == c, m, m_in)
        m = jnp.maximum(g_last[c:c + 1, :] + m, a_max[c:c + 1, :])
        m_out = jnp.where(crow == c, m, m_out)
    mi_s[...] = m_in
    mo_s[...] = m_out
    gl_s[...] = jnp.broadcast_to(g_last, (nchunk, V7X_LANES))

    mg = mg_ref[...]

    def chunk_step(c, carry):
        cmat, nrow = carry
        base = pl.multiple_of(c * CHUNK, CHUNK)
        sl = pl.ds(base, CHUNK)
        one = pl.ds(c, 1)
        qc = q_s[sl, :]
        kc = k_s[sl, :]
        vc = vm_ref[0, sl, :].astype(BF16)
        icol = icol_s[sl, :]
        gcol = gcol_s[sl, :]
        irow_c = ri_s[one, :]
        grow_c = gf_s[one, :]
        m = mi_s[one, :]
        m_new = mo_s[one, :]
        g_last = gl_s[one, :]

        d = gcol - grow_c + irow_c
        d = jnp.where(tril, d, NEG_INF)
        inter = gcol + m
        m_t = jnp.maximum(inter, jnp.max(d, axis=1, keepdims=True))
        w_intra = jnp.exp(d - m_t)
        w_inter = jnp.exp(inter - m_t)
        qb = qc.astype(BF16)
        kb = kc.astype(BF16)
        qk = _dot_nt(qb, kb) * w_intra
        num = _dot(qk.astype(BF16), vc) + w_inter * _dot(qb, cmat.astype(BF16))
        den = jnp.sum(qk, axis=1, keepdims=True) + w_inter * jnp.sum(qc * nrow, axis=1, keepdims=True)
        hc = num / jnp.maximum(jnp.abs(den), jnp.exp(-m_t))

        a_col = g_last - gcol + icol
        decay = jnp.exp(g_last + m - m_new)
        wk_col = jnp.exp(a_col - m_new)
        kw = wk_col * kc
        c_new = decay * cmat + _dot(kw.T.astype(BF16), vc)
        n_new = decay * nrow + jnp.sum(kw, axis=0, keepdims=True)

        hn = hc * lax.rsqrt(jnp.mean(hc * hc, axis=1, keepdims=True) + RMS_EPS) * mg
        hn = jax.nn.sigmoid(om_ref[0, sl, :]) * hn
        o_ref[0, sl, :] = hn.astype(o_ref.dtype)
        return c_new, n_new

    init = (jnp.zeros((HEAD_DIM_M, HEAD_DIM_M), F32), jnp.zeros((1, HEAD_DIM_M), F32))
    lax.fori_loop(0, nchunk, chunk_step, init, unroll=4)


def _mlstm(proj3d, gcol3d, grow3d, conv_w, conv_b, wq, wk, bias_col, bias_row, mnorm):
    batch, seq, _ = proj3d.shape
    blk = (1, seq, V7X_LANES)
    a0 = 3 * D_ATTN // V7X_LANES
    nh = N_HEADS_M
    return pl.pallas_call(
        functools.partial(_mlstm_kernel, seq=seq),
        grid=(batch, nh),
        in_specs=[
            pl.BlockSpec(blk, lambda b, h: (b, 0, a0 + h)),
            pl.BlockSpec(blk, lambda b, h: (b, 0, a0 + nh + h)),
            pl.BlockSpec(blk, lambda b, h: (b, 0, a0 + 2 * nh + h)),
            pl.BlockSpec(blk, lambda b, h: (b, 0, 0)),
            pl.BlockSpec((1, V7X_SUBLANES, seq), lambda b, h: (b, 0, 0)),
            pl.BlockSpec((CONV_WIDTH, V7X_LANES), lambda b, h: (0, h)),
            pl.BlockSpec((1, V7X_LANES), lambda b, h: (0, h)),
            pl.BlockSpec((1, HEAD_DIM_M, HEAD_DIM_M), lambda b, h: (h, 0, 0)),
            pl.BlockSpec((1, HEAD_DIM_M, HEAD_DIM_M), lambda b, h: (h, 0, 0)),
            pl.BlockSpec((1, V7X_LANES), lambda b, h: (0, 0)),
            pl.BlockSpec((V7X_SUBLANES, seq), lambda b, h: (0, 0)),
            pl.BlockSpec((1, V7X_LANES), lambda b, h: (0, h)),
        ],
        out_specs=pl.BlockSpec(blk, lambda b, h: (b, 0, h)),
        out_shape=jax.ShapeDtypeStruct((batch, seq, D_MLSTM), BF16),
        scratch_shapes=[
            pltpu.VMEM((seq + V7X_SUBLANES, V7X_LANES), F32),
        ] + [pltpu.VMEM((seq, V7X_LANES), F32) for _ in range(5)]
          + [pltpu.VMEM((seq // CHUNK, V7X_LANES), F32) for _ in range(6)],
        compiler_params=_compiler_params(("arbitrary", "arbitrary")),
        name="mlstm",
    )(proj3d, proj3d, proj3d, gcol3d, grow3d, conv_w, conv_b, wq, wk, bias_col, bias_row, mnorm)


def _layer_norm(z, g, b):
    mu = jnp.mean(z, axis=1, keepdims=True)
    zc = z - mu
    var = jnp.mean(zc * zc, axis=1, keepdims=True)
    return zc * lax.rsqrt(var + LN_EPS) * g + b


def _router_kernel(attn_ref, hm_ref, x_ref, woa_ref, wom_ref, g_ref, b_ref, wr_ref, br_ref,
                   h_ref, post_ref, col_ref, tab_ref, tot_ref, carry):
    i = pl.program_id(0)
    tm = TOK_TILE

    @pl.when(i == 0)
    def _():
        carry[...] = jnp.zeros_like(carry)

    y = _dot(attn_ref[...], woa_ref[...]) + _dot(hm_ref[...], wom_ref[...])
    hval = _layer_norm(DEEPNORM_ALPHA * x_ref[...] + y, g_ref[...], b_ref[...])
    h_ref[...] = hval

    lane = lax.broadcasted_iota(jnp.int32, (tm, V7X_LANES), 1)
    lane_f = lane.astype(F32)
    logits = _dot(hval.astype(BF16), wr_ref[...]) + br_ref[...]
    logits = jnp.where(lane < N_EXPERTS, logits, NEG_INF)

    sel, vals = [], []
    for _ in range(TOP_K):
        mx = jnp.max(logits, axis=1, keepdims=True)
        idx = jnp.min(jnp.where(logits == mx, lane_f, float(V7X_LANES)), axis=1, keepdims=True)
        hit = lane_f == idx
        logits = jnp.where(hit, NEG_INF, logits)
        sel.append(hit)
        vals.append(mx)
    exps = [jnp.exp(v - vals[0]) for v in vals]
    den = exps[0] + exps[1] + exps[2] + exps[3]
    gates = [e / den for e in exps]

    onehot = jnp.zeros((tm, V7X_LANES), F32)
    for hit in sel:
        onehot = jnp.where(hit, 1.0, onehot)
    ri = lax.broadcasted_iota(jnp.int32, (tm, tm), 0)
    ci = lax.broadcasted_iota(jnp.int32, (tm, tm), 1)
    strict_lower = jnp.where(ri > ci, 1.0, 0.0).astype(BF16)
    rank = _dot(strict_lower, onehot.astype(BF16))
    cnt = jnp.sum(onehot, axis=0, keepdims=True)
    cnt_al = jnp.floor((cnt + (RUN_ALIGN - 1)) * (1.0 / RUN_ALIGN))
    er = lax.broadcasted_iota(jnp.int32, (V7X_LANES, V7X_LANES), 0)
    ec = lax.broadcasted_iota(jnp.int32, (V7X_LANES, V7X_LANES), 1)
    before = jnp.where(er < ec, 1.0, 0.0).astype(BF16)
    cnt8 = jnp.broadcast_to(cnt_al, (V7X_SUBLANES, V7X_LANES)).astype(BF16)
    slot = _dot(cnt8, before)[0:1, :] * float(RUN_ALIGN)
    cnt_al = cnt_al * float(RUN_ALIGN)

    col = jnp.zeros((tm, V7X_LANES), F32)
    for k in range(TOP_K):
        lpos = jnp.sum(jnp.where(sel[k], slot + rank, 0.0), axis=1, keepdims=True)
        col = jnp.where(lane == k, lpos, col)
        col = jnp.where(lane == TOP_K + k, gates[k], col)
    col_ref[...] = col
    post_ref[0] = col.T[:V7X_SUBLANES, :]

    base = carry[...]
    tab = jnp.zeros((V7X_SUBLANES, V7X_LANES), F32)
    sub = lax.broadcasted_iota(jnp.int32, (V7X_SUBLANES, V7X_LANES), 0)
    tab = jnp.where(sub == 0, cnt_al, tab)
    tab = jnp.where(sub == 1, base, tab)
    tab = jnp.where(sub == 2, slot, tab)
    tab_ref[0] = tab.astype(jnp.int32)
    carry[...] = base + cnt_al
    tot_ref[...] = jnp.broadcast_to(base + cnt_al, (V7X_SUBLANES, V7X_LANES)).astype(jnp.int32)


def _outproj_router(attn2d, hm2d, x2d, wo_a, wo_m, ln_g, ln_b, w_r, b_r):
    t = x2d.shape[0]
    tm = TOK_TILE
    nt = t // tm
    const = lambda i: (0, 0)
    return pl.pallas_call(
        _router_kernel,
        grid=(nt,),
        in_specs=[
            pl.BlockSpec((tm, D_ATTN), lambda i: (i, 0)),
            pl.BlockSpec((tm, D_MLSTM), lambda i: (i, 0)),
            pl.BlockSpec((tm, D_MODEL), lambda i: (i, 0)),
            pl.BlockSpec((D_ATTN, D_MODEL), const),
            pl.BlockSpec((D_MLSTM, D_MODEL), const),
            pl.BlockSpec((1, D_MODEL), const),
            pl.BlockSpec((1, D_MODEL), const),
            pl.BlockSpec((D_MODEL, V7X_LANES), const),
            pl.BlockSpec((1, V7X_LANES), const),
        ],
        out_specs=[
            pl.BlockSpec((tm, D_MODEL), lambda i: (i, 0)),
            pl.BlockSpec((1, V7X_SUBLANES, tm), lambda i: (i, 0, 0)),
            pl.BlockSpec((tm, V7X_LANES), lambda i: (i, 0)),
            pl.BlockSpec((1, V7X_SUBLANES, V7X_LANES), lambda i: (i, 0, 0)),
            pl.BlockSpec((V7X_SUBLANES, V7X_LANES), const),
        ],
        out_shape=[
            jax.ShapeDtypeStruct((t, D_MODEL), F32),
            jax.ShapeDtypeStruct((nt, V7X_SUBLANES, tm), F32),
            jax.ShapeDtypeStruct((t, V7X_LANES), F32),
            jax.ShapeDtypeStruct((nt, V7X_SUBLANES, V7X_LANES), jnp.int32),
            jax.ShapeDtypeStruct((V7X_SUBLANES, V7X_LANES), jnp.int32),
        ],
        scratch_shapes=[pltpu.VMEM((1, V7X_LANES), F32)],
        compiler_params=_compiler_params(("arbitrary",)),
        name="outproj_router",
    )(attn2d, hm2d, x2d, wo_a, wo_m, ln_g, ln_b, w_r, b_r)


def _run_copy(src, dst, start_src, start_dst, n, sem):
    n = pl.multiple_of(n, RUN_ALIGN)
    return pltpu.make_async_copy(
        src.at[pl.ds(pl.multiple_of(start_src, RUN_ALIGN), n)],
        dst.at[pl.ds(pl.multiple_of(start_dst, RUN_ALIGN), n)],
        sem)


def _dispatch_kernel(cnt_sm, slot_sm, dst_sm, tail_sm, post_ref, h_ref, xs_hbm, ybuf, zbuf, sem, zsem):
    i = pl.program_id(0)
    nt = pl.num_programs(0)
    cur = i % 2
    hb = h_ref[...].astype(BF16)
    pos = post_ref[0]
    chunk = 256
    for rc in range(ROWS_LOCAL // chunk):
        rows = (lax.broadcasted_iota(jnp.int32, (chunk, TOK_TILE), 0) + rc * chunk).astype(F32)
        p = jnp.zeros((chunk, TOK_TILE), F32)
        for k in range(TOP_K):
            p = jnp.where(rows == pos[k:k + 1, :], 1.0, p)
        ybuf[cur, rc * chunk:(rc + 1) * chunk, :] = _dot(p.astype(BF16), hb)

    def total_rows(tile):
        return lax.fori_loop(0, N_EXPERTS, lambda e, acc: acc + cnt_sm[tile * N_EXPERTS + e], 0)

    @pl.when(i > 0)
    def _():
        n_prev = total_rows(i - 1)
        _run_copy(ybuf.at[1 - cur], xs_hbm, 0, 0, n_prev, sem).wait()

    def issue(e, carry):
        n = cnt_sm[i * N_EXPERTS + e]

        @pl.when(n > 0)
        def _():
            _run_copy(ybuf.at[cur], xs_hbm, slot_sm[i * N_EXPERTS + e], dst_sm[i * N_EXPERTS + e], n, sem).start()
        return carry

    lax.fori_loop(0, N_EXPERTS, issue, 0)

    @pl.when(i == nt - 1)
    def _():
        n_cur = total_rows(i)
        _run_copy(ybuf.at[cur], xs_hbm, 0, 0, n_cur, sem).wait()
        zbuf[...] = jnp.zeros_like(zbuf)

        def ztail(wait):
            def body(e, carry):
                n = tail_sm[N_EXPERTS + e]

                @pl.when(n > 0)
                def _():
                    cp = _run_copy(zbuf, xs_hbm, 0, tail_sm[e], n, zsem)
                    if wait:
                        cp.wait()
                    else:
                        cp.start()
                return carry
            return body

        lax.fori_loop(0, N_EXPERTS, ztail(False), 0)
        lax.fori_loop(0, N_EXPERTS, ztail(True), 0)

        def zblock(wait):
            def body(j, carry):
                cp = _run_copy(zbuf, xs_hbm, 0, j * ROW_BLOCK, ROW_BLOCK, zsem)
                if wait:
                    cp.wait()
                else:
                    cp.start()
                return carry
            return body

        n_blocks = xs_hbm.shape[0] // ROW_BLOCK
        lax.fori_loop(tail_sm[2 * N_EXPERTS], n_blocks, zblock(False), 0)
        lax.fori_loop(tail_sm[2 * N_EXPERTS], n_blocks, zblock(True), 0)


def _dispatch(cnt, slot, dst, tail, post, h2d, n_rows):
    t = h2d.shape[0]
    nt = t // TOK_TILE
    return pl.pallas_call(
        _dispatch_kernel,
        grid_spec=pltpu.PrefetchScalarGridSpec(
            num_scalar_prefetch=4,
            grid=(nt,),
            in_specs=[
                pl.BlockSpec((1, V7X_SUBLANES, TOK_TILE), lambda i, *_: (i, 0, 0)),
                pl.BlockSpec((TOK_TILE, D_MODEL), lambda i, *_: (i, 0)),
            ],
            out_specs=pl.BlockSpec(memory_space=pl.ANY),
            scratch_shapes=[
                pltpu.VMEM((2, ROWS_LOCAL, D_MODEL), F32),
                pltpu.VMEM((ROW_BLOCK, D_MODEL), F32),
                pltpu.SemaphoreType.DMA(()),
                pltpu.SemaphoreType.DMA(()),
            ],
        ),
        out_shape=jax.ShapeDtypeStruct((n_rows, D_MODEL), F32),
        compiler_params=_compiler_params(("arbitrary",)),
        name="moe_dispatch",
    )(cnt, slot, dst, tail, post, h2d)


def _expert_kernel(bexp_sm, nused_sm, xs_ref, wg_ref, bg_ref, wu_ref, bu_ref, wd_ref, bd_ref, ys_ref,
                   hbuf, wg_b, wu_b, wd_b):
    j = pl.program_id(0)
    used = j < nused_sm[0]
    new_expert = jnp.logical_or(j == 0, bexp_sm[j] != bexp_sm[jnp.maximum(j, 1) - 1])

    @pl.when(jnp.logical_and(used, new_expert))
    def _():
        def cast(i, carry):
            sl = pl.ds(pl.multiple_of(i * 128, 128), 128)
            wg_b[sl, :] = wg_ref[0, sl, :].astype(BF16)
            wu_b[sl, :] = wu_ref[0, sl, :].astype(BF16)
            wd_b[sl, :] = wd_ref[0, sl, :].astype(BF16)
            return carry
        lax.fori_loop(0, D_MODEL // 128, cast, 0)

    @pl.when(used)
    def _():
        xb = xs_ref[...].astype(BF16)
        step = 512
        for n in range(0, D_MODEL, step):
            g = _dot(xb, wg_b[:, n:n + step]) + bg_ref[0, :, n:n + step]
            u = _dot(xb, wu_b[:, n:n + step]) + bu_ref[0, :, n:n + step]
            g = jnp.minimum(g, SWIGLU_LIMIT)
            u = jnp.clip(u, -SWIGLU_LIMIT, SWIGLU_LIMIT)
            hbuf[:, n:n + step] = (g * jax.nn.sigmoid(SWIGLU_ALPHA * g) * (u + 1.0)).astype(BF16)
        ys_ref[...] = _dot(hbuf[...], wd_b[...]) + bd_ref[0]

    @pl.when(jnp.logical_not(used))
    def _():
        ys_ref[...] = jnp.zeros_like(ys_ref)


def _experts(bexp, nused, xs, wg, bg, wu, bu, wd, bd):
    n_rows = xs.shape[0]
    nb = n_rows // ROW_BLOCK

    def row_map(j, be, nu):
        return (jnp.minimum(j, nu[0] - 1), 0)

    def w_map(j, be, nu):
        return (be[jnp.minimum(j, nu[0] - 1)], 0, 0)

    wspec = pl.BlockSpec((1, D_MODEL, D_MODEL), w_map)
    bspec = pl.BlockSpec((1, 1, D_MODEL), w_map)
    return pl.pallas_call(
        _expert_kernel,
        grid_spec=pltpu.PrefetchScalarGridSpec(
            num_scalar_prefetch=2,
            grid=(nb,),
            in_specs=[pl.BlockSpec((ROW_BLOCK, D_MODEL), row_map), wspec, bspec, wspec, bspec, wspec, bspec],
            out_specs=pl.BlockSpec((ROW_BLOCK, D_MODEL), lambda j, be, nu: (j, 0)),
            scratch_shapes=[pltpu.VMEM((ROW_BLOCK, D_MODEL), BF16)]
            + [pltpu.VMEM((D_MODEL, D_MODEL), BF16) for _ in range(3)],
        ),
        out_shape=jax.ShapeDtypeStruct((n_rows, D_MODEL), F32),
        compiler_params=_compiler_params(("arbitrary",)),
        name="moe_experts",
    )(bexp, nused, xs, wg, bg, wu, bu, wd, bd)


def _combine_kernel(cnt_sm, slot_sm, dst_sm, col_ref, h_ref, g_ref, b_ref, ys_hbm, o_ref, ybuf, sem):
    i = pl.program_id(0)
    nt = pl.num_programs(0)
    cur = i % 2

    def fetch(tile, buf, wait):
        def body(e, carry):
            n = cnt_sm[tile * N_EXPERTS + e]

            @pl.when(n > 0)
            def _():
                cp = _run_copy(ys_hbm, ybuf.at[buf], dst_sm[tile * N_EXPERTS + e], slot_sm[tile * N_EXPERTS + e],
                               n, sem.at[buf])
                if wait:
                    cp.wait()
                else:
                    cp.start()
            return carry
        lax.fori_loop(0, N_EXPERTS, body, 0)

    @pl.when(i == 0)
    def _():
        ybuf[...] = jnp.zeros_like(ybuf)
        fetch(0, 0, False)

    @pl.when(i + 1 < nt)
    def _():
        fetch(i + 1, 1 - cur, False)

    fetch(i, cur, True)

    col = col_ref[...]
    moe = jnp.zeros((TOK_TILE, D_MODEL), F32)
    chunk = 256
    for rc in range(ROWS_LOCAL // chunk):
        rows = (lax.broadcasted_iota(jnp.int32, (TOK_TILE, chunk), 1) + rc * chunk).astype(F32)
        s = jnp.zeros((TOK_TILE, chunk), F32)
        for k in range(TOP_K):
            s = jnp.where(rows == col[:, k:k + 1], col[:, TOP_K + k:TOP_K + k + 1], s)
        moe = moe + _dot(s.astype(BF16), ybuf[cur, rc * chunk:(rc + 1) * chunk, :].astype(BF16))
    o_ref[...] = _layer_norm(DEEPNORM_ALPHA * h_ref[...] + moe, g_ref[...], b_ref[...])


def _combine(cnt, slot, dst, col, h2d, ln_g, ln_b, ys):
    t = h2d.shape[0]
    nt = t // TOK_TILE
    return pl.pallas_call(
        _combine_kernel,
        grid_spec=pltpu.PrefetchScalarGridSpec(
            num_scalar_prefetch=3,
            grid=(nt,),
            in_specs=[
                pl.BlockSpec((TOK_TILE, V7X_LANES), lambda i, *_: (i, 0)),
                pl.BlockSpec((TOK_TILE, D_MODEL), lambda i, *_: (i, 0)),
                pl.BlockSpec((1, D_MODEL), lambda i, *_: (0, 0)),
                pl.BlockSpec((1, D_MODEL), lambda i, *_: (0, 0)),
                pl.BlockSpec(memory_space=pl.ANY),
            ],
            out_specs=pl.BlockSpec((TOK_TILE, D_MODEL), lambda i, *_: (i, 0)),
            scratch_shapes=[
                pltpu.VMEM((2, ROWS_LOCAL, D_MODEL), F32),
                pltpu.SemaphoreType.DMA((2,)),
            ],
        ),
        out_shape=jax.ShapeDtypeStruct((t, D_MODEL), F32),
        compiler_params=_compiler_params(("arbitrary",)),
        name="moe_combine",
    )(cnt, slot, dst, col, h2d, ln_g, ln_b, ys)


def _pad_lanes(a, width=V7X_LANES):
    return jnp.pad(a, ((0, 0), (0, width - a.shape[1])))


def _layer(h3d, w_in, conv_w, conv_b, w_mq, w_mk, b_igate, b_fgate, mnorm_g, w_out,
           ln1_g, ln1_b, w_router, b_router, w_gate, b_gate, w_up, b_up, w_down, b_down, ln2_g, ln2_b):
    batch, seq, _ = h3d.shape
    t = batch * seq
    x2d = h3d.reshape(t, D_MODEL)

    w_main = w_in[:, :PROJ_MAIN].astype(BF16)
    w_gates = _pad_lanes(w_in[:, PROJ_MAIN:]).astype(BF16)
    proj, gcol, grow = _inproj(x2d, w_main, w_gates, batch, seq)
    proj3d = proj.reshape(batch, seq, PROJ_MAIN)
    attn = _attention(proj3d)
    gate_bias = jnp.concatenate([b_igate, b_fgate]).astype(F32)
    bias_col = _pad_lanes(gate_bias[None, :])
    bias_row = jnp.broadcast_to(gate_bias[:, None], (V7X_SUBLANES, seq))
    hm = _mlstm(proj3d, gcol.reshape(batch, seq, V7X_LANES), grow, conv_w, conv_b[None, :],
                w_mq.astype(BF16), w_mk.astype(BF16), bias_col, bias_row, mnorm_g[None, :])

    wo = w_out.astype(BF16)
    h2d, post, col, tab, tot = _outproj_router(
        attn.reshape(t, D_ATTN), hm.reshape(t, D_MLSTM), x2d, wo[:D_ATTN], wo[D_ATTN:],
        ln1_g[None, :], ln1_b[None, :], _pad_lanes(w_router).astype(BF16), _pad_lanes(b_router[None, :]))

    nt = t // TOK_TILE
    total = tot[0, :N_EXPERTS]
    region = (total + ROW_BLOCK - 1) // ROW_BLOCK * ROW_BLOCK
    region_end = jnp.cumsum(region)
    region_start = region_end - region
    n_rows = (t * TOP_K + nt * N_EXPERTS * (RUN_ALIGN - 1)) // ROW_BLOCK * ROW_BLOCK + (N_EXPERTS + 1) * ROW_BLOCK
    nb = n_rows // ROW_BLOCK
    cnt = tab[:, 0, :N_EXPERTS].reshape(-1)
    slot = tab[:, 2, :N_EXPERTS].reshape(-1)
    dst = (tab[:, 1, :N_EXPERTS] + region_start[None, :]).reshape(-1)
    tail = jnp.concatenate([region_start + total, region - total, region_end[-1:] // ROW_BLOCK]).astype(jnp.int32)
    block_row = jnp.arange(nb, dtype=jnp.int32) * ROW_BLOCK
    bexp = jnp.minimum(jnp.sum(region_end[None, :] <= block_row[:, None], axis=1), N_EXPERTS - 1).astype(jnp.int32)
    nused = (region_end[-1:] // ROW_BLOCK).astype(jnp.int32)

    xs = _dispatch(cnt, slot, dst, tail, post, h2d, n_rows)
    ys = _experts(bexp, nused, xs, w_gate, b_gate[:, None, :], w_up, b_up[:, None, :], w_down, b_down[:, None, :])
    out = _combine(cnt, slot, dst, col, h2d, ln2_g[None, :], ln2_b[None, :], ys)
    return out.reshape(batch, seq, D_MODEL)


def kernel(x, w_in, conv_w, conv_b, w_mq, w_mk, b_igate, b_fgate, mnorm_g, w_out, ln1_g, ln1_b, w_router, b_router, w_gate, b_gate, w_up, b_up, w_down, b_down, ln2_g, ln2_b):
    h = x
    for l in range(w_in.shape[0]):
        h = _layer(h, w_in[l], conv_w[l], conv_b[l], w_mq[l], w_mk[l], b_igate[l], b_fgate[l], mnorm_g[l],
                   w_out[l], ln1_g[l], ln1_b[l], w_router[l], b_router[l], w_gate[l], b_gate[l], w_up[l],
                   b_up[l], w_down[l], b_down[l], ln2_g[l], ln2_b[l])
    return h
```

```python
import functools
import math

import jax
import jax.numpy as jnp
from jax import lax
from jax.experimental import pallas as pl
from jax.experimental.pallas import tpu as pltpu

F32 = jnp.float32
BF16 = jnp.bfloat16
NEG_INF = float("-inf")

V7X_LANES = 128
V7X_SUBLANES = 8
V7X_VMEM_LIMIT_BYTES = 56 * 1024 * 1024

D_MODEL = 1024
D_ATTN = 512
HEAD_DIM_A = 64
D_MLSTM = 512
N_HEADS_M = 4
HEAD_DIM_M = 128
CONV_WIDTH = 4
CHUNK = 128
DILATED_BRANCHES = ((128, 1), (512, 4), (2048, 16))
N_EXPERTS = 32
TOP_K = 4
SWIGLU_LIMIT = 7.0
SWIGLU_ALPHA = 1.702
DEEPNORM_ALPHA = 2.0 ** 0.25
LN_EPS = 1e-5
RMS_EPS = 1e-6

PROJ_MAIN = 3 * D_ATTN + 3 * D_MLSTM
TOK_TILE = 256
RUN_ALIGN = V7X_SUBLANES
ROWS_LOCAL = TOK_TILE * TOP_K + N_EXPERTS * RUN_ALIGN
ROW_BLOCK = 512
ROUTER_SUBTILES = 4


def _dot(a, b):
    return jnp.dot(a, b, preferred_element_type=F32)


def _dot_nt(a, b):
    return lax.dot_general(a, b, (((1,), (1,)), ((), ())), preferred_element_type=F32)


def _split3(x):
    hi = x.astype(BF16)
    r1 = x - hi.astype(F32)
    mid = r1.astype(BF16)
    lo = (r1 - mid.astype(F32)).astype(BF16)
    return hi, mid, lo


def _compiler_params(sem):
    return pltpu.CompilerParams(dimension_semantics=sem, vmem_limit_bytes=V7X_VMEM_LIMIT_BYTES)


def _inproj_kernel(x_ref, w_ref, wg_ref, proj_ref, gcol_ref, grow_ref):
    xb = x_ref[...].astype(BF16)
    step = 512
    for n in range(0, PROJ_MAIN, step):
        proj_ref[:, n:n + step] = _dot(xb, w_ref[:, n:n + step])
    g = _dot(xb, wg_ref[...])
    gcol_ref[...] = g
    grow_ref[0] = g.T[:V7X_SUBLANES, :]


def _inproj(x2d, w_main, w_gates, batch, seq):
    t = x2d.shape[0]
    tm = 512
    per_b = seq // tm
    return pl.pallas_call(
        _inproj_kernel,
        grid=(t // tm,),
        in_specs=[
            pl.BlockSpec((tm, D_MODEL), lambda i: (i, 0)),
            pl.BlockSpec((D_MODEL, PROJ_MAIN), lambda i: (0, 0)),
            pl.BlockSpec((D_MODEL, V7X_LANES), lambda i: (0, 0)),
        ],
        out_specs=[
            pl.BlockSpec((tm, PROJ_MAIN), lambda i: (i, 0)),
            pl.BlockSpec((tm, V7X_LANES), lambda i: (i, 0)),
            pl.BlockSpec((1, V7X_SUBLANES, tm), lambda i: (i // per_b, 0, i % per_b)),
        ],
        out_shape=[
            jax.ShapeDtypeStruct((t, PROJ_MAIN), F32),
            jax.ShapeDtypeStruct((t, V7X_LANES), F32),
            jax.ShapeDtypeStruct((batch, V7X_SUBLANES, seq), F32),
        ],
        compiler_params=_compiler_params(("arbitrary",)),
        name="inproj",
    )(x2d, w_main, w_gates)


ATTN_GROUP = 8


def _attn_kernel(q_ref, k_ref, v_ref, o_ref, q0s, q1s, bias_b, bias_f, sg, pg,
                 o0, o1, o2, l0, l1, l2, m0, m1, m2, x0, x1, x2, *, seq):
    obufs, lbufs, mbufs, xbufs = (o0, o1, o2), (l0, l1, l2), (m0, m1, m2), (x0, x1, x2)
    two = 2 * CHUNK
    head0 = lax.broadcasted_iota(jnp.int32, (CHUNK, V7X_LANES), 1) < HEAD_DIM_A
    qscale = HEAD_DIM_A ** -0.5 * math.log2(math.e)

    def prep(i, carry):
        sl = pl.ds(pl.multiple_of(i * two, two), two)
        h0 = lax.broadcasted_iota(jnp.int32, (two, V7X_LANES), 1) < HEAD_DIM_A
        q = q_ref[0, sl, :] * qscale
        q0s[sl, :] = jnp.where(h0, q, 0.0)
        q1s[sl, :] = jnp.where(h0, 0.0, q)
        return carry

    lax.fori_loop(0, seq // two, prep, 0)

    qi = lax.broadcasted_iota(jnp.int32, (two, two), 0) % CHUNK
    kj = lax.broadcasted_iota(jnp.int32, (two, two), 1)
    bias_b[...] = jnp.where((kj >= qi) & (kj <= qi + CHUNK), 0.0, NEG_INF)
    qf = lax.broadcasted_iota(jnp.int32, (two, CHUNK), 0) % CHUNK
    kf = lax.broadcasted_iota(jnp.int32, (two, CHUNK), 1)
    bias_f[...] = jnp.where(kf <= qf, 0.0, NEG_INF)

    def run_group(c, dil, starts, has_prev):
        assert len(starts) <= ATTN_GROUP
        nk = two if has_prev else CHUNK
        bias_ref = bias_b if has_prev else bias_f

        def rows(s0):
            return pl.ds(s0, CHUNK) if dil == 1 else pl.ds(s0, CHUNK, stride=dil)

        def keys(ref, st):
            if has_prev:
                return jnp.concatenate([ref[0, rows(st - dil * CHUNK), :], ref[0, rows(st), :]], axis=0).astype(BF16)
            return ref[0, rows(st), :].astype(BF16)

        def both_heads(x):
            return jnp.where(head0, x[:CHUNK], x[CHUNK:])

        for j, st in enumerate(starts):
            q2 = jnp.concatenate([q0s[rows(st), :], q1s[rows(st), :]], axis=0).astype(BF16)
            sg[j, :, 0:nk] = _dot_nt(q2, keys(k_ref, st)) + bias_ref[...]
        for j, st in enumerate(starts):
            s = sg[j, :, 0:nk]
            m = jnp.max(s, axis=1, keepdims=True)
            pg[j, :, 0:nk] = jnp.exp2(s - m).astype(BF16)
            ma = jnp.broadcast_to(m[:CHUNK], (CHUNK, V7X_LANES))
            mb = jnp.broadcast_to(m[CHUNK:], (CHUNK, V7X_LANES))
            mbufs[c][rows(st), :] = jnp.where(head0, ma, mb)
            xbufs[c][rows(st), :] = jnp.where(head0, mb, ma)
        khead0 = lax.broadcasted_iota(jnp.int32, (nk, V7X_LANES), 1) < HEAD_DIM_A
        for j, st in enumerate(starts):
            if has_prev:
                vv = jnp.concatenate([v_ref[0, rows(st - dil * CHUNK), :], v_ref[0, rows(st), :]], axis=0)
            else:
                vv = v_ref[0, rows(st), :]
            oa = _dot(pg[j, 0:CHUNK, 0:nk], jnp.where(khead0, vv, 1.0).astype(BF16))
            ob = _dot(pg[j, CHUNK:two, 0:nk], jnp.where(khead0, 1.0, vv).astype(BF16))
            obufs[c][rows(st), :] = jnp.where(head0, oa, ob)
            lbufs[c][rows(st), :] = jnp.where(head0, ob, oa)

    for c, (window, dil) in enumerate(DILATED_BRANCHES):
        assert window // dil == CHUNK
        nb = seq // (dil * CHUNK)
        span = dil * CHUNK
        if dil == 1:
            run_group(c, dil, [0], False)
            per = 5
            assert (nb - 1) % per == 0

            def band1(g, carry, c=c, dil=dil, per=per, span=span):
                base = span + g * (per * span)
                run_group(c, dil, [base + span * j for j in range(per)], True)
                return carry

            lax.fori_loop(0, (nb - 1) // per, band1, 0)
        elif nb > 1:
            run_group(c, dil, list(range(dil)), False)

            def band2(g, carry, c=c, dil=dil, nb=nb, span=span):
                starts = [2 * g + rr + span * b for rr in range(2) for b in range(1, nb)]
                run_group(c, dil, starts, True)
                return carry

            lax.fori_loop(0, dil // 2, band2, 0)
        else:
            def firsts(g, carry, c=c, dil=dil):
                run_group(c, dil, [ATTN_GROUP * g + j for j in range(ATTN_GROUP)], False)
                return carry

            lax.fori_loop(0, dil // ATTN_GROUP, firsts, 0)

    def combine(i, carry):
        sl = pl.ds(pl.multiple_of(i * 256, 256), 256)
        ma, mb, mc = m0[sl, :], m1[sl, :], m2[sl, :]
        mx = jnp.maximum(jnp.maximum(ma, mb), mc)
        wa, wb, wc = jnp.exp2(ma - mx), jnp.exp2(mb - mx), jnp.exp2(mc - mx)
        xa, xb, xc = x0[sl, :], x1[sl, :], x2[sl, :]
        xm = jnp.maximum(jnp.maximum(xa, xb), xc)
        den = (jnp.exp2(xa - xm) * l0[sl, :] + jnp.exp2(xb - xm) * l1[sl, :] + jnp.exp2(xc - xm) * l2[sl, :])
        den = pltpu.roll(den, HEAD_DIM_A, axis=1)
        out = (wa * o0[sl, :] + wb * o1[sl, :] + wc * o2[sl, :]) / den
        o_ref[0, sl, :] = out.astype(o_ref.dtype)
        return carry

    lax.fori_loop(0, seq // 256, combine, 0)


def _attention(proj3d):
    batch, seq, _ = proj3d.shape
    nblk = D_ATTN // V7X_LANES
    blk = (1, seq, V7X_LANES)
    scratch = ([pltpu.VMEM((seq, V7X_LANES), F32), pltpu.VMEM((seq, V7X_LANES), F32),
                pltpu.VMEM((2 * CHUNK, 2 * CHUNK), F32), pltpu.VMEM((2 * CHUNK, CHUNK), F32),
                pltpu.VMEM((ATTN_GROUP, 2 * CHUNK, 2 * CHUNK), F32),
                pltpu.VMEM((ATTN_GROUP, 2 * CHUNK, 2 * CHUNK), BF16)]
               + [pltpu.VMEM((seq, V7X_LANES), F32) for _ in range(12)])
    return pl.pallas_call(
        functools.partial(_attn_kernel, seq=seq),
        grid=(batch, nblk),
        in_specs=[
            pl.BlockSpec(blk, lambda b, g: (b, 0, g)),
            pl.BlockSpec(blk, lambda b, g: (b, 0, nblk + g)),
            pl.BlockSpec(blk, lambda b, g: (b, 0, 2 * nblk + g)),
        ],
        out_specs=pl.BlockSpec(blk, lambda b, g: (b, 0, g)),
        out_shape=jax.ShapeDtypeStruct((batch, seq, D_ATTN), BF16),
        scratch_shapes=scratch,
        compiler_params=_compiler_params(("arbitrary", "arbitrary")),
        name="dilated_attention",
    )(proj3d, proj3d, proj3d)


def _log_sigmoid(x):
    return jnp.minimum(x, 0.0) - jnp.log1p(jnp.exp(-jnp.abs(x)))


def _mlstm_kernel(xm_ref, vm_ref, om_ref, gcol_ref, grow_ref, cw_ref, cb_ref, wq_ref, wk_ref,
                  bcol_ref, brow_ref, mg_ref, o_ref,
                  xpad, q_s, k_s, icol_s, lfcol_s, gcol_s, ri_s, rf_s, gf_s, mi_s, mo_s, gl_s, *, seq):
    h = pl.program_id(1)
    nchunk = seq // CHUNK
    lane = lax.broadcasted_iota(jnp.int32, (CHUNK, V7X_LANES), 1)
    ri = lax.broadcasted_iota(jnp.int32, (CHUNK, CHUNK), 0)
    ci = lax.broadcasted_iota(jnp.int32, (CHUNK, CHUNK), 1)
    tril = ri >= ci
    tril_b = jnp.where(tril, 1.0, 0.0).astype(BF16)
    triu_b = jnp.where(ri <= ci, 1.0, 0.0).astype(BF16)

    pad = V7X_SUBLANES
    xpad[0:pad, :] = jnp.zeros((pad, V7X_LANES), F32)
    xpad[pad:pad + seq, :] = xm_ref[0]
    wq = wq_ref[0]
    wk = wk_ref[0]
    qscale = HEAD_DIM_M ** -0.5

    def conv_step(i, carry):
        base = pl.multiple_of(i * 256, 256)
        y = jnp.broadcast_to(cb_ref[...], (256, V7X_LANES))
        xw = xpad[pl.ds(base, 256 + pad), :]
        for j in range(CONV_WIDTH):
            off = pad - (CONV_WIDTH - 1) + j
            y = y + cw_ref[j:j + 1, :] * xw[off:off + 256, :]
        xc = (y * jax.nn.sigmoid(y)).astype(BF16)
        q_s[pl.ds(base, 256), :] = _dot(xc, wq) * qscale
        k_s[pl.ds(base, 256), :] = _dot(xc, wk)
        gc = gcol_ref[0, pl.ds(base, 256), :] + bcol_ref[...]
        lane256 = lax.broadcasted_iota(jnp.int32, (256, V7X_LANES), 1)
        ic = jnp.sum(jnp.where(lane256 == h, gc, 0.0), axis=1, keepdims=True)
        fc = jnp.sum(jnp.where(lane256 == h + N_HEADS_M, gc, 0.0), axis=1, keepdims=True)
        icol_s[pl.ds(base, 256), :] = jnp.broadcast_to(ic, (256, V7X_LANES))
        lfcol_s[pl.ds(base, 256), :] = jnp.broadcast_to(_log_sigmoid(fc), (256, V7X_LANES))
        return carry

    lax.fori_loop(0, seq // 256, conv_step, 0)

    gr = grow_ref[0] + brow_ref[...]
    sub = lax.broadcasted_iota(jnp.int32, (V7X_SUBLANES, seq), 0)
    irow = jnp.sum(jnp.where(sub == h, gr, 0.0), axis=0, keepdims=True)
    frow = jnp.sum(jnp.where(sub == h + N_HEADS_M, gr, 0.0), axis=0, keepdims=True)
    lfrow = _log_sigmoid(frow)
    for c in range(nchunk):
        ri_s[c:c + 1, :] = irow[:, c * CHUNK:(c + 1) * CHUNK]
        rf_s[c:c + 1, :] = lfrow[:, c * CHUNK:(c + 1) * CHUNK]

    def cum_step(c, carry):
        sl = pl.ds(pl.multiple_of(c * CHUNK, CHUNK), CHUNK)
        gcol_s[sl, :] = sum(_dot(tril_b, part) for part in _split3(lfcol_s[sl, :]))
        return carry

    lax.fori_loop(0, nchunk, cum_step, 0, unroll=4)
    irows = ri_s[...]
    grows = sum(_dot(part, triu_b) for part in _split3(rf_s[...]))
    gf_s[...] = grows
    g_last = grows[:, CHUNK - 1:CHUNK]
    a_max = jnp.max(g_last - grows + irows, axis=1, keepdims=True)

    crow = lax.broadcasted_iota(jnp.int32, (nchunk, V7X_LANES), 0)
    m = jnp.zeros((1, 1), F32)
    m_in = jnp.zeros((nchunk, V7X_LANES), F32)
    m_out = jnp.zeros((nchunk, V7X_LANES), F32)
    for c in range(nchunk):
        m_in = jnp.where(crow == c, m, m_in)
        m = jnp.maximum(g_last[c:c + 1, :] + m, a_max[c:c + 1, :])
        m_out = jnp.where(crow == c, m, m_out)
    mi_s[...] = m_in
    mo_s[...] = m_out
    gl_s[...] = jnp.broadcast_to(g_last, (nchunk, V7X_LANES))

    mg = mg_ref[...]

    def chunk_step(c, carry):
        cmat, nrow = carry
        base = pl.multiple_of(c * CHUNK, CHUNK)
        sl = pl.ds(base, CHUNK)
        one = pl.ds(c, 1)
        qc = q_s[sl, :]
        kc = k_s[sl, :]
        vc = vm_ref[0, sl, :].astype(BF16)
        icol = icol_s[sl, :]
        gcol = gcol_s[sl, :]
        irow_c = ri_s[one, :]
        grow_c = gf_s[one, :]
        m = mi_s[one, :]
        m_new = mo_s[one, :]
        g_last = gl_s[one, :]

        d = gcol - grow_c + irow_c
        d = jnp.where(tril, d, NEG_INF)
        inter = gcol + m
        m_t = jnp.maximum(inter, jnp.max(d, axis=1, keepdims=True))
        w_intra = jnp.exp(d - m_t)
        w_inter = jnp.exp(inter - m_t)
        qb = qc.astype(BF16)
        kb = kc.astype(BF16)
        qk = _dot_nt(qb, kb) * w_intra
        num = _dot(qk.astype(BF16), vc) + w_inter * _dot(qb, cmat.astype(BF16))
        den = jnp.sum(qk, axis=1, keepdims=True) + w_inter * jnp.sum(qc * nrow, axis=1, keepdims=True)
        hc = num / jnp.maximum(jnp.abs(den), jnp.exp(-m_t))

        a_col = g_last - gcol + icol
        decay = jnp.exp(g_last + m - m_new)
        wk_col = jnp.exp(a_col - m_new)
        kw = wk_col * kc
        c_new = decay * cmat + _dot(kw.T.astype(BF16), vc)
        n_new = decay * nrow + jnp.sum(kw, axis=0, keepdims=True)

        hn = hc * lax.rsqrt(jnp.mean(hc * hc, axis=1, keepdims=True) + RMS_EPS) * mg
        hn = jax.nn.sigmoid(om_ref[0, sl, :]) * hn
        o_ref[0, sl, :] = hn.astype(o_ref.dtype)
        return c_new, n_new

    init = (jnp.zeros((HEAD_DIM_M, HEAD_DIM_M), F32), jnp.zeros((1, HEAD_DIM_M), F32))
    lax.fori_loop(0, nchunk, chunk_step, init, unroll=4)


def _mlstm(proj3d, gcol3d, grow3d, conv_w, conv_b, wq, wk, bias_col, bias_row, mnorm):
    batch, seq, _ = proj3d.shape
    blk = (1, seq, V7X_LANES)
    a0 = 3 * D_ATTN // V7X_LANES
    nh = N_HEADS_M
    return pl.pallas_call(
        functools.partial(_mlstm_kernel, seq=seq),
        grid=(batch, nh),
        in_specs=[
            pl.BlockSpec(blk, lambda b, h: (b, 0, a0 + h)),
            pl.BlockSpec(blk, lambda b, h: (b, 0, a0 + nh + h)),
            pl.BlockSpec(blk, lambda b, h: (b, 0, a0 + 2 * nh + h)),
            pl.BlockSpec(blk, lambda b, h: (b, 0, 0)),
            pl.BlockSpec((1, V7X_SUBLANES, seq), lambda b, h: (b, 0, 0)),
            pl.BlockSpec((CONV_WIDTH, V7X_LANES), lambda b, h: (0, h)),
            pl.BlockSpec((1, V7X_LANES), lambda b, h: (0, h)),
            pl.BlockSpec((1, HEAD_DIM_M, HEAD_DIM_M), lambda b, h: (h, 0, 0)),
            pl.BlockSpec((1, HEAD_DIM_M, HEAD_DIM_M), lambda b, h: (h, 0, 0)),
            pl.BlockSpec((1, V7X_LANES), lambda b, h: (0, 0)),
            pl.BlockSpec((V7X_SUBLANES, seq), lambda b, h: (0, 0)),
            pl.BlockSpec((1, V7X_LANES), lambda b, h: (0, h)),
        ],
        out_specs=pl.BlockSpec(blk, lambda b, h: (b, 0, h)),
        out_shape=jax.ShapeDtypeStruct((batch, seq, D_MLSTM), BF16),
        scratch_shapes=[
            pltpu.VMEM((seq + V7X_SUBLANES, V7X_LANES), F32),
        ] + [pltpu.VMEM((seq, V7X_LANES), F32) for _ in range(5)]
          + [pltpu.VMEM((seq // CHUNK, V7X_LANES), F32) for _ in range(6)],
        compiler_params=_compiler_params(("arbitrary", "arbitrary")),
        name="mlstm",
    )(proj3d, proj3d, proj3d, gcol3d, grow3d, conv_w, conv_b, wq, wk, bias_col, bias_row, mnorm)


def _layer_norm(z, g, b):
    mu = jnp.mean(z, axis=1, keepdims=True)
    zc = z - mu
    var = jnp.mean(zc * zc, axis=1, keepdims=True)
    return zc * lax.rsqrt(var + LN_EPS) * g + b


def _router_kernel(attn_ref, hm_ref, x_ref, woa_ref, wom_ref, g_ref, b_ref, wr_ref, br_ref,
                   h_ref, post_ref, col_ref, tab_ref, tot_ref, carry, *, nsub):
    i = pl.program_id(0)
    tm = TOK_TILE

    @pl.when(i == 0)
    def _():
        carry[...] = jnp.zeros_like(carry)

    ne = N_EXPERTS
    ex = lax.broadcasted_iota(jnp.int32, (ne, tm), 0).astype(F32)
    ri = lax.broadcasted_iota(jnp.int32, (tm, tm), 0)
    ci = lax.broadcasted_iota(jnp.int32, (tm, tm), 1)
    earlier = jnp.where(ri < ci, 1.0, 0.0).astype(BF16)
    er = lax.broadcasted_iota(jnp.int32, (ne, ne), 0)
    ec = lax.broadcasted_iota(jnp.int32, (ne, ne), 1)
    lower = jnp.where(ec < er, 1.0, 0.0).astype(BF16)
    diag = (lax.broadcasted_iota(jnp.int32, (ne, V7X_LANES), 0)
            == lax.broadcasted_iota(jnp.int32, (ne, V7X_LANES), 1))
    sub = lax.broadcasted_iota(jnp.int32, (V7X_SUBLANES, V7X_LANES), 0)
    base = carry[...]

    def to_lanes(colvec):
        return jnp.sum(jnp.where(diag, colvec, 0.0), axis=0, keepdims=True)

    for s in range(nsub):
        rs = slice(s * tm, (s + 1) * tm)
        y = _dot(attn_ref[rs, :], woa_ref[...]) + _dot(hm_ref[rs, :], wom_ref[...])
        hval = _layer_norm(DEEPNORM_ALPHA * x_ref[rs, :] + y, g_ref[...], b_ref[...])
        h_ref[rs, :] = hval

        logits = _dot(hval.astype(BF16), wr_ref[...]) + br_ref[...]
        lt = logits.T[:ne, :]

        sel, vals = [], []
        for _ in range(TOP_K):
            mx = jnp.max(lt, axis=0, keepdims=True)
            idx = jnp.min(jnp.where(lt == mx, ex, float(ne)), axis=0, keepdims=True)
            hit = ex == idx
            lt = jnp.where(hit, NEG_INF, lt)
            sel.append(hit)
            vals.append(mx)
        exps = [jnp.exp(v - vals[0]) for v in vals]
        den = exps[0] + exps[1] + exps[2] + exps[3]
        gates = [e / den for e in exps]

        onehot = jnp.zeros((ne, tm), F32)
        for hit in sel:
            onehot = jnp.where(hit, 1.0, onehot)
        rank = _dot(onehot.astype(BF16), earlier)
        cnt = jnp.broadcast_to(jnp.sum(onehot, axis=1, keepdims=True), (ne, V7X_LANES))
        cnt_al = jnp.floor((cnt + (RUN_ALIGN - 1)) * (1.0 / RUN_ALIGN))
        slot = _dot(lower, cnt_al.astype(BF16)) * float(RUN_ALIGN)
        cnt_al = cnt_al * float(RUN_ALIGN)
        where_row = slot[:, 0:1] + rank

        rows = [jnp.sum(jnp.where(sel[k], where_row, 0.0), axis=0, keepdims=True) for k in range(TOP_K)]
        info = jnp.concatenate(rows + gates, axis=0)
        post_ref[s] = info
        col_ref[rs, :] = jnp.concatenate([info, jnp.zeros((V7X_LANES - 2 * TOP_K, tm), F32)], axis=0).T

        cnt_row = to_lanes(cnt_al)
        tab = jnp.zeros((V7X_SUBLANES, V7X_LANES), F32)
        tab = jnp.where(sub == 0, cnt_row, tab)
        tab = jnp.where(sub == 1, base, tab)
        tab = jnp.where(sub == 2, to_lanes(slot), tab)
        tab_ref[s] = tab.astype(jnp.int32)
        base = base + cnt_row

    carry[...] = base
    tot_ref[...] = jnp.broadcast_to(base, (V7X_SUBLANES, V7X_LANES)).astype(jnp.int32)


def _outproj_router(attn2d, hm2d, x2d, wo_a, wo_m, ln_g, ln_b, w_r, b_r):
    t = x2d.shape[0]
    nsub = ROUTER_SUBTILES
    tm = TOK_TILE * nsub
    nt = t // TOK_TILE
    const = lambda i: (0, 0)
    return pl.pallas_call(
        functools.partial(_router_kernel, nsub=nsub),
        grid=(t // tm,),
        in_specs=[
            pl.BlockSpec((tm, D_ATTN), lambda i: (i, 0)),
            pl.BlockSpec((tm, D_MLSTM), lambda i: (i, 0)),
            pl.BlockSpec((tm, D_MODEL), lambda i: (i, 0)),
            pl.BlockSpec((D_ATTN, D_MODEL), const),
            pl.BlockSpec((D_MLSTM, D_MODEL), const),
            pl.BlockSpec((1, D_MODEL), const),
            pl.BlockSpec((1, D_MODEL), const),
            pl.BlockSpec((D_MODEL, V7X_LANES), const),
            pl.BlockSpec((1, V7X_LANES), const),
        ],
        out_specs=[
            pl.BlockSpec((tm, D_MODEL), lambda i: (i, 0)),
            pl.BlockSpec((nsub, V7X_SUBLANES, TOK_TILE), lambda i: (i, 0, 0)),
            pl.BlockSpec((tm, V7X_LANES), lambda i: (i, 0)),
            pl.BlockSpec((nsub, V7X_SUBLANES, V7X_LANES), lambda i: (i, 0, 0)),
            pl.BlockSpec((V7X_SUBLANES, V7X_LANES), const),
        ],
        out_shape=[
            jax.ShapeDtypeStruct((t, D_MODEL), F32),
            jax.ShapeDtypeStruct((nt, V7X_SUBLANES, TOK_TILE), F32),
            jax.ShapeDtypeStruct((t, V7X_LANES), F32),
            jax.ShapeDtypeStruct((nt, V7X_SUBLANES, V7X_LANES), jnp.int32),
            jax.ShapeDtypeStruct((V7X_SUBLANES, V7X_LANES), jnp.int32),
        ],
        scratch_shapes=[pltpu.VMEM((1, V7X_LANES), F32)],
        compiler_params=_compiler_params(("arbitrary",)),
        name="outproj_router",
    )(attn2d, hm2d, x2d, wo_a, wo_m, ln_g, ln_b, w_r, b_r)


def _run_copy(src, dst, start_src, start_dst, n, sem):
    n = pl.multiple_of(n, RUN_ALIGN)
    return pltpu.make_async_copy(
        src.at[pl.ds(pl.multiple_of(start_src, RUN_ALIGN), n)],
        dst.at[pl.ds(pl.multiple_of(start_dst, RUN_ALIGN), n)],
        sem)


def _dispatch_kernel(cnt_sm, slot_sm, dst_sm, tail_sm, post_ref, h_ref, xs_hbm, ybuf, zbuf, sem, zsem):
    i = pl.program_id(0)
    nt = pl.num_programs(0)
    cur = i % 2
    hb = h_ref[...].astype(BF16)
    pos = post_ref[0]
    chunk = 256
    for rc in range(ROWS_LOCAL // chunk):
        rows = (lax.broadcasted_iota(jnp.int32, (chunk, TOK_TILE), 0) + rc * chunk).astype(F32)
        p = jnp.zeros((chunk, TOK_TILE), F32)
        for k in range(TOP_K):
            p = jnp.where(rows == pos[k:k + 1, :], 1.0, p)
        ybuf[cur, rc * chunk:(rc + 1) * chunk, :] = _dot(p.astype(BF16), hb)

    def total_rows(tile):
        return lax.fori_loop(0, N_EXPERTS, lambda e, acc: acc + cnt_sm[tile * N_EXPERTS + e], 0)

    @pl.when(i > 0)
    def _():
        n_prev = total_rows(i - 1)
        _run_copy(ybuf.at[1 - cur], xs_hbm, 0, 0, n_prev, sem).wait()

    def issue(e, carry):
        n = cnt_sm[i * N_EXPERTS + e]

        @pl.when(n > 0)
        def _():
            _run_copy(ybuf.at[cur], xs_hbm, slot_sm[i * N_EXPERTS + e], dst_sm[i * N_EXPERTS + e], n, sem).start()
        return carry

    lax.fori_loop(0, N_EXPERTS, issue, 0)

    @pl.when(i == nt - 1)
    def _():
        n_cur = total_rows(i)
        _run_copy(ybuf.at[cur], xs_hbm, 0, 0, n_cur, sem).wait()
        zbuf[...] = jnp.zeros_like(zbuf)

        def ztail(wait):
            def body(e, carry):
                n = tail_sm[N_EXPERTS + e]

                @pl.when(n > 0)
                def _():
                    cp = _run_copy(zbuf, xs_hbm, 0, tail_sm[e], n, zsem)
                    if wait:
                        cp.wait()
                    else:
                        cp.start()
                return carry
            return body

        lax.fori_loop(0, N_EXPERTS, ztail(False), 0)
        lax.fori_loop(0, N_EXPERTS, ztail(True), 0)

        def zblock(wait):
            def body(j, carry):
                cp = _run_copy(zbuf, xs_hbm, 0, j * ROW_BLOCK, ROW_BLOCK, zsem)
                if wait:
                    cp.wait()
                else:
                    cp.start()
                return carry
            return body

        n_blocks = xs_hbm.shape[0] // ROW_BLOCK
        lax.fori_loop(tail_sm[2 * N_EXPERTS], n_blocks, zblock(False), 0)
        lax.fori_loop(tail_sm[2 * N_EXPERTS], n_blocks, zblock(True), 0)


def _dispatch(cnt, slot, dst, tail, post, h2d, n_rows):
    t = h2d.shape[0]
    nt = t // TOK_TILE
    return pl.pallas_call(
        _dispatch_kernel,
        grid_spec=pltpu.PrefetchScalarGridSpec(
            num_scalar_prefetch=4,
            grid=(nt,),
            in_specs=[
                pl.BlockSpec((1, V7X_SUBLANES, TOK_TILE), lambda i, *_: (i, 0, 0)),
                pl.BlockSpec((TOK_TILE, D_MODEL), lambda i, *_: (i, 0)),
            ],
            out_specs=pl.BlockSpec(memory_space=pl.ANY),
            scratch_shapes=[
                pltpu.VMEM((2, ROWS_LOCAL, D_MODEL), F32),
                pltpu.VMEM((ROW_BLOCK, D_MODEL), F32),
                pltpu.SemaphoreType.DMA(()),
                pltpu.SemaphoreType.DMA(()),
            ],
        ),
        out_shape=jax.ShapeDtypeStruct((n_rows, D_MODEL), F32),
        compiler_params=_compiler_params(("arbitrary",)),
        name="moe_dispatch",
    )(cnt, slot, dst, tail, post, h2d)


def _expert_kernel(bexp_sm, nused_sm, xs_ref, wg_ref, bg_ref, wu_ref, bu_ref, wd_ref, bd_ref, ys_ref,
                   hbuf, wg_b, wu_b, wd_b):
    j = pl.program_id(0)
    used = j < nused_sm[0]
    new_expert = jnp.logical_or(j == 0, bexp_sm[j] != bexp_sm[jnp.maximum(j, 1) - 1])

    @pl.when(jnp.logical_and(used, new_expert))
    def _():
        def cast(i, carry):
            sl = pl.ds(pl.multiple_of(i * 128, 128), 128)
            wg_b[sl, :] = wg_ref[0, sl, :].astype(BF16)
            wu_b[sl, :] = wu_ref[0, sl, :].astype(BF16)
            wd_b[sl, :] = wd_ref[0, sl, :].astype(BF16)
            return carry
        lax.fori_loop(0, D_MODEL // 128, cast, 0)

    @pl.when(used)
    def _():
        xb = xs_ref[...].astype(BF16)
        step = 512
        for n in range(0, D_MODEL, step):
            g = _dot(xb, wg_b[:, n:n + step]) + bg_ref[0, :, n:n + step]
            u = _dot(xb, wu_b[:, n:n + step]) + bu_ref[0, :, n:n + step]
            g = jnp.minimum(g, SWIGLU_LIMIT)
            u = jnp.clip(u, -SWIGLU_LIMIT, SWIGLU_LIMIT)
            hbuf[:, n:n + step] = (g * jax.nn.sigmoid(SWIGLU_ALPHA * g) * (u + 1.0)).astype(BF16)
        ys_ref[...] = _dot(hbuf[...], wd_b[...]) + bd_ref[0]

    @pl.when(jnp.logical_not(used))
    def _():
        ys_ref[...] = jnp.zeros_like(ys_ref)


def _experts(bexp, nused, xs, wg, bg, wu, bu, wd, bd):
    n_rows = xs.shape[0]
    nb = n_rows // ROW_BLOCK

    def row_map(j, be, nu):
        return (jnp.minimum(j, nu[0] - 1), 0)

    def w_map(j, be, nu):
        return (be[jnp.minimum(j, nu[0] - 1)], 0, 0)

    wspec = pl.BlockSpec((1, D_MODEL, D_MODEL), w_map)
    bspec = pl.BlockSpec((1, 1, D_MODEL), w_map)
    return pl.pallas_call(
        _expert_kernel,
        grid_spec=pltpu.PrefetchScalarGridSpec(
            num_scalar_prefetch=2,
            grid=(nb,),
            in_specs=[pl.BlockSpec((ROW_BLOCK, D_MODEL), row_map), wspec, bspec, wspec, bspec, wspec, bspec],
            out_specs=pl.BlockSpec((ROW_BLOCK, D_MODEL), lambda j, be, nu: (j, 0)),
            scratch_shapes=[pltpu.VMEM((ROW_BLOCK, D_MODEL), BF16)]
            + [pltpu.VMEM((D_MODEL, D_MODEL), BF16) for _ in range(3)],
        ),
        out_shape=jax.ShapeDtypeStruct((n_rows, D_MODEL), F32),
        compiler_params=_compiler_params(("arbitrary",)),
        name="moe_experts",
    )(bexp, nused, xs, wg, bg, wu, bu, wd, bd)


def _combine_kernel(cnt_sm, slot_sm, dst_sm, col_ref, h_ref, g_ref, b_ref, ys_hbm, o_ref, ybuf, sbuf, sem):
    i = pl.program_id(0)
    nt = pl.num_programs(0)
    cur = i % 2

    def fetch(tile, buf, wait):
        def body(e, carry):
            n = cnt_sm[tile * N_EXPERTS + e]

            @pl.when(n > 0)
            def _():
                cp = _run_copy(ys_hbm, ybuf.at[buf], dst_sm[tile * N_EXPERTS + e], slot_sm[tile * N_EXPERTS + e],
                               n, sem.at[buf])
                if wait:
                    cp.wait()
                else:
                    cp.start()
            return carry
        lax.fori_loop(0, N_EXPERTS, body, 0)

    @pl.when(i == 0)
    def _():
        ybuf[...] = jnp.zeros_like(ybuf)
        fetch(0, 0, False)

    @pl.when(i + 1 < nt)
    def _():
        fetch(i + 1, 1 - cur, False)

    fetch(i, cur, True)

    col = col_ref[...]
    chunk = 256
    for rc in range(ROWS_LOCAL // chunk):
        rows = (lax.broadcasted_iota(jnp.int32, (TOK_TILE, chunk), 1) + rc * chunk).astype(F32)
        s = jnp.zeros((TOK_TILE, chunk), F32)
        for k in range(TOP_K):
            s = jnp.where(rows == col[:, k:k + 1], col[:, TOP_K + k:TOP_K + k + 1], s)
        sbuf[:, rc * chunk:(rc + 1) * chunk] = s.astype(BF16)
    for n in range(0, D_MODEL, chunk):
        moe = _dot(sbuf[...], ybuf[cur, :, n:n + chunk].astype(BF16))
        o_ref[:, n:n + chunk] = DEEPNORM_ALPHA * h_ref[:, n:n + chunk] + moe
    o_ref[...] = _layer_norm(o_ref[...], g_ref[...], b_ref[...])


def _combine(cnt, slot, dst, col, h2d, ln_g, ln_b, ys):
    t = h2d.shape[0]
    nt = t // TOK_TILE
    return pl.pallas_call(
        _combine_kernel,
        grid_spec=pltpu.PrefetchScalarGridSpec(
            num_scalar_prefetch=3,
            grid=(nt,),
            in_specs=[
                pl.BlockSpec((TOK_TILE, V7X_LANES), lambda i, *_: (i, 0)),
                pl.BlockSpec((TOK_TILE, D_MODEL), lambda i, *_: (i, 0)),
                pl.BlockSpec((1, D_MODEL), lambda i, *_: (0, 0)),
                pl.BlockSpec((1, D_MODEL), lambda i, *_: (0, 0)),
                pl.BlockSpec(memory_space=pl.ANY),
            ],
            out_specs=pl.BlockSpec((TOK_TILE, D_MODEL), lambda i, *_: (i, 0)),
            scratch_shapes=[
                pltpu.VMEM((2, ROWS_LOCAL, D_MODEL), F32),
                pltpu.VMEM((TOK_TILE, ROWS_LOCAL), BF16),
                pltpu.SemaphoreType.DMA((2,)),
            ],
        ),
        out_shape=jax.ShapeDtypeStruct((t, D_MODEL), F32),
        compiler_params=_compiler_params(("arbitrary",)),
        name="moe_combine",
    )(cnt, slot, dst, col, h2d, ln_g, ln_b, ys)


def _pad_lanes(a, width=V7X_LANES):
    return jnp.pad(a, ((0, 0), (0, width - a.shape[1])))


def _layer(h3d, w_in, conv_w, conv_b, w_mq, w_mk, b_igate, b_fgate, mnorm_g, w_out,
           ln1_g, ln1_b, w_router, b_router, w_gate, b_gate, w_up, b_up, w_down, b_down, ln2_g, ln2_b):
    batch, seq, _ = h3d.shape
    t = batch * seq
    x2d = h3d.reshape(t, D_MODEL)

    w_main = w_in[:, :PROJ_MAIN].astype(BF16)
    w_gates = _pad_lanes(w_in[:, PROJ_MAIN:]).astype(BF16)
    proj, gcol, grow = _inproj(x2d, w_main, w_gates, batch, seq)
    proj3d = proj.reshape(batch, seq, PROJ_MAIN)
    attn = _attention(proj3d)
    gate_bias = jnp.concatenate([b_igate, b_fgate]).astype(F32)
    bias_col = _pad_lanes(gate_bias[None, :])
    bias_row = jnp.broadcast_to(gate_bias[:, None], (V7X_SUBLANES, seq))
    hm = _mlstm(proj3d, gcol.reshape(batch, seq, V7X_LANES), grow, conv_w, conv_b[None, :],
                w_mq.astype(BF16), w_mk.astype(BF16), bias_col, bias_row, mnorm_g[None, :])

    wo = w_out.astype(BF16)
    h2d, post, col, tab, tot = _outproj_router(
        attn.reshape(t, D_ATTN), hm.reshape(t, D_MLSTM), x2d, wo[:D_ATTN], wo[D_ATTN:],
        ln1_g[None, :], ln1_b[None, :], _pad_lanes(w_router).astype(BF16), _pad_lanes(b_router[None, :]))

    nt = t // TOK_TILE
    total = tot[0, :N_EXPERTS]
    region = (total + ROW_BLOCK - 1) // ROW_BLOCK * ROW_BLOCK
    region_end = jnp.cumsum(region)
    region_start = region_end - region
    n_rows = (t * TOP_K + nt * N_EXPERTS * (RUN_ALIGN - 1)) // ROW_BLOCK * ROW_BLOCK + (N_EXPERTS + 1) * ROW_BLOCK
    nb = n_rows // ROW_BLOCK
    cnt = tab[:, 0, :N_EXPERTS].reshape(-1)
    slot = tab[:, 2, :N_EXPERTS].reshape(-1)
    dst = (tab[:, 1, :N_EXPERTS] + region_start[None, :]).reshape(-1)
    tail = jnp.concatenate([region_start + total, region - total, region_end[-1:] // ROW_BLOCK]).astype(jnp.int32)
    block_row = jnp.arange(nb, dtype=jnp.int32) * ROW_BLOCK
    bexp = jnp.minimum(jnp.sum(region_end[None, :] <= block_row[:, None], axis=1), N_EXPERTS - 1).astype(jnp.int32)
    nused = (region_end[-1:] // ROW_BLOCK).astype(jnp.int32)

    xs = _dispatch(cnt, slot, dst, tail, post, h2d, n_rows)
    ys = _experts(bexp, nused, xs, w_gate, b_gate[:, None, :], w_up, b_up[:, None, :], w_down, b_down[:, None, :])
    out = _combine(cnt, slot, dst, col, h2d, ln2_g[None, :], ln2_b[None, :], ys)
    return out.reshape(batch, seq, D_MODEL)


def kernel(x, w_in, conv_w, conv_b, w_mq, w_mk, b_igate, b_fgate, mnorm_g, w_out, ln1_g, ln1_b, w_router, b_router, w_gate, b_gate, w_up, b_up, w_down, b_down, ln2_g, ln2_b):
    h = x
    for l in range(w_in.shape[0]):
        h = _layer(h, w_in[l], conv_w[l], conv_b[l], w_mq[l], w_mk[l], b_igate[l], b_fgate[l], mnorm_g[l],
                   w_out[l], ln1_g[l], ln1_b[l], w_router[l], b_router[l], w_gate[l], b_gate[l], w_up[l],
                   b_up[l], w_down[l], b_down[l], ln2_g[l], ln2_b[l])
    return h
```

```python
import functools
import math

import jax
import jax.numpy as jnp
from jax import lax
from jax.experimental import pallas as pl
from jax.experimental.pallas import tpu as pltpu

F32 = jnp.float32
BF16 = jnp.bfloat16
NEG_INF = float("-inf")

V7X_LANES = 128
V7X_SUBLANES = 8
V7X_VMEM_LIMIT_BYTES = 56 * 1024 * 1024

D_MODEL = 1024
D_ATTN = 512
HEAD_DIM_A = 64
D_MLSTM = 512
N_HEADS_M = 4
HEAD_DIM_M = 128
CONV_WIDTH = 4
CHUNK = 128
DILATED_BRANCHES = ((128, 1), (512, 4), (2048, 16))
N_EXPERTS = 32
TOP_K = 4
SWIGLU_LIMIT = 7.0
SWIGLU_ALPHA = 1.702
DEEPNORM_ALPHA = 2.0 ** 0.25
LN_EPS = 1e-5
RMS_EPS = 1e-6

PROJ_MAIN = 3 * D_ATTN + 3 * D_MLSTM
TOK_TILE = 256
RUN_ALIGN = V7X_SUBLANES
ROWS_LOCAL = TOK_TILE * TOP_K + N_EXPERTS * RUN_ALIGN
ROW_BLOCK = 512
ROUTER_SUBTILES = 4


def _dot(a, b):
    return jnp.dot(a, b, preferred_element_type=F32)


def _dot_nt(a, b):
    return lax.dot_general(a, b, (((1,), (1,)), ((), ())), preferred_element_type=F32)


def _split3(x):
    hi = x.astype(BF16)
    r1 = x - hi.astype(F32)
    mid = r1.astype(BF16)
    lo = (r1 - mid.astype(F32)).astype(BF16)
    return hi, mid, lo


def _compiler_params(sem):
    return pltpu.CompilerParams(dimension_semantics=sem, vmem_limit_bytes=V7X_VMEM_LIMIT_BYTES)


def _inproj_kernel(x_ref, w_ref, wg_ref, proj_ref, gcol_ref, grow_ref):
    xb = x_ref[...].astype(BF16)
    step = 512
    for n in range(0, PROJ_MAIN, step):
        proj_ref[:, n:n + step] = _dot(xb, w_ref[:, n:n + step])
    g = _dot(xb, wg_ref[...])
    gcol_ref[...] = g
    grow_ref[0] = g.T[:V7X_SUBLANES, :]


def _inproj(x2d, w_main, w_gates, batch, seq):
    t = x2d.shape[0]
    tm = 512
    per_b = seq // tm
    return pl.pallas_call(
        _inproj_kernel,
        grid=(t // tm,),
        in_specs=[
            pl.BlockSpec((tm, D_MODEL), lambda i: (i, 0)),
            pl.BlockSpec((D_MODEL, PROJ_MAIN), lambda i: (0, 0)),
            pl.BlockSpec((D_MODEL, V7X_LANES), lambda i: (0, 0)),
        ],
        out_specs=[
            pl.BlockSpec((tm, PROJ_MAIN), lambda i: (i, 0)),
            pl.BlockSpec((tm, V7X_LANES), lambda i: (i, 0)),
            pl.BlockSpec((1, V7X_SUBLANES, tm), lambda i: (i // per_b, 0, i % per_b)),
        ],
        out_shape=[
            jax.ShapeDtypeStruct((t, PROJ_MAIN), F32),
            jax.ShapeDtypeStruct((t, V7X_LANES), F32),
            jax.ShapeDtypeStruct((batch, V7X_SUBLANES, seq), F32),
        ],
        compiler_params=_compiler_params(("arbitrary",)),
        name="inproj",
    )(x2d, w_main, w_gates)


ATTN_GROUP = 8


def _attn_kernel(q_ref, k_ref, v_ref, o_ref, q0s, q1s, bias_b, bias_f, sg, pg,
                 o0, o1, o2, l0, l1, l2, m0, m1, m2, x0, x1, x2, *, seq):
    obufs, lbufs, mbufs, xbufs = (o0, o1, o2), (l0, l1, l2), (m0, m1, m2), (x0, x1, x2)
    two = 2 * CHUNK
    head0 = lax.broadcasted_iota(jnp.int32, (CHUNK, V7X_LANES), 1) < HEAD_DIM_A
    qscale = HEAD_DIM_A ** -0.5 * math.log2(math.e)

    def prep(i, carry):
        sl = pl.ds(pl.multiple_of(i * two, two), two)
        h0 = lax.broadcasted_iota(jnp.int32, (two, V7X_LANES), 1) < HEAD_DIM_A
        q = q_ref[0, sl, :] * qscale
        q0s[sl, :] = jnp.where(h0, q, 0.0)
        q1s[sl, :] = jnp.where(h0, 0.0, q)
        return carry

    lax.fori_loop(0, seq // two, prep, 0)

    qi = lax.broadcasted_iota(jnp.int32, (two, two), 0) % CHUNK
    kj = lax.broadcasted_iota(jnp.int32, (two, two), 1)
    bias_b[...] = jnp.where((kj >= qi) & (kj <= qi + CHUNK), 0.0, NEG_INF)
    qf = lax.broadcasted_iota(jnp.int32, (two, CHUNK), 0) % CHUNK
    kf = lax.broadcasted_iota(jnp.int32, (two, CHUNK), 1)
    bias_f[...] = jnp.where(kf <= qf, 0.0, NEG_INF)

    def run_group(c, dil, starts, has_prev):
        assert len(starts) <= ATTN_GROUP
        nk = two if has_prev else CHUNK
        bias_ref = bias_b if has_prev else bias_f

        def rows(s0):
            return pl.ds(s0, CHUNK) if dil == 1 else pl.ds(s0, CHUNK, stride=dil)

        def keys(ref, st):
            if has_prev:
                return jnp.concatenate([ref[0, rows(st - dil * CHUNK), :], ref[0, rows(st), :]], axis=0).astype(BF16)
            return ref[0, rows(st), :].astype(BF16)

        def both_heads(x):
            return jnp.where(head0, x[:CHUNK], x[CHUNK:])

        for j, st in enumerate(starts):
            q2 = jnp.concatenate([q0s[rows(st), :], q1s[rows(st), :]], axis=0).astype(BF16)
            sg[j, :, 0:nk] = _dot_nt(q2, keys(k_ref, st)) + bias_ref[...]
        for j, st in enumerate(starts):
            s = sg[j, :, 0:nk]
            m = jnp.max(s, axis=1, keepdims=True)
            pg[j, :, 0:nk] = jnp.exp2(s - m).astype(BF16)
            ma = jnp.broadcast_to(m[:CHUNK], (CHUNK, V7X_LANES))
            mb = jnp.broadcast_to(m[CHUNK:], (CHUNK, V7X_LANES))
            mbufs[c][rows(st), :] = jnp.where(head0, ma, mb)
            xbufs[c][rows(st), :] = jnp.where(head0, mb, ma)
        khead0 = lax.broadcasted_iota(jnp.int32, (nk, V7X_LANES), 1) < HEAD_DIM_A
        for j, st in enumerate(starts):
            if has_prev:
                vv = jnp.concatenate([v_ref[0, rows(st - dil * CHUNK), :], v_ref[0, rows(st), :]], axis=0)
            else:
                vv = v_ref[0, rows(st), :]
            oa = _dot(pg[j, 0:CHUNK, 0:nk], jnp.where(khead0, vv, 1.0).astype(BF16))
            ob = _dot(pg[j, CHUNK:two, 0:nk], jnp.where(khead0, 1.0, vv).astype(BF16))
            obufs[c][rows(st), :] = jnp.where(head0, oa, ob)
            lbufs[c][rows(st), :] = jnp.where(head0, ob, oa)

    for c, (window, dil) in enumerate(DILATED_BRANCHES):
        assert window // dil == CHUNK
        nb = seq // (dil * CHUNK)
        span = dil * CHUNK
        if dil == 1:
            run_group(c, dil, [0], False)
            per = 5
            assert (nb - 1) % per == 0

            def band1(g, carry, c=c, dil=dil, per=per, span=span):
                base = span + g * (per * span)
                run_group(c, dil, [base + span * j for j in range(per)], True)
                return carry

            lax.fori_loop(0, (nb - 1) // per, band1, 0)
        elif nb > 1:
            run_group(c, dil, list(range(dil)), False)

            def band2(g, carry, c=c, dil=dil, nb=nb, span=span):
                starts = [2 * g + rr + span * b for rr in range(2) for b in range(1, nb)]
                run_group(c, dil, starts, True)
                return carry

            lax.fori_loop(0, dil // 2, band2, 0)
        else:
            def firsts(g, carry, c=c, dil=dil):
                run_group(c, dil, [ATTN_GROUP * g + j for j in range(ATTN_GROUP)], False)
                return carry

            lax.fori_loop(0, dil // ATTN_GROUP, firsts, 0)

    def combine(i, carry):
        sl = pl.ds(pl.multiple_of(i * 256, 256), 256)
        ma, mb, mc = m0[sl, :], m1[sl, :], m2[sl, :]
        mx = jnp.maximum(jnp.maximum(ma, mb), mc)
        wa, wb, wc = jnp.exp2(ma - mx), jnp.exp2(mb - mx), jnp.exp2(mc - mx)
        xa, xb, xc = x0[sl, :], x1[sl, :], x2[sl, :]
        xm = jnp.maximum(jnp.maximum(xa, xb), xc)
        den = (jnp.exp2(xa - xm) * l0[sl, :] + jnp.exp2(xb - xm) * l1[sl, :] + jnp.exp2(xc - xm) * l2[sl, :])
        den = pltpu.roll(den, HEAD_DIM_A, axis=1)
        out = (wa * o0[sl, :] + wb * o1[sl, :] + wc * o2[sl, :]) / den
        o_ref[0, sl, :] = out.astype(o_ref.dtype)
        return carry

    lax.fori_loop(0, seq // 256, combine, 0)


def _attention(proj3d):
    batch, seq, _ = proj3d.shape
    nblk = D_ATTN // V7X_LANES
    blk = (1, seq, V7X_LANES)
    scratch = ([pltpu.VMEM((seq, V7X_LANES), F32), pltpu.VMEM((seq, V7X_LANES), F32),
                pltpu.VMEM((2 * CHUNK, 2 * CHUNK), F32), pltpu.VMEM((2 * CHUNK, CHUNK), F32),
                pltpu.VMEM((ATTN_GROUP, 2 * CHUNK, 2 * CHUNK), F32),
                pltpu.VMEM((ATTN_GROUP, 2 * CHUNK, 2 * CHUNK), BF16)]
               + [pltpu.VMEM((seq, V7X_LANES), F32) for _ in range(12)])
    return pl.pallas_call(
        functools.partial(_attn_kernel, seq=seq),
        grid=(batch, nblk),
        in_specs=[
            pl.BlockSpec(blk, lambda b, g: (b, 0, g)),
            pl.BlockSpec(blk, lambda b, g: (b, 0, nblk + g)),
            pl.BlockSpec(blk, lambda b, g: (b, 0, 2 * nblk + g)),
        ],
        out_specs=pl.BlockSpec(blk, lambda b, g: (b, 0, g)),
        out_shape=jax.ShapeDtypeStruct((batch, seq, D_ATTN), BF16),
        scratch_shapes=scratch,
        compiler_params=_compiler_params(("arbitrary", "arbitrary")),
        name="dilated_attention",
    )(proj3d, proj3d, proj3d)


def _log_sigmoid(x):
    return jnp.minimum(x, 0.0) - jnp.log1p(jnp.exp(-jnp.abs(x)))


MLSTM_HEADS_PER_STEP = 2


def _mlstm_kernel(xm_ref, vm_ref, om_ref, gcol_ref, grow_ref, cw_ref, cb_ref, wq_ref, wk_ref,
                  bcol_ref, brow_ref, mg_ref, o_ref, *scratch, seq):
    nh = MLSTM_HEADS_PER_STEP
    per_head = len(scratch) // nh
    nchunk = seq // CHUNK
    ri = lax.broadcasted_iota(jnp.int32, (CHUNK, CHUNK), 0)
    ci = lax.broadcasted_iota(jnp.int32, (CHUNK, CHUNK), 1)
    tril = ri >= ci
    tril_b = jnp.where(tril, 1.0, 0.0).astype(BF16)
    triu_b = jnp.where(ri <= ci, 1.0, 0.0).astype(BF16)
    pad = V7X_SUBLANES
    qscale = HEAD_DIM_M ** -0.5

    class Head:
        def __init__(self, hh):
            (self.xpad, self.q_s, self.k_s, self.icol_s, self.lfcol_s, self.gcol_s,
             self.ri_s, self.rf_s, self.gf_s, self.mi_s, self.mo_s, self.gl_s) = scratch[hh * per_head:(hh + 1) * per_head]
            self.h = pl.program_id(1) * nh + hh
            self.cols = slice(hh * V7X_LANES, (hh + 1) * V7X_LANES)
            self.wq = wq_ref[hh]
            self.wk = wk_ref[hh]
            self.mg = mg_ref[:, self.cols]

    heads = [Head(hh) for hh in range(nh)]

    for hd in heads:
        hd.xpad[0:pad, :] = jnp.zeros((pad, V7X_LANES), F32)
        hd.xpad[pad:pad + seq, :] = xm_ref[0, :, hd.cols]

    def conv_step(i, carry):
        base = pl.multiple_of(i * 256, 256)
        rows = pl.ds(base, 256)
        gc = gcol_ref[0, rows, :] + bcol_ref[...]
        lane256 = lax.broadcasted_iota(jnp.int32, (256, V7X_LANES), 1)
        for hd in heads:
            y = jnp.broadcast_to(cb_ref[:, hd.cols], (256, V7X_LANES))
            xw = hd.xpad[pl.ds(base, 256 + pad), :]
            for j in range(CONV_WIDTH):
                off = pad - (CONV_WIDTH - 1) + j
                y = y + cw_ref[j:j + 1, hd.cols] * xw[off:off + 256, :]
            xc = (y * jax.nn.sigmoid(y)).astype(BF16)
            hd.q_s[rows, :] = _dot(xc, hd.wq) * qscale
            hd.k_s[rows, :] = _dot(xc, hd.wk)
            ic = jnp.sum(jnp.where(lane256 == hd.h, gc, 0.0), axis=1, keepdims=True)
            fc = jnp.sum(jnp.where(lane256 == hd.h + N_HEADS_M, gc, 0.0), axis=1, keepdims=True)
            hd.icol_s[rows, :] = jnp.broadcast_to(ic, (256, V7X_LANES))
            hd.lfcol_s[rows, :] = jnp.broadcast_to(_log_sigmoid(fc), (256, V7X_LANES))
        return carry

    lax.fori_loop(0, seq // 256, conv_step, 0)

    gr = grow_ref[0] + brow_ref[...]
    sub = lax.broadcasted_iota(jnp.int32, (V7X_SUBLANES, seq), 0)
    for hd in heads:
        irow = jnp.sum(jnp.where(sub == hd.h, gr, 0.0), axis=0, keepdims=True)
        lfrow = _log_sigmoid(jnp.sum(jnp.where(sub == hd.h + N_HEADS_M, gr, 0.0), axis=0, keepdims=True))
        for c in range(nchunk):
            hd.ri_s[c:c + 1, :] = irow[:, c * CHUNK:(c + 1) * CHUNK]
            hd.rf_s[c:c + 1, :] = lfrow[:, c * CHUNK:(c + 1) * CHUNK]

    def cum_step(c, carry):
        sl = pl.ds(pl.multiple_of(c * CHUNK, CHUNK), CHUNK)
        for hd in heads:
            hd.gcol_s[sl, :] = sum(_dot(tril_b, part) for part in _split3(hd.lfcol_s[sl, :]))
        return carry

    lax.fori_loop(0, nchunk, cum_step, 0, unroll=2)

    crow = lax.broadcasted_iota(jnp.int32, (nchunk, V7X_LANES), 0)
    for hd in heads:
        irows = hd.ri_s[...]
        grows = sum(_dot(part, triu_b) for part in _split3(hd.rf_s[...]))
        hd.gf_s[...] = grows
        g_last = grows[:, CHUNK - 1:CHUNK]
        a_max = jnp.max(g_last - grows + irows, axis=1, keepdims=True)
        m = jnp.zeros((1, 1), F32)
        m_in = jnp.zeros((nchunk, V7X_LANES), F32)
        m_out = jnp.zeros((nchunk, V7X_LANES), F32)
        for c in range(nchunk):
            m_in = jnp.where(crow == c, m, m_in)
            m = jnp.maximum(g_last[c:c + 1, :] + m, a_max[c:c + 1, :])
            m_out = jnp.where(crow == c, m, m_out)
        hd.mi_s[...] = m_in
        hd.mo_s[...] = m_out
        hd.gl_s[...] = jnp.broadcast_to(g_last, (nchunk, V7X_LANES))

    def chunk_head(hd, c, cmat, nrow):
        sl = pl.ds(pl.multiple_of(c * CHUNK, CHUNK), CHUNK)
        one = pl.ds(c, 1)
        qc = hd.q_s[sl, :]
        kc = hd.k_s[sl, :]
        vc = vm_ref[0, sl, hd.cols].astype(BF16)
        icol = hd.icol_s[sl, :]
        gcol = hd.gcol_s[sl, :]
        irow_c = hd.ri_s[one, :]
        grow_c = hd.gf_s[one, :]
        m = hd.mi_s[one, :]
        m_new = hd.mo_s[one, :]
        g_last = hd.gl_s[one, :]

        d = gcol - grow_c + irow_c
        d = jnp.where(tril, d, NEG_INF)
        inter = gcol + m
        m_t = jnp.maximum(inter, jnp.max(d, axis=1, keepdims=True))
        w_intra = jnp.exp(d - m_t)
        w_inter = jnp.exp(inter - m_t)
        qb = qc.astype(BF16)
        kb = kc.astype(BF16)
        qk = _dot_nt(qb, kb) * w_intra
        num = _dot(qk.astype(BF16), vc) + w_inter * _dot(qb, cmat.astype(BF16))
        den = jnp.sum(qk, axis=1, keepdims=True) + w_inter * jnp.sum(qc * nrow, axis=1, keepdims=True)
        hc = num / jnp.maximum(jnp.abs(den), jnp.exp(-m_t))

        a_col = g_last - gcol + icol
        decay = jnp.exp(g_last + m - m_new)
        wk_col = jnp.exp(a_col - m_new)
        kw = wk_col * kc
        c_new = decay * cmat + _dot(kw.T.astype(BF16), vc)
        n_new = decay * nrow + jnp.sum(kw, axis=0, keepdims=True)

        hn = hc * lax.rsqrt(jnp.mean(hc * hc, axis=1, keepdims=True) + RMS_EPS) * hd.mg
        hn = jax.nn.sigmoid(om_ref[0, sl, hd.cols]) * hn
        o_ref[0, sl, hd.cols] = hn.astype(o_ref.dtype)
        return c_new, n_new

    def chunk_step(c, carry):
        return tuple(chunk_head(hd, c, *carry[i]) for i, hd in enumerate(heads))

    init = tuple((jnp.zeros((HEAD_DIM_M, HEAD_DIM_M), F32), jnp.zeros((1, HEAD_DIM_M), F32)) for _ in heads)
    lax.fori_loop(0, nchunk, chunk_step, init, unroll=2)


def _mlstm(proj3d, gcol3d, grow3d, conv_w, conv_b, wq, wk, bias_col, bias_row, mnorm):
    batch, seq, _ = proj3d.shape
    nh = MLSTM_HEADS_PER_STEP
    width = nh * V7X_LANES
    blk = (1, seq, width)
    a0 = 3 * D_ATTN // width
    ng = N_HEADS_M // nh
    per_head = ([pltpu.VMEM((seq + V7X_SUBLANES, V7X_LANES), F32)]
                + [pltpu.VMEM((seq, V7X_LANES), F32) for _ in range(5)]
                + [pltpu.VMEM((seq // CHUNK, V7X_LANES), F32) for _ in range(6)])
    return pl.pallas_call(
        functools.partial(_mlstm_kernel, seq=seq),
        grid=(batch, ng),
        in_specs=[
            pl.BlockSpec(blk, lambda b, g: (b, 0, a0 + g)),
            pl.BlockSpec(blk, lambda b, g: (b, 0, a0 + ng + g)),
            pl.BlockSpec(blk, lambda b, g: (b, 0, a0 + 2 * ng + g)),
            pl.BlockSpec((1, seq, V7X_LANES), lambda b, g: (b, 0, 0)),
            pl.BlockSpec((1, V7X_SUBLANES, seq), lambda b, g: (b, 0, 0)),
            pl.BlockSpec((CONV_WIDTH, width), lambda b, g: (0, g)),
            pl.BlockSpec((1, width), lambda b, g: (0, g)),
            pl.BlockSpec((nh, HEAD_DIM_M, HEAD_DIM_M), lambda b, g: (g, 0, 0)),
            pl.BlockSpec((nh, HEAD_DIM_M, HEAD_DIM_M), lambda b, g: (g, 0, 0)),
            pl.BlockSpec((1, V7X_LANES), lambda b, g: (0, 0)),
            pl.BlockSpec((V7X_SUBLANES, seq), lambda b, g: (0, 0)),
            pl.BlockSpec((1, width), lambda b, g: (0, g)),
        ],
        out_specs=pl.BlockSpec(blk, lambda b, g: (b, 0, g)),
        out_shape=jax.ShapeDtypeStruct((batch, seq, D_MLSTM), BF16),
        scratch_shapes=per_head * nh,
        compiler_params=_compiler_params(("arbitrary", "arbitrary")),
        name="mlstm",
    )(proj3d, proj3d, proj3d, gcol3d, grow3d, conv_w, conv_b, wq, wk, bias_col, bias_row, mnorm)


def _layer_norm(z, g, b):
    mu = jnp.mean(z, axis=1, keepdims=True)
    zc = z - mu
    var = jnp.mean(zc * zc, axis=1, keepdims=True)
    return zc * lax.rsqrt(var + LN_EPS) * g + b


def _router_kernel(attn_ref, hm_ref, x_ref, woa_ref, wom_ref, g_ref, b_ref, wr_ref, br_ref,
                   h_ref, post_ref, col_ref, tab_ref, tot_ref, carry, *, nsub):
    i = pl.program_id(0)
    tm = TOK_TILE

    @pl.when(i == 0)
    def _():
        carry[...] = jnp.zeros_like(carry)

    ne = N_EXPERTS
    ex = lax.broadcasted_iota(jnp.int32, (ne, tm), 0).astype(F32)
    ri = lax.broadcasted_iota(jnp.int32, (tm, tm), 0)
    ci = lax.broadcasted_iota(jnp.int32, (tm, tm), 1)
    earlier = jnp.where(ri < ci, 1.0, 0.0).astype(BF16)
    er = lax.broadcasted_iota(jnp.int32, (ne, ne), 0)
    ec = lax.broadcasted_iota(jnp.int32, (ne, ne), 1)
    lower = jnp.where(ec < er, 1.0, 0.0).astype(BF16)
    diag = (lax.broadcasted_iota(jnp.int32, (ne, V7X_LANES), 0)
            == lax.broadcasted_iota(jnp.int32, (ne, V7X_LANES), 1))
    sub = lax.broadcasted_iota(jnp.int32, (V7X_SUBLANES, V7X_LANES), 0)
    base = carry[...]

    def to_lanes(colvec):
        return jnp.sum(jnp.where(diag, colvec, 0.0), axis=0, keepdims=True)

    for s in range(nsub):
        rs = slice(s * tm, (s + 1) * tm)
        y = _dot(attn_ref[rs, :], woa_ref[...]) + _dot(hm_ref[rs, :], wom_ref[...])
        hval = _layer_norm(DEEPNORM_ALPHA * x_ref[rs, :] + y, g_ref[...], b_ref[...])
        h_ref[rs, :] = hval

        logits = _dot(hval.astype(BF16), wr_ref[...]) + br_ref[...]
        lt = logits.T[:ne, :]

        sel, vals = [], []
        for _ in range(TOP_K):
            mx = jnp.max(lt, axis=0, keepdims=True)
            idx = jnp.min(jnp.where(lt == mx, ex, float(ne)), axis=0, keepdims=True)
            hit = ex == idx
            lt = jnp.where(hit, NEG_INF, lt)
            sel.append(hit)
            vals.append(mx)
        exps = [jnp.exp(v - vals[0]) for v in vals]
        den = exps[0] + exps[1] + exps[2] + exps[3]
        gates = [e / den for e in exps]

        onehot = jnp.zeros((ne, tm), F32)
        for hit in sel:
            onehot = jnp.where(hit, 1.0, onehot)
        rank = _dot(onehot.astype(BF16), earlier)
        cnt = jnp.broadcast_to(jnp.sum(onehot, axis=1, keepdims=True), (ne, V7X_LANES))
        cnt_al = jnp.floor((cnt + (RUN_ALIGN - 1)) * (1.0 / RUN_ALIGN))
        slot = _dot(lower, cnt_al.astype(BF16)) * float(RUN_ALIGN)
        cnt_al = cnt_al * float(RUN_ALIGN)
        where_row = slot[:, 0:1] + rank

        rows = [jnp.sum(jnp.where(sel[k], where_row, 0.0), axis=0, keepdims=True) for k in range(TOP_K)]
        info = jnp.concatenate(rows + gates, axis=0)
        post_ref[s] = info
        col_ref[rs, :] = jnp.concatenate([info, jnp.zeros((V7X_LANES - 2 * TOP_K, tm), F32)], axis=0).T

        cnt_row = to_lanes(cnt_al)
        tab = jnp.zeros((V7X_SUBLANES, V7X_LANES), F32)
        tab = jnp.where(sub == 0, cnt_row, tab)
        tab = jnp.where(sub == 1, base, tab)
        tab = jnp.where(sub == 2, to_lanes(slot), tab)
        tab_ref[s] = tab.astype(jnp.int32)
        base = base + cnt_row

    carry[...] = base
    tot_ref[...] = jnp.broadcast_to(base, (V7X_SUBLANES, V7X_LANES)).astype(jnp.int32)


def _outproj_router(attn2d, hm2d, x2d, wo_a, wo_m, ln_g, ln_b, w_r, b_r):
    t = x2d.shape[0]
    nsub = ROUTER_SUBTILES
    tm = TOK_TILE * nsub
    nt = t // TOK_TILE
    const = lambda i: (0, 0)
    return pl.pallas_call(
        functools.partial(_router_kernel, nsub=nsub),
        grid=(t // tm,),
        in_specs=[
            pl.BlockSpec((tm, D_ATTN), lambda i: (i, 0)),
            pl.BlockSpec((tm, D_MLSTM), lambda i: (i, 0)),
            pl.BlockSpec((tm, D_MODEL), lambda i: (i, 0)),
            pl.BlockSpec((D_ATTN, D_MODEL), const),
            pl.BlockSpec((D_MLSTM, D_MODEL), const),
            pl.BlockSpec((1, D_MODEL), const),
            pl.BlockSpec((1, D_MODEL), const),
            pl.BlockSpec((D_MODEL, V7X_LANES), const),
            pl.BlockSpec((1, V7X_LANES), const),
        ],
        out_specs=[
            pl.BlockSpec((tm, D_MODEL), lambda i: (i, 0)),
            pl.BlockSpec((nsub, V7X_SUBLANES, TOK_TILE), lambda i: (i, 0, 0)),
            pl.BlockSpec((tm, V7X_LANES), lambda i: (i, 0)),
            pl.BlockSpec((nsub, V7X_SUBLANES, V7X_LANES), lambda i: (i, 0, 0)),
            pl.BlockSpec((V7X_SUBLANES, V7X_LANES), const),
        ],
        out_shape=[
            jax.ShapeDtypeStruct((t, D_MODEL), F32),
            jax.ShapeDtypeStruct((nt, V7X_SUBLANES, TOK_TILE), F32),
            jax.ShapeDtypeStruct((t, V7X_LANES), F32),
            jax.ShapeDtypeStruct((nt, V7X_SUBLANES, V7X_LANES), jnp.int32),
            jax.ShapeDtypeStruct((V7X_SUBLANES, V7X_LANES), jnp.int32),
        ],
        scratch_shapes=[pltpu.VMEM((1, V7X_LANES), F32)],
        compiler_params=_compiler_params(("arbitrary",)),
        name="outproj_router",
    )(attn2d, hm2d, x2d, wo_a, wo_m, ln_g, ln_b, w_r, b_r)


def _run_copy(src, dst, start_src, start_dst, n, sem):
    n = pl.multiple_of(n, RUN_ALIGN)
    return pltpu.make_async_copy(
        src.at[pl.ds(pl.multiple_of(start_src, RUN_ALIGN), n)],
        dst.at[pl.ds(pl.multiple_of(start_dst, RUN_ALIGN), n)],
        sem)


def _dispatch_kernel(cnt_sm, slot_sm, dst_sm, tail_sm, post_ref, h_ref, xs_hbm, ybuf, zbuf, sem, zsem):
    i = pl.program_id(0)
    nt = pl.num_programs(0)
    cur = i % 2
    hb = h_ref[...].astype(BF16)
    pos = post_ref[0]
    chunk = 256
    for rc in range(ROWS_LOCAL // chunk):
        rows = (lax.broadcasted_iota(jnp.int32, (chunk, TOK_TILE), 0) + rc * chunk).astype(F32)
        p = jnp.zeros((chunk, TOK_TILE), F32)
        for k in range(TOP_K):
            p = jnp.where(rows == pos[k:k + 1, :], 1.0, p)
        ybuf[cur, rc * chunk:(rc + 1) * chunk, :] = _dot(p.astype(BF16), hb)

    def total_rows(tile):
        return cnt_sm[nt * N_EXPERTS + tile]

    @pl.when(i > 0)
    def _():
        n_prev = total_rows(i - 1)
        _run_copy(ybuf.at[1 - cur], xs_hbm, 0, 0, n_prev, sem).wait()

    def issue(e, carry):
        n = cnt_sm[i * N_EXPERTS + e]

        @pl.when(n > 0)
        def _():
            _run_copy(ybuf.at[cur], xs_hbm, slot_sm[i * N_EXPERTS + e], dst_sm[i * N_EXPERTS + e], n, sem).start()
        return carry

    lax.fori_loop(0, N_EXPERTS, issue, 0, unroll=4)

    @pl.when(i == nt - 1)
    def _():
        n_cur = total_rows(i)
        _run_copy(ybuf.at[cur], xs_hbm, 0, 0, n_cur, sem).wait()
        zbuf[...] = jnp.zeros_like(zbuf)

        def ztail(wait):
            def body(e, carry):
                n = tail_sm[N_EXPERTS + e]

                @pl.when(n > 0)
                def _():
                    cp = _run_copy(zbuf, xs_hbm, 0, tail_sm[e], n, zsem)
                    if wait:
                        cp.wait()
                    else:
                        cp.start()
                return carry
            return body

        lax.fori_loop(0, N_EXPERTS, ztail(False), 0)
        lax.fori_loop(0, N_EXPERTS, ztail(True), 0)

        def zblock(wait):
            def body(j, carry):
                cp = _run_copy(zbuf, xs_hbm, 0, j * ROW_BLOCK, ROW_BLOCK, zsem)
                if wait:
                    cp.wait()
                else:
                    cp.start()
                return carry
            return body

        n_blocks = xs_hbm.shape[0] // ROW_BLOCK
        lax.fori_loop(tail_sm[2 * N_EXPERTS], n_blocks, zblock(False), 0)
        lax.fori_loop(tail_sm[2 * N_EXPERTS], n_blocks, zblock(True), 0)


def _dispatch(cnt, slot, dst, tail, post, h2d, n_rows):
    t = h2d.shape[0]
    nt = t // TOK_TILE
    return pl.pallas_call(
        _dispatch_kernel,
        grid_spec=pltpu.PrefetchScalarGridSpec(
            num_scalar_prefetch=4,
            grid=(nt,),
            in_specs=[
                pl.BlockSpec((1, V7X_SUBLANES, TOK_TILE), lambda i, *_: (i, 0, 0)),
                pl.BlockSpec((TOK_TILE, D_MODEL), lambda i, *_: (i, 0)),
            ],
            out_specs=pl.BlockSpec(memory_space=pl.ANY),
            scratch_shapes=[
                pltpu.VMEM((2, ROWS_LOCAL, D_MODEL), F32),
                pltpu.VMEM((ROW_BLOCK, D_MODEL), F32),
                pltpu.SemaphoreType.DMA(()),
                pltpu.SemaphoreType.DMA(()),
            ],
        ),
        out_shape=jax.ShapeDtypeStruct((n_rows, D_MODEL), F32),
        compiler_params=_compiler_params(("arbitrary",)),
        name="moe_dispatch",
    )(cnt, slot, dst, tail, post, h2d)


def _expert_kernel(bexp_sm, nused_sm, xs_ref, wg_ref, bg_ref, wu_ref, bu_ref, wd_ref, bd_ref, ys_ref,
                   hbuf, wg_b, wu_b, wd_b):
    j = pl.program_id(0)
    used = j < nused_sm[0]
    new_expert = jnp.logical_or(j == 0, bexp_sm[j] != bexp_sm[jnp.maximum(j, 1) - 1])

    @pl.when(jnp.logical_and(used, new_expert))
    def _():
        def cast(i, carry):
            sl = pl.ds(pl.multiple_of(i * 128, 128), 128)
            wg_b[sl, :] = wg_ref[0, sl, :].astype(BF16)
            wu_b[sl, :] = wu_ref[0, sl, :].astype(BF16)
            wd_b[sl, :] = wd_ref[0, sl, :].astype(BF16)
            return carry
        lax.fori_loop(0, D_MODEL // 128, cast, 0)

    @pl.when(used)
    def _():
        xb = xs_ref[...].astype(BF16)
        step = 512
        for n in range(0, D_MODEL, step):
            g = _dot(xb, wg_b[:, n:n + step]) + bg_ref[0, :, n:n + step]
            u = _dot(xb, wu_b[:, n:n + step]) + bu_ref[0, :, n:n + step]
            g = jnp.minimum(g, SWIGLU_LIMIT)
            u = jnp.clip(u, -SWIGLU_LIMIT, SWIGLU_LIMIT)
            hbuf[:, n:n + step] = (g * jax.nn.sigmoid(SWIGLU_ALPHA * g) * (u + 1.0)).astype(BF16)
        ys_ref[...] = _dot(hbuf[...], wd_b[...]) + bd_ref[0]

    @pl.when(jnp.logical_not(used))
    def _():
        ys_ref[...] = jnp.zeros_like(ys_ref)


def _experts(bexp, nused, xs, wg, bg, wu, bu, wd, bd):
    n_rows = xs.shape[0]
    nb = n_rows // ROW_BLOCK

    def row_map(j, be, nu):
        return (jnp.minimum(j, nu[0] - 1), 0)

    def w_map(j, be, nu):
        return (be[jnp.minimum(j, nu[0] - 1)], 0, 0)

    wspec = pl.BlockSpec((1, D_MODEL, D_MODEL), w_map)
    bspec = pl.BlockSpec((1, 1, D_MODEL), w_map)
    return pl.pallas_call(
        _expert_kernel,
        grid_spec=pltpu.PrefetchScalarGridSpec(
            num_scalar_prefetch=2,
            grid=(nb,),
            in_specs=[pl.BlockSpec((ROW_BLOCK, D_MODEL), row_map), wspec, bspec, wspec, bspec, wspec, bspec],
            out_specs=pl.BlockSpec((ROW_BLOCK, D_MODEL), lambda j, be, nu: (j, 0)),
            scratch_shapes=[pltpu.VMEM((ROW_BLOCK, D_MODEL), BF16)]
            + [pltpu.VMEM((D_MODEL, D_MODEL), BF16) for _ in range(3)],
        ),
        out_shape=jax.ShapeDtypeStruct((n_rows, D_MODEL), F32),
        compiler_params=_compiler_params(("arbitrary",)),
        name="moe_experts",
    )(bexp, nused, xs, wg, bg, wu, bu, wd, bd)


def _combine_kernel(cnt_sm, slot_sm, dst_sm, col_ref, h_ref, g_ref, b_ref, ys_hbm, o_ref, ybuf, sbuf, sem):
    i = pl.program_id(0)
    nt = pl.num_programs(0)
    cur = i % 2

    def fetch(tile, buf):
        def body(e, carry):
            n = cnt_sm[tile * N_EXPERTS + e]

            @pl.when(n > 0)
            def _():
                _run_copy(ys_hbm, ybuf.at[buf], dst_sm[tile * N_EXPERTS + e], slot_sm[tile * N_EXPERTS + e],
                          n, sem.at[buf]).start()
            return carry
        lax.fori_loop(0, N_EXPERTS, body, 0, unroll=4)

    @pl.when(i == 0)
    def _():
        ybuf[...] = jnp.zeros_like(ybuf)
        fetch(0, 0)

    @pl.when(i + 1 < nt)
    def _():
        fetch(i + 1, 1 - cur)

    _run_copy(ys_hbm, ybuf.at[cur], 0, 0, cnt_sm[nt * N_EXPERTS + i], sem.at[cur]).wait()

    col = col_ref[...]
    chunk = 256
    for rc in range(ROWS_LOCAL // chunk):
        rows = (lax.broadcasted_iota(jnp.int32, (TOK_TILE, chunk), 1) + rc * chunk).astype(F32)
        s = jnp.zeros((TOK_TILE, chunk), F32)
        for k in range(TOP_K):
            s = jnp.where(rows == col[:, k:k + 1], col[:, TOP_K + k:TOP_K + k + 1], s)
        sbuf[:, rc * chunk:(rc + 1) * chunk] = s.astype(BF16)
    for n in range(0, D_MODEL, chunk):
        moe = _dot(sbuf[...], ybuf[cur, :, n:n + chunk].astype(BF16))
        o_ref[:, n:n + chunk] = DEEPNORM_ALPHA * h_ref[:, n:n + chunk] + moe
    o_ref[...] = _layer_norm(o_ref[...], g_ref[...], b_ref[...])


def _combine(cnt, slot, dst, col, h2d, ln_g, ln_b, ys):
    t = h2d.shape[0]
    nt = t // TOK_TILE
    return pl.pallas_call(
        _combine_kernel,
        grid_spec=pltpu.PrefetchScalarGridSpec(
            num_scalar_prefetch=3,
            grid=(nt,),
            in_specs=[
                pl.BlockSpec((TOK_TILE, V7X_LANES), lambda i, *_: (i, 0)),
                pl.BlockSpec((TOK_TILE, D_MODEL), lambda i, *_: (i, 0)),
                pl.BlockSpec((1, D_MODEL), lambda i, *_: (0, 0)),
                pl.BlockSpec((1, D_MODEL), lambda i, *_: (0, 0)),
                pl.BlockSpec(memory_space=pl.ANY),
            ],
            out_specs=pl.BlockSpec((TOK_TILE, D_MODEL), lambda i, *_: (i, 0)),
            scratch_shapes=[
                pltpu.VMEM((2, ROWS_LOCAL, D_MODEL), F32),
                pltpu.VMEM((TOK_TILE, ROWS_LOCAL), BF16),
                pltpu.SemaphoreType.DMA((2,)),
            ],
        ),
        out_shape=jax.ShapeDtypeStruct((t, D_MODEL), F32),
        compiler_params=_compiler_params(("arbitrary",)),
        name="moe_combine",
    )(cnt, slot, dst, col, h2d, ln_g, ln_b, ys)


def _pad_lanes(a, width=V7X_LANES):
    return jnp.pad(a, ((0, 0), (0, width - a.shape[1])))


def _layer(h3d, w_in, conv_w, conv_b, w_mq, w_mk, b_igate, b_fgate, mnorm_g, w_out,
           ln1_g, ln1_b, w_router, b_router, w_gate, b_gate, w_up, b_up, w_down, b_down, ln2_g, ln2_b):
    batch, seq, _ = h3d.shape
    t = batch * seq
    x2d = h3d.reshape(t, D_MODEL)

    w_main = w_in[:, :PROJ_MAIN].astype(BF16)
    w_gates = _pad_lanes(w_in[:, PROJ_MAIN:]).astype(BF16)
    proj, gcol, grow = _inproj(x2d, w_main, w_gates, batch, seq)
    proj3d = proj.reshape(batch, seq, PROJ_MAIN)
    attn = _attention(proj3d)
    gate_bias = jnp.concatenate([b_igate, b_fgate]).astype(F32)
    bias_col = _pad_lanes(gate_bias[None, :])
    bias_row = jnp.broadcast_to(gate_bias[:, None], (V7X_SUBLANES, seq))
    hm = _mlstm(proj3d, gcol.reshape(batch, seq, V7X_LANES), grow, conv_w, conv_b[None, :],
                w_mq.astype(BF16), w_mk.astype(BF16), bias_col, bias_row, mnorm_g[None, :])

    wo = w_out.astype(BF16)
    h2d, post, col, tab, tot = _outproj_router(
        attn.reshape(t, D_ATTN), hm.reshape(t, D_MLSTM), x2d, wo[:D_ATTN], wo[D_ATTN:],
        ln1_g[None, :], ln1_b[None, :], _pad_lanes(w_router).astype(BF16), _pad_lanes(b_router[None, :]))

    nt = t // TOK_TILE
    total = tot[0, :N_EXPERTS]
    region = (total + ROW_BLOCK - 1) // ROW_BLOCK * ROW_BLOCK
    region_end = jnp.cumsum(region)
    region_start = region_end - region
    n_rows = (t * TOP_K + nt * N_EXPERTS * (RUN_ALIGN - 1)) // ROW_BLOCK * ROW_BLOCK + (N_EXPERTS + 1) * ROW_BLOCK
    nb = n_rows // ROW_BLOCK
    cnt = tab[:, 0, :N_EXPERTS]
    cnt = jnp.concatenate([cnt.reshape(-1), jnp.sum(cnt, axis=1)])
    slot = tab[:, 2, :N_EXPERTS].reshape(-1)
    dst = (tab[:, 1, :N_EXPERTS] + region_start[None, :]).reshape(-1)
    tail = jnp.concatenate([region_start + total, region - total, region_end[-1:] // ROW_BLOCK]).astype(jnp.int32)
    block_row = jnp.arange(nb, dtype=jnp.int32) * ROW_BLOCK
    bexp = jnp.minimum(jnp.sum(region_end[None, :] <= block_row[:, None], axis=1), N_EXPERTS - 1).astype(jnp.int32)
    nused = (region_end[-1:] // ROW_BLOCK).astype(jnp.int32)

    xs = _dispatch(cnt, slot, dst, tail, post, h2d, n_rows)
    ys = _experts(bexp, nused, xs, w_gate, b_gate[:, None, :], w_up, b_up[:, None, :], w_down, b_down[:, None, :])
    out = _combine(cnt, slot, dst, col, h2d, ln2_g[None, :], ln2_b[None, :], ys)
    return out.reshape(batch, seq, D_MODEL)


def kernel(x, w_in, conv_w, conv_b, w_mq, w_mk, b_igate, b_fgate, mnorm_g, w_out, ln1_g, ln1_b, w_router, b_router, w_gate, b_gate, w_up, b_up, w_down, b_down, ln2_g, ln2_b):
    h = x
    for l in range(w_in.shape[0]):
        h = _layer(h, w_in[l], conv_w[l], conv_b[l], w_mq[l], w_mk[l], b_igate[l], b_fgate[l], mnorm_g[l],
                   w_out[l], ln1_g[l], ln1_b[l], w_router[l], b_router[l], w_gate[l], b_gate[l], w_up[l],
                   b_up[l], w_down[l], b_down[l], ln2_g[l], ln2_b[l])
    return h
```

```python
import functools
import math

import jax
import jax.numpy as jnp
from jax import lax
from jax.experimental import pallas as pl
from jax.experimental.pallas import tpu as pltpu

F32 = jnp.float32
BF16 = jnp.bfloat16
NEG_INF = float("-inf")

V7X_LANES = 128
V7X_SUBLANES = 8
V7X_VMEM_LIMIT_BYTES = 56 * 1024 * 1024

D_MODEL = 1024
D_ATTN = 512
HEAD_DIM_A = 64
D_MLSTM = 512
N_HEADS_M = 4
HEAD_DIM_M = 128
CONV_WIDTH = 4
CHUNK = 128
DILATED_BRANCHES = ((128, 1), (512, 4), (2048, 16))
N_EXPERTS = 32
TOP_K = 4
SWIGLU_LIMIT = 7.0
SWIGLU_ALPHA = 1.702
DEEPNORM_ALPHA = 2.0 ** 0.25
LN_EPS = 1e-5
RMS_EPS = 1e-6

PROJ_MAIN = 3 * D_ATTN + 3 * D_MLSTM
TOK_TILE = 256
RUN_ALIGN = V7X_SUBLANES
ROWS_LOCAL = TOK_TILE * TOP_K + N_EXPERTS * RUN_ALIGN
ROW_BLOCK = 512
ROUTER_SUBTILES = 4


def _dot(a, b):
    return jnp.dot(a, b, preferred_element_type=F32)


def _dot_nt(a, b):
    return lax.dot_general(a, b, (((1,), (1,)), ((), ())), preferred_element_type=F32)


def _split3(x):
    hi = x.astype(BF16)
    r1 = x - hi.astype(F32)
    mid = r1.astype(BF16)
    lo = (r1 - mid.astype(F32)).astype(BF16)
    return hi, mid, lo


def _compiler_params(sem):
    return pltpu.CompilerParams(dimension_semantics=sem, vmem_limit_bytes=V7X_VMEM_LIMIT_BYTES)


def _inproj_kernel(x_ref, w_ref, wg_ref, proj_ref, gcol_ref, grow_ref):
    xb = x_ref[...].astype(BF16)
    step = 512
    for n in range(0, PROJ_MAIN, step):
        proj_ref[:, n:n + step] = _dot(xb, w_ref[:, n:n + step])
    g = _dot(xb, wg_ref[...])
    gcol_ref[...] = g
    grow_ref[0] = g.T[:V7X_SUBLANES, :]


def _inproj(x2d, w_main, w_gates, batch, seq):
    t = x2d.shape[0]
    tm = 512
    per_b = seq // tm
    return pl.pallas_call(
        _inproj_kernel,
        grid=(t // tm,),
        in_specs=[
            pl.BlockSpec((tm, D_MODEL), lambda i: (i, 0)),
            pl.BlockSpec((D_MODEL, PROJ_MAIN), lambda i: (0, 0)),
            pl.BlockSpec((D_MODEL, V7X_LANES), lambda i: (0, 0)),
        ],
        out_specs=[
            pl.BlockSpec((tm, PROJ_MAIN), lambda i: (i, 0)),
            pl.BlockSpec((tm, V7X_LANES), lambda i: (i, 0)),
            pl.BlockSpec((1, V7X_SUBLANES, tm), lambda i: (i // per_b, 0, i % per_b)),
        ],
        out_shape=[
            jax.ShapeDtypeStruct((t, PROJ_MAIN), F32),
            jax.ShapeDtypeStruct((t, V7X_LANES), F32),
            jax.ShapeDtypeStruct((batch, V7X_SUBLANES, seq), F32),
        ],
        compiler_params=_compiler_params(("arbitrary",)),
        name="inproj",
    )(x2d, w_main, w_gates)


ATTN_GROUP = 8


def _attn_kernel(q_ref, k_ref, v_ref, o_ref, q0s, q1s, bias_b, bias_f, sg, pg,
                 o0, o1, o2, l0, l1, l2, m0, m1, m2, x0, x1, x2, *, seq):
    obufs, lbufs, mbufs, xbufs = (o0, o1, o2), (l0, l1, l2), (m0, m1, m2), (x0, x1, x2)
    two = 2 * CHUNK
    head0 = lax.broadcasted_iota(jnp.int32, (CHUNK, V7X_LANES), 1) < HEAD_DIM_A
    qscale = HEAD_DIM_A ** -0.5 * math.log2(math.e)

    def prep(i, carry):
        sl = pl.ds(pl.multiple_of(i * two, two), two)
        h0 = lax.broadcasted_iota(jnp.int32, (two, V7X_LANES), 1) < HEAD_DIM_A
        q = q_ref[0, sl, :] * qscale
        q0s[sl, :] = jnp.where(h0, q, 0.0)
        q1s[sl, :] = jnp.where(h0, 0.0, q)
        return carry

    lax.fori_loop(0, seq // two, prep, 0)

    qi = lax.broadcasted_iota(jnp.int32, (two, two), 0) % CHUNK
    kj = lax.broadcasted_iota(jnp.int32, (two, two), 1)
    bias_b[...] = jnp.where((kj >= qi) & (kj <= qi + CHUNK), 0.0, NEG_INF)
    qf = lax.broadcasted_iota(jnp.int32, (two, CHUNK), 0) % CHUNK
    kf = lax.broadcasted_iota(jnp.int32, (two, CHUNK), 1)
    bias_f[...] = jnp.where(kf <= qf, 0.0, NEG_INF)

    def run_group(c, dil, starts, has_prev):
        assert len(starts) <= ATTN_GROUP
        nk = two if has_prev else CHUNK
        bias_ref = bias_b if has_prev else bias_f

        def rows(s0):
            return pl.ds(s0, CHUNK) if dil == 1 else pl.ds(s0, CHUNK, stride=dil)

        def keys(ref, st):
            if has_prev:
                return jnp.concatenate([ref[0, rows(st - dil * CHUNK), :], ref[0, rows(st), :]], axis=0).astype(BF16)
            return ref[0, rows(st), :].astype(BF16)

        def both_heads(x):
            return jnp.where(head0, x[:CHUNK], x[CHUNK:])

        for j, st in enumerate(starts):
            q2 = jnp.concatenate([q0s[rows(st), :], q1s[rows(st), :]], axis=0).astype(BF16)
            sg[j, :, 0:nk] = _dot_nt(q2, keys(k_ref, st)) + bias_ref[...]
        for j, st in enumerate(starts):
            s = sg[j, :, 0:nk]
            m = jnp.max(s, axis=1, keepdims=True)
            pg[j, :, 0:nk] = jnp.exp2(s - m).astype(BF16)
            ma = jnp.broadcast_to(m[:CHUNK], (CHUNK, V7X_LANES))
            mb = jnp.broadcast_to(m[CHUNK:], (CHUNK, V7X_LANES))
            mbufs[c][rows(st), :] = jnp.where(head0, ma, mb)
            xbufs[c][rows(st), :] = jnp.where(head0, mb, ma)
        khead0 = lax.broadcasted_iota(jnp.int32, (nk, V7X_LANES), 1) < HEAD_DIM_A
        for j, st in enumerate(starts):
            if has_prev:
                vv = jnp.concatenate([v_ref[0, rows(st - dil * CHUNK), :], v_ref[0, rows(st), :]], axis=0)
            else:
                vv = v_ref[0, rows(st), :]
            oa = _dot(pg[j, 0:CHUNK, 0:nk], jnp.where(khead0, vv, 1.0).astype(BF16))
            ob = _dot(pg[j, CHUNK:two, 0:nk], jnp.where(khead0, 1.0, vv).astype(BF16))
            obufs[c][rows(st), :] = jnp.where(head0, oa, ob)
            lbufs[c][rows(st), :] = jnp.where(head0, ob, oa)

    for c, (window, dil) in enumerate(DILATED_BRANCHES):
        assert window // dil == CHUNK
        nb = seq // (dil * CHUNK)
        span = dil * CHUNK
        if dil == 1:
            run_group(c, dil, [0], False)
            per = 5
            assert (nb - 1) % per == 0

            def band1(g, carry, c=c, dil=dil, per=per, span=span):
                base = span + g * (per * span)
                run_group(c, dil, [base + span * j for j in range(per)], True)
                return carry

            lax.fori_loop(0, (nb - 1) // per, band1, 0)
        elif nb > 1:
            run_group(c, dil, list(range(dil)), False)

            def band2(g, carry, c=c, dil=dil, nb=nb, span=span):
                starts = [2 * g + rr + span * b for rr in range(2) for b in range(1, nb)]
                run_group(c, dil, starts, True)
                return carry

            lax.fori_loop(0, dil // 2, band2, 0)
        else:
            def firsts(g, carry, c=c, dil=dil):
                run_group(c, dil, [ATTN_GROUP * g + j for j in range(ATTN_GROUP)], False)
                return carry

            lax.fori_loop(0, dil // ATTN_GROUP, firsts, 0)

    def combine(i, carry):
        sl = pl.ds(pl.multiple_of(i * 256, 256), 256)
        ma, mb, mc = m0[sl, :], m1[sl, :], m2[sl, :]
        mx = jnp.maximum(jnp.maximum(ma, mb), mc)
        wa, wb, wc = jnp.exp2(ma - mx), jnp.exp2(mb - mx), jnp.exp2(mc - mx)
        xa, xb, xc = x0[sl, :], x1[sl, :], x2[sl, :]
        xm = jnp.maximum(jnp.maximum(xa, xb), xc)
        den = (jnp.exp2(xa - xm) * l0[sl, :] + jnp.exp2(xb - xm) * l1[sl, :] + jnp.exp2(xc - xm) * l2[sl, :])
        den = pltpu.roll(den, HEAD_DIM_A, axis=1)
        out = (wa * o0[sl, :] + wb * o1[sl, :] + wc * o2[sl, :]) / den
        o_ref[0, sl, :] = out.astype(o_ref.dtype)
        return carry

    lax.fori_loop(0, seq // 256, combine, 0)


def _attention(proj3d):
    batch, seq, _ = proj3d.shape
    nblk = D_ATTN // V7X_LANES
    blk = (1, seq, V7X_LANES)
    scratch = ([pltpu.VMEM((seq, V7X_LANES), F32), pltpu.VMEM((seq, V7X_LANES), F32),
                pltpu.VMEM((2 * CHUNK, 2 * CHUNK), F32), pltpu.VMEM((2 * CHUNK, CHUNK), F32),
                pltpu.VMEM((ATTN_GROUP, 2 * CHUNK, 2 * CHUNK), F32),
                pltpu.VMEM((ATTN_GROUP, 2 * CHUNK, 2 * CHUNK), BF16)]
               + [pltpu.VMEM((seq, V7X_LANES), F32) for _ in range(12)])
    return pl.pallas_call(
        functools.partial(_attn_kernel, seq=seq),
        grid=(batch, nblk),
        in_specs=[
            pl.BlockSpec(blk, lambda b, g: (b, 0, g)),
            pl.BlockSpec(blk, lambda b, g: (b, 0, nblk + g)),
            pl.BlockSpec(blk, lambda b, g: (b, 0, 2 * nblk + g)),
        ],
        out_specs=pl.BlockSpec(blk, lambda b, g: (b, 0, g)),
        out_shape=jax.ShapeDtypeStruct((batch, seq, D_ATTN), BF16),
        scratch_shapes=scratch,
        compiler_params=_compiler_params(("arbitrary", "arbitrary")),
        name="dilated_attention",
    )(proj3d, proj3d, proj3d)


def _log_sigmoid(x):
    return jnp.minimum(x, 0.0) - jnp.log1p(jnp.exp(-jnp.abs(x)))


MLSTM_HEADS_PER_STEP = 2


def _mlstm_kernel(xm_ref, vm_ref, om_ref, gcol_ref, grow_ref, cw_ref, cb_ref, wq_ref, wk_ref,
                  bcol_ref, brow_ref, mg_ref, o_ref, *scratch, seq):
    nh = MLSTM_HEADS_PER_STEP
    per_head = len(scratch) // nh
    nchunk = seq // CHUNK
    ri = lax.broadcasted_iota(jnp.int32, (CHUNK, CHUNK), 0)
    ci = lax.broadcasted_iota(jnp.int32, (CHUNK, CHUNK), 1)
    tril = ri >= ci
    tril_b = jnp.where(tril, 1.0, 0.0).astype(BF16)
    triu_b = jnp.where(ri <= ci, 1.0, 0.0).astype(BF16)
    pad = V7X_SUBLANES
    qscale = HEAD_DIM_M ** -0.5

    class Head:
        def __init__(self, hh):
            (self.xpad, self.q_s, self.k_s, self.icol_s, self.lfcol_s, self.gcol_s,
             self.ri_s, self.rf_s, self.gf_s, self.mi_s, self.mo_s, self.gl_s) = scratch[hh * per_head:(hh + 1) * per_head]
            self.h = pl.program_id(1) * nh + hh
            self.cols = slice(hh * V7X_LANES, (hh + 1) * V7X_LANES)
            self.wq = wq_ref[hh]
            self.wk = wk_ref[hh]
            self.mg = mg_ref[:, self.cols]

    heads = [Head(hh) for hh in range(nh)]

    for hd in heads:
        hd.xpad[0:pad, :] = jnp.zeros((pad, V7X_LANES), F32)
        hd.xpad[pad:pad + seq, :] = xm_ref[0, :, hd.cols]

    def conv_step(i, carry):
        base = pl.multiple_of(i * 256, 256)
        rows = pl.ds(base, 256)
        gc = gcol_ref[0, rows, :] + bcol_ref[...]
        lane256 = lax.broadcasted_iota(jnp.int32, (256, V7X_LANES), 1)
        for hd in heads:
            y = jnp.broadcast_to(cb_ref[:, hd.cols], (256, V7X_LANES))
            xw = hd.xpad[pl.ds(base, 256 + pad), :]
            for j in range(CONV_WIDTH):
                off = pad - (CONV_WIDTH - 1) + j
                y = y + cw_ref[j:j + 1, hd.cols] * xw[off:off + 256, :]
            xc = (y * jax.nn.sigmoid(y)).astype(BF16)
            hd.q_s[rows, :] = _dot(xc, hd.wq) * qscale
            hd.k_s[rows, :] = _dot(xc, hd.wk)
            ic = jnp.sum(jnp.where(lane256 == hd.h, gc, 0.0), axis=1, keepdims=True)
            fc = jnp.sum(jnp.where(lane256 == hd.h + N_HEADS_M, gc, 0.0), axis=1, keepdims=True)
            hd.icol_s[rows, :] = jnp.broadcast_to(ic, (256, V7X_LANES))
            hd.lfcol_s[rows, :] = jnp.broadcast_to(_log_sigmoid(fc), (256, V7X_LANES))
        return carry

    lax.fori_loop(0, seq // 256, conv_step, 0)

    gr = grow_ref[0] + brow_ref[...]
    sub = lax.broadcasted_iota(jnp.int32, (V7X_SUBLANES, seq), 0)
    for hd in heads:
        irow = jnp.sum(jnp.where(sub == hd.h, gr, 0.0), axis=0, keepdims=True)
        lfrow = _log_sigmoid(jnp.sum(jnp.where(sub == hd.h + N_HEADS_M, gr, 0.0), axis=0, keepdims=True))
        for c in range(nchunk):
            hd.ri_s[c:c + 1, :] = irow[:, c * CHUNK:(c + 1) * CHUNK]
            hd.rf_s[c:c + 1, :] = lfrow[:, c * CHUNK:(c + 1) * CHUNK]

    def cum_step(c, carry):
        sl = pl.ds(pl.multiple_of(c * CHUNK, CHUNK), CHUNK)
        for hd in heads:
            hd.gcol_s[sl, :] = sum(_dot(tril_b, part) for part in _split3(hd.lfcol_s[sl, :]))
        return carry

    lax.fori_loop(0, nchunk, cum_step, 0, unroll=2)

    crow = lax.broadcasted_iota(jnp.int32, (nchunk, V7X_LANES), 0)
    for hd in heads:
        irows = hd.ri_s[...]
        grows = sum(_dot(part, triu_b) for part in _split3(hd.rf_s[...]))
        hd.gf_s[...] = grows
        g_last = grows[:, CHUNK - 1:CHUNK]
        a_max = jnp.max(g_last - grows + irows, axis=1, keepdims=True)
        m = jnp.zeros((1, 1), F32)
        m_in = jnp.zeros((nchunk, V7X_LANES), F32)
        m_out = jnp.zeros((nchunk, V7X_LANES), F32)
        for c in range(nchunk):
            m_in = jnp.where(crow == c, m, m_in)
            m = jnp.maximum(g_last[c:c + 1, :] + m, a_max[c:c + 1, :])
            m_out = jnp.where(crow == c, m, m_out)
        hd.mi_s[...] = m_in
        hd.mo_s[...] = m_out
        hd.gl_s[...] = jnp.broadcast_to(g_last, (nchunk, V7X_LANES))

    def chunk_head(hd, c, cmat, nrow):
        sl = pl.ds(pl.multiple_of(c * CHUNK, CHUNK), CHUNK)
        one = pl.ds(c, 1)
        qc = hd.q_s[sl, :]
        kc = hd.k_s[sl, :]
        vc = vm_ref[0, sl, hd.cols].astype(BF16)
        icol = hd.icol_s[sl, :]
        gcol = hd.gcol_s[sl, :]
        irow_c = hd.ri_s[one, :]
        grow_c = hd.gf_s[one, :]
        m = hd.mi_s[one, :]
        m_new = hd.mo_s[one, :]
        g_last = hd.gl_s[one, :]

        d = gcol - grow_c + irow_c
        d = jnp.where(tril, d, NEG_INF)
        inter = gcol + m
        m_t = jnp.maximum(inter, jnp.max(d, axis=1, keepdims=True))
        w_intra = jnp.exp(d - m_t)
        w_inter = jnp.exp(inter - m_t)
        qb = qc.astype(BF16)
        kb = kc.astype(BF16)
        qk = _dot_nt(qb, kb) * w_intra
        num = _dot(qk.astype(BF16), vc) + w_inter * _dot(qb, cmat.astype(BF16))
        den = jnp.sum(qk, axis=1, keepdims=True) + w_inter * jnp.sum(qc * nrow, axis=1, keepdims=True)
        hc = num / jnp.maximum(jnp.abs(den), jnp.exp(-m_t))

        a_col = g_last - gcol + icol
        decay = jnp.exp(g_last + m - m_new)
        wk_col = jnp.exp(a_col - m_new)
        kw = wk_col * kc
        c_new = decay * cmat + _dot(kw.T.astype(BF16), vc)
        n_new = decay * nrow + jnp.sum(kw, axis=0, keepdims=True)

        hn = hc * lax.rsqrt(jnp.mean(hc * hc, axis=1, keepdims=True) + RMS_EPS) * hd.mg
        hn = jax.nn.sigmoid(om_ref[0, sl, hd.cols]) * hn
        o_ref[0, sl, hd.cols] = hn.astype(o_ref.dtype)
        return c_new, n_new

    def chunk_step(c, carry):
        return tuple(chunk_head(hd, c, *carry[i]) for i, hd in enumerate(heads))

    init = tuple((jnp.zeros((HEAD_DIM_M, HEAD_DIM_M), F32), jnp.zeros((1, HEAD_DIM_M), F32)) for _ in heads)
    lax.fori_loop(0, nchunk, chunk_step, init, unroll=2)


def _mlstm(proj3d, gcol3d, grow3d, conv_w, conv_b, wq, wk, bias_col, bias_row, mnorm):
    batch, seq, _ = proj3d.shape
    nh = MLSTM_HEADS_PER_STEP
    width = nh * V7X_LANES
    blk = (1, seq, width)
    a0 = 3 * D_ATTN // width
    ng = N_HEADS_M // nh
    per_head = ([pltpu.VMEM((seq + V7X_SUBLANES, V7X_LANES), F32)]
                + [pltpu.VMEM((seq, V7X_LANES), F32) for _ in range(5)]
                + [pltpu.VMEM((seq // CHUNK, V7X_LANES), F32) for _ in range(6)])
    return pl.pallas_call(
        functools.partial(_mlstm_kernel, seq=seq),
        grid=(batch, ng),
        in_specs=[
            pl.BlockSpec(blk, lambda b, g: (b, 0, a0 + g)),
            pl.BlockSpec(blk, lambda b, g: (b, 0, a0 + ng + g)),
            pl.BlockSpec(blk, lambda b, g: (b, 0, a0 + 2 * ng + g)),
            pl.BlockSpec((1, seq, V7X_LANES), lambda b, g: (b, 0, 0)),
            pl.BlockSpec((1, V7X_SUBLANES, seq), lambda b, g: (b, 0, 0)),
            pl.BlockSpec((CONV_WIDTH, width), lambda b, g: (0, g)),
            pl.BlockSpec((1, width), lambda b, g: (0, g)),
            pl.BlockSpec((nh, HEAD_DIM_M, HEAD_DIM_M), lambda b, g: (g, 0, 0)),
            pl.BlockSpec((nh, HEAD_DIM_M, HEAD_DIM_M), lambda b, g: (g, 0, 0)),
            pl.BlockSpec((1, V7X_LANES), lambda b, g: (0, 0)),
            pl.BlockSpec((V7X_SUBLANES, seq), lambda b, g: (0, 0)),
            pl.BlockSpec((1, width), lambda b, g: (0, g)),
        ],
        out_specs=pl.BlockSpec(blk, lambda b, g: (b, 0, g)),
        out_shape=jax.ShapeDtypeStruct((batch, seq, D_MLSTM), BF16),
        scratch_shapes=per_head * nh,
        compiler_params=_compiler_params(("arbitrary", "arbitrary")),
        name="mlstm",
    )(proj3d, proj3d, proj3d, gcol3d, grow3d, conv_w, conv_b, wq, wk, bias_col, bias_row, mnorm)


def _layer_norm(z, g, b):
    mu = jnp.mean(z, axis=1, keepdims=True)
    zc = z - mu
    var = jnp.mean(zc * zc, axis=1, keepdims=True)
    return zc * lax.rsqrt(var + LN_EPS) * g + b


def _router_kernel(attn_ref, hm_ref, x_ref, woa_ref, wom_ref, g_ref, b_ref, wr_ref, br_ref,
                   h_ref, post_ref, col_ref, tab_ref, tot_ref, carry, *, nsub):
    i = pl.program_id(0)
    tm = TOK_TILE

    @pl.when(i == 0)
    def _():
        carry[...] = jnp.zeros_like(carry)

    ne = N_EXPERTS
    ex = lax.broadcasted_iota(jnp.int32, (ne, tm), 0).astype(F32)
    ri = lax.broadcasted_iota(jnp.int32, (tm, tm), 0)
    ci = lax.broadcasted_iota(jnp.int32, (tm, tm), 1)
    earlier = jnp.where(ri < ci, 1.0, 0.0).astype(BF16)
    er = lax.broadcasted_iota(jnp.int32, (ne, ne), 0)
    ec = lax.broadcasted_iota(jnp.int32, (ne, ne), 1)
    lower = jnp.where(ec < er, 1.0, 0.0).astype(BF16)
    diag = (lax.broadcasted_iota(jnp.int32, (ne, V7X_LANES), 0)
            == lax.broadcasted_iota(jnp.int32, (ne, V7X_LANES), 1))
    sub = lax.broadcasted_iota(jnp.int32, (V7X_SUBLANES, V7X_LANES), 0)
    base = carry[...]

    def to_lanes(colvec):
        return jnp.sum(jnp.where(diag, colvec, 0.0), axis=0, keepdims=True)

    for s in range(nsub):
        rs = slice(s * tm, (s + 1) * tm)
        h_ref[rs, :] = (DEEPNORM_ALPHA * x_ref[rs, :] + _dot(attn_ref[rs, :], woa_ref[...])
                        + _dot(hm_ref[rs, :], wom_ref[...]))
    for s in range(nsub):
        rs = slice(s * tm, (s + 1) * tm)
        hval = _layer_norm(h_ref[rs, :], g_ref[...], b_ref[...])
        h_ref[rs, :] = hval

        logits = _dot(hval.astype(BF16), wr_ref[...]) + br_ref[...]
        lt = logits.T[:ne, :]

        sel, vals = [], []
        for _ in range(TOP_K):
            mx = jnp.max(lt, axis=0, keepdims=True)
            idx = jnp.min(jnp.where(lt == mx, ex, float(ne)), axis=0, keepdims=True)
            hit = ex == idx
            lt = jnp.where(hit, NEG_INF, lt)
            sel.append(hit)
            vals.append(mx)
        exps = [jnp.exp(v - vals[0]) for v in vals]
        den = exps[0] + exps[1] + exps[2] + exps[3]
        gates = [e / den for e in exps]

        onehot = jnp.zeros((ne, tm), F32)
        for hit in sel:
            onehot = jnp.where(hit, 1.0, onehot)
        rank = _dot(onehot.astype(BF16), earlier)
        cnt = jnp.broadcast_to(jnp.sum(onehot, axis=1, keepdims=True), (ne, V7X_LANES))
        cnt_al = jnp.floor((cnt + (RUN_ALIGN - 1)) * (1.0 / RUN_ALIGN))
        slot = _dot(lower, cnt_al.astype(BF16)) * float(RUN_ALIGN)
        cnt_al = cnt_al * float(RUN_ALIGN)
        where_row = slot[:, 0:1] + rank

        rows = [jnp.sum(jnp.where(sel[k], where_row, 0.0), axis=0, keepdims=True) for k in range(TOP_K)]
        info = jnp.concatenate(rows + gates, axis=0)
        post_ref[s] = info
        col_ref[rs, :] = jnp.concatenate([info, jnp.zeros((V7X_LANES - 2 * TOP_K, tm), F32)], axis=0).T

        cnt_row = to_lanes(cnt_al)
        tab = jnp.zeros((V7X_SUBLANES, V7X_LANES), F32)
        tab = jnp.where(sub == 0, cnt_row, tab)
        tab = jnp.where(sub == 1, base, tab)
        tab = jnp.where(sub == 2, to_lanes(slot), tab)
        tab_ref[s] = tab.astype(jnp.int32)
        base = base + cnt_row

    carry[...] = base
    tot_ref[...] = jnp.broadcast_to(base, (V7X_SUBLANES, V7X_LANES)).astype(jnp.int32)


def _outproj_router(attn2d, hm2d, x2d, wo_a, wo_m, ln_g, ln_b, w_r, b_r):
    t = x2d.shape[0]
    nsub = ROUTER_SUBTILES
    tm = TOK_TILE * nsub
    nt = t // TOK_TILE
    const = lambda i: (0, 0)
    return pl.pallas_call(
        functools.partial(_router_kernel, nsub=nsub),
        grid=(t // tm,),
        in_specs=[
            pl.BlockSpec((tm, D_ATTN), lambda i: (i, 0)),
            pl.BlockSpec((tm, D_MLSTM), lambda i: (i, 0)),
            pl.BlockSpec((tm, D_MODEL), lambda i: (i, 0)),
            pl.BlockSpec((D_ATTN, D_MODEL), const),
            pl.BlockSpec((D_MLSTM, D_MODEL), const),
            pl.BlockSpec((1, D_MODEL), const),
            pl.BlockSpec((1, D_MODEL), const),
            pl.BlockSpec((D_MODEL, V7X_LANES), const),
            pl.BlockSpec((1, V7X_LANES), const),
        ],
        out_specs=[
            pl.BlockSpec((tm, D_MODEL), lambda i: (i, 0)),
            pl.BlockSpec((nsub, V7X_SUBLANES, TOK_TILE), lambda i: (i, 0, 0)),
            pl.BlockSpec((tm, V7X_LANES), lambda i: (i, 0)),
            pl.BlockSpec((nsub, V7X_SUBLANES, V7X_LANES), lambda i: (i, 0, 0)),
            pl.BlockSpec((V7X_SUBLANES, V7X_LANES), const),
        ],
        out_shape=[
            jax.ShapeDtypeStruct((t, D_MODEL), F32),
            jax.ShapeDtypeStruct((nt, V7X_SUBLANES, TOK_TILE), F32),
            jax.ShapeDtypeStruct((t, V7X_LANES), F32),
            jax.ShapeDtypeStruct((nt, V7X_SUBLANES, V7X_LANES), jnp.int32),
            jax.ShapeDtypeStruct((V7X_SUBLANES, V7X_LANES), jnp.int32),
        ],
        scratch_shapes=[pltpu.VMEM((1, V7X_LANES), F32)],
        compiler_params=_compiler_params(("arbitrary",)),
        name="outproj_router",
    )(attn2d, hm2d, x2d, wo_a, wo_m, ln_g, ln_b, w_r, b_r)


def _run_copy(src, dst, start_src, start_dst, n, sem):
    n = pl.multiple_of(n, RUN_ALIGN)
    return pltpu.make_async_copy(
        src.at[pl.ds(pl.multiple_of(start_src, RUN_ALIGN), n)],
        dst.at[pl.ds(pl.multiple_of(start_dst, RUN_ALIGN), n)],
        sem)


def _dispatch_kernel(cnt_sm, slot_sm, dst_sm, tail_sm, post_ref, h_ref, xs_hbm, ybuf, zbuf, sem, zsem):
    i = pl.program_id(0)
    nt = pl.num_programs(0)
    cur = i % 2

    @pl.when(i == 0)
    def _():
        for b in range(2):
            ybuf[b, ROWS_LOCAL:ROWS_LOCAL + RUN_ALIGN, :] = jnp.zeros((RUN_ALIGN, D_MODEL), F32)

    hb = h_ref[...].astype(BF16)
    pos = post_ref[0]
    chunk = 256
    for rc in range(ROWS_LOCAL // chunk):
        rows = (lax.broadcasted_iota(jnp.int32, (chunk, TOK_TILE), 0) + rc * chunk).astype(F32)
        p = jnp.zeros((chunk, TOK_TILE), F32)
        for k in range(TOP_K):
            p = jnp.where(rows == pos[k:k + 1, :], 1.0, p)
        ybuf[cur, rc * chunk:(rc + 1) * chunk, :] = _dot(p.astype(BF16), hb)

    def total_rows(tile):
        return cnt_sm[nt * N_EXPERTS + tile]

    @pl.when(i > 0)
    def _():
        n_prev = total_rows(i - 1)
        _run_copy(ybuf.at[1 - cur], xs_hbm, 0, 0, n_prev, sem).wait()

    for e in range(N_EXPERTS):
        k = i * N_EXPERTS + e
        _run_copy(ybuf.at[cur], xs_hbm, slot_sm[k], dst_sm[k], cnt_sm[k], sem).start()

    @pl.when(i == nt - 1)
    def _():
        n_cur = total_rows(i)
        _run_copy(ybuf.at[cur], xs_hbm, 0, 0, n_cur, sem).wait()
        zbuf[...] = jnp.zeros_like(zbuf)

        def ztail(wait):
            def body(e, carry):
                n = tail_sm[N_EXPERTS + e]

                @pl.when(n > 0)
                def _():
                    cp = _run_copy(zbuf, xs_hbm, 0, tail_sm[e], n, zsem)
                    if wait:
                        cp.wait()
                    else:
                        cp.start()
                return carry
            return body

        lax.fori_loop(0, N_EXPERTS, ztail(False), 0)
        lax.fori_loop(0, N_EXPERTS, ztail(True), 0)

        def zblock(wait):
            def body(j, carry):
                cp = _run_copy(zbuf, xs_hbm, 0, j * ROW_BLOCK, ROW_BLOCK, zsem)
                if wait:
                    cp.wait()
                else:
                    cp.start()
                return carry
            return body

        n_blocks = xs_hbm.shape[0] // ROW_BLOCK
        lax.fori_loop(tail_sm[2 * N_EXPERTS], n_blocks, zblock(False), 0)
        lax.fori_loop(tail_sm[2 * N_EXPERTS], n_blocks, zblock(True), 0)


def _dispatch(cnt, slot, dst, tail, post, h2d, n_rows):
    t = h2d.shape[0]
    nt = t // TOK_TILE
    return pl.pallas_call(
        _dispatch_kernel,
        grid_spec=pltpu.PrefetchScalarGridSpec(
            num_scalar_prefetch=4,
            grid=(nt,),
            in_specs=[
                pl.BlockSpec((1, V7X_SUBLANES, TOK_TILE), lambda i, *_: (i, 0, 0)),
                pl.BlockSpec((TOK_TILE, D_MODEL), lambda i, *_: (i, 0)),
            ],
            out_specs=pl.BlockSpec(memory_space=pl.ANY),
            scratch_shapes=[
                pltpu.VMEM((2, ROWS_LOCAL + RUN_ALIGN, D_MODEL), F32),
                pltpu.VMEM((ROW_BLOCK, D_MODEL), F32),
                pltpu.SemaphoreType.DMA(()),
                pltpu.SemaphoreType.DMA(()),
            ],
        ),
        out_shape=jax.ShapeDtypeStruct((n_rows, D_MODEL), F32),
        compiler_params=_compiler_params(("arbitrary",)),
        name="moe_dispatch",
    )(cnt, slot, dst, tail, post, h2d)


def _expert_kernel(bexp_sm, nused_sm, xs_ref, wg_ref, bg_ref, wu_ref, bu_ref, wd_ref, bd_ref, ys_ref,
                   hbuf, wg_b, wu_b, wd_b):
    j = pl.program_id(0)
    used = j < nused_sm[0]
    new_expert = jnp.logical_or(j == 0, bexp_sm[j] != bexp_sm[jnp.maximum(j, 1) - 1])

    @pl.when(jnp.logical_and(used, new_expert))
    def _():
        def cast(i, carry):
            sl = pl.ds(pl.multiple_of(i * 128, 128), 128)
            wg_b[sl, :] = wg_ref[0, sl, :].astype(BF16)
            wu_b[sl, :] = wu_ref[0, sl, :].astype(BF16)
            wd_b[sl, :] = wd_ref[0, sl, :].astype(BF16)
            return carry
        lax.fori_loop(0, D_MODEL // 128, cast, 0)

    @pl.when(used)
    def _():
        xb = xs_ref[...].astype(BF16)
        step = 512
        for n in range(0, D_MODEL, step):
            g = _dot(xb, wg_b[:, n:n + step]) + bg_ref[0, :, n:n + step]
            u = _dot(xb, wu_b[:, n:n + step]) + bu_ref[0, :, n:n + step]
            g = jnp.minimum(g, SWIGLU_LIMIT)
            u = jnp.clip(u, -SWIGLU_LIMIT, SWIGLU_LIMIT)
            hbuf[:, n:n + step] = (g * jax.nn.sigmoid(SWIGLU_ALPHA * g) * (u + 1.0)).astype(BF16)
        ys_ref[...] = _dot(hbuf[...], wd_b[...]) + bd_ref[0]

    @pl.when(jnp.logical_not(used))
    def _():
        ys_ref[...] = jnp.zeros_like(ys_ref)


def _experts(bexp, nused, xs, wg, bg, wu, bu, wd, bd):
    n_rows = xs.shape[0]
    nb = n_rows // ROW_BLOCK

    def row_map(j, be, nu):
        return (jnp.minimum(j, nu[0] - 1), 0)

    def w_map(j, be, nu):
        return (be[jnp.minimum(j, nu[0] - 1)], 0, 0)

    wspec = pl.BlockSpec((1, D_MODEL, D_MODEL), w_map)
    bspec = pl.BlockSpec((1, 1, D_MODEL), w_map)
    return pl.pallas_call(
        _expert_kernel,
        grid_spec=pltpu.PrefetchScalarGridSpec(
            num_scalar_prefetch=2,
            grid=(nb,),
            in_specs=[pl.BlockSpec((ROW_BLOCK, D_MODEL), row_map), wspec, bspec, wspec, bspec, wspec, bspec],
            out_specs=pl.BlockSpec((ROW_BLOCK, D_MODEL), lambda j, be, nu: (j, 0)),
            scratch_shapes=[pltpu.VMEM((ROW_BLOCK, D_MODEL), BF16)]
            + [pltpu.VMEM((D_MODEL, D_MODEL), BF16) for _ in range(3)],
        ),
        out_shape=jax.ShapeDtypeStruct((n_rows, D_MODEL), F32),
        compiler_params=_compiler_params(("arbitrary",)),
        name="moe_experts",
    )(bexp, nused, xs, wg, bg, wu, bu, wd, bd)


def _combine_kernel(cnt_sm, slot_sm, dst_sm, col_ref, h_ref, g_ref, b_ref, ys_hbm, o_ref, ybuf, sbuf, sem):
    i = pl.program_id(0)
    nt = pl.num_programs(0)
    cur = i % 2

    def fetch(tile, buf):
        for e in range(N_EXPERTS):
            k = tile * N_EXPERTS + e
            _run_copy(ys_hbm, ybuf.at[buf], dst_sm[k], slot_sm[k], cnt_sm[k], sem.at[buf]).start()

    @pl.when(i == 0)
    def _():
        ybuf[...] = jnp.zeros_like(ybuf)
        fetch(0, 0)

    @pl.when(i + 1 < nt)
    def _():
        fetch(i + 1, 1 - cur)

    _run_copy(ys_hbm, ybuf.at[cur], 0, 0, cnt_sm[nt * N_EXPERTS + i], sem.at[cur]).wait()

    col = col_ref[...]
    chunk = 256
    for rc in range(ROWS_LOCAL // chunk):
        rows = (lax.broadcasted_iota(jnp.int32, (TOK_TILE, chunk), 1) + rc * chunk).astype(F32)
        s = jnp.zeros((TOK_TILE, chunk), F32)
        for k in range(TOP_K):
            s = jnp.where(rows == col[:, k:k + 1], col[:, TOP_K + k:TOP_K + k + 1], s)
        sbuf[:, rc * chunk:(rc + 1) * chunk] = s.astype(BF16)
    for n in range(0, D_MODEL, chunk):
        moe = _dot(sbuf[...], ybuf[cur, 0:ROWS_LOCAL, n:n + chunk].astype(BF16))
        o_ref[:, n:n + chunk] = DEEPNORM_ALPHA * h_ref[:, n:n + chunk] + moe
    o_ref[...] = _layer_norm(o_ref[...], g_ref[...], b_ref[...])


def _combine(cnt, slot, dst, col, h2d, ln_g, ln_b, ys):
    t = h2d.shape[0]
    nt = t // TOK_TILE
    return pl.pallas_call(
        _combine_kernel,
        grid_spec=pltpu.PrefetchScalarGridSpec(
            num_scalar_prefetch=3,
            grid=(nt,),
            in_specs=[
                pl.BlockSpec((TOK_TILE, V7X_LANES), lambda i, *_: (i, 0)),
                pl.BlockSpec((TOK_TILE, D_MODEL), lambda i, *_: (i, 0)),
                pl.BlockSpec((1, D_MODEL), lambda i, *_: (0, 0)),
                pl.BlockSpec((1, D_MODEL), lambda i, *_: (0, 0)),
                pl.BlockSpec(memory_space=pl.ANY),
            ],
            out_specs=pl.BlockSpec((TOK_TILE, D_MODEL), lambda i, *_: (i, 0)),
            scratch_shapes=[
                pltpu.VMEM((2, ROWS_LOCAL + RUN_ALIGN, D_MODEL), F32),
                pltpu.VMEM((TOK_TILE, ROWS_LOCAL), BF16),
                pltpu.SemaphoreType.DMA((2,)),
            ],
        ),
        out_shape=jax.ShapeDtypeStruct((t, D_MODEL), F32),
        compiler_params=_compiler_params(("arbitrary",)),
        name="moe_combine",
    )(cnt, slot, dst, col, h2d, ln_g, ln_b, ys)


def _pad_lanes(a, width=V7X_LANES):
    return jnp.pad(a, ((0, 0), (0, width - a.shape[1])))


def _layer(h3d, w_in, conv_w, conv_b, w_mq, w_mk, b_igate, b_fgate, mnorm_g, w_out,
           ln1_g, ln1_b, w_router, b_router, w_gate, b_gate, w_up, b_up, w_down, b_down, ln2_g, ln2_b):
    batch, seq, _ = h3d.shape
    t = batch * seq
    x2d = h3d.reshape(t, D_MODEL)

    w_main = w_in[:, :PROJ_MAIN].astype(BF16)
    w_gates = _pad_lanes(w_in[:, PROJ_MAIN:]).astype(BF16)
    proj, gcol, grow = _inproj(x2d, w_main, w_gates, batch, seq)
    proj3d = proj.reshape(batch, seq, PROJ_MAIN)
    attn = _attention(proj3d)
    gate_bias = jnp.concatenate([b_igate, b_fgate]).astype(F32)
    bias_col = _pad_lanes(gate_bias[None, :])
    bias_row = jnp.broadcast_to(gate_bias[:, None], (V7X_SUBLANES, seq))
    hm = _mlstm(proj3d, gcol.reshape(batch, seq, V7X_LANES), grow, conv_w, conv_b[None, :],
                w_mq.astype(BF16), w_mk.astype(BF16), bias_col, bias_row, mnorm_g[None, :])

    wo = w_out.astype(BF16)
    h2d, post, col, tab, tot = _outproj_router(
        attn.reshape(t, D_ATTN), hm.reshape(t, D_MLSTM), x2d, wo[:D_ATTN], wo[D_ATTN:],
        ln1_g[None, :], ln1_b[None, :], _pad_lanes(w_router).astype(BF16), _pad_lanes(b_router[None, :]))

    nt = t // TOK_TILE
    total = tot[0, :N_EXPERTS]
    region = (total + ROW_BLOCK - 1) // ROW_BLOCK * ROW_BLOCK
    region_end = jnp.cumsum(region)
    region_start = region_end - region
    n_rows = (t * TOP_K + nt * N_EXPERTS * (RUN_ALIGN - 1)) // ROW_BLOCK * ROW_BLOCK + (N_EXPERTS + 1) * ROW_BLOCK
    nb = n_rows // ROW_BLOCK
    runs = tab[:, 0, :N_EXPERTS]
    empty = runs == 0
    cnt = jnp.where(empty, RUN_ALIGN, runs)
    cnt = jnp.concatenate([cnt.reshape(-1), jnp.sum(cnt, axis=1)])
    slot = jnp.where(empty, ROWS_LOCAL, tab[:, 2, :N_EXPERTS]).reshape(-1)
    dst = jnp.where(empty, n_rows - ROW_BLOCK, tab[:, 1, :N_EXPERTS] + region_start[None, :]).reshape(-1)
    tail = jnp.concatenate([region_start + total, region - total, region_end[-1:] // ROW_BLOCK]).astype(jnp.int32)
    block_row = jnp.arange(nb, dtype=jnp.int32) * ROW_BLOCK
    bexp = jnp.minimum(jnp.sum(region_end[None, :] <= block_row[:, None], axis=1), N_EXPERTS - 1).astype(jnp.int32)
    nused = (region_end[-1:] // ROW_BLOCK).astype(jnp.int32)

    xs = _dispatch(cnt, slot, dst, tail, post, h2d, n_rows)
    ys = _experts(bexp, nused, xs, w_gate, b_gate[:, None, :], w_up, b_up[:, None, :], w_down, b_down[:, None, :])
    out = _combine(cnt, slot, dst, col, h2d, ln2_g[None, :], ln2_b[None, :], ys)
    return out.reshape(batch, seq, D_MODEL)


def kernel(x, w_in, conv_w, conv_b, w_mq, w_mk, b_igate, b_fgate, mnorm_g, w_out, ln1_g, ln1_b, w_router, b_router, w_gate, b_gate, w_up, b_up, w_down, b_down, ln2_g, ln2_b):
    h = x
    for l in range(w_in.shape[0]):
        h = _layer(h, w_in[l], conv_w[l], conv_b[l], w_mq[l], w_mk[l], b_igate[l], b_fgate[l], mnorm_g[l],
                   w_out[l], ln1_g[l], ln1_b[l], w_router[l], b_router[l], w_gate[l], b_gate[l], w_up[l],
                   b_up[l], w_down[l], b_down[l], ln2_g[l], ln2_b[l])
    return h
```

```python
import functools
import math

import jax
import jax.numpy as jnp
from jax import lax
from jax.experimental import pallas as pl
from jax.experimental.pallas import tpu as pltpu

F32 = jnp.float32
BF16 = jnp.bfloat16
NEG_INF = float("-inf")

V7X_LANES = 128
V7X_SUBLANES = 8
V7X_VMEM_LIMIT_BYTES = 56 * 1024 * 1024

D_MODEL = 1024
D_ATTN = 512
HEAD_DIM_A = 64
D_MLSTM = 512
N_HEADS_M = 4
HEAD_DIM_M = 128
CONV_WIDTH = 4
CHUNK = 128
DILATED_BRANCHES = ((128, 1), (512, 4), (2048, 16))
N_EXPERTS = 32
TOP_K = 4
SWIGLU_LIMIT = 7.0
SWIGLU_ALPHA = 1.702
DEEPNORM_ALPHA = 2.0 ** 0.25
LN_EPS = 1e-5
RMS_EPS = 1e-6

PROJ_MAIN = 3 * D_ATTN + 3 * D_MLSTM
TOK_TILE = 256
RUN_ALIGN = V7X_SUBLANES
ROWS_LOCAL = TOK_TILE * TOP_K + N_EXPERTS * RUN_ALIGN
ROW_BLOCK = 512
ROUTER_SUBTILES = 4


def _dot(a, b):
    return jnp.dot(a, b, preferred_element_type=F32)


def _dot_nt(a, b):
    return lax.dot_general(a, b, (((1,), (1,)), ((), ())), preferred_element_type=F32)


def _split3(x):
    hi = x.astype(BF16)
    r1 = x - hi.astype(F32)
    mid = r1.astype(BF16)
    lo = (r1 - mid.astype(F32)).astype(BF16)
    return hi, mid, lo


def _compiler_params(sem):
    return pltpu.CompilerParams(dimension_semantics=sem, vmem_limit_bytes=V7X_VMEM_LIMIT_BYTES)


def _inproj_kernel(x_ref, w_ref, wg_ref, proj_ref, gcol_ref, grow_ref):
    xb = x_ref[...].astype(BF16)
    step = 512
    for n in range(0, PROJ_MAIN, step):
        proj_ref[:, n:n + step] = _dot(xb, w_ref[:, n:n + step])
    g = _dot(xb, wg_ref[...])
    gcol_ref[...] = g
    grow_ref[0] = g.T[:V7X_SUBLANES, :]


def _inproj(x2d, w_main, w_gates, batch, seq):
    t = x2d.shape[0]
    tm = 512
    per_b = seq // tm
    return pl.pallas_call(
        _inproj_kernel,
        grid=(t // tm,),
        in_specs=[
            pl.BlockSpec((tm, D_MODEL), lambda i: (i, 0)),
            pl.BlockSpec((D_MODEL, PROJ_MAIN), lambda i: (0, 0)),
            pl.BlockSpec((D_MODEL, V7X_LANES), lambda i: (0, 0)),
        ],
        out_specs=[
            pl.BlockSpec((tm, PROJ_MAIN), lambda i: (i, 0)),
            pl.BlockSpec((tm, V7X_LANES), lambda i: (i, 0)),
            pl.BlockSpec((1, V7X_SUBLANES, tm), lambda i: (i // per_b, 0, i % per_b)),
        ],
        out_shape=[
            jax.ShapeDtypeStruct((t, PROJ_MAIN), F32),
            jax.ShapeDtypeStruct((t, V7X_LANES), F32),
            jax.ShapeDtypeStruct((batch, V7X_SUBLANES, seq), F32),
        ],
        compiler_params=_compiler_params(("arbitrary",)),
        name="inproj",
    )(x2d, w_main, w_gates)


ATTN_GROUP = 8


def _attn_kernel(q_ref, k_ref, v_ref, o_ref, q0s, q1s, bias_b, bias_f, sg, pg,
                 o0, o1, o2, l0, l1, l2, m0, m1, m2, x0, x1, x2, *, seq):
    obufs, lbufs, mbufs, xbufs = (o0, o1, o2), (l0, l1, l2), (m0, m1, m2), (x0, x1, x2)
    two = 2 * CHUNK
    head0 = lax.broadcasted_iota(jnp.int32, (CHUNK, V7X_LANES), 1) < HEAD_DIM_A
    qscale = HEAD_DIM_A ** -0.5 * math.log2(math.e)

    def prep(i, carry):
        sl = pl.ds(pl.multiple_of(i * two, two), two)
        h0 = lax.broadcasted_iota(jnp.int32, (two, V7X_LANES), 1) < HEAD_DIM_A
        q = q_ref[0, sl, :] * qscale
        q0s[sl, :] = jnp.where(h0, q, 0.0)
        q1s[sl, :] = jnp.where(h0, 0.0, q)
        return carry

    lax.fori_loop(0, seq // two, prep, 0)

    qi = lax.broadcasted_iota(jnp.int32, (two, two), 0) % CHUNK
    kj = lax.broadcasted_iota(jnp.int32, (two, two), 1)
    bias_b[...] = jnp.where((kj >= qi) & (kj <= qi + CHUNK), 0.0, NEG_INF)
    qf = lax.broadcasted_iota(jnp.int32, (two, CHUNK), 0) % CHUNK
    kf = lax.broadcasted_iota(jnp.int32, (two, CHUNK), 1)
    bias_f[...] = jnp.where(kf <= qf, 0.0, NEG_INF)

    def run_group(c, dil, starts, has_prev):
        assert len(starts) <= ATTN_GROUP
        nk = two if has_prev else CHUNK
        bias_ref = bias_b if has_prev else bias_f

        def rows(s0):
            return pl.ds(s0, CHUNK) if dil == 1 else pl.ds(s0, CHUNK, stride=dil)

        def keys(ref, st):
            if has_prev:
                return jnp.concatenate([ref[0, rows(st - dil * CHUNK), :], ref[0, rows(st), :]], axis=0).astype(BF16)
            return ref[0, rows(st), :].astype(BF16)

        def both_heads(x):
            return jnp.where(head0, x[:CHUNK], x[CHUNK:])

        for j, st in enumerate(starts):
            q2 = jnp.concatenate([q0s[rows(st), :], q1s[rows(st), :]], axis=0).astype(BF16)
            sg[j, :, 0:nk] = _dot_nt(q2, keys(k_ref, st)) + bias_ref[...]
        for j, st in enumerate(starts):
            s = sg[j, :, 0:nk]
            m = jnp.max(s, axis=1, keepdims=True)
            pg[j, :, 0:nk] = jnp.exp2(s - m).astype(BF16)
            ma = jnp.broadcast_to(m[:CHUNK], (CHUNK, V7X_LANES))
            mb = jnp.broadcast_to(m[CHUNK:], (CHUNK, V7X_LANES))
            mbufs[c][rows(st), :] = jnp.where(head0, ma, mb)
            xbufs[c][rows(st), :] = jnp.where(head0, mb, ma)
        khead0 = lax.broadcasted_iota(jnp.int32, (nk, V7X_LANES), 1) < HEAD_DIM_A
        for j, st in enumerate(starts):
            if has_prev:
                vv = jnp.concatenate([v_ref[0, rows(st - dil * CHUNK), :], v_ref[0, rows(st), :]], axis=0)
            else:
                vv = v_ref[0, rows(st), :]
            oa = _dot(pg[j, 0:CHUNK, 0:nk], jnp.where(khead0, vv, 1.0).astype(BF16))
            ob = _dot(pg[j, CHUNK:two, 0:nk], jnp.where(khead0, 1.0, vv).astype(BF16))
            obufs[c][rows(st), :] = jnp.where(head0, oa, ob)
            lbufs[c][rows(st), :] = jnp.where(head0, ob, oa)

    for c, (window, dil) in enumerate(DILATED_BRANCHES):
        assert window // dil == CHUNK
        nb = seq // (dil * CHUNK)
        span = dil * CHUNK
        if dil == 1:
            run_group(c, dil, [0], False)
            per = 5
            assert (nb - 1) % per == 0

            def band1(g, carry, c=c, dil=dil, per=per, span=span):
                base = span + g * (per * span)
                run_group(c, dil, [base + span * j for j in range(per)], True)
                return carry

            lax.fori_loop(0, (nb - 1) // per, band1, 0)
        elif nb > 1:
            run_group(c, dil, list(range(dil)), False)

            def band2(g, carry, c=c, dil=dil, nb=nb, span=span):
                starts = [2 * g + rr + span * b for rr in range(2) for b in range(1, nb)]
                run_group(c, dil, starts, True)
                return carry

            lax.fori_loop(0, dil // 2, band2, 0)
        else:
            def firsts(g, carry, c=c, dil=dil):
                run_group(c, dil, [ATTN_GROUP * g + j for j in range(ATTN_GROUP)], False)
                return carry

            lax.fori_loop(0, dil // ATTN_GROUP, firsts, 0)

    def combine(i, carry):
        sl = pl.ds(pl.multiple_of(i * 256, 256), 256)
        ma, mb, mc = m0[sl, :], m1[sl, :], m2[sl, :]
        mx = jnp.maximum(jnp.maximum(ma, mb), mc)
        wa, wb, wc = jnp.exp2(ma - mx), jnp.exp2(mb - mx), jnp.exp2(mc - mx)
        xa, xb, xc = x0[sl, :], x1[sl, :], x2[sl, :]
        xm = jnp.maximum(jnp.maximum(xa, xb), xc)
        den = (jnp.exp2(xa - xm) * l0[sl, :] + jnp.exp2(xb - xm) * l1[sl, :] + jnp.exp2(xc - xm) * l2[sl, :])
        den = pltpu.roll(den, HEAD_DIM_A, axis=1)
        out = (wa * o0[sl, :] + wb * o1[sl, :] + wc * o2[sl, :]) / den
        o_ref[0, sl, :] = out.astype(o_ref.dtype)
        return carry

    lax.fori_loop(0, seq // 256, combine, 0)


def _attention(proj3d):
    batch, seq, _ = proj3d.shape
    nblk = D_ATTN // V7X_LANES
    blk = (1, seq, V7X_LANES)
    scratch = ([pltpu.VMEM((seq, V7X_LANES), F32), pltpu.VMEM((seq, V7X_LANES), F32),
                pltpu.VMEM((2 * CHUNK, 2 * CHUNK), F32), pltpu.VMEM((2 * CHUNK, CHUNK), F32),
                pltpu.VMEM((ATTN_GROUP, 2 * CHUNK, 2 * CHUNK), F32),
                pltpu.VMEM((ATTN_GROUP, 2 * CHUNK, 2 * CHUNK), BF16)]
               + [pltpu.VMEM((seq, V7X_LANES), F32) for _ in range(12)])
    return pl.pallas_call(
        functools.partial(_attn_kernel, seq=seq),
        grid=(batch, nblk),
        in_specs=[
            pl.BlockSpec(blk, lambda b, g: (b, 0, g)),
            pl.BlockSpec(blk, lambda b, g: (b, 0, nblk + g)),
            pl.BlockSpec(blk, lambda b, g: (b, 0, 2 * nblk + g)),
        ],
        out_specs=pl.BlockSpec(blk, lambda b, g: (b, 0, g)),
        out_shape=jax.ShapeDtypeStruct((batch, seq, D_ATTN), BF16),
        scratch_shapes=scratch,
        compiler_params=_compiler_params(("arbitrary", "arbitrary")),
        name="dilated_attention",
    )(proj3d, proj3d, proj3d)


def _log_sigmoid(x):
    return jnp.minimum(x, 0.0) - jnp.log1p(jnp.exp(-jnp.abs(x)))


MLSTM_HEADS_PER_STEP = 2


def _mlstm_kernel(xm_ref, vm_ref, om_ref, gcol_ref, grow_ref, cw_ref, cb_ref, wq_ref, wk_ref,
                  bcol_ref, brow_ref, mg_ref, o_ref, *scratch, seq):
    nh = MLSTM_HEADS_PER_STEP
    per_head = len(scratch) // nh
    nchunk = seq // CHUNK
    ri = lax.broadcasted_iota(jnp.int32, (CHUNK, CHUNK), 0)
    ci = lax.broadcasted_iota(jnp.int32, (CHUNK, CHUNK), 1)
    tril = ri >= ci
    tril_b = jnp.where(tril, 1.0, 0.0).astype(BF16)
    triu_b = jnp.where(ri <= ci, 1.0, 0.0).astype(BF16)
    pad = V7X_SUBLANES
    qscale = HEAD_DIM_M ** -0.5

    class Head:
        def __init__(self, hh):
            (self.xpad, self.q_s, self.k_s, self.icol_s, self.lfcol_s, self.gcol_s,
             self.ri_s, self.rf_s, self.gf_s, self.mi_s, self.mo_s, self.gl_s) = scratch[hh * per_head:(hh + 1) * per_head]
            self.h = pl.program_id(1) * nh + hh
            self.cols = slice(hh * V7X_LANES, (hh + 1) * V7X_LANES)
            self.wq = wq_ref[hh]
            self.wk = wk_ref[hh]
            self.mg = mg_ref[:, self.cols]

    heads = [Head(hh) for hh in range(nh)]

    for hd in heads:
        hd.xpad[0:pad, :] = jnp.zeros((pad, V7X_LANES), F32)
        hd.xpad[pad:pad + seq, :] = xm_ref[0, :, hd.cols]

    def conv_step(i, carry):
        base = pl.multiple_of(i * 256, 256)
        rows = pl.ds(base, 256)
        gc = gcol_ref[0, rows, :] + bcol_ref[...]
        lane256 = lax.broadcasted_iota(jnp.int32, (256, V7X_LANES), 1)
        for hd in heads:
            y = jnp.broadcast_to(cb_ref[:, hd.cols], (256, V7X_LANES))
            xw = hd.xpad[pl.ds(base, 256 + pad), :]
            for j in range(CONV_WIDTH):
                off = pad - (CONV_WIDTH - 1) + j
                y = y + cw_ref[j:j + 1, hd.cols] * xw[off:off + 256, :]
            xc = (y * jax.nn.sigmoid(y)).astype(BF16)
            hd.q_s[rows, :] = _dot(xc, hd.wq) * qscale
            hd.k_s[rows, :] = _dot(xc, hd.wk)
            ic = jnp.sum(jnp.where(lane256 == hd.h, gc, 0.0), axis=1, keepdims=True)
            fc = jnp.sum(jnp.where(lane256 == hd.h + N_HEADS_M, gc, 0.0), axis=1, keepdims=True)
            hd.icol_s[rows, :] = jnp.broadcast_to(ic, (256, V7X_LANES))
            hd.lfcol_s[rows, :] = jnp.broadcast_to(_log_sigmoid(fc), (256, V7X_LANES))
        return carry

    lax.fori_loop(0, seq // 256, conv_step, 0)

    gr = grow_ref[0] + brow_ref[...]
    sub = lax.broadcasted_iota(jnp.int32, (V7X_SUBLANES, seq), 0)
    for hd in heads:
        irow = jnp.sum(jnp.where(sub == hd.h, gr, 0.0), axis=0, keepdims=True)
        lfrow = _log_sigmoid(jnp.sum(jnp.where(sub == hd.h + N_HEADS_M, gr, 0.0), axis=0, keepdims=True))
        for c in range(nchunk):
            hd.ri_s[c:c + 1, :] = irow[:, c * CHUNK:(c + 1) * CHUNK]
            hd.rf_s[c:c + 1, :] = lfrow[:, c * CHUNK:(c + 1) * CHUNK]

    def cum_step(c, carry):
        sl = pl.ds(pl.multiple_of(c * CHUNK, CHUNK), CHUNK)
        for hd in heads:
            hd.gcol_s[sl, :] = sum(_dot(tril_b, part) for part in _split3(hd.lfcol_s[sl, :]))
        return carry

    lax.fori_loop(0, nchunk, cum_step, 0, unroll=2)

    crow = lax.broadcasted_iota(jnp.int32, (nchunk, V7X_LANES), 0)
    for hd in heads:
        irows = hd.ri_s[...]
        grows = sum(_dot(part, triu_b) for part in _split3(hd.rf_s[...]))
        hd.gf_s[...] = grows
        g_last = grows[:, CHUNK - 1:CHUNK]
        a_max = jnp.max(g_last - grows + irows, axis=1, keepdims=True)
        m = jnp.zeros((1, 1), F32)
        m_in = jnp.zeros((nchunk, V7X_LANES), F32)
        m_out = jnp.zeros((nchunk, V7X_LANES), F32)
        for c in range(nchunk):
            m_in = jnp.where(crow == c, m, m_in)
            m = jnp.maximum(g_last[c:c + 1, :] + m, a_max[c:c + 1, :])
            m_out = jnp.where(crow == c, m, m_out)
        hd.mi_s[...] = m_in
        hd.mo_s[...] = m_out
        hd.gl_s[...] = jnp.broadcast_to(g_last, (nchunk, V7X_LANES))

    def chunk_head(hd, c, cmat, nrow):
        sl = pl.ds(pl.multiple_of(c * CHUNK, CHUNK), CHUNK)
        one = pl.ds(c, 1)
        qc = hd.q_s[sl, :]
        kc = hd.k_s[sl, :]
        vc = vm_ref[0, sl, hd.cols].astype(BF16)
        icol = hd.icol_s[sl, :]
        gcol = hd.gcol_s[sl, :]
        irow_c = hd.ri_s[one, :]
        grow_c = hd.gf_s[one, :]
        m = hd.mi_s[one, :]
        m_new = hd.mo_s[one, :]
        g_last = hd.gl_s[one, :]

        d = gcol - grow_c + irow_c
        d = jnp.where(tril, d, NEG_INF)
        inter = gcol + m
        m_t = jnp.maximum(inter, jnp.max(d, axis=1, keepdims=True))
        w_intra = jnp.exp(d - m_t)
        w_inter = jnp.exp(inter - m_t)
        qb = qc.astype(BF16)
        kb = kc.astype(BF16)
        qk = _dot_nt(qb, kb) * w_intra
        num = _dot(qk.astype(BF16), vc) + w_inter * _dot(qb, cmat.astype(BF16))
        den = jnp.sum(qk, axis=1, keepdims=True) + w_inter * jnp.sum(qc * nrow, axis=1, keepdims=True)
        hc = num / jnp.maximum(jnp.abs(den), jnp.exp(-m_t))

        a_col = g_last - gcol + icol
        decay = jnp.exp(g_last + m - m_new)
        wk_col = jnp.exp(a_col - m_new)
        kw = wk_col * kc
        c_new = decay * cmat + _dot(kw.T.astype(BF16), vc)
        n_new = decay * nrow + jnp.sum(kw, axis=0, keepdims=True)

        hn = hc * lax.rsqrt(jnp.mean(hc * hc, axis=1, keepdims=True) + RMS_EPS) * hd.mg
        hn = jax.nn.sigmoid(om_ref[0, sl, hd.cols]) * hn
        o_ref[0, sl, hd.cols] = hn.astype(o_ref.dtype)
        return c_new, n_new

    def chunk_step(c, carry):
        return tuple(chunk_head(hd, c, *carry[i]) for i, hd in enumerate(heads))

    init = tuple((jnp.zeros((HEAD_DIM_M, HEAD_DIM_M), F32), jnp.zeros((1, HEAD_DIM_M), F32)) for _ in heads)
    lax.fori_loop(0, nchunk, chunk_step, init, unroll=2)


def _mlstm(proj3d, gcol3d, grow3d, conv_w, conv_b, wq, wk, bias_col, bias_row, mnorm):
    batch, seq, _ = proj3d.shape
    nh = MLSTM_HEADS_PER_STEP
    width = nh * V7X_LANES
    blk = (1, seq, width)
    a0 = 3 * D_ATTN // width
    ng = N_HEADS_M // nh
    per_head = ([pltpu.VMEM((seq + V7X_SUBLANES, V7X_LANES), F32)]
                + [pltpu.VMEM((seq, V7X_LANES), F32) for _ in range(5)]
                + [pltpu.VMEM((seq // CHUNK, V7X_LANES), F32) for _ in range(6)])
    return pl.pallas_call(
        functools.partial(_mlstm_kernel, seq=seq),
        grid=(batch, ng),
        in_specs=[
            pl.BlockSpec(blk, lambda b, g: (b, 0, a0 + g)),
            pl.BlockSpec(blk, lambda b, g: (b, 0, a0 + ng + g)),
            pl.BlockSpec(blk, lambda b, g: (b, 0, a0 + 2 * ng + g)),
            pl.BlockSpec((1, seq, V7X_LANES), lambda b, g: (b, 0, 0)),
            pl.BlockSpec((1, V7X_SUBLANES, seq), lambda b, g: (b, 0, 0)),
            pl.BlockSpec((CONV_WIDTH, width), lambda b, g: (0, g)),
            pl.BlockSpec((1, width), lambda b, g: (0, g)),
            pl.BlockSpec((nh, HEAD_DIM_M, HEAD_DIM_M), lambda b, g: (g, 0, 0)),
            pl.BlockSpec((nh, HEAD_DIM_M, HEAD_DIM_M), lambda b, g: (g, 0, 0)),
            pl.BlockSpec((1, V7X_LANES), lambda b, g: (0, 0)),
            pl.BlockSpec((V7X_SUBLANES, seq), lambda b, g: (0, 0)),
            pl.BlockSpec((1, width), lambda b, g: (0, g)),
        ],
        out_specs=pl.BlockSpec(blk, lambda b, g: (b, 0, g)),
        out_shape=jax.ShapeDtypeStruct((batch, seq, D_MLSTM), BF16),
        scratch_shapes=per_head * nh,
        compiler_params=_compiler_params(("arbitrary", "arbitrary")),
        name="mlstm",
    )(proj3d, proj3d, proj3d, gcol3d, grow3d, conv_w, conv_b, wq, wk, bias_col, bias_row, mnorm)


def _layer_norm(z, g, b):
    mu = jnp.mean(z, axis=1, keepdims=True)
    zc = z - mu
    var = jnp.mean(zc * zc, axis=1, keepdims=True)
    return zc * lax.rsqrt(var + LN_EPS) * g + b


def _router_kernel(attn_ref, hm_ref, x_ref, woa_ref, wom_ref, g_ref, b_ref, wr_ref, br_ref,
                   h_ref, post_ref, col_ref, tab_ref, tot_ref, carry, *, nsub):
    i = pl.program_id(0)
    tm = TOK_TILE

    @pl.when(i == 0)
    def _():
        carry[...] = jnp.zeros_like(carry)

    ne = N_EXPERTS
    ex = lax.broadcasted_iota(jnp.int32, (ne, tm), 0).astype(F32)
    ri = lax.broadcasted_iota(jnp.int32, (tm, tm), 0)
    ci = lax.broadcasted_iota(jnp.int32, (tm, tm), 1)
    earlier = jnp.where(ri < ci, 1.0, 0.0).astype(BF16)
    er = lax.broadcasted_iota(jnp.int32, (ne, ne), 0)
    ec = lax.broadcasted_iota(jnp.int32, (ne, ne), 1)
    lower = jnp.where(ec < er, 1.0, 0.0).astype(BF16)
    diag = (lax.broadcasted_iota(jnp.int32, (ne, V7X_LANES), 0)
            == lax.broadcasted_iota(jnp.int32, (ne, V7X_LANES), 1))
    sub = lax.broadcasted_iota(jnp.int32, (V7X_SUBLANES, V7X_LANES), 0)
    base = carry[...]

    def to_lanes(colvec):
        return jnp.sum(jnp.where(diag, colvec, 0.0), axis=0, keepdims=True)

    for s in range(nsub):
        rs = slice(s * tm, (s + 1) * tm)
        h_ref[rs, :] = (DEEPNORM_ALPHA * x_ref[rs, :] + _dot(attn_ref[rs, :], woa_ref[...])
                        + _dot(hm_ref[rs, :], wom_ref[...]))
    for s in range(nsub):
        rs = slice(s * tm, (s + 1) * tm)
        hval = _layer_norm(h_ref[rs, :], g_ref[...], b_ref[...])
        h_ref[rs, :] = hval

        logits = _dot(hval.astype(BF16), wr_ref[...]) + br_ref[...]
        lt = logits.T[:ne, :]

        sel, vals = [], []
        for _ in range(TOP_K):
            mx = jnp.max(lt, axis=0, keepdims=True)
            idx = jnp.min(jnp.where(lt == mx, ex, float(ne)), axis=0, keepdims=True)
            hit = ex == idx
            lt = jnp.where(hit, NEG_INF, lt)
            sel.append(hit)
            vals.append(mx)
        exps = [jnp.exp(v - vals[0]) for v in vals]
        den = exps[0] + exps[1] + exps[2] + exps[3]
        gates = [e / den for e in exps]

        onehot = jnp.zeros((ne, tm), F32)
        for hit in sel:
            onehot = jnp.where(hit, 1.0, onehot)
        rank = _dot(onehot.astype(BF16), earlier)
        cnt = jnp.broadcast_to(jnp.sum(onehot, axis=1, keepdims=True), (ne, V7X_LANES))
        cnt_al = jnp.floor((cnt + (RUN_ALIGN - 1)) * (1.0 / RUN_ALIGN))
        slot = _dot(lower, cnt_al.astype(BF16)) * float(RUN_ALIGN)
        cnt_al = cnt_al * float(RUN_ALIGN)
        where_row = slot[:, 0:1] + rank

        rows = [jnp.sum(jnp.where(sel[k], where_row, 0.0), axis=0, keepdims=True) for k in range(TOP_K)]
        info = jnp.concatenate(rows + gates, axis=0)
        post_ref[s] = info
        col_ref[rs, :] = jnp.concatenate([info, jnp.zeros((V7X_LANES - 2 * TOP_K, tm), F32)], axis=0).T

        cnt_row = to_lanes(cnt_al)
        tab = jnp.zeros((V7X_SUBLANES, V7X_LANES), F32)
        tab = jnp.where(sub == 0, cnt_row, tab)
        tab = jnp.where(sub == 1, base, tab)
        tab = jnp.where(sub == 2, to_lanes(slot), tab)
        tab_ref[s] = tab.astype(jnp.int32)
        base = base + cnt_row

    carry[...] = base
    tot_ref[...] = jnp.broadcast_to(base, (V7X_SUBLANES, V7X_LANES)).astype(jnp.int32)


def _outproj_router(attn2d, hm2d, x2d, wo_a, wo_m, ln_g, ln_b, w_r, b_r):
    t = x2d.shape[0]
    nsub = ROUTER_SUBTILES
    tm = TOK_TILE * nsub
    nt = t // TOK_TILE
    const = lambda i: (0, 0)
    return pl.pallas_call(
        functools.partial(_router_kernel, nsub=nsub),
        grid=(t // tm,),
        in_specs=[
            pl.BlockSpec((tm, D_ATTN), lambda i: (i, 0)),
            pl.BlockSpec((tm, D_MLSTM), lambda i: (i, 0)),
            pl.BlockSpec((tm, D_MODEL), lambda i: (i, 0)),
            pl.BlockSpec((D_ATTN, D_MODEL), const),
            pl.BlockSpec((D_MLSTM, D_MODEL), const),
            pl.BlockSpec((1, D_MODEL), const),
            pl.BlockSpec((1, D_MODEL), const),
            pl.BlockSpec((D_MODEL, V7X_LANES), const),
            pl.BlockSpec((1, V7X_LANES), const),
        ],
        out_specs=[
            pl.BlockSpec((tm, D_MODEL), lambda i: (i, 0)),
            pl.BlockSpec((nsub, V7X_SUBLANES, TOK_TILE), lambda i: (i, 0, 0)),
            pl.BlockSpec((tm, V7X_LANES), lambda i: (i, 0)),
            pl.BlockSpec((nsub, V7X_SUBLANES, V7X_LANES), lambda i: (i, 0, 0)),
            pl.BlockSpec((V7X_SUBLANES, V7X_LANES), const),
        ],
        out_shape=[
            jax.ShapeDtypeStruct((t, D_MODEL), F32),
            jax.ShapeDtypeStruct((nt, V7X_SUBLANES, TOK_TILE), F32),
            jax.ShapeDtypeStruct((t, V7X_LANES), F32),
            jax.ShapeDtypeStruct((nt, V7X_SUBLANES, V7X_LANES), jnp.int32),
            jax.ShapeDtypeStruct((V7X_SUBLANES, V7X_LANES), jnp.int32),
        ],
        scratch_shapes=[pltpu.VMEM((1, V7X_LANES), F32)],
        compiler_params=_compiler_params(("arbitrary",)),
        name="outproj_router",
    )(attn2d, hm2d, x2d, wo_a, wo_m, ln_g, ln_b, w_r, b_r)


def _run_copy(src, dst, start_src, start_dst, n, sem):
    n = pl.multiple_of(n, RUN_ALIGN)
    return pltpu.make_async_copy(
        src.at[pl.ds(pl.multiple_of(start_src, RUN_ALIGN), n)],
        dst.at[pl.ds(pl.multiple_of(start_dst, RUN_ALIGN), n)],
        sem)


def _dispatch_kernel(cnt_sm, slot_sm, dst_sm, tail_sm, post_ref, h_ref, xs_hbm, ybuf, zbuf, sem, zsem):
    i = pl.program_id(0)
    nt = pl.num_programs(0)
    cur = i % 2

    n_blocks = xs_hbm.shape[0] // ROW_BLOCK

    def zero_fill(wait):
        def go(cp):
            if wait:
                cp.wait()
            else:
                cp.start()

        def tail(e, carry):
            n = tail_sm[N_EXPERTS + e]

            @pl.when(n > 0)
            def _():
                go(_run_copy(zbuf, xs_hbm, 0, tail_sm[e], n, zsem))
            return carry

        def block(j, carry):
            go(_run_copy(zbuf, xs_hbm, 0, j * ROW_BLOCK, ROW_BLOCK, zsem))
            return carry

        lax.fori_loop(0, N_EXPERTS, tail, 0)
        lax.fori_loop(tail_sm[2 * N_EXPERTS], n_blocks - 1, block, 0)

    @pl.when(i == 0)
    def _():
        for b in range(2):
            ybuf[b, ROWS_LOCAL:ROWS_LOCAL + RUN_ALIGN, :] = jnp.zeros((RUN_ALIGN, D_MODEL), F32)
        zbuf[...] = jnp.zeros_like(zbuf)
        zero_fill(False)

    hb = h_ref[...].astype(BF16)
    pos = post_ref[0]
    chunk = 256
    for rc in range(ROWS_LOCAL // chunk):
        rows = (lax.broadcasted_iota(jnp.int32, (chunk, TOK_TILE), 0) + rc * chunk).astype(F32)
        p = jnp.zeros((chunk, TOK_TILE), F32)
        for k in range(TOP_K):
            p = jnp.where(rows == pos[k:k + 1, :], 1.0, p)
        ybuf[cur, rc * chunk:(rc + 1) * chunk, :] = _dot(p.astype(BF16), hb)

    def total_rows(tile):
        return cnt_sm[nt * N_EXPERTS + tile]

    @pl.when(i > 0)
    def _():
        n_prev = total_rows(i - 1)
        _run_copy(ybuf.at[1 - cur], xs_hbm, 0, 0, n_prev, sem).wait()

    for e in range(N_EXPERTS):
        k = i * N_EXPERTS + e
        _run_copy(ybuf.at[cur], xs_hbm, slot_sm[k], dst_sm[k], cnt_sm[k], sem).start()

    @pl.when(i == nt - 1)
    def _():
        _run_copy(ybuf.at[cur], xs_hbm, 0, 0, total_rows(i), sem).wait()
        zero_fill(True)
        last = _run_copy(zbuf, xs_hbm, 0, (n_blocks - 1) * ROW_BLOCK, ROW_BLOCK, zsem)
        last.start()
        last.wait()


def _dispatch(cnt, slot, dst, tail, post, h2d, n_rows):
    t = h2d.shape[0]
    nt = t // TOK_TILE
    return pl.pallas_call(
        _dispatch_kernel,
        grid_spec=pltpu.PrefetchScalarGridSpec(
            num_scalar_prefetch=4,
            grid=(nt,),
            in_specs=[
                pl.BlockSpec((1, V7X_SUBLANES, TOK_TILE), lambda i, *_: (i, 0, 0)),
                pl.BlockSpec((TOK_TILE, D_MODEL), lambda i, *_: (i, 0)),
            ],
            out_specs=pl.BlockSpec(memory_space=pl.ANY),
            scratch_shapes=[
                pltpu.VMEM((2, ROWS_LOCAL + RUN_ALIGN, D_MODEL), F32),
                pltpu.VMEM((ROW_BLOCK, D_MODEL), F32),
                pltpu.SemaphoreType.DMA(()),
                pltpu.SemaphoreType.DMA(()),
            ],
        ),
        out_shape=jax.ShapeDtypeStruct((n_rows, D_MODEL), F32),
        compiler_params=_compiler_params(("arbitrary",)),
        name="moe_dispatch",
    )(cnt, slot, dst, tail, post, h2d)


def _expert_kernel(bexp_sm, nused_sm, nvalid_sm, xs_ref, wg_ref, bg_ref, wu_ref, bu_ref, wd_ref, bd_ref, ys_ref,
                   hbuf, wg_b, wu_b, wd_b):
    j = pl.program_id(0)
    used = j < nused_sm[0]
    new_expert = jnp.logical_or(j == 0, bexp_sm[j] != bexp_sm[jnp.maximum(j, 1) - 1])

    @pl.when(jnp.logical_and(used, new_expert))
    def _():
        def cast(i, carry):
            sl = pl.ds(pl.multiple_of(i * 128, 128), 128)
            wg_b[sl, :] = wg_ref[0, sl, :].astype(BF16)
            wu_b[sl, :] = wu_ref[0, sl, :].astype(BF16)
            wd_b[sl, :] = wd_ref[0, sl, :].astype(BF16)
            return carry
        lax.fori_loop(0, D_MODEL // 128, cast, 0)

    def mlp(rows):
        xb = xs_ref[0:rows, :].astype(BF16)
        step = 512
        for n in range(0, D_MODEL, step):
            g = _dot(xb, wg_b[:, n:n + step]) + bg_ref[0, :, n:n + step]
            u = _dot(xb, wu_b[:, n:n + step]) + bu_ref[0, :, n:n + step]
            g = jnp.minimum(g, SWIGLU_LIMIT)
            u = jnp.clip(u, -SWIGLU_LIMIT, SWIGLU_LIMIT)
            hbuf[0:rows, n:n + step] = (g * jax.nn.sigmoid(SWIGLU_ALPHA * g) * (u + 1.0)).astype(BF16)
        ys_ref[0:rows, :] = _dot(hbuf[0:rows, :], wd_b[...]) + bd_ref[0]

    half = ROW_BLOCK // 2
    upper_rows = nvalid_sm[j] > half

    @pl.when(jnp.logical_and(used, upper_rows))
    def _():
        mlp(ROW_BLOCK)

    @pl.when(jnp.logical_and(used, jnp.logical_not(upper_rows)))
    def _():
        mlp(half)
        ys_ref[half:ROW_BLOCK, :] = jnp.zeros((ROW_BLOCK - half, D_MODEL), F32)


def _experts(bexp, nused, nvalid, xs, wg, bg, wu, bu, wd, bd):
    n_rows = xs.shape[0]
    nb = n_rows // ROW_BLOCK

    def row_map(j, be, nu, nv):
        return (jnp.minimum(j, nu[0] - 1), 0)

    def w_map(j, be, nu, nv):
        return (be[jnp.minimum(j, nu[0] - 1)], 0, 0)

    wspec = pl.BlockSpec((1, D_MODEL, D_MODEL), w_map)
    bspec = pl.BlockSpec((1, 1, D_MODEL), w_map)
    return pl.pallas_call(
        _expert_kernel,
        grid_spec=pltpu.PrefetchScalarGridSpec(
            num_scalar_prefetch=3,
            grid=(nb,),
            in_specs=[pl.BlockSpec((ROW_BLOCK, D_MODEL), row_map), wspec, bspec, wspec, bspec, wspec, bspec],
            out_specs=pl.BlockSpec((ROW_BLOCK, D_MODEL), row_map),
            scratch_shapes=[pltpu.VMEM((ROW_BLOCK, D_MODEL), BF16)]
            + [pltpu.VMEM((D_MODEL, D_MODEL), BF16) for _ in range(3)],
        ),
        out_shape=jax.ShapeDtypeStruct((n_rows, D_MODEL), F32),
        input_output_aliases={3: 0},
        compiler_params=_compiler_params(("arbitrary",)),
        name="moe_experts",
    )(bexp, nused, nvalid, xs, wg, bg, wu, bu, wd, bd)


def _combine_kernel(cnt_sm, slot_sm, dst_sm, col_ref, h_ref, g_ref, b_ref, ys_hbm, o_ref, ybuf, sbuf, sem):
    i = pl.program_id(0)
    nt = pl.num_programs(0)
    cur = i % 2

    def fetch(tile, buf):
        for e in range(N_EXPERTS):
            k = tile * N_EXPERTS + e
            _run_copy(ys_hbm, ybuf.at[buf], dst_sm[k], slot_sm[k], cnt_sm[k], sem.at[buf]).start()

    @pl.when(i == 0)
    def _():
        ybuf[...] = jnp.zeros_like(ybuf)
        fetch(0, 0)

    @pl.when(i + 1 < nt)
    def _():
        fetch(i + 1, 1 - cur)

    _run_copy(ys_hbm, ybuf.at[cur], 0, 0, cnt_sm[nt * N_EXPERTS + i], sem.at[cur]).wait()

    col = col_ref[...]
    chunk = 256
    for rc in range(ROWS_LOCAL // chunk):
        rows = (lax.broadcasted_iota(jnp.int32, (TOK_TILE, chunk), 1) + rc * chunk).astype(F32)
        s = jnp.zeros((TOK_TILE, chunk), F32)
        for k in range(TOP_K):
            s = jnp.where(rows == col[:, k:k + 1], col[:, TOP_K + k:TOP_K + k + 1], s)
        sbuf[:, rc * chunk:(rc + 1) * chunk] = s.astype(BF16)
    for n in range(0, D_MODEL, chunk):
        moe = _dot(sbuf[...], ybuf[cur, 0:ROWS_LOCAL, n:n + chunk].astype(BF16))
        o_ref[:, n:n + chunk] = DEEPNORM_ALPHA * h_ref[:, n:n + chunk] + moe
    o_ref[...] = _layer_norm(o_ref[...], g_ref[...], b_ref[...])


def _combine(cnt, slot, dst, col, h2d, ln_g, ln_b, ys):
    t = h2d.shape[0]
    nt = t // TOK_TILE
    return pl.pallas_call(
        _combine_kernel,
        grid_spec=pltpu.PrefetchScalarGridSpec(
            num_scalar_prefetch=3,
            grid=(nt,),
            in_specs=[
                pl.BlockSpec((TOK_TILE, V7X_LANES), lambda i, *_: (i, 0)),
                pl.BlockSpec((TOK_TILE, D_MODEL), lambda i, *_: (i, 0)),
                pl.BlockSpec((1, D_MODEL), lambda i, *_: (0, 0)),
                pl.BlockSpec((1, D_MODEL), lambda i, *_: (0, 0)),
                pl.BlockSpec(memory_space=pl.ANY),
            ],
            out_specs=pl.BlockSpec((TOK_TILE, D_MODEL), lambda i, *_: (i, 0)),
            scratch_shapes=[
                pltpu.VMEM((2, ROWS_LOCAL + RUN_ALIGN, D_MODEL), F32),
                pltpu.VMEM((TOK_TILE, ROWS_LOCAL), BF16),
                pltpu.SemaphoreType.DMA((2,)),
            ],
        ),
        out_shape=jax.ShapeDtypeStruct((t, D_MODEL), F32),
        compiler_params=_compiler_params(("arbitrary",)),
        name="moe_combine",
    )(cnt, slot, dst, col, h2d, ln_g, ln_b, ys)


def _pad_lanes(a, width=V7X_LANES):
    return jnp.pad(a, ((0, 0), (0, width - a.shape[1])))


def _layer(h3d, w_in, conv_w, conv_b, w_mq, w_mk, b_igate, b_fgate, mnorm_g, w_out,
           ln1_g, ln1_b, w_router, b_router, w_gate, b_gate, w_up, b_up, w_down, b_down, ln2_g, ln2_b):
    batch, seq, _ = h3d.shape
    t = batch * seq
    x2d = h3d.reshape(t, D_MODEL)

    w_main = w_in[:, :PROJ_MAIN].astype(BF16)
    w_gates = _pad_lanes(w_in[:, PROJ_MAIN:]).astype(BF16)
    proj, gcol, grow = _inproj(x2d, w_main, w_gates, batch, seq)
    proj3d = proj.reshape(batch, seq, PROJ_MAIN)
    attn = _attention(proj3d)
    gate_bias = jnp.concatenate([b_igate, b_fgate]).astype(F32)
    bias_col = _pad_lanes(gate_bias[None, :])
    bias_row = jnp.broadcast_to(gate_bias[:, None], (V7X_SUBLANES, seq))
    hm = _mlstm(proj3d, gcol.reshape(batch, seq, V7X_LANES), grow, conv_w, conv_b[None, :],
                w_mq.astype(BF16), w_mk.astype(BF16), bias_col, bias_row, mnorm_g[None, :])

    wo = w_out.astype(BF16)
    h2d, post, col, tab, tot = _outproj_router(
        attn.reshape(t, D_ATTN), hm.reshape(t, D_MLSTM), x2d, wo[:D_ATTN], wo[D_ATTN:],
        ln1_g[None, :], ln1_b[None, :], _pad_lanes(w_router).astype(BF16), _pad_lanes(b_router[None, :]))

    nt = t // TOK_TILE
    total = tot[0, :N_EXPERTS]
    region = (total + ROW_BLOCK - 1) // ROW_BLOCK * ROW_BLOCK
    region_end = jnp.cumsum(region)
    region_start = region_end - region
    n_rows = (t * TOP_K + nt * N_EXPERTS * (RUN_ALIGN - 1)) // ROW_BLOCK * ROW_BLOCK + (N_EXPERTS + 1) * ROW_BLOCK
    nb = n_rows // ROW_BLOCK
    runs = tab[:, 0, :N_EXPERTS]
    empty = runs == 0
    cnt = jnp.where(empty, RUN_ALIGN, runs)
    cnt = jnp.concatenate([cnt.reshape(-1), jnp.sum(cnt, axis=1)])
    slot = jnp.where(empty, ROWS_LOCAL, tab[:, 2, :N_EXPERTS]).reshape(-1)
    dst = jnp.where(empty, n_rows - ROW_BLOCK, tab[:, 1, :N_EXPERTS] + region_start[None, :]).reshape(-1)
    tail = jnp.concatenate([region_start + total, region - total, region_end[-1:] // ROW_BLOCK]).astype(jnp.int32)
    block_row = jnp.arange(nb, dtype=jnp.int32) * ROW_BLOCK
    bexp = jnp.minimum(jnp.sum(region_end[None, :] <= block_row[:, None], axis=1), N_EXPERTS - 1).astype(jnp.int32)
    nused = (region_end[-1:] // ROW_BLOCK).astype(jnp.int32)
    nvalid = jnp.clip((region_start + total)[bexp] - block_row, 0, ROW_BLOCK).astype(jnp.int32)

    xs = _dispatch(cnt, slot, dst, tail, post, h2d, n_rows)
    ys = _experts(bexp, nused, nvalid, xs, w_gate, b_gate[:, None, :], w_up, b_up[:, None, :], w_down, b_down[:, None, :])
    out = _combine(cnt, slot, dst, col, h2d, ln2_g[None, :], ln2_b[None, :], ys)
    return out.reshape(batch, seq, D_MODEL)


def kernel(x, w_in, conv_w, conv_b, w_mq, w_mk, b_igate, b_fgate, mnorm_g, w_out, ln1_g, ln1_b, w_router, b_router, w_gate, b_gate, w_up, b_up, w_down, b_down, ln2_g, ln2_b):
    h = x
    for l in range(w_in.shape[0]):
        h = _layer(h, w_in[l], conv_w[l], conv_b[l], w_mq[l], w_mk[l], b_igate[l], b_fgate[l], mnorm_g[l],
                   w_out[l], ln1_g[l], ln1_b[l], w_router[l], b_router[l], w_gate[l], b_gate[l], w_up[l],
                   b_up[l], w_down[l], b_down[l], ln2_g[l], ln2_b[l])
    return h
```

```python
import functools
import math

import jax
import jax.numpy as jnp
from jax import lax
from jax.experimental import pallas as pl
from jax.experimental.pallas import tpu as pltpu

F32 = jnp.float32
BF16 = jnp.bfloat16
NEG_INF = float("-inf")

V7X_LANES = 128
V7X_SUBLANES = 8
V7X_VMEM_LIMIT_BYTES = 56 * 1024 * 1024

D_MODEL = 1024
D_ATTN = 512
HEAD_DIM_A = 64
D_MLSTM = 512
N_HEADS_M = 4
HEAD_DIM_M = 128
CONV_WIDTH = 4
CHUNK = 128
DILATED_BRANCHES = ((128, 1), (512, 4), (2048, 16))
N_EXPERTS = 32
TOP_K = 4
SWIGLU_LIMIT = 7.0
SWIGLU_ALPHA = 1.702
DEEPNORM_ALPHA = 2.0 ** 0.25
LN_EPS = 1e-5
RMS_EPS = 1e-6

PROJ_MAIN = 3 * D_ATTN + 3 * D_MLSTM
TOK_TILE = 256
RUN_ALIGN = V7X_SUBLANES
ROWS_LOCAL = TOK_TILE * TOP_K + N_EXPERTS * RUN_ALIGN
ROW_BLOCK = 512
ROUTER_SUBTILES = 4


def _dot(a, b):
    return jnp.dot(a, b, preferred_element_type=F32)


def _dot_nt(a, b):
    return lax.dot_general(a, b, (((1,), (1,)), ((), ())), preferred_element_type=F32)


def _split3(x):
    hi = x.astype(BF16)
    r1 = x - hi.astype(F32)
    mid = r1.astype(BF16)
    lo = (r1 - mid.astype(F32)).astype(BF16)
    return hi, mid, lo


def _compiler_params(sem):
    return pltpu.CompilerParams(dimension_semantics=sem, vmem_limit_bytes=V7X_VMEM_LIMIT_BYTES)


def _inproj_kernel(x_ref, w_ref, wg_ref, proj_ref, gcol_ref, grow_ref):
    xb = x_ref[...].astype(BF16)
    step = 512
    for n in range(0, PROJ_MAIN, step):
        proj_ref[:, n:n + step] = _dot(xb, w_ref[:, n:n + step])
    g = _dot(xb, wg_ref[...])
    gcol_ref[...] = g
    grow_ref[0] = g.T[:V7X_SUBLANES, :]


def _inproj(x2d, w_main, w_gates, batch, seq):
    t = x2d.shape[0]
    tm = 512
    per_b = seq // tm
    return pl.pallas_call(
        _inproj_kernel,
        grid=(t // tm,),
        in_specs=[
            pl.BlockSpec((tm, D_MODEL), lambda i: (i, 0)),
            pl.BlockSpec((D_MODEL, PROJ_MAIN), lambda i: (0, 0)),
            pl.BlockSpec((D_MODEL, V7X_LANES), lambda i: (0, 0)),
        ],
        out_specs=[
            pl.BlockSpec((tm, PROJ_MAIN), lambda i: (i, 0)),
            pl.BlockSpec((tm, V7X_LANES), lambda i: (i, 0)),
            pl.BlockSpec((1, V7X_SUBLANES, tm), lambda i: (i // per_b, 0, i % per_b)),
        ],
        out_shape=[
            jax.ShapeDtypeStruct((t, PROJ_MAIN), F32),
            jax.ShapeDtypeStruct((t, V7X_LANES), F32),
            jax.ShapeDtypeStruct((batch, V7X_SUBLANES, seq), F32),
        ],
        compiler_params=_compiler_params(("arbitrary",)),
        name="inproj",
    )(x2d, w_main, w_gates)


ATTN_GROUP = 8


def _attn_kernel(q_ref, k_ref, v_ref, o_ref, q0s, q1s, bias_b, bias_f, sg, pg,
                 o0, o1, o2, l0, l1, l2, m0, m1, m2, x0, x1, x2, *, seq):
    obufs, lbufs, mbufs, xbufs = (o0, o1, o2), (l0, l1, l2), (m0, m1, m2), (x0, x1, x2)
    two = 2 * CHUNK
    head0 = lax.broadcasted_iota(jnp.int32, (CHUNK, V7X_LANES), 1) < HEAD_DIM_A
    qscale = HEAD_DIM_A ** -0.5 * math.log2(math.e)

    def prep(i, carry):
        sl = pl.ds(pl.multiple_of(i * two, two), two)
        h0 = lax.broadcasted_iota(jnp.int32, (two, V7X_LANES), 1) < HEAD_DIM_A
        q = q_ref[0, sl, :] * qscale
        q0s[sl, :] = jnp.where(h0, q, 0.0)
        q1s[sl, :] = jnp.where(h0, 0.0, q)
        return carry

    lax.fori_loop(0, seq // two, prep, 0)

    qi = lax.broadcasted_iota(jnp.int32, (two, two), 0) % CHUNK
    kj = lax.broadcasted_iota(jnp.int32, (two, two), 1)
    bias_b[...] = jnp.where((kj >= qi) & (kj <= qi + CHUNK), 0.0, NEG_INF)
    qf = lax.broadcasted_iota(jnp.int32, (two, CHUNK), 0) % CHUNK
    kf = lax.broadcasted_iota(jnp.int32, (two, CHUNK), 1)
    bias_f[...] = jnp.where(kf <= qf, 0.0, NEG_INF)

    def run_group(c, dil, starts, has_prev):
        assert len(starts) <= ATTN_GROUP
        nk = two if has_prev else CHUNK
        bias_ref = bias_b if has_prev else bias_f

        def rows(s0):
            return pl.ds(s0, CHUNK) if dil == 1 else pl.ds(s0, CHUNK, stride=dil)

        def keys(ref, st):
            if has_prev:
                return jnp.concatenate([ref[0, rows(st - dil * CHUNK), :], ref[0, rows(st), :]], axis=0).astype(BF16)
            return ref[0, rows(st), :].astype(BF16)

        def both_heads(x):
            return jnp.where(head0, x[:CHUNK], x[CHUNK:])

        for j, st in enumerate(starts):
            q2 = jnp.concatenate([q0s[rows(st), :], q1s[rows(st), :]], axis=0).astype(BF16)
            sg[j, :, 0:nk] = _dot_nt(q2, keys(k_ref, st)) + bias_ref[...]
        for j, st in enumerate(starts):
            s = sg[j, :, 0:nk]
            m = jnp.max(s, axis=1, keepdims=True)
            pg[j, :, 0:nk] = jnp.exp2(s - m).astype(BF16)
            ma = jnp.broadcast_to(m[:CHUNK], (CHUNK, V7X_LANES))
            mb = jnp.broadcast_to(m[CHUNK:], (CHUNK, V7X_LANES))
            mbufs[c][rows(st), :] = jnp.where(head0, ma, mb)
            xbufs[c][rows(st), :] = jnp.where(head0, mb, ma)
        khead0 = lax.broadcasted_iota(jnp.int32, (nk, V7X_LANES), 1) < HEAD_DIM_A
        for j, st in enumerate(starts):
            if has_prev:
                vv = jnp.concatenate([v_ref[0, rows(st - dil * CHUNK), :], v_ref[0, rows(st), :]], axis=0)
            else:
                vv = v_ref[0, rows(st), :]
            oa = _dot(pg[j, 0:CHUNK, 0:nk], jnp.where(khead0, vv, 1.0).astype(BF16))
            ob = _dot(pg[j, CHUNK:two, 0:nk], jnp.where(khead0, 1.0, vv).astype(BF16))
            obufs[c][rows(st), :] = jnp.where(head0, oa, ob)
            lbufs[c][rows(st), :] = jnp.where(head0, ob, oa)

    for c, (window, dil) in enumerate(DILATED_BRANCHES):
        assert window // dil == CHUNK
        nb = seq // (dil * CHUNK)
        span = dil * CHUNK
        if dil == 1:
            run_group(c, dil, [0], False)
            per = 5
            assert (nb - 1) % per == 0

            def band1(g, carry, c=c, dil=dil, per=per, span=span):
                base = span + g * (per * span)
                run_group(c, dil, [base + span * j for j in range(per)], True)
                return carry

            lax.fori_loop(0, (nb - 1) // per, band1, 0, unroll=True)
        elif nb > 1:
            run_group(c, dil, list(range(dil)), False)

            def band2(g, carry, c=c, dil=dil, nb=nb, span=span):
                starts = [2 * g + rr + span * b for rr in range(2) for b in range(1, nb)]
                run_group(c, dil, starts, True)
                return carry

            lax.fori_loop(0, dil // 2, band2, 0, unroll=True)
        else:
            def firsts(g, carry, c=c, dil=dil):
                run_group(c, dil, [ATTN_GROUP * g + j for j in range(ATTN_GROUP)], False)
                return carry

            lax.fori_loop(0, dil // ATTN_GROUP, firsts, 0)

    def combine(i, carry):
        sl = pl.ds(pl.multiple_of(i * 256, 256), 256)
        ma, mb, mc = m0[sl, :], m1[sl, :], m2[sl, :]
        mx = jnp.maximum(jnp.maximum(ma, mb), mc)
        wa, wb, wc = jnp.exp2(ma - mx), jnp.exp2(mb - mx), jnp.exp2(mc - mx)
        xa, xb, xc = x0[sl, :], x1[sl, :], x2[sl, :]
        xm = jnp.maximum(jnp.maximum(xa, xb), xc)
        den = (jnp.exp2(xa - xm) * l0[sl, :] + jnp.exp2(xb - xm) * l1[sl, :] + jnp.exp2(xc - xm) * l2[sl, :])
        den = pltpu.roll(den, HEAD_DIM_A, axis=1)
        out = (wa * o0[sl, :] + wb * o1[sl, :] + wc * o2[sl, :]) / den
        o_ref[0, sl, :] = out.astype(o_ref.dtype)
        return carry

    lax.fori_loop(0, seq // 256, combine, 0)


def _attention(proj3d):
    batch, seq, _ = proj3d.shape
    nblk = D_ATTN // V7X_LANES
    blk = (1, seq, V7X_LANES)
    scratch = ([pltpu.VMEM((seq, V7X_LANES), F32), pltpu.VMEM((seq, V7X_LANES), F32),
                pltpu.VMEM((2 * CHUNK, 2 * CHUNK), F32), pltpu.VMEM((2 * CHUNK, CHUNK), F32),
                pltpu.VMEM((ATTN_GROUP, 2 * CHUNK, 2 * CHUNK), F32),
                pltpu.VMEM((ATTN_GROUP, 2 * CHUNK, 2 * CHUNK), BF16)]
               + [pltpu.VMEM((seq, V7X_LANES), F32) for _ in range(12)])
    return pl.pallas_call(
        functools.partial(_attn_kernel, seq=seq),
        grid=(batch, nblk),
        in_specs=[
            pl.BlockSpec(blk, lambda b, g: (b, 0, g)),
            pl.BlockSpec(blk, lambda b, g: (b, 0, nblk + g)),
            pl.BlockSpec(blk, lambda b, g: (b, 0, 2 * nblk + g)),
        ],
        out_specs=pl.BlockSpec(blk, lambda b, g: (b, 0, g)),
        out_shape=jax.ShapeDtypeStruct((batch, seq, D_ATTN), BF16),
        scratch_shapes=scratch,
        compiler_params=_compiler_params(("arbitrary", "arbitrary")),
        name="dilated_attention",
    )(proj3d, proj3d, proj3d)


def _log_sigmoid(x):
    return jnp.minimum(x, 0.0) - jnp.log1p(jnp.exp(-jnp.abs(x)))


MLSTM_HEADS_PER_STEP = 2


def _mlstm_kernel(xm_ref, vm_ref, om_ref, gcol_ref, grow_ref, cw_ref, cb_ref, wq_ref, wk_ref,
                  bcol_ref, brow_ref, mg_ref, o_ref, *scratch, seq):
    nh = MLSTM_HEADS_PER_STEP
    per_head = len(scratch) // nh
    nchunk = seq // CHUNK
    ri = lax.broadcasted_iota(jnp.int32, (CHUNK, CHUNK), 0)
    ci = lax.broadcasted_iota(jnp.int32, (CHUNK, CHUNK), 1)
    tril = ri >= ci
    tril_b = jnp.where(tril, 1.0, 0.0).astype(BF16)
    triu_b = jnp.where(ri <= ci, 1.0, 0.0).astype(BF16)
    pad = V7X_SUBLANES
    qscale = HEAD_DIM_M ** -0.5

    class Head:
        def __init__(self, hh):
            (self.xpad, self.q_s, self.k_s, self.icol_s, self.lfcol_s, self.gcol_s,
             self.ri_s, self.rf_s, self.gf_s, self.mi_s, self.mo_s, self.gl_s) = scratch[hh * per_head:(hh + 1) * per_head]
            self.h = pl.program_id(1) * nh + hh
            self.cols = slice(hh * V7X_LANES, (hh + 1) * V7X_LANES)
            self.wq = wq_ref[hh]
            self.wk = wk_ref[hh]
            self.mg = mg_ref[:, self.cols]

    heads = [Head(hh) for hh in range(nh)]

    for hd in heads:
        hd.xpad[0:pad, :] = jnp.zeros((pad, V7X_LANES), F32)
        hd.xpad[pad:pad + seq, :] = xm_ref[0, :, hd.cols]

    def conv_step(i, carry):
        base = pl.multiple_of(i * 256, 256)
        rows = pl.ds(base, 256)
        gc = gcol_ref[0, rows, :] + bcol_ref[...]
        lane256 = lax.broadcasted_iota(jnp.int32, (256, V7X_LANES), 1)
        for hd in heads:
            y = jnp.broadcast_to(cb_ref[:, hd.cols], (256, V7X_LANES))
            xw = hd.xpad[pl.ds(base, 256 + pad), :]
            for j in range(CONV_WIDTH):
                off = pad - (CONV_WIDTH - 1) + j
                y = y + cw_ref[j:j + 1, hd.cols] * xw[off:off + 256, :]
            xc = (y * jax.nn.sigmoid(y)).astype(BF16)
            hd.q_s[rows, :] = _dot(xc, hd.wq) * qscale
            hd.k_s[rows, :] = _dot(xc, hd.wk)
            ic = jnp.sum(jnp.where(lane256 == hd.h, gc, 0.0), axis=1, keepdims=True)
            fc = jnp.sum(jnp.where(lane256 == hd.h + N_HEADS_M, gc, 0.0), axis=1, keepdims=True)
            hd.icol_s[rows, :] = jnp.broadcast_to(ic, (256, V7X_LANES))
            hd.lfcol_s[rows, :] = jnp.broadcast_to(_log_sigmoid(fc), (256, V7X_LANES))
        return carry

    lax.fori_loop(0, seq // 256, conv_step, 0)

    gr = grow_ref[0] + brow_ref[...]
    sub = lax.broadcasted_iota(jnp.int32, (V7X_SUBLANES, seq), 0)
    for hd in heads:
        irow = jnp.sum(jnp.where(sub == hd.h, gr, 0.0), axis=0, keepdims=True)
        lfrow = _log_sigmoid(jnp.sum(jnp.where(sub == hd.h + N_HEADS_M, gr, 0.0), axis=0, keepdims=True))
        for c in range(nchunk):
            hd.ri_s[c:c + 1, :] = irow[:, c * CHUNK:(c + 1) * CHUNK]
            hd.rf_s[c:c + 1, :] = lfrow[:, c * CHUNK:(c + 1) * CHUNK]

    def cum_step(c, carry):
        sl = pl.ds(pl.multiple_of(c * CHUNK, CHUNK), CHUNK)
        for hd in heads:
            hd.gcol_s[sl, :] = sum(_dot(tril_b, part) for part in _split3(hd.lfcol_s[sl, :]))
        return carry

    lax.fori_loop(0, nchunk, cum_step, 0, unroll=2)

    crow = lax.broadcasted_iota(jnp.int32, (nchunk, V7X_LANES), 0)
    for hd in heads:
        irows = hd.ri_s[...]
        grows = sum(_dot(part, triu_b) for part in _split3(hd.rf_s[...]))
        hd.gf_s[...] = grows
        g_last = grows[:, CHUNK - 1:CHUNK]
        a_max = jnp.max(g_last - grows + irows, axis=1, keepdims=True)
        m = jnp.zeros((1, 1), F32)
        m_in = jnp.zeros((nchunk, V7X_LANES), F32)
        m_out = jnp.zeros((nchunk, V7X_LANES), F32)
        for c in range(nchunk):
            m_in = jnp.where(crow == c, m, m_in)
            m = jnp.maximum(g_last[c:c + 1, :] + m, a_max[c:c + 1, :])
            m_out = jnp.where(crow == c, m, m_out)
        hd.mi_s[...] = m_in
        hd.mo_s[...] = m_out
        hd.gl_s[...] = jnp.broadcast_to(g_last, (nchunk, V7X_LANES))

    def chunk_head(hd, c, cmat, nrow):
        sl = pl.ds(pl.multiple_of(c * CHUNK, CHUNK), CHUNK)
        one = pl.ds(c, 1)
        qc = hd.q_s[sl, :]
        kc = hd.k_s[sl, :]
        vc = vm_ref[0, sl, hd.cols].astype(BF16)
        icol = hd.icol_s[sl, :]
        gcol = hd.gcol_s[sl, :]
        irow_c = hd.ri_s[one, :]
        grow_c = hd.gf_s[one, :]
        m = hd.mi_s[one, :]
        m_new = hd.mo_s[one, :]
        g_last = hd.gl_s[one, :]

        d = gcol - grow_c + irow_c
        d = jnp.where(tril, d, NEG_INF)
        inter = gcol + m
        m_t = jnp.maximum(inter, jnp.max(d, axis=1, keepdims=True))
        w_intra = jnp.exp(d - m_t)
        w_inter = jnp.exp(inter - m_t)
        qb = qc.astype(BF16)
        kb = kc.astype(BF16)
        qk = _dot_nt(qb, kb) * w_intra
        num = _dot(qk.astype(BF16), vc) + w_inter * _dot(qb, cmat.astype(BF16))
        den = jnp.sum(qk, axis=1, keepdims=True) + w_inter * jnp.sum(qc * nrow, axis=1, keepdims=True)
        hc = num / jnp.maximum(jnp.abs(den), jnp.exp(-m_t))

        a_col = g_last - gcol + icol
        decay = jnp.exp(g_last + m - m_new)
        wk_col = jnp.exp(a_col - m_new)
        kw = wk_col * kc
        c_new = decay * cmat + _dot(kw.T.astype(BF16), vc)
        n_new = decay * nrow + jnp.sum(kw, axis=0, keepdims=True)

        hn = hc * lax.rsqrt(jnp.mean(hc * hc, axis=1, keepdims=True) + RMS_EPS) * hd.mg
        hn = jax.nn.sigmoid(om_ref[0, sl, hd.cols]) * hn
        o_ref[0, sl, hd.cols] = hn.astype(o_ref.dtype)
        return c_new, n_new

    def chunk_step(c, carry):
        return tuple(chunk_head(hd, c, *carry[i]) for i, hd in enumerate(heads))

    init = tuple((jnp.zeros((HEAD_DIM_M, HEAD_DIM_M), F32), jnp.zeros((1, HEAD_DIM_M), F32)) for _ in heads)
    lax.fori_loop(0, nchunk, chunk_step, init, unroll=2)


def _mlstm(proj3d, gcol3d, grow3d, conv_w, conv_b, wq, wk, bias_col, bias_row, mnorm):
    batch, seq, _ = proj3d.shape
    nh = MLSTM_HEADS_PER_STEP
    width = nh * V7X_LANES
    blk = (1, seq, width)
    a0 = 3 * D_ATTN // width
    ng = N_HEADS_M // nh
    per_head = ([pltpu.VMEM((seq + V7X_SUBLANES, V7X_LANES), F32)]
                + [pltpu.VMEM((seq, V7X_LANES), F32) for _ in range(5)]
                + [pltpu.VMEM((seq // CHUNK, V7X_LANES), F32) for _ in range(6)])
    return pl.pallas_call(
        functools.partial(_mlstm_kernel, seq=seq),
        grid=(batch, ng),
        in_specs=[
            pl.BlockSpec(blk, lambda b, g: (b, 0, a0 + g)),
            pl.BlockSpec(blk, lambda b, g: (b, 0, a0 + ng + g)),
            pl.BlockSpec(blk, lambda b, g: (b, 0, a0 + 2 * ng + g)),
            pl.BlockSpec((1, seq, V7X_LANES), lambda b, g: (b, 0, 0)),
            pl.BlockSpec((1, V7X_SUBLANES, seq), lambda b, g: (b, 0, 0)),
            pl.BlockSpec((CONV_WIDTH, width), lambda b, g: (0, g)),
            pl.BlockSpec((1, width), lambda b, g: (0, g)),
            pl.BlockSpec((nh, HEAD_DIM_M, HEAD_DIM_M), lambda b, g: (g, 0, 0)),
            pl.BlockSpec((nh, HEAD_DIM_M, HEAD_DIM_M), lambda b, g: (g, 0, 0)),
            pl.BlockSpec((1, V7X_LANES), lambda b, g: (0, 0)),
            pl.BlockSpec((V7X_SUBLANES, seq), lambda b, g: (0, 0)),
            pl.BlockSpec((1, width), lambda b, g: (0, g)),
        ],
        out_specs=pl.BlockSpec(blk, lambda b, g: (b, 0, g)),
        out_shape=jax.ShapeDtypeStruct((batch, seq, D_MLSTM), BF16),
        scratch_shapes=per_head * nh,
        compiler_params=_compiler_params(("arbitrary", "arbitrary")),
        name="mlstm",
    )(proj3d, proj3d, proj3d, gcol3d, grow3d, conv_w, conv_b, wq, wk, bias_col, bias_row, mnorm)


def _layer_norm(z, g, b):
    mu = jnp.mean(z, axis=1, keepdims=True)
    zc = z - mu
    var = jnp.mean(zc * zc, axis=1, keepdims=True)
    return zc * lax.rsqrt(var + LN_EPS) * g + b


def _router_kernel(attn_ref, hm_ref, x_ref, woa_ref, wom_ref, g_ref, b_ref, wr_ref, br_ref,
                   h_ref, post_ref, col_ref, tab_ref, tot_ref, carry, *, nsub):
    i = pl.program_id(0)
    tm = TOK_TILE

    @pl.when(i == 0)
    def _():
        carry[...] = jnp.zeros_like(carry)

    ne = N_EXPERTS
    ex = lax.broadcasted_iota(jnp.int32, (ne, tm), 0).astype(F32)
    ri = lax.broadcasted_iota(jnp.int32, (tm, tm), 0)
    ci = lax.broadcasted_iota(jnp.int32, (tm, tm), 1)
    earlier = jnp.where(ri < ci, 1.0, 0.0).astype(BF16)
    er = lax.broadcasted_iota(jnp.int32, (ne, ne), 0)
    ec = lax.broadcasted_iota(jnp.int32, (ne, ne), 1)
    lower = jnp.where(ec < er, 1.0, 0.0).astype(BF16)
    diag = (lax.broadcasted_iota(jnp.int32, (ne, V7X_LANES), 0)
            == lax.broadcasted_iota(jnp.int32, (ne, V7X_LANES), 1))
    sub = lax.broadcasted_iota(jnp.int32, (V7X_SUBLANES, V7X_LANES), 0)
    base = carry[...]

    def to_lanes(colvec):
        return jnp.sum(jnp.where(diag, colvec, 0.0), axis=0, keepdims=True)

    for s in range(nsub):
        rs = slice(s * tm, (s + 1) * tm)
        h_ref[rs, :] = (DEEPNORM_ALPHA * x_ref[rs, :] + _dot(attn_ref[rs, :], woa_ref[...])
                        + _dot(hm_ref[rs, :], wom_ref[...]))
    for s in range(nsub):
        rs = slice(s * tm, (s + 1) * tm)
        hval = _layer_norm(h_ref[rs, :], g_ref[...], b_ref[...])
        h_ref[rs, :] = hval

        logits = _dot(hval.astype(BF16), wr_ref[...]) + br_ref[...]
        lt = logits.T[:ne, :]

        sel, vals = [], []
        for _ in range(TOP_K):
            mx = jnp.max(lt, axis=0, keepdims=True)
            idx = jnp.min(jnp.where(lt == mx, ex, float(ne)), axis=0, keepdims=True)
            hit = ex == idx
            lt = jnp.where(hit, NEG_INF, lt)
            sel.append(hit)
            vals.append(mx)
        exps = [jnp.exp(v - vals[0]) for v in vals]
        den = exps[0] + exps[1] + exps[2] + exps[3]
        gates = [e / den for e in exps]

        onehot = jnp.zeros((ne, tm), F32)
        for hit in sel:
            onehot = jnp.where(hit, 1.0, onehot)
        rank = _dot(onehot.astype(BF16), earlier)
        cnt = jnp.broadcast_to(jnp.sum(onehot, axis=1, keepdims=True), (ne, V7X_LANES))
        cnt_al = jnp.floor((cnt + (RUN_ALIGN - 1)) * (1.0 / RUN_ALIGN))
        slot = _dot(lower, cnt_al.astype(BF16)) * float(RUN_ALIGN)
        cnt_al = cnt_al * float(RUN_ALIGN)
        where_row = slot[:, 0:1] + rank

        rows = [jnp.sum(jnp.where(sel[k], where_row, 0.0), axis=0, keepdims=True) for k in range(TOP_K)]
        info = jnp.concatenate(rows + gates, axis=0)
        post_ref[s] = info
        col_ref[rs, :] = jnp.concatenate([info, jnp.zeros((V7X_LANES - 2 * TOP_K, tm), F32)], axis=0).T

        cnt_row = to_lanes(cnt_al)
        tab = jnp.zeros((V7X_SUBLANES, V7X_LANES), F32)
        tab = jnp.where(sub == 0, cnt_row, tab)
        tab = jnp.where(sub == 1, base, tab)
        tab = jnp.where(sub == 2, to_lanes(slot), tab)
        tab_ref[s] = tab.astype(jnp.int32)
        base = base + cnt_row

    carry[...] = base
    tot_ref[...] = jnp.broadcast_to(base, (V7X_SUBLANES, V7X_LANES)).astype(jnp.int32)


def _outproj_router(attn2d, hm2d, x2d, wo_a, wo_m, ln_g, ln_b, w_r, b_r):
    t = x2d.shape[0]
    nsub = ROUTER_SUBTILES
    tm = TOK_TILE * nsub
    nt = t // TOK_TILE
    const = lambda i: (0, 0)
    return pl.pallas_call(
        functools.partial(_router_kernel, nsub=nsub),
        grid=(t // tm,),
        in_specs=[
            pl.BlockSpec((tm, D_ATTN), lambda i: (i, 0)),
            pl.BlockSpec((tm, D_MLSTM), lambda i: (i, 0)),
            pl.BlockSpec((tm, D_MODEL), lambda i: (i, 0)),
            pl.BlockSpec((D_ATTN, D_MODEL), const),
            pl.BlockSpec((D_MLSTM, D_MODEL), const),
            pl.BlockSpec((1, D_MODEL), const),
            pl.BlockSpec((1, D_MODEL), const),
            pl.BlockSpec((D_MODEL, V7X_LANES), const),
            pl.BlockSpec((1, V7X_LANES), const),
        ],
        out_specs=[
            pl.BlockSpec((tm, D_MODEL), lambda i: (i, 0)),
            pl.BlockSpec((nsub, V7X_SUBLANES, TOK_TILE), lambda i: (i, 0, 0)),
            pl.BlockSpec((tm, V7X_LANES), lambda i: (i, 0)),
            pl.BlockSpec((nsub, V7X_SUBLANES, V7X_LANES), lambda i: (i, 0, 0)),
            pl.BlockSpec((V7X_SUBLANES, V7X_LANES), const),
        ],
        out_shape=[
            jax.ShapeDtypeStruct((t, D_MODEL), F32),
            jax.ShapeDtypeStruct((nt, V7X_SUBLANES, TOK_TILE), F32),
            jax.ShapeDtypeStruct((t, V7X_LANES), F32),
            jax.ShapeDtypeStruct((nt, V7X_SUBLANES, V7X_LANES), jnp.int32),
            jax.ShapeDtypeStruct((V7X_SUBLANES, V7X_LANES), jnp.int32),
        ],
        scratch_shapes=[pltpu.VMEM((1, V7X_LANES), F32)],
        compiler_params=_compiler_params(("arbitrary",)),
        name="outproj_router",
    )(attn2d, hm2d, x2d, wo_a, wo_m, ln_g, ln_b, w_r, b_r)


def _run_copy(src, dst, start_src, start_dst, n, sem):
    n = pl.multiple_of(n, RUN_ALIGN)
    return pltpu.make_async_copy(
        src.at[pl.ds(pl.multiple_of(start_src, RUN_ALIGN), n)],
        dst.at[pl.ds(pl.multiple_of(start_dst, RUN_ALIGN), n)],
        sem)


def _dispatch_kernel(cnt_sm, slot_sm, dst_sm, tail_sm, post_ref, h_ref, xs_hbm, ybuf, zbuf, sem, zsem):
    i = pl.program_id(0)
    nt = pl.num_programs(0)
    cur = i % 2

    n_blocks = xs_hbm.shape[0] // ROW_BLOCK

    def zero_fill(wait):
        def go(cp):
            if wait:
                cp.wait()
            else:
                cp.start()

        def tail(e, carry):
            n = tail_sm[N_EXPERTS + e]

            @pl.when(n > 0)
            def _():
                go(_run_copy(zbuf, xs_hbm, 0, tail_sm[e], n, zsem))
            return carry

        def block(j, carry):
            go(_run_copy(zbuf, xs_hbm, 0, j * ROW_BLOCK, ROW_BLOCK, zsem))
            return carry

        lax.fori_loop(0, N_EXPERTS, tail, 0)
        lax.fori_loop(tail_sm[2 * N_EXPERTS], n_blocks - 1, block, 0)

    @pl.when(i == 0)
    def _():
        for b in range(2):
            ybuf[b, ROWS_LOCAL:ROWS_LOCAL + RUN_ALIGN, :] = jnp.zeros((RUN_ALIGN, D_MODEL), F32)
        zbuf[...] = jnp.zeros_like(zbuf)
        zero_fill(False)

    hb = h_ref[...].astype(BF16)
    pos = post_ref[0]
    chunk = 256
    for rc in range(ROWS_LOCAL // chunk):
        rows = (lax.broadcasted_iota(jnp.int32, (chunk, TOK_TILE), 0) + rc * chunk).astype(F32)
        p = jnp.zeros((chunk, TOK_TILE), F32)
        for k in range(TOP_K):
            p = jnp.where(rows == pos[k:k + 1, :], 1.0, p)
        ybuf[cur, rc * chunk:(rc + 1) * chunk, :] = _dot(p.astype(BF16), hb)

    def total_rows(tile):
        return cnt_sm[nt * N_EXPERTS + tile]

    @pl.when(i > 0)
    def _():
        n_prev = total_rows(i - 1)
        _run_copy(ybuf.at[1 - cur], xs_hbm, 0, 0, n_prev, sem).wait()

    for e in range(N_EXPERTS):
        k = i * N_EXPERTS + e
        _run_copy(ybuf.at[cur], xs_hbm, slot_sm[k], dst_sm[k], cnt_sm[k], sem).start()

    @pl.when(i == nt - 1)
    def _():
        _run_copy(ybuf.at[cur], xs_hbm, 0, 0, total_rows(i), sem).wait()
        zero_fill(True)
        last = _run_copy(zbuf, xs_hbm, 0, (n_blocks - 1) * ROW_BLOCK, ROW_BLOCK, zsem)
        last.start()
        last.wait()


def _dispatch(cnt, slot, dst, tail, post, h2d, n_rows):
    t = h2d.shape[0]
    nt = t // TOK_TILE
    return pl.pallas_call(
        _dispatch_kernel,
        grid_spec=pltpu.PrefetchScalarGridSpec(
            num_scalar_prefetch=4,
            grid=(nt,),
            in_specs=[
                pl.BlockSpec((1, V7X_SUBLANES, TOK_TILE), lambda i, *_: (i, 0, 0)),
                pl.BlockSpec((TOK_TILE, D_MODEL), lambda i, *_: (i, 0)),
            ],
            out_specs=pl.BlockSpec(memory_space=pl.ANY),
            scratch_shapes=[
                pltpu.VMEM((2, ROWS_LOCAL + RUN_ALIGN, D_MODEL), F32),
                pltpu.VMEM((ROW_BLOCK, D_MODEL), F32),
                pltpu.SemaphoreType.DMA(()),
                pltpu.SemaphoreType.DMA(()),
            ],
        ),
        out_shape=jax.ShapeDtypeStruct((n_rows, D_MODEL), F32),
        compiler_params=_compiler_params(("arbitrary",)),
        name="moe_dispatch",
    )(cnt, slot, dst, tail, post, h2d)


def _expert_kernel(bexp_sm, nused_sm, nvalid_sm, xs_ref, wg_ref, bg_ref, wu_ref, bu_ref, wd_ref, bd_ref, ys_ref,
                   hbuf, wg_b, wu_b, wd_b):
    j = pl.program_id(0)
    used = j < nused_sm[0]
    new_expert = jnp.logical_or(j == 0, bexp_sm[j] != bexp_sm[jnp.maximum(j, 1) - 1])

    @pl.when(jnp.logical_and(used, new_expert))
    def _():
        def cast(i, carry):
            sl = pl.ds(pl.multiple_of(i * 128, 128), 128)
            wg_b[sl, :] = wg_ref[0, sl, :].astype(BF16)
            wu_b[sl, :] = wu_ref[0, sl, :].astype(BF16)
            wd_b[sl, :] = wd_ref[0, sl, :].astype(BF16)
            return carry
        lax.fori_loop(0, D_MODEL // 128, cast, 0)

    def mlp(rows):
        xb = xs_ref[0:rows, :].astype(BF16)
        step = 512
        for n in range(0, D_MODEL, step):
            g = _dot(xb, wg_b[:, n:n + step]) + bg_ref[0, :, n:n + step]
            u = _dot(xb, wu_b[:, n:n + step]) + bu_ref[0, :, n:n + step]
            g = jnp.minimum(g, SWIGLU_LIMIT)
            u = jnp.clip(u, -SWIGLU_LIMIT, SWIGLU_LIMIT)
            hbuf[0:rows, n:n + step] = (g * jax.nn.sigmoid(SWIGLU_ALPHA * g) * (u + 1.0)).astype(BF16)
        ys_ref[0:rows, :] = _dot(hbuf[0:rows, :], wd_b[...]) + bd_ref[0]

    half = ROW_BLOCK // 2
    upper_rows = nvalid_sm[j] > half

    @pl.when(jnp.logical_and(used, upper_rows))
    def _():
        mlp(ROW_BLOCK)

    @pl.when(jnp.logical_and(used, jnp.logical_not(upper_rows)))
    def _():
        mlp(half)
        ys_ref[half:ROW_BLOCK, :] = jnp.zeros((ROW_BLOCK - half, D_MODEL), F32)


def _experts(bexp, nused, nvalid, xs, wg, bg, wu, bu, wd, bd):
    n_rows = xs.shape[0]
    nb = n_rows // ROW_BLOCK

    def row_map(j, be, nu, nv):
        return (jnp.minimum(j, nu[0] - 1), 0)

    def w_map(j, be, nu, nv):
        return (be[jnp.minimum(j, nu[0] - 1)], 0, 0)

    wspec = pl.BlockSpec((1, D_MODEL, D_MODEL), w_map)
    bspec = pl.BlockSpec((1, 1, D_MODEL), w_map)
    return pl.pallas_call(
        _expert_kernel,
        grid_spec=pltpu.PrefetchScalarGridSpec(
            num_scalar_prefetch=3,
            grid=(nb,),
            in_specs=[pl.BlockSpec((ROW_BLOCK, D_MODEL), row_map), wspec, bspec, wspec, bspec, wspec, bspec],
            out_specs=pl.BlockSpec((ROW_BLOCK, D_MODEL), row_map),
            scratch_shapes=[pltpu.VMEM((ROW_BLOCK, D_MODEL), BF16)]
            + [pltpu.VMEM((D_MODEL, D_MODEL), BF16) for _ in range(3)],
        ),
        out_shape=jax.ShapeDtypeStruct((n_rows, D_MODEL), F32),
        input_output_aliases={3: 0},
        compiler_params=_compiler_params(("arbitrary",)),
        name="moe_experts",
    )(bexp, nused, nvalid, xs, wg, bg, wu, bu, wd, bd)


def _combine_kernel(cnt_sm, slot_sm, dst_sm, col_ref, h_ref, g_ref, b_ref, ys_hbm, o_ref, ybuf, sbuf, sem):
    i = pl.program_id(0)
    nt = pl.num_programs(0)
    cur = i % 2

    def fetch(tile, buf):
        for e in range(N_EXPERTS):
            k = tile * N_EXPERTS + e
            _run_copy(ys_hbm, ybuf.at[buf], dst_sm[k], slot_sm[k], cnt_sm[k], sem.at[buf]).start()

    @pl.when(i == 0)
    def _():
        ybuf[...] = jnp.zeros_like(ybuf)
        fetch(0, 0)

    @pl.when(i + 1 < nt)
    def _():
        fetch(i + 1, 1 - cur)

    _run_copy(ys_hbm, ybuf.at[cur], 0, 0, cnt_sm[nt * N_EXPERTS + i], sem.at[cur]).wait()

    col = col_ref[...]
    chunk = 256
    for rc in range(ROWS_LOCAL // chunk):
        rows = (lax.broadcasted_iota(jnp.int32, (TOK_TILE, chunk), 1) + rc * chunk).astype(F32)
        s = jnp.zeros((TOK_TILE, chunk), F32)
        for k in range(TOP_K):
            s = jnp.where(rows == col[:, k:k + 1], col[:, TOP_K + k:TOP_K + k + 1], s)
        sbuf[:, rc * chunk:(rc + 1) * chunk] = s.astype(BF16)
    for n in range(0, D_MODEL, chunk):
        moe = _dot(sbuf[...], ybuf[cur, 0:ROWS_LOCAL, n:n + chunk].astype(BF16))
        o_ref[:, n:n + chunk] = DEEPNORM_ALPHA * h_ref[:, n:n + chunk] + moe
    o_ref[...] = _layer_norm(o_ref[...], g_ref[...], b_ref[...])


def _combine(cnt, slot, dst, col, h2d, ln_g, ln_b, ys):
    t = h2d.shape[0]
    nt = t // TOK_TILE
    return pl.pallas_call(
        _combine_kernel,
        grid_spec=pltpu.PrefetchScalarGridSpec(
            num_scalar_prefetch=3,
            grid=(nt,),
            in_specs=[
                pl.BlockSpec((TOK_TILE, V7X_LANES), lambda i, *_: (i, 0)),
                pl.BlockSpec((TOK_TILE, D_MODEL), lambda i, *_: (i, 0)),
                pl.BlockSpec((1, D_MODEL), lambda i, *_: (0, 0)),
                pl.BlockSpec((1, D_MODEL), lambda i, *_: (0, 0)),
                pl.BlockSpec(memory_space=pl.ANY),
            ],
            out_specs=pl.BlockSpec((TOK_TILE, D_MODEL), lambda i, *_: (i, 0)),
            scratch_shapes=[
                pltpu.VMEM((2, ROWS_LOCAL + RUN_ALIGN, D_MODEL), F32),
                pltpu.VMEM((TOK_TILE, ROWS_LOCAL), BF16),
                pltpu.SemaphoreType.DMA((2,)),
            ],
        ),
        out_shape=jax.ShapeDtypeStruct((t, D_MODEL), F32),
        compiler_params=_compiler_params(("arbitrary",)),
        name="moe_combine",
    )(cnt, slot, dst, col, h2d, ln_g, ln_b, ys)


def _pad_lanes(a, width=V7X_LANES):
    return jnp.pad(a, ((0, 0), (0, width - a.shape[1])))


def _layer(h3d, w_in, conv_w, conv_b, w_mq, w_mk, b_igate, b_fgate, mnorm_g, w_out,
           ln1_g, ln1_b, w_router, b_router, w_gate, b_gate, w_up, b_up, w_down, b_down, ln2_g, ln2_b):
    batch, seq, _ = h3d.shape
    t = batch * seq
    x2d = h3d.reshape(t, D_MODEL)

    w_main = w_in[:, :PROJ_MAIN].astype(BF16)
    w_gates = _pad_lanes(w_in[:, PROJ_MAIN:]).astype(BF16)
    proj, gcol, grow = _inproj(x2d, w_main, w_gates, batch, seq)
    proj3d = proj.reshape(batch, seq, PROJ_MAIN)
    attn = _attention(proj3d)
    gate_bias = jnp.concatenate([b_igate, b_fgate]).astype(F32)
    bias_col = _pad_lanes(gate_bias[None, :])
    bias_row = jnp.broadcast_to(gate_bias[:, None], (V7X_SUBLANES, seq))
    hm = _mlstm(proj3d, gcol.reshape(batch, seq, V7X_LANES), grow, conv_w, conv_b[None, :],
                w_mq.astype(BF16), w_mk.astype(BF16), bias_col, bias_row, mnorm_g[None, :])

    wo = w_out.astype(BF16)
    h2d, post, col, tab, tot = _outproj_router(
        attn.reshape(t, D_ATTN), hm.reshape(t, D_MLSTM), x2d, wo[:D_ATTN], wo[D_ATTN:],
        ln1_g[None, :], ln1_b[None, :], _pad_lanes(w_router).astype(BF16), _pad_lanes(b_router[None, :]))

    nt = t // TOK_TILE
    total = tot[0, :N_EXPERTS]
    region = (total + ROW_BLOCK - 1) // ROW_BLOCK * ROW_BLOCK
    region_end = jnp.cumsum(region)
    region_start = region_end - region
    n_rows = (t * TOP_K + nt * N_EXPERTS * (RUN_ALIGN - 1)) // ROW_BLOCK * ROW_BLOCK + (N_EXPERTS + 1) * ROW_BLOCK
    nb = n_rows // ROW_BLOCK
    runs = tab[:, 0, :N_EXPERTS]
    empty = runs == 0
    cnt = jnp.where(empty, RUN_ALIGN, runs)
    cnt = jnp.concatenate([cnt.reshape(-1), jnp.sum(cnt, axis=1)])
    slot = jnp.where(empty, ROWS_LOCAL, tab[:, 2, :N_EXPERTS]).reshape(-1)
    dst = jnp.where(empty, n_rows - ROW_BLOCK, tab[:, 1, :N_EXPERTS] + region_start[None, :]).reshape(-1)
    tail = jnp.concatenate([region_start + total, region - total, region_end[-1:] // ROW_BLOCK]).astype(jnp.int32)
    block_row = jnp.arange(nb, dtype=jnp.int32) * ROW_BLOCK
    bexp = jnp.minimum(jnp.sum(region_end[None, :] <= block_row[:, None], axis=1), N_EXPERTS - 1).astype(jnp.int32)
    nused = (region_end[-1:] // ROW_BLOCK).astype(jnp.int32)
    own = bexp[:, None] == jnp.arange(N_EXPERTS, dtype=jnp.int32)[None, :]
    used_end = jnp.sum(jnp.where(own, (region_start + total)[None, :], 0), axis=1)
    nvalid = jnp.clip(used_end - block_row, 0, ROW_BLOCK).astype(jnp.int32)

    xs = _dispatch(cnt, slot, dst, tail, post, h2d, n_rows)
    ys = _experts(bexp, nused, nvalid, xs, w_gate, b_gate[:, None, :], w_up, b_up[:, None, :], w_down, b_down[:, None, :])
    out = _combine(cnt, slot, dst, col, h2d, ln2_g[None, :], ln2_b[None, :], ys)
    return out.reshape(batch, seq, D_MODEL)


def kernel(x, w_in, conv_w, conv_b, w_mq, w_mk, b_igate, b_fgate, mnorm_g, w_out, ln1_g, ln1_b, w_router, b_router, w_gate, b_gate, w_up, b_up, w_down, b_down, ln2_g, ln2_b):
    h = x
    for l in range(w_in.shape[0]):
        h = _layer(h, w_in[l], conv_w[l], conv_b[l], w_mq[l], w_mk[l], b_igate[l], b_fgate[l], mnorm_g[l],
                   w_out[l], ln1_g[l], ln1_b[l], w_router[l], b_router[l], w_gate[l], b_gate[l], w_up[l],
                   b_up[l], w_down[l], b_down[l], ln2_g[l], ln2_b[l])
    return h
```

```python
import functools
import math

import jax
import jax.numpy as jnp
from jax import lax
from jax.experimental import pallas as pl
from jax.experimental.pallas import tpu as pltpu

F32 = jnp.float32
BF16 = jnp.bfloat16
NEG_INF = float("-inf")

V7X_LANES = 128
V7X_SUBLANES = 8
V7X_VMEM_LIMIT_BYTES = 56 * 1024 * 1024

D_MODEL = 1024
D_ATTN = 512
HEAD_DIM_A = 64
D_MLSTM = 512
N_HEADS_M = 4
HEAD_DIM_M = 128
CONV_WIDTH = 4
CHUNK = 128
DILATED_BRANCHES = ((128, 1), (512, 4), (2048, 16))
N_EXPERTS = 32
TOP_K = 4
SWIGLU_LIMIT = 7.0
SWIGLU_ALPHA = 1.702
DEEPNORM_ALPHA = 2.0 ** 0.25
LN_EPS = 1e-5
RMS_EPS = 1e-6

PROJ_MAIN = 3 * D_ATTN + 3 * D_MLSTM
TOK_TILE = 256
RUN_ALIGN = V7X_SUBLANES
ROWS_LOCAL = TOK_TILE * TOP_K + N_EXPERTS * RUN_ALIGN
ROW_BLOCK = 512
ROUTER_SUBTILES = 4


def _dot(a, b):
    return jnp.dot(a, b, preferred_element_type=F32)


def _dot_nt(a, b):
    return lax.dot_general(a, b, (((1,), (1,)), ((), ())), preferred_element_type=F32)


def _split3(x):
    hi = x.astype(BF16)
    r1 = x - hi.astype(F32)
    mid = r1.astype(BF16)
    lo = (r1 - mid.astype(F32)).astype(BF16)
    return hi, mid, lo


def _compiler_params(sem):
    return pltpu.CompilerParams(dimension_semantics=sem, vmem_limit_bytes=V7X_VMEM_LIMIT_BYTES)


def _inproj_kernel(x_ref, w_ref, wg_ref, proj_ref, gcol_ref, grow_ref):
    xb = x_ref[...].astype(BF16)
    step = 512
    for n in range(0, PROJ_MAIN, step):
        proj_ref[:, n:n + step] = _dot(xb, w_ref[:, n:n + step])
    g = _dot(xb, wg_ref[...])
    gcol_ref[...] = g
    grow_ref[0] = g.T[:V7X_SUBLANES, :]


def _inproj(x2d, w_main, w_gates, batch, seq):
    t = x2d.shape[0]
    tm = 512
    per_b = seq // tm
    return pl.pallas_call(
        _inproj_kernel,
        grid=(t // tm,),
        in_specs=[
            pl.BlockSpec((tm, D_MODEL), lambda i: (i, 0)),
            pl.BlockSpec((D_MODEL, PROJ_MAIN), lambda i: (0, 0)),
            pl.BlockSpec((D_MODEL, V7X_LANES), lambda i: (0, 0)),
        ],
        out_specs=[
            pl.BlockSpec((tm, PROJ_MAIN), lambda i: (i, 0)),
            pl.BlockSpec((tm, V7X_LANES), lambda i: (i, 0)),
            pl.BlockSpec((1, V7X_SUBLANES, tm), lambda i: (i // per_b, 0, i % per_b)),
        ],
        out_shape=[
            jax.ShapeDtypeStruct((t, PROJ_MAIN), F32),
            jax.ShapeDtypeStruct((t, V7X_LANES), F32),
            jax.ShapeDtypeStruct((batch, V7X_SUBLANES, seq), F32),
        ],
        compiler_params=_compiler_params(("arbitrary",)),
        name="inproj",
    )(x2d, w_main, w_gates)


ATTN_GROUP = 8


def _attn_kernel(q_ref, k_ref, v_ref, o_ref, q0s, q1s, bias_b, bias_f, sg, pg,
                 o0, o1, o2, l0, l1, l2, m0, m1, m2, x0, x1, x2, *, seq):
    obufs, lbufs, mbufs, xbufs = (o0, o1, o2), (l0, l1, l2), (m0, m1, m2), (x0, x1, x2)
    two = 2 * CHUNK
    head0 = lax.broadcasted_iota(jnp.int32, (CHUNK, V7X_LANES), 1) < HEAD_DIM_A
    qscale = HEAD_DIM_A ** -0.5 * math.log2(math.e)

    def prep(i, carry):
        sl = pl.ds(pl.multiple_of(i * two, two), two)
        h0 = lax.broadcasted_iota(jnp.int32, (two, V7X_LANES), 1) < HEAD_DIM_A
        q = q_ref[0, sl, :] * qscale
        q0s[sl, :] = jnp.where(h0, q, 0.0)
        q1s[sl, :] = jnp.where(h0, 0.0, q)
        return carry

    lax.fori_loop(0, seq // two, prep, 0)

    qi = lax.broadcasted_iota(jnp.int32, (two, two), 0) % CHUNK
    kj = lax.broadcasted_iota(jnp.int32, (two, two), 1)
    bias_b[...] = jnp.where((kj >= qi) & (kj <= qi + CHUNK), 0.0, NEG_INF)
    qf = lax.broadcasted_iota(jnp.int32, (two, CHUNK), 0) % CHUNK
    kf = lax.broadcasted_iota(jnp.int32, (two, CHUNK), 1)
    bias_f[...] = jnp.where(kf <= qf, 0.0, NEG_INF)

    def run_group(c, dil, starts, has_prev):
        assert len(starts) <= ATTN_GROUP
        nk = two if has_prev else CHUNK
        bias_ref = bias_b if has_prev else bias_f

        def rows(s0):
            return pl.ds(s0, CHUNK) if dil == 1 else pl.ds(s0, CHUNK, stride=dil)

        def keys(ref, st):
            if has_prev:
                return jnp.concatenate([ref[0, rows(st - dil * CHUNK), :], ref[0, rows(st), :]], axis=0).astype(BF16)
            return ref[0, rows(st), :].astype(BF16)

        def both_heads(x):
            return jnp.where(head0, x[:CHUNK], x[CHUNK:])

        for j, st in enumerate(starts):
            q2 = jnp.concatenate([q0s[rows(st), :], q1s[rows(st), :]], axis=0).astype(BF16)
            sg[j, :, 0:nk] = _dot_nt(q2, keys(k_ref, st)) + bias_ref[...]
        for j, st in enumerate(starts):
            s = sg[j, :, 0:nk]
            m = jnp.max(s, axis=1, keepdims=True)
            pg[j, :, 0:nk] = jnp.exp2(s - m).astype(BF16)
            ma = jnp.broadcast_to(m[:CHUNK], (CHUNK, V7X_LANES))
            mb = jnp.broadcast_to(m[CHUNK:], (CHUNK, V7X_LANES))
            mbufs[c][rows(st), :] = jnp.where(head0, ma, mb)
            xbufs[c][rows(st), :] = jnp.where(head0, mb, ma)
        khead0 = lax.broadcasted_iota(jnp.int32, (nk, V7X_LANES), 1) < HEAD_DIM_A
        for j, st in enumerate(starts):
            if has_prev:
                vv = jnp.concatenate([v_ref[0, rows(st - dil * CHUNK), :], v_ref[0, rows(st), :]], axis=0)
            else:
                vv = v_ref[0, rows(st), :]
            oa = _dot(pg[j, 0:CHUNK, 0:nk], jnp.where(khead0, vv, 1.0).astype(BF16))
            ob = _dot(pg[j, CHUNK:two, 0:nk], jnp.where(khead0, 1.0, vv).astype(BF16))
            obufs[c][rows(st), :] = jnp.where(head0, oa, ob)
            lbufs[c][rows(st), :] = jnp.where(head0, ob, oa)

    for c, (window, dil) in enumerate(DILATED_BRANCHES):
        assert window // dil == CHUNK
        nb = seq // (dil * CHUNK)
        span = dil * CHUNK
        if dil == 1:
            run_group(c, dil, [0], False)
            per = 5
            assert (nb - 1) % per == 0

            def band1(g, carry, c=c, dil=dil, per=per, span=span):
                base = span + g * (per * span)
                run_group(c, dil, [base + span * j for j in range(per)], True)
                return carry

            lax.fori_loop(0, (nb - 1) // per, band1, 0, unroll=True)
        elif nb > 1:
            run_group(c, dil, list(range(dil)), False)

            def band2(g, carry, c=c, dil=dil, nb=nb, span=span):
                starts = [2 * g + rr + span * b for rr in range(2) for b in range(1, nb)]
                run_group(c, dil, starts, True)
                return carry

            lax.fori_loop(0, dil // 2, band2, 0, unroll=True)
        else:
            def firsts(g, carry, c=c, dil=dil):
                run_group(c, dil, [ATTN_GROUP * g + j for j in range(ATTN_GROUP)], False)
                return carry

            lax.fori_loop(0, dil // ATTN_GROUP, firsts, 0)

    def combine(i, carry):
        sl = pl.ds(pl.multiple_of(i * 256, 256), 256)
        ma, mb, mc = m0[sl, :], m1[sl, :], m2[sl, :]
        mx = jnp.maximum(jnp.maximum(ma, mb), mc)
        wa, wb, wc = jnp.exp2(ma - mx), jnp.exp2(mb - mx), jnp.exp2(mc - mx)
        xa, xb, xc = x0[sl, :], x1[sl, :], x2[sl, :]
        xm = jnp.maximum(jnp.maximum(xa, xb), xc)
        den = (jnp.exp2(xa - xm) * l0[sl, :] + jnp.exp2(xb - xm) * l1[sl, :] + jnp.exp2(xc - xm) * l2[sl, :])
        den = pltpu.roll(den, HEAD_DIM_A, axis=1)
        out = (wa * o0[sl, :] + wb * o1[sl, :] + wc * o2[sl, :]) / den
        o_ref[0, sl, :] = out.astype(o_ref.dtype)
        return carry

    lax.fori_loop(0, seq // 256, combine, 0)


def _attention(proj3d):
    batch, seq, _ = proj3d.shape
    nblk = D_ATTN // V7X_LANES
    blk = (1, seq, V7X_LANES)
    scratch = ([pltpu.VMEM((seq, V7X_LANES), F32), pltpu.VMEM((seq, V7X_LANES), F32),
                pltpu.VMEM((2 * CHUNK, 2 * CHUNK), F32), pltpu.VMEM((2 * CHUNK, CHUNK), F32),
                pltpu.VMEM((ATTN_GROUP, 2 * CHUNK, 2 * CHUNK), F32),
                pltpu.VMEM((ATTN_GROUP, 2 * CHUNK, 2 * CHUNK), BF16)]
               + [pltpu.VMEM((seq, V7X_LANES), F32) for _ in range(12)])
    return pl.pallas_call(
        functools.partial(_attn_kernel, seq=seq),
        grid=(batch, nblk),
        in_specs=[
            pl.BlockSpec(blk, lambda b, g: (b, 0, g)),
            pl.BlockSpec(blk, lambda b, g: (b, 0, nblk + g)),
            pl.BlockSpec(blk, lambda b, g: (b, 0, 2 * nblk + g)),
        ],
        out_specs=pl.BlockSpec(blk, lambda b, g: (b, 0, g)),
        out_shape=jax.ShapeDtypeStruct((batch, seq, D_ATTN), BF16),
        scratch_shapes=scratch,
        compiler_params=_compiler_params(("arbitrary", "arbitrary")),
        name="dilated_attention",
    )(proj3d, proj3d, proj3d)


def _log_sigmoid(x):
    return jnp.minimum(x, 0.0) - jnp.log1p(jnp.exp(-jnp.abs(x)))


MLSTM_HEADS_PER_STEP = 2


def _mlstm_kernel(xm_ref, vm_ref, om_ref, gcol_ref, grow_ref, cw_ref, cb_ref, wq_ref, wk_ref,
                  bcol_ref, brow_ref, mg_ref, o_ref, *scratch, seq):
    nh = MLSTM_HEADS_PER_STEP
    per_head = len(scratch) // nh
    nchunk = seq // CHUNK
    ri = lax.broadcasted_iota(jnp.int32, (CHUNK, CHUNK), 0)
    ci = lax.broadcasted_iota(jnp.int32, (CHUNK, CHUNK), 1)
    tril = ri >= ci
    tril_b = jnp.where(tril, 1.0, 0.0).astype(BF16)
    triu_b = jnp.where(ri <= ci, 1.0, 0.0).astype(BF16)
    pad = V7X_SUBLANES
    qscale = HEAD_DIM_M ** -0.5

    class Head:
        def __init__(self, hh):
            (self.xpad, self.q_s, self.k_s, self.icol_s, self.lfcol_s, self.gcol_s,
             self.ri_s, self.rf_s, self.gf_s, self.mi_s, self.mo_s, self.gl_s) = scratch[hh * per_head:(hh + 1) * per_head]
            self.h = pl.program_id(1) * nh + hh
            self.cols = slice(hh * V7X_LANES, (hh + 1) * V7X_LANES)
            self.wq = wq_ref[hh]
            self.wk = wk_ref[hh]
            self.mg = mg_ref[:, self.cols]

    heads = [Head(hh) for hh in range(nh)]

    for hd in heads:
        hd.xpad[0:pad, :] = jnp.zeros((pad, V7X_LANES), F32)
        hd.xpad[pad:pad + seq, :] = xm_ref[0, :, hd.cols]

    def conv_step(i, carry):
        base = pl.multiple_of(i * 256, 256)
        rows = pl.ds(base, 256)
        gc = gcol_ref[0, rows, :] + bcol_ref[...]
        lane256 = lax.broadcasted_iota(jnp.int32, (256, V7X_LANES), 1)
        for hd in heads:
            y = jnp.broadcast_to(cb_ref[:, hd.cols], (256, V7X_LANES))
            xw = hd.xpad[pl.ds(base, 256 + pad), :]
            for j in range(CONV_WIDTH):
                off = pad - (CONV_WIDTH - 1) + j
                y = y + cw_ref[j:j + 1, hd.cols] * xw[off:off + 256, :]
            xc = (y * jax.nn.sigmoid(y)).astype(BF16)
            hd.q_s[rows, :] = _dot(xc, hd.wq) * qscale
            hd.k_s[rows, :] = _dot(xc, hd.wk)
            ic = jnp.sum(jnp.where(lane256 == hd.h, gc, 0.0), axis=1, keepdims=True)
            fc = jnp.sum(jnp.where(lane256 == hd.h + N_HEADS_M, gc, 0.0), axis=1, keepdims=True)
            hd.icol_s[rows, :] = jnp.broadcast_to(ic, (256, V7X_LANES))
            hd.lfcol_s[rows, :] = jnp.broadcast_to(_log_sigmoid(fc), (256, V7X_LANES))
        return carry

    lax.fori_loop(0, seq // 256, conv_step, 0)

    gr = grow_ref[0] + brow_ref[...]
    sub = lax.broadcasted_iota(jnp.int32, (V7X_SUBLANES, seq), 0)
    for hd in heads:
        irow = jnp.sum(jnp.where(sub == hd.h, gr, 0.0), axis=0, keepdims=True)
        lfrow = _log_sigmoid(jnp.sum(jnp.where(sub == hd.h + N_HEADS_M, gr, 0.0), axis=0, keepdims=True))
        for c in range(nchunk):
            hd.ri_s[c:c + 1, :] = irow[:, c * CHUNK:(c + 1) * CHUNK]
            hd.rf_s[c:c + 1, :] = lfrow[:, c * CHUNK:(c + 1) * CHUNK]

    def cum_step(c, carry):
        sl = pl.ds(pl.multiple_of(c * CHUNK, CHUNK), CHUNK)
        for hd in heads:
            hd.gcol_s[sl, :] = sum(_dot(tril_b, part) for part in _split3(hd.lfcol_s[sl, :]))
        return carry

    lax.fori_loop(0, nchunk, cum_step, 0, unroll=2)

    crow = lax.broadcasted_iota(jnp.int32, (nchunk, V7X_LANES), 0)
    for hd in heads:
        irows = hd.ri_s[...]
        grows = sum(_dot(part, triu_b) for part in _split3(hd.rf_s[...]))
        hd.gf_s[...] = grows
        g_last = grows[:, CHUNK - 1:CHUNK]
        a_max = jnp.max(g_last - grows + irows, axis=1, keepdims=True)
        m = jnp.zeros((1, 1), F32)
        m_in = jnp.zeros((nchunk, V7X_LANES), F32)
        m_out = jnp.zeros((nchunk, V7X_LANES), F32)
        for c in range(nchunk):
            m_in = jnp.where(crow == c, m, m_in)
            m = jnp.maximum(g_last[c:c + 1, :] + m, a_max[c:c + 1, :])
            m_out = jnp.where(crow == c, m, m_out)
        hd.mi_s[...] = m_in
        hd.mo_s[...] = m_out
        hd.gl_s[...] = jnp.broadcast_to(g_last, (nchunk, V7X_LANES))

    def chunk_head(hd, c, cmat, nrow):
        sl = pl.ds(pl.multiple_of(c * CHUNK, CHUNK), CHUNK)
        one = pl.ds(c, 1)
        qc = hd.q_s[sl, :]
        kc = hd.k_s[sl, :]
        vc = vm_ref[0, sl, hd.cols].astype(BF16)
        icol = hd.icol_s[sl, :]
        gcol = hd.gcol_s[sl, :]
        irow_c = hd.ri_s[one, :]
        grow_c = hd.gf_s[one, :]
        m = hd.mi_s[one, :]
        m_new = hd.mo_s[one, :]
        g_last = hd.gl_s[one, :]

        d = gcol - grow_c + irow_c
        d = jnp.where(tril, d, NEG_INF)
        inter = gcol + m
        m_t = jnp.maximum(inter, jnp.max(d, axis=1, keepdims=True))
        w_intra = jnp.exp(d - m_t)
        w_inter = jnp.exp(inter - m_t)
        qb = qc.astype(BF16)
        kb = kc.astype(BF16)
        qk = _dot_nt(qb, kb) * w_intra
        num = _dot(qk.astype(BF16), vc) + w_inter * _dot(qb, cmat.astype(BF16))
        den = jnp.sum(qk, axis=1, keepdims=True) + w_inter * jnp.sum(qc * nrow, axis=1, keepdims=True)
        hc = num / jnp.maximum(jnp.abs(den), jnp.exp(-m_t))

        a_col = g_last - gcol + icol
        decay = jnp.exp(g_last + m - m_new)
        wk_col = jnp.exp(a_col - m_new)
        kw = wk_col * kc
        c_new = decay * cmat + _dot(kw.T.astype(BF16), vc)
        n_new = decay * nrow + jnp.sum(kw, axis=0, keepdims=True)

        hn = hc * lax.rsqrt(jnp.mean(hc * hc, axis=1, keepdims=True) + RMS_EPS) * hd.mg
        hn = jax.nn.sigmoid(om_ref[0, sl, hd.cols]) * hn
        o_ref[0, sl, hd.cols] = hn.astype(o_ref.dtype)
        return c_new, n_new

    def chunk_step(c, carry):
        return tuple(chunk_head(hd, c, *carry[i]) for i, hd in enumerate(heads))

    init = tuple((jnp.zeros((HEAD_DIM_M, HEAD_DIM_M), F32), jnp.zeros((1, HEAD_DIM_M), F32)) for _ in heads)
    lax.fori_loop(0, nchunk, chunk_step, init, unroll=8)


def _mlstm(proj3d, gcol3d, grow3d, conv_w, conv_b, wq, wk, bias_col, bias_row, mnorm):
    batch, seq, _ = proj3d.shape
    nh = MLSTM_HEADS_PER_STEP
    width = nh * V7X_LANES
    blk = (1, seq, width)
    a0 = 3 * D_ATTN // width
    ng = N_HEADS_M // nh
    per_head = ([pltpu.VMEM((seq + V7X_SUBLANES, V7X_LANES), F32)]
                + [pltpu.VMEM((seq, V7X_LANES), F32) for _ in range(5)]
                + [pltpu.VMEM((seq // CHUNK, V7X_LANES), F32) for _ in range(6)])
    return pl.pallas_call(
        functools.partial(_mlstm_kernel, seq=seq),
        grid=(batch, ng),
        in_specs=[
            pl.BlockSpec(blk, lambda b, g: (b, 0, a0 + g)),
            pl.BlockSpec(blk, lambda b, g: (b, 0, a0 + ng + g)),
            pl.BlockSpec(blk, lambda b, g: (b, 0, a0 + 2 * ng + g)),
            pl.BlockSpec((1, seq, V7X_LANES), lambda b, g: (b, 0, 0)),
            pl.BlockSpec((1, V7X_SUBLANES, seq), lambda b, g: (b, 0, 0)),
            pl.BlockSpec((CONV_WIDTH, width), lambda b, g: (0, g)),
            pl.BlockSpec((1, width), lambda b, g: (0, g)),
            pl.BlockSpec((nh, HEAD_DIM_M, HEAD_DIM_M), lambda b, g: (g, 0, 0)),
            pl.BlockSpec((nh, HEAD_DIM_M, HEAD_DIM_M), lambda b, g: (g, 0, 0)),
            pl.BlockSpec((1, V7X_LANES), lambda b, g: (0, 0)),
            pl.BlockSpec((V7X_SUBLANES, seq), lambda b, g: (0, 0)),
            pl.BlockSpec((1, width), lambda b, g: (0, g)),
        ],
        out_specs=pl.BlockSpec(blk, lambda b, g: (b, 0, g)),
        out_shape=jax.ShapeDtypeStruct((batch, seq, D_MLSTM), BF16),
        scratch_shapes=per_head * nh,
        compiler_params=_compiler_params(("arbitrary", "arbitrary")),
        name="mlstm",
    )(proj3d, proj3d, proj3d, gcol3d, grow3d, conv_w, conv_b, wq, wk, bias_col, bias_row, mnorm)


def _layer_norm(z, g, b):
    mu = jnp.mean(z, axis=1, keepdims=True)
    zc = z - mu
    var = jnp.mean(zc * zc, axis=1, keepdims=True)
    return zc * lax.rsqrt(var + LN_EPS) * g + b


def _router_kernel(attn_ref, hm_ref, x_ref, woa_ref, wom_ref, g_ref, b_ref, wr_ref, br_ref,
                   h_ref, post_ref, col_ref, tab_ref, tot_ref, carry, *, nsub):
    i = pl.program_id(0)
    tm = TOK_TILE

    @pl.when(i == 0)
    def _():
        carry[...] = jnp.zeros_like(carry)

    ne = N_EXPERTS
    ex = lax.broadcasted_iota(jnp.int32, (ne, tm), 0).astype(F32)
    ri = lax.broadcasted_iota(jnp.int32, (tm, tm), 0)
    ci = lax.broadcasted_iota(jnp.int32, (tm, tm), 1)
    earlier = jnp.where(ri < ci, 1.0, 0.0).astype(BF16)
    er = lax.broadcasted_iota(jnp.int32, (ne, ne), 0)
    ec = lax.broadcasted_iota(jnp.int32, (ne, ne), 1)
    lower = jnp.where(ec < er, 1.0, 0.0).astype(BF16)
    diag = (lax.broadcasted_iota(jnp.int32, (ne, V7X_LANES), 0)
            == lax.broadcasted_iota(jnp.int32, (ne, V7X_LANES), 1))
    sub = lax.broadcasted_iota(jnp.int32, (V7X_SUBLANES, V7X_LANES), 0)
    base = carry[...]

    def to_lanes(colvec):
        return jnp.sum(jnp.where(diag, colvec, 0.0), axis=0, keepdims=True)

    for s in range(nsub):
        rs = slice(s * tm, (s + 1) * tm)
        h_ref[rs, :] = (DEEPNORM_ALPHA * x_ref[rs, :] + _dot(attn_ref[rs, :], woa_ref[...])
                        + _dot(hm_ref[rs, :], wom_ref[...]))
    for s in range(nsub):
        rs = slice(s * tm, (s + 1) * tm)
        hval = _layer_norm(h_ref[rs, :], g_ref[...], b_ref[...])
        h_ref[rs, :] = hval

        logits = _dot(hval.astype(BF16), wr_ref[...]) + br_ref[...]
        lt = logits.T[:ne, :]

        sel, vals = [], []
        for _ in range(TOP_K):
            mx = jnp.max(lt, axis=0, keepdims=True)
            idx = jnp.min(jnp.where(lt == mx, ex, float(ne)), axis=0, keepdims=True)
            hit = ex == idx
            lt = jnp.where(hit, NEG_INF, lt)
            sel.append(hit)
            vals.append(mx)
        exps = [jnp.exp(v - vals[0]) for v in vals]
        den = exps[0] + exps[1] + exps[2] + exps[3]
        gates = [e / den for e in exps]

        onehot = jnp.zeros((ne, tm), F32)
        for hit in sel:
            onehot = jnp.where(hit, 1.0, onehot)
        rank = _dot(onehot.astype(BF16), earlier)
        cnt = jnp.broadcast_to(jnp.sum(onehot, axis=1, keepdims=True), (ne, V7X_LANES))
        cnt_al = jnp.floor((cnt + (RUN_ALIGN - 1)) * (1.0 / RUN_ALIGN))
        slot = _dot(lower, cnt_al.astype(BF16)) * float(RUN_ALIGN)
        cnt_al = cnt_al * float(RUN_ALIGN)
        where_row = slot[:, 0:1] + rank

        rows = [jnp.sum(jnp.where(sel[k], where_row, 0.0), axis=0, keepdims=True) for k in range(TOP_K)]
        info = jnp.concatenate(rows + gates, axis=0)
        post_ref[s] = info
        col_ref[rs, :] = jnp.concatenate([info, jnp.zeros((V7X_LANES - 2 * TOP_K, tm), F32)], axis=0).T

        cnt_row = to_lanes(cnt_al)
        tab = jnp.zeros((V7X_SUBLANES, V7X_LANES), F32)
        tab = jnp.where(sub == 0, cnt_row, tab)
        tab = jnp.where(sub == 1, base, tab)
        tab = jnp.where(sub == 2, to_lanes(slot), tab)
        tab_ref[s] = tab.astype(jnp.int32)
        base = base + cnt_row

    carry[...] = base
    tot_ref[...] = jnp.broadcast_to(base, (V7X_SUBLANES, V7X_LANES)).astype(jnp.int32)


def _outproj_router(attn2d, hm2d, x2d, wo_a, wo_m, ln_g, ln_b, w_r, b_r):
    t = x2d.shape[0]
    nsub = ROUTER_SUBTILES
    tm = TOK_TILE * nsub
    nt = t // TOK_TILE
    const = lambda i: (0, 0)
    return pl.pallas_call(
        functools.partial(_router_kernel, nsub=nsub),
        grid=(t // tm,),
        in_specs=[
            pl.BlockSpec((tm, D_ATTN), lambda i: (i, 0)),
            pl.BlockSpec((tm, D_MLSTM), lambda i: (i, 0)),
            pl.BlockSpec((tm, D_MODEL), lambda i: (i, 0)),
            pl.BlockSpec((D_ATTN, D_MODEL), const),
            pl.BlockSpec((D_MLSTM, D_MODEL), const),
            pl.BlockSpec((1, D_MODEL), const),
            pl.BlockSpec((1, D_MODEL), const),
            pl.BlockSpec((D_MODEL, V7X_LANES), const),
            pl.BlockSpec((1, V7X_LANES), const),
        ],
        out_specs=[
            pl.BlockSpec((tm, D_MODEL), lambda i: (i, 0)),
            pl.BlockSpec((nsub, V7X_SUBLANES, TOK_TILE), lambda i: (i, 0, 0)),
            pl.BlockSpec((tm, V7X_LANES), lambda i: (i, 0)),
            pl.BlockSpec((nsub, V7X_SUBLANES, V7X_LANES), lambda i: (i, 0, 0)),
            pl.BlockSpec((V7X_SUBLANES, V7X_LANES), const),
        ],
        out_shape=[
            jax.ShapeDtypeStruct((t, D_MODEL), F32),
            jax.ShapeDtypeStruct((nt, V7X_SUBLANES, TOK_TILE), F32),
            jax.ShapeDtypeStruct((t, V7X_LANES), F32),
            jax.ShapeDtypeStruct((nt, V7X_SUBLANES, V7X_LANES), jnp.int32),
            jax.ShapeDtypeStruct((V7X_SUBLANES, V7X_LANES), jnp.int32),
        ],
        scratch_shapes=[pltpu.VMEM((1, V7X_LANES), F32)],
        compiler_params=_compiler_params(("arbitrary",)),
        name="outproj_router",
    )(attn2d, hm2d, x2d, wo_a, wo_m, ln_g, ln_b, w_r, b_r)


def _run_copy(src, dst, start_src, start_dst, n, sem):
    n = pl.multiple_of(n, RUN_ALIGN)
    return pltpu.make_async_copy(
        src.at[pl.ds(pl.multiple_of(start_src, RUN_ALIGN), n)],
        dst.at[pl.ds(pl.multiple_of(start_dst, RUN_ALIGN), n)],
        sem)


def _dispatch_kernel(cnt_sm, slot_sm, dst_sm, tail_sm, post_ref, h_ref, xs_hbm, ybuf, zbuf, sem, zsem):
    i = pl.program_id(0)
    nt = pl.num_programs(0)
    cur = i % 2

    n_blocks = xs_hbm.shape[0] // ROW_BLOCK

    def zero_fill(wait):
        def go(cp):
            if wait:
                cp.wait()
            else:
                cp.start()

        def tail(e, carry):
            n = tail_sm[N_EXPERTS + e]

            @pl.when(n > 0)
            def _():
                go(_run_copy(zbuf, xs_hbm, 0, tail_sm[e], n, zsem))
            return carry

        def block(j, carry):
            go(_run_copy(zbuf, xs_hbm, 0, j * ROW_BLOCK, ROW_BLOCK, zsem))
            return carry

        lax.fori_loop(0, N_EXPERTS, tail, 0)
        lax.fori_loop(tail_sm[2 * N_EXPERTS], n_blocks - 1, block, 0)

    @pl.when(i == 0)
    def _():
        for b in range(2):
            ybuf[b, ROWS_LOCAL:ROWS_LOCAL + RUN_ALIGN, :] = jnp.zeros((RUN_ALIGN, D_MODEL), F32)
        zbuf[...] = jnp.zeros_like(zbuf)
        zero_fill(False)

    hb = h_ref[...].astype(BF16)
    pos = post_ref[0]
    chunk = 256
    for rc in range(ROWS_LOCAL // chunk):
        rows = (lax.broadcasted_iota(jnp.int32, (chunk, TOK_TILE), 0) + rc * chunk).astype(F32)
        p = jnp.zeros((chunk, TOK_TILE), F32)
        for k in range(TOP_K):
            p = jnp.where(rows == pos[k:k + 1, :], 1.0, p)
        ybuf[cur, rc * chunk:(rc + 1) * chunk, :] = _dot(p.astype(BF16), hb)

    def total_rows(tile):
        return cnt_sm[nt * N_EXPERTS + tile]

    @pl.when(i > 0)
    def _():
        n_prev = total_rows(i - 1)
        _run_copy(ybuf.at[1 - cur], xs_hbm, 0, 0, n_prev, sem).wait()

    for e in range(N_EXPERTS):
        k = i * N_EXPERTS + e
        _run_copy(ybuf.at[cur], xs_hbm, slot_sm[k], dst_sm[k], cnt_sm[k], sem).start()

    @pl.when(i == nt - 1)
    def _():
        _run_copy(ybuf.at[cur], xs_hbm, 0, 0, total_rows(i), sem).wait()
        zero_fill(True)
        last = _run_copy(zbuf, xs_hbm, 0, (n_blocks - 1) * ROW_BLOCK, ROW_BLOCK, zsem)
        last.start()
        last.wait()


def _dispatch(cnt, slot, dst, tail, post, h2d, n_rows):
    t = h2d.shape[0]
    nt = t // TOK_TILE
    return pl.pallas_call(
        _dispatch_kernel,
        grid_spec=pltpu.PrefetchScalarGridSpec(
            num_scalar_prefetch=4,
            grid=(nt,),
            in_specs=[
                pl.BlockSpec((1, V7X_SUBLANES, TOK_TILE), lambda i, *_: (i, 0, 0)),
                pl.BlockSpec((TOK_TILE, D_MODEL), lambda i, *_: (i, 0)),
            ],
            out_specs=pl.BlockSpec(memory_space=pl.ANY),
            scratch_shapes=[
                pltpu.VMEM((2, ROWS_LOCAL + RUN_ALIGN, D_MODEL), F32),
                pltpu.VMEM((ROW_BLOCK, D_MODEL), F32),
                pltpu.SemaphoreType.DMA(()),
                pltpu.SemaphoreType.DMA(()),
            ],
        ),
        out_shape=jax.ShapeDtypeStruct((n_rows, D_MODEL), F32),
        compiler_params=_compiler_params(("arbitrary",)),
        name="moe_dispatch",
    )(cnt, slot, dst, tail, post, h2d)


def _expert_kernel(bexp_sm, nused_sm, nvalid_sm, xs_ref, wg_ref, bg_ref, wu_ref, bu_ref, wd_ref, bd_ref, ys_ref,
                   hbuf, wg_b, wu_b, wd_b):
    j = pl.program_id(0)
    used = j < nused_sm[0]
    new_expert = jnp.logical_or(j == 0, bexp_sm[j] != bexp_sm[jnp.maximum(j, 1) - 1])

    @pl.when(jnp.logical_and(used, new_expert))
    def _():
        def cast(i, carry):
            sl = pl.ds(pl.multiple_of(i * 128, 128), 128)
            wg_b[sl, :] = wg_ref[0, sl, :].astype(BF16)
            wu_b[sl, :] = wu_ref[0, sl, :].astype(BF16)
            wd_b[sl, :] = wd_ref[0, sl, :].astype(BF16)
            return carry
        lax.fori_loop(0, D_MODEL // 128, cast, 0)

    def mlp(rows):
        xb = xs_ref[0:rows, :].astype(BF16)
        step = 512
        for n in range(0, D_MODEL, step):
            g = _dot(xb, wg_b[:, n:n + step]) + bg_ref[0, :, n:n + step]
            u = _dot(xb, wu_b[:, n:n + step]) + bu_ref[0, :, n:n + step]
            g = jnp.minimum(g, SWIGLU_LIMIT)
            u = jnp.clip(u, -SWIGLU_LIMIT, SWIGLU_LIMIT)
            hbuf[0:rows, n:n + step] = (g * jax.nn.sigmoid(SWIGLU_ALPHA * g) * (u + 1.0)).astype(BF16)
        ys_ref[0:rows, :] = _dot(hbuf[0:rows, :], wd_b[...]) + bd_ref[0]

    half = ROW_BLOCK // 2
    upper_rows = nvalid_sm[j] > half

    @pl.when(jnp.logical_and(used, upper_rows))
    def _():
        mlp(ROW_BLOCK)

    @pl.when(jnp.logical_and(used, jnp.logical_not(upper_rows)))
    def _():
        mlp(half)
        ys_ref[half:ROW_BLOCK, :] = jnp.zeros((ROW_BLOCK - half, D_MODEL), F32)


def _experts(bexp, nused, nvalid, xs, wg, bg, wu, bu, wd, bd):
    n_rows = xs.shape[0]
    nb = n_rows // ROW_BLOCK

    def row_map(j, be, nu, nv):
        return (jnp.minimum(j, nu[0] - 1), 0)

    def w_map(j, be, nu, nv):
        return (be[jnp.minimum(j, nu[0] - 1)], 0, 0)

    wspec = pl.BlockSpec((1, D_MODEL, D_MODEL), w_map)
    bspec = pl.BlockSpec((1, 1, D_MODEL), w_map)
    return pl.pallas_call(
        _expert_kernel,
        grid_spec=pltpu.PrefetchScalarGridSpec(
            num_scalar_prefetch=3,
            grid=(nb,),
            in_specs=[pl.BlockSpec((ROW_BLOCK, D_MODEL), row_map), wspec, bspec, wspec, bspec, wspec, bspec],
            out_specs=pl.BlockSpec((ROW_BLOCK, D_MODEL), row_map),
            scratch_shapes=[pltpu.VMEM((ROW_BLOCK, D_MODEL), BF16)]
            + [pltpu.VMEM((D_MODEL, D_MODEL), BF16) for _ in range(3)],
        ),
        out_shape=jax.ShapeDtypeStruct((n_rows, D_MODEL), F32),
        input_output_aliases={3: 0},
        compiler_params=_compiler_params(("arbitrary",)),
        name="moe_experts",
    )(bexp, nused, nvalid, xs, wg, bg, wu, bu, wd, bd)


COMBINE_TILES = 2


def _combine_kernel(cnt_sm, slot_sm, dst_sm, col_ref, h_ref, g_ref, b_ref, ys_hbm, o_ref, ybuf, sbuf, sem):
    i = pl.program_id(0)
    ns = pl.num_programs(0)
    nt = ns * COMBINE_TILES
    cur = i % 2

    def fetch(step, buf):
        for s in range(COMBINE_TILES):
            for e in range(N_EXPERTS):
                k = (step * COMBINE_TILES + s) * N_EXPERTS + e
                _run_copy(ys_hbm, ybuf.at[buf, s], dst_sm[k], slot_sm[k], cnt_sm[k], sem.at[buf]).start()

    @pl.when(i == 0)
    def _():
        ybuf[...] = jnp.zeros_like(ybuf)
        fetch(0, 0)

    @pl.when(i + 1 < ns)
    def _():
        fetch(i + 1, 1 - cur)

    for s in range(COMBINE_TILES):
        _run_copy(ys_hbm, ybuf.at[cur, s], 0, 0, cnt_sm[nt * N_EXPERTS + i * COMBINE_TILES + s], sem.at[cur]).wait()

    chunk = 256
    for s in range(COMBINE_TILES):
        rs = slice(s * TOK_TILE, (s + 1) * TOK_TILE)
        col = col_ref[rs, :]
        for rc in range(ROWS_LOCAL // chunk):
            rows = (lax.broadcasted_iota(jnp.int32, (TOK_TILE, chunk), 1) + rc * chunk).astype(F32)
            sel = jnp.zeros((TOK_TILE, chunk), F32)
            for k in range(TOP_K):
                sel = jnp.where(rows == col[:, k:k + 1], col[:, TOP_K + k:TOP_K + k + 1], sel)
            sbuf[s, :, rc * chunk:(rc + 1) * chunk] = sel.astype(BF16)
    for s in range(COMBINE_TILES):
        rs = slice(s * TOK_TILE, (s + 1) * TOK_TILE)
        for n in range(0, D_MODEL, chunk):
            moe = _dot(sbuf[s], ybuf[cur, s, 0:ROWS_LOCAL, n:n + chunk].astype(BF16))
            o_ref[rs, n:n + chunk] = DEEPNORM_ALPHA * h_ref[rs, n:n + chunk] + moe
    for s in range(COMBINE_TILES):
        rs = slice(s * TOK_TILE, (s + 1) * TOK_TILE)
        o_ref[rs, :] = _layer_norm(o_ref[rs, :], g_ref[...], b_ref[...])


def _combine(cnt, slot, dst, col, h2d, ln_g, ln_b, ys):
    t = h2d.shape[0]
    tm = TOK_TILE * COMBINE_TILES
    return pl.pallas_call(
        _combine_kernel,
        grid_spec=pltpu.PrefetchScalarGridSpec(
            num_scalar_prefetch=3,
            grid=(t // tm,),
            in_specs=[
                pl.BlockSpec((tm, V7X_LANES), lambda i, *_: (i, 0)),
                pl.BlockSpec((tm, D_MODEL), lambda i, *_: (i, 0)),
                pl.BlockSpec((1, D_MODEL), lambda i, *_: (0, 0)),
                pl.BlockSpec((1, D_MODEL), lambda i, *_: (0, 0)),
                pl.BlockSpec(memory_space=pl.ANY),
            ],
            out_specs=pl.BlockSpec((tm, D_MODEL), lambda i, *_: (i, 0)),
            scratch_shapes=[
                pltpu.VMEM((2, COMBINE_TILES, ROWS_LOCAL + RUN_ALIGN, D_MODEL), F32),
                pltpu.VMEM((COMBINE_TILES, TOK_TILE, ROWS_LOCAL), BF16),
                pltpu.SemaphoreType.DMA((2,)),
            ],
        ),
        out_shape=jax.ShapeDtypeStruct((t, D_MODEL), F32),
        compiler_params=_compiler_params(("arbitrary",)),
        name="moe_combine",
    )(cnt, slot, dst, col, h2d, ln_g, ln_b, ys)


def _pad_lanes(a, width=V7X_LANES):
    return jnp.pad(a, ((0, 0), (0, width - a.shape[1])))


def _layer(h3d, w_in, conv_w, conv_b, w_mq, w_mk, b_igate, b_fgate, mnorm_g, w_out,
           ln1_g, ln1_b, w_router, b_router, w_gate, b_gate, w_up, b_up, w_down, b_down, ln2_g, ln2_b):
    batch, seq, _ = h3d.shape
    t = batch * seq
    x2d = h3d.reshape(t, D_MODEL)

    w_main = w_in[:, :PROJ_MAIN].astype(BF16)
    w_gates = _pad_lanes(w_in[:, PROJ_MAIN:]).astype(BF16)
    proj, gcol, grow = _inproj(x2d, w_main, w_gates, batch, seq)
    proj3d = proj.reshape(batch, seq, PROJ_MAIN)
    attn = _attention(proj3d)
    gate_bias = jnp.concatenate([b_igate, b_fgate]).astype(F32)
    bias_col = _pad_lanes(gate_bias[None, :])
    bias_row = jnp.broadcast_to(gate_bias[:, None], (V7X_SUBLANES, seq))
    hm = _mlstm(proj3d, gcol.reshape(batch, seq, V7X_LANES), grow, conv_w, conv_b[None, :],
                w_mq.astype(BF16), w_mk.astype(BF16), bias_col, bias_row, mnorm_g[None, :])

    wo = w_out.astype(BF16)
    h2d, post, col, tab, tot = _outproj_router(
        attn.reshape(t, D_ATTN), hm.reshape(t, D_MLSTM), x2d, wo[:D_ATTN], wo[D_ATTN:],
        ln1_g[None, :], ln1_b[None, :], _pad_lanes(w_router).astype(BF16), _pad_lanes(b_router[None, :]))

    nt = t // TOK_TILE
    total = tot[0, :N_EXPERTS]
    region = (total + ROW_BLOCK - 1) // ROW_BLOCK * ROW_BLOCK
    region_end = jnp.cumsum(region)
    region_start = region_end - region
    n_rows = (t * TOP_K + nt * N_EXPERTS * (RUN_ALIGN - 1)) // ROW_BLOCK * ROW_BLOCK + (N_EXPERTS + 1) * ROW_BLOCK
    nb = n_rows // ROW_BLOCK
    runs = tab[:, 0, :N_EXPERTS]
    empty = runs == 0
    cnt = jnp.where(empty, RUN_ALIGN, runs)
    cnt = jnp.concatenate([cnt.reshape(-1), jnp.sum(cnt, axis=1)])
    slot = jnp.where(empty, ROWS_LOCAL, tab[:, 2, :N_EXPERTS]).reshape(-1)
    dst = jnp.where(empty, n_rows - ROW_BLOCK, tab[:, 1, :N_EXPERTS] + region_start[None, :]).reshape(-1)
    tail = jnp.concatenate([region_start + total, region - total, region_end[-1:] // ROW_BLOCK]).astype(jnp.int32)
    block_row = jnp.arange(nb, dtype=jnp.int32) * ROW_BLOCK
    bexp = jnp.minimum(jnp.sum(region_end[None, :] <= block_row[:, None], axis=1), N_EXPERTS - 1).astype(jnp.int32)
    nused = (region_end[-1:] // ROW_BLOCK).astype(jnp.int32)
    own = bexp[:, None] == jnp.arange(N_EXPERTS, dtype=jnp.int32)[None, :]
    used_end = jnp.sum(jnp.where(own, (region_start + total)[None, :], 0), axis=1)
    nvalid = jnp.clip(used_end - block_row, 0, ROW_BLOCK).astype(jnp.int32)

    xs = _dispatch(cnt, slot, dst, tail, post, h2d, n_rows)
    ys = _experts(bexp, nused, nvalid, xs, w_gate, b_gate[:, None, :], w_up, b_up[:, None, :], w_down, b_down[:, None, :])
    out = _combine(cnt, slot, dst, col, h2d, ln2_g[None, :], ln2_b[None, :], ys)
    return out.reshape(batch, seq, D_MODEL)


def kernel(x, w_in, conv_w, conv_b, w_mq, w_mk, b_igate, b_fgate, mnorm_g, w_out, ln1_g, ln1_b, w_router, b_router, w_gate, b_gate, w_up, b_up, w_down, b_down, ln2_g, ln2_b):
    h = x
    for l in range(w_in.shape[0]):
        h = _layer(h, w_in[l], conv_w[l], conv_b[l], w_mq[l], w_mk[l], b_igate[l], b_fgate[l], mnorm_g[l],
                   w_out[l], ln1_g[l], ln1_b[l], w_router[l], b_router[l], w_gate[l], b_gate[l], w_up[l],
                   b_up[l], w_down[l], b_down[l], ln2_g[l], ln2_b[l])
    return h
```

```python
import functools
import math

import jax
import jax.numpy as jnp
from jax import lax
from jax.experimental import pallas as pl
from jax.experimental.pallas import tpu as pltpu

F32 = jnp.float32
BF16 = jnp.bfloat16
NEG_INF = float("-inf")

V7X_LANES = 128
V7X_SUBLANES = 8
V7X_VMEM_LIMIT_BYTES = 56 * 1024 * 1024

D_MODEL = 1024
D_ATTN = 512
HEAD_DIM_A = 64
D_MLSTM = 512
N_HEADS_M = 4
HEAD_DIM_M = 128
CONV_WIDTH = 4
CHUNK = 128
DILATED_BRANCHES = ((128, 1), (512, 4), (2048, 16))
N_EXPERTS = 32
TOP_K = 4
SWIGLU_LIMIT = 7.0
SWIGLU_ALPHA = 1.702
DEEPNORM_ALPHA = 2.0 ** 0.25
LN_EPS = 1e-5
RMS_EPS = 1e-6

PROJ_MAIN = 3 * D_ATTN + 3 * D_MLSTM
TOK_TILE = 256
RUN_ALIGN = V7X_SUBLANES
ROWS_LOCAL = TOK_TILE * TOP_K + N_EXPERTS * RUN_ALIGN
ROW_BLOCK = 512
ROUTER_SUBTILES = 4


def _dot(a, b):
    return jnp.dot(a, b, preferred_element_type=F32)


def _dot_nt(a, b):
    return lax.dot_general(a, b, (((1,), (1,)), ((), ())), preferred_element_type=F32)


def _split3(x):
    hi = x.astype(BF16)
    r1 = x - hi.astype(F32)
    mid = r1.astype(BF16)
    lo = (r1 - mid.astype(F32)).astype(BF16)
    return hi, mid, lo


def _compiler_params(sem):
    return pltpu.CompilerParams(dimension_semantics=sem, vmem_limit_bytes=V7X_VMEM_LIMIT_BYTES)


def _inproj_kernel(x_ref, w_ref, wg_ref, proj_ref, gcol_ref, grow_ref):
    xb = x_ref[...].astype(BF16)
    step = 512
    for n in range(0, PROJ_MAIN, step):
        proj_ref[:, n:n + step] = _dot(xb, w_ref[:, n:n + step])
    g = _dot(xb, wg_ref[...])
    gcol_ref[...] = g
    grow_ref[0] = g.T[:V7X_SUBLANES, :]


def _inproj(x2d, w_main, w_gates, batch, seq):
    t = x2d.shape[0]
    tm = 512
    per_b = seq // tm
    return pl.pallas_call(
        _inproj_kernel,
        grid=(t // tm,),
        in_specs=[
            pl.BlockSpec((tm, D_MODEL), lambda i: (i, 0)),
            pl.BlockSpec((D_MODEL, PROJ_MAIN), lambda i: (0, 0)),
            pl.BlockSpec((D_MODEL, V7X_LANES), lambda i: (0, 0)),
        ],
        out_specs=[
            pl.BlockSpec((tm, PROJ_MAIN), lambda i: (i, 0)),
            pl.BlockSpec((tm, V7X_LANES), lambda i: (i, 0)),
            pl.BlockSpec((1, V7X_SUBLANES, tm), lambda i: (i // per_b, 0, i % per_b)),
        ],
        out_shape=[
            jax.ShapeDtypeStruct((t, PROJ_MAIN), F32),
            jax.ShapeDtypeStruct((t, V7X_LANES), F32),
            jax.ShapeDtypeStruct((batch, V7X_SUBLANES, seq), F32),
        ],
        compiler_params=_compiler_params(("arbitrary",)),
        name="inproj",
    )(x2d, w_main, w_gates)


ATTN_GROUP = 8


def _attn_kernel(q_ref, k_ref, v_ref, o_ref, q0s, q1s, bias_b, bias_f, sg, pg,
                 o0, o1, o2, l0, l1, l2, m0, m1, m2, x0, x1, x2, *, seq):
    obufs, lbufs, mbufs, xbufs = (o0, o1, o2), (l0, l1, l2), (m0, m1, m2), (x0, x1, x2)
    two = 2 * CHUNK
    head0 = lax.broadcasted_iota(jnp.int32, (CHUNK, V7X_LANES), 1) < HEAD_DIM_A
    qscale = HEAD_DIM_A ** -0.5 * math.log2(math.e)

    def prep(i, carry):
        sl = pl.ds(pl.multiple_of(i * two, two), two)
        h0 = lax.broadcasted_iota(jnp.int32, (two, V7X_LANES), 1) < HEAD_DIM_A
        q = q_ref[0, sl, :] * qscale
        q0s[sl, :] = jnp.where(h0, q, 0.0)
        q1s[sl, :] = jnp.where(h0, 0.0, q)
        return carry

    lax.fori_loop(0, seq // two, prep, 0)

    qi = lax.broadcasted_iota(jnp.int32, (two, two), 0) % CHUNK
    kj = lax.broadcasted_iota(jnp.int32, (two, two), 1)
    bias_b[...] = jnp.where((kj >= qi) & (kj <= qi + CHUNK), 0.0, NEG_INF)
    qf = lax.broadcasted_iota(jnp.int32, (two, CHUNK), 0) % CHUNK
    kf = lax.broadcasted_iota(jnp.int32, (two, CHUNK), 1)
    bias_f[...] = jnp.where(kf <= qf, 0.0, NEG_INF)

    def run_group(c, dil, starts, has_prev):
        assert len(starts) <= ATTN_GROUP
        nk = two if has_prev else CHUNK
        bias_ref = bias_b if has_prev else bias_f

        def rows(s0):
            return pl.ds(s0, CHUNK) if dil == 1 else pl.ds(s0, CHUNK, stride=dil)

        def keys(ref, st):
            if has_prev:
                return jnp.concatenate([ref[0, rows(st - dil * CHUNK), :], ref[0, rows(st), :]], axis=0).astype(BF16)
            return ref[0, rows(st), :].astype(BF16)

        def both_heads(x):
            return jnp.where(head0, x[:CHUNK], x[CHUNK:])

        for j, st in enumerate(starts):
            q2 = jnp.concatenate([q0s[rows(st), :], q1s[rows(st), :]], axis=0).astype(BF16)
            sg[j, :, 0:nk] = _dot_nt(q2, keys(k_ref, st)) + bias_ref[...]
        for j, st in enumerate(starts):
            s = sg[j, :, 0:nk]
            m = jnp.max(s, axis=1, keepdims=True)
            pg[j, :, 0:nk] = jnp.exp2(s - m).astype(BF16)
            ma = jnp.broadcast_to(m[:CHUNK], (CHUNK, V7X_LANES))
            mb = jnp.broadcast_to(m[CHUNK:], (CHUNK, V7X_LANES))
            mbufs[c][rows(st), :] = jnp.where(head0, ma, mb)
            xbufs[c][rows(st), :] = jnp.where(head0, mb, ma)
        khead0 = lax.broadcasted_iota(jnp.int32, (nk, V7X_LANES), 1) < HEAD_DIM_A
        for j, st in enumerate(starts):
            if has_prev:
                vv = jnp.concatenate([v_ref[0, rows(st - dil * CHUNK), :], v_ref[0, rows(st), :]], axis=0)
            else:
                vv = v_ref[0, rows(st), :]
            oa = _dot(pg[j, 0:CHUNK, 0:nk], jnp.where(khead0, vv, 1.0).astype(BF16))
            ob = _dot(pg[j, CHUNK:two, 0:nk], jnp.where(khead0, 1.0, vv).astype(BF16))
            obufs[c][rows(st), :] = jnp.where(head0, oa, ob)
            lbufs[c][rows(st), :] = jnp.where(head0, ob, oa)

    for c, (window, dil) in enumerate(DILATED_BRANCHES):
        assert window // dil == CHUNK
        nb = seq // (dil * CHUNK)
        span = dil * CHUNK
        if dil == 1:
            run_group(c, dil, [0], False)
            per = 5
            assert (nb - 1) % per == 0

            def band1(g, carry, c=c, dil=dil, per=per, span=span):
                base = span + g * (per * span)
                run_group(c, dil, [base + span * j for j in range(per)], True)
                return carry

            lax.fori_loop(0, (nb - 1) // per, band1, 0, unroll=True)
        elif nb > 1:
            run_group(c, dil, list(range(dil)), False)

            def band2(g, carry, c=c, dil=dil, nb=nb, span=span):
                starts = [2 * g + rr + span * b for rr in range(2) for b in range(1, nb)]
                run_group(c, dil, starts, True)
                return carry

            lax.fori_loop(0, dil // 2, band2, 0, unroll=True)
        else:
            def firsts(g, carry, c=c, dil=dil):
                run_group(c, dil, [ATTN_GROUP * g + j for j in range(ATTN_GROUP)], False)
                return carry

            lax.fori_loop(0, dil // ATTN_GROUP, firsts, 0)

    def combine(i, carry):
        sl = pl.ds(pl.multiple_of(i * 256, 256), 256)
        ma, mb, mc = m0[sl, :], m1[sl, :], m2[sl, :]
        mx = jnp.maximum(jnp.maximum(ma, mb), mc)
        wa, wb, wc = jnp.exp2(ma - mx), jnp.exp2(mb - mx), jnp.exp2(mc - mx)
        xa, xb, xc = x0[sl, :], x1[sl, :], x2[sl, :]
        xm = jnp.maximum(jnp.maximum(xa, xb), xc)
        den = (jnp.exp2(xa - xm) * l0[sl, :] + jnp.exp2(xb - xm) * l1[sl, :] + jnp.exp2(xc - xm) * l2[sl, :])
        den = pltpu.roll(den, HEAD_DIM_A, axis=1)
        out = (wa * o0[sl, :] + wb * o1[sl, :] + wc * o2[sl, :]) / den
        o_ref[0, sl, :] = out.astype(o_ref.dtype)
        return carry

    lax.fori_loop(0, seq // 256, combine, 0)


def _attention(proj3d):
    batch, seq, _ = proj3d.shape
    nblk = D_ATTN // V7X_LANES
    blk = (1, seq, V7X_LANES)
    scratch = ([pltpu.VMEM((seq, V7X_LANES), F32), pltpu.VMEM((seq, V7X_LANES), F32),
                pltpu.VMEM((2 * CHUNK, 2 * CHUNK), F32), pltpu.VMEM((2 * CHUNK, CHUNK), F32),
                pltpu.VMEM((ATTN_GROUP, 2 * CHUNK, 2 * CHUNK), F32),
                pltpu.VMEM((ATTN_GROUP, 2 * CHUNK, 2 * CHUNK), BF16)]
               + [pltpu.VMEM((seq, V7X_LANES), F32) for _ in range(12)])
    return pl.pallas_call(
        functools.partial(_attn_kernel, seq=seq),
        grid=(batch, nblk),
        in_specs=[
            pl.BlockSpec(blk, lambda b, g: (b, 0, g)),
            pl.BlockSpec(blk, lambda b, g: (b, 0, nblk + g)),
            pl.BlockSpec(blk, lambda b, g: (b, 0, 2 * nblk + g)),
        ],
        out_specs=pl.BlockSpec(blk, lambda b, g: (b, 0, g)),
        out_shape=jax.ShapeDtypeStruct((batch, seq, D_ATTN), BF16),
        scratch_shapes=scratch,
        compiler_params=_compiler_params(("arbitrary", "arbitrary")),
        name="dilated_attention",
    )(proj3d, proj3d, proj3d)


def _log_sigmoid(x):
    return jnp.minimum(x, 0.0) - jnp.log1p(jnp.exp(-jnp.abs(x)))


MLSTM_HEADS_PER_STEP = 2


def _mlstm_kernel(xm_ref, vm_ref, om_ref, gcol_ref, grow_ref, cw_ref, cb_ref, wq_ref, wk_ref,
                  bcol_ref, brow_ref, mg_ref, o_ref, *scratch, seq):
    nh = MLSTM_HEADS_PER_STEP
    per_head = len(scratch) // nh
    nchunk = seq // CHUNK
    ri = lax.broadcasted_iota(jnp.int32, (CHUNK, CHUNK), 0)
    ci = lax.broadcasted_iota(jnp.int32, (CHUNK, CHUNK), 1)
    tril = ri >= ci
    tril_b = jnp.where(tril, 1.0, 0.0).astype(BF16)
    triu_b = jnp.where(ri <= ci, 1.0, 0.0).astype(BF16)
    pad = V7X_SUBLANES
    qscale = HEAD_DIM_M ** -0.5

    class Head:
        def __init__(self, hh):
            (self.xpad, self.q_s, self.k_s, self.icol_s, self.lfcol_s, self.gcol_s,
             self.ri_s, self.rf_s, self.gf_s, self.mi_s, self.mo_s, self.gl_s) = scratch[hh * per_head:(hh + 1) * per_head]
            self.h = pl.program_id(1) * nh + hh
            self.cols = slice(hh * V7X_LANES, (hh + 1) * V7X_LANES)
            self.wq = wq_ref[hh]
            self.wk = wk_ref[hh]
            self.mg = mg_ref[:, self.cols]

    heads = [Head(hh) for hh in range(nh)]

    for hd in heads:
        hd.xpad[0:pad, :] = jnp.zeros((pad, V7X_LANES), F32)
        hd.xpad[pad:pad + seq, :] = xm_ref[0, :, hd.cols]

    def conv_step(i, carry):
        base = pl.multiple_of(i * 256, 256)
        rows = pl.ds(base, 256)
        gc = gcol_ref[0, rows, :] + bcol_ref[...]
        lane256 = lax.broadcasted_iota(jnp.int32, (256, V7X_LANES), 1)
        for hd in heads:
            y = jnp.broadcast_to(cb_ref[:, hd.cols], (256, V7X_LANES))
            xw = hd.xpad[pl.ds(base, 256 + pad), :]
            for j in range(CONV_WIDTH):
                off = pad - (CONV_WIDTH - 1) + j
                y = y + cw_ref[j:j + 1, hd.cols] * xw[off:off + 256, :]
            xc = (y * jax.nn.sigmoid(y)).astype(BF16)
            hd.q_s[rows, :] = _dot(xc, hd.wq) * qscale
            hd.k_s[rows, :] = _dot(xc, hd.wk)
            ic = jnp.sum(jnp.where(lane256 == hd.h, gc, 0.0), axis=1, keepdims=True)
            fc = jnp.sum(jnp.where(lane256 == hd.h + N_HEADS_M, gc, 0.0), axis=1, keepdims=True)
            hd.icol_s[rows, :] = jnp.broadcast_to(ic, (256, V7X_LANES))
            hd.lfcol_s[rows, :] = jnp.broadcast_to(_log_sigmoid(fc), (256, V7X_LANES))
        return carry

    lax.fori_loop(0, seq // 256, conv_step, 0)

    gr = grow_ref[0] + brow_ref[...]
    sub = lax.broadcasted_iota(jnp.int32, (V7X_SUBLANES, seq), 0)
    for hd in heads:
        irow = jnp.sum(jnp.where(sub == hd.h, gr, 0.0), axis=0, keepdims=True)
        lfrow = _log_sigmoid(jnp.sum(jnp.where(sub == hd.h + N_HEADS_M, gr, 0.0), axis=0, keepdims=True))
        for c in range(nchunk):
            hd.ri_s[c:c + 1, :] = irow[:, c * CHUNK:(c + 1) * CHUNK]
            hd.rf_s[c:c + 1, :] = lfrow[:, c * CHUNK:(c + 1) * CHUNK]

    def cum_step(c, carry):
        sl = pl.ds(pl.multiple_of(c * CHUNK, CHUNK), CHUNK)
        for hd in heads:
            hd.gcol_s[sl, :] = sum(_dot(tril_b, part) for part in _split3(hd.lfcol_s[sl, :]))
        return carry

    lax.fori_loop(0, nchunk, cum_step, 0, unroll=2)

    crow = lax.broadcasted_iota(jnp.int32, (nchunk, V7X_LANES), 0)
    for hd in heads:
        irows = hd.ri_s[...]
        grows = sum(_dot(part, triu_b) for part in _split3(hd.rf_s[...]))
        hd.gf_s[...] = grows
        g_last = grows[:, CHUNK - 1:CHUNK]
        a_max = jnp.max(g_last - grows + irows, axis=1, keepdims=True)
        m = jnp.zeros((1, 1), F32)
        m_in = jnp.zeros((nchunk, V7X_LANES), F32)
        m_out = jnp.zeros((nchunk, V7X_LANES), F32)
        for c in range(nchunk):
            m_in = jnp.where(crow == c, m, m_in)
            m = jnp.maximum(g_last[c:c + 1, :] + m, a_max[c:c + 1, :])
            m_out = jnp.where(crow == c, m, m_out)
        hd.mi_s[...] = m_in
        hd.mo_s[...] = m_out
        hd.gl_s[...] = jnp.broadcast_to(g_last, (nchunk, V7X_LANES))

    def chunk_head(hd, c, cmat, nrow):
        sl = pl.ds(pl.multiple_of(c * CHUNK, CHUNK), CHUNK)
        one = pl.ds(c, 1)
        qc = hd.q_s[sl, :]
        kc = hd.k_s[sl, :]
        vc = vm_ref[0, sl, hd.cols].astype(BF16)
        icol = hd.icol_s[sl, :]
        gcol = hd.gcol_s[sl, :]
        irow_c = hd.ri_s[one, :]
        grow_c = hd.gf_s[one, :]
        m = hd.mi_s[one, :]
        m_new = hd.mo_s[one, :]
        g_last = hd.gl_s[one, :]

        d = gcol - grow_c + irow_c
        d = jnp.where(tril, d, NEG_INF)
        inter = gcol + m
        m_t = jnp.maximum(inter, jnp.max(d, axis=1, keepdims=True))
        w_intra = jnp.exp(d - m_t)
        w_inter = jnp.exp(inter - m_t)
        qb = qc.astype(BF16)
        kb = kc.astype(BF16)
        qk = _dot_nt(qb, kb) * w_intra
        num = _dot(qk.astype(BF16), vc) + w_inter * _dot(qb, cmat.astype(BF16))
        den = jnp.sum(qk, axis=1, keepdims=True) + w_inter * jnp.sum(qc * nrow, axis=1, keepdims=True)
        hc = num / jnp.maximum(jnp.abs(den), jnp.exp(-m_t))

        a_col = g_last - gcol + icol
        decay = jnp.exp(g_last + m - m_new)
        wk_col = jnp.exp(a_col - m_new)
        kw = wk_col * kc
        c_new = decay * cmat + _dot(kw.T.astype(BF16), vc)
        n_new = decay * nrow + jnp.sum(kw, axis=0, keepdims=True)

        hn = hc * lax.rsqrt(jnp.mean(hc * hc, axis=1, keepdims=True) + RMS_EPS) * hd.mg
        hn = jax.nn.sigmoid(om_ref[0, sl, hd.cols]) * hn
        o_ref[0, sl, hd.cols] = hn.astype(o_ref.dtype)
        return c_new, n_new

    def chunk_step(c, carry):
        return tuple(chunk_head(hd, c, *carry[i]) for i, hd in enumerate(heads))

    init = tuple((jnp.zeros((HEAD_DIM_M, HEAD_DIM_M), F32), jnp.zeros((1, HEAD_DIM_M), F32)) for _ in heads)
    lax.fori_loop(0, nchunk, chunk_step, init, unroll=8)


def _mlstm(proj3d, gcol3d, grow3d, conv_w, conv_b, wq, wk, bias_col, bias_row, mnorm):
    batch, seq, _ = proj3d.shape
    nh = MLSTM_HEADS_PER_STEP
    width = nh * V7X_LANES
    blk = (1, seq, width)
    a0 = 3 * D_ATTN // width
    ng = N_HEADS_M // nh
    per_head = ([pltpu.VMEM((seq + V7X_SUBLANES, V7X_LANES), F32)]
                + [pltpu.VMEM((seq, V7X_LANES), F32) for _ in range(5)]
                + [pltpu.VMEM((seq // CHUNK, V7X_LANES), F32) for _ in range(6)])
    return pl.pallas_call(
        functools.partial(_mlstm_kernel, seq=seq),
        grid=(batch, ng),
        in_specs=[
            pl.BlockSpec(blk, lambda b, g: (b, 0, a0 + g)),
            pl.BlockSpec(blk, lambda b, g: (b, 0, a0 + ng + g)),
            pl.BlockSpec(blk, lambda b, g: (b, 0, a0 + 2 * ng + g)),
            pl.BlockSpec((1, seq, V7X_LANES), lambda b, g: (b, 0, 0)),
            pl.BlockSpec((1, V7X_SUBLANES, seq), lambda b, g: (b, 0, 0)),
            pl.BlockSpec((CONV_WIDTH, width), lambda b, g: (0, g)),
            pl.BlockSpec((1, width), lambda b, g: (0, g)),
            pl.BlockSpec((nh, HEAD_DIM_M, HEAD_DIM_M), lambda b, g: (g, 0, 0)),
            pl.BlockSpec((nh, HEAD_DIM_M, HEAD_DIM_M), lambda b, g: (g, 0, 0)),
            pl.BlockSpec((1, V7X_LANES), lambda b, g: (0, 0)),
            pl.BlockSpec((V7X_SUBLANES, seq), lambda b, g: (0, 0)),
            pl.BlockSpec((1, width), lambda b, g: (0, g)),
        ],
        out_specs=pl.BlockSpec(blk, lambda b, g: (b, 0, g)),
        out_shape=jax.ShapeDtypeStruct((batch, seq, D_MLSTM), BF16),
        scratch_shapes=per_head * nh,
        compiler_params=_compiler_params(("arbitrary", "arbitrary")),
        name="mlstm",
    )(proj3d, proj3d, proj3d, gcol3d, grow3d, conv_w, conv_b, wq, wk, bias_col, bias_row, mnorm)


def _layer_norm(z, g, b):
    mu = jnp.mean(z, axis=1, keepdims=True)
    zc = z - mu
    var = jnp.mean(zc * zc, axis=1, keepdims=True)
    return zc * lax.rsqrt(var + LN_EPS) * g + b


def _router_kernel(attn_ref, hm_ref, x_ref, woa_ref, wom_ref, g_ref, b_ref, wr_ref, br_ref,
                   h_ref, post_ref, col_ref, tab_ref, tot_ref, carry, *, nsub):
    i = pl.program_id(0)
    tm = TOK_TILE

    @pl.when(i == 0)
    def _():
        carry[...] = jnp.zeros_like(carry)

    ne = N_EXPERTS
    ex = lax.broadcasted_iota(jnp.int32, (ne, tm), 0).astype(F32)
    ri = lax.broadcasted_iota(jnp.int32, (tm, tm), 0)
    ci = lax.broadcasted_iota(jnp.int32, (tm, tm), 1)
    earlier = jnp.where(ri < ci, 1.0, 0.0).astype(BF16)
    er = lax.broadcasted_iota(jnp.int32, (ne, ne), 0)
    ec = lax.broadcasted_iota(jnp.int32, (ne, ne), 1)
    lower = jnp.where(ec < er, 1.0, 0.0).astype(BF16)
    diag = (lax.broadcasted_iota(jnp.int32, (ne, V7X_LANES), 0)
            == lax.broadcasted_iota(jnp.int32, (ne, V7X_LANES), 1))
    sub = lax.broadcasted_iota(jnp.int32, (V7X_SUBLANES, V7X_LANES), 0)
    base = carry[...]

    def to_lanes(colvec):
        return jnp.sum(jnp.where(diag, colvec, 0.0), axis=0, keepdims=True)

    for s in range(nsub):
        rs = slice(s * tm, (s + 1) * tm)
        h_ref[rs, :] = (DEEPNORM_ALPHA * x_ref[rs, :] + _dot(attn_ref[rs, :], woa_ref[...])
                        + _dot(hm_ref[rs, :], wom_ref[...]))
    for s in range(nsub):
        rs = slice(s * tm, (s + 1) * tm)
        hval = _layer_norm(h_ref[rs, :], g_ref[...], b_ref[...])
        h_ref[rs, :] = hval

        logits = _dot(hval.astype(BF16), wr_ref[...]) + br_ref[...]
        lt = logits.T[:ne, :]

        sel, vals = [], []
        for _ in range(TOP_K):
            mx = jnp.max(lt, axis=0, keepdims=True)
            idx = jnp.min(jnp.where(lt == mx, ex, float(ne)), axis=0, keepdims=True)
            hit = ex == idx
            lt = jnp.where(hit, NEG_INF, lt)
            sel.append(hit)
            vals.append(mx)
        exps = [jnp.exp(v - vals[0]) for v in vals]
        den = exps[0] + exps[1] + exps[2] + exps[3]
        gates = [e / den for e in exps]

        onehot = jnp.zeros((ne, tm), F32)
        for hit in sel:
            onehot = jnp.where(hit, 1.0, onehot)
        rank = _dot(onehot.astype(BF16), earlier)
        cnt = jnp.broadcast_to(jnp.sum(onehot, axis=1, keepdims=True), (ne, V7X_LANES))
        cnt_al = jnp.floor((cnt + (RUN_ALIGN - 1)) * (1.0 / RUN_ALIGN))
        slot = _dot(lower, cnt_al.astype(BF16)) * float(RUN_ALIGN)
        cnt_al = cnt_al * float(RUN_ALIGN)
        where_row = slot[:, 0:1] + rank

        rows = [jnp.sum(jnp.where(sel[k], where_row, 0.0), axis=0, keepdims=True) for k in range(TOP_K)]
        info = jnp.concatenate(rows + gates, axis=0)
        post_ref[s] = info
        col_ref[rs, :] = jnp.concatenate([info, jnp.zeros((V7X_LANES - 2 * TOP_K, tm), F32)], axis=0).T

        cnt_row = to_lanes(cnt_al)
        tab = jnp.zeros((V7X_SUBLANES, V7X_LANES), F32)
        tab = jnp.where(sub == 0, cnt_row, tab)
        tab = jnp.where(sub == 1, base, tab)
        tab = jnp.where(sub == 2, to_lanes(slot), tab)
        tab_ref[s] = tab.astype(jnp.int32)
        base = base + cnt_row

    carry[...] = base
    tot_ref[...] = jnp.broadcast_to(base, (V7X_SUBLANES, V7X_LANES)).astype(jnp.int32)


def _outproj_router(attn2d, hm2d, x2d, wo_a, wo_m, ln_g, ln_b, w_r, b_r):
    t = x2d.shape[0]
    nsub = ROUTER_SUBTILES
    tm = TOK_TILE * nsub
    nt = t // TOK_TILE
    const = lambda i: (0, 0)
    return pl.pallas_call(
        functools.partial(_router_kernel, nsub=nsub),
        grid=(t // tm,),
        in_specs=[
            pl.BlockSpec((tm, D_ATTN), lambda i: (i, 0)),
            pl.BlockSpec((tm, D_MLSTM), lambda i: (i, 0)),
            pl.BlockSpec((tm, D_MODEL), lambda i: (i, 0)),
            pl.BlockSpec((D_ATTN, D_MODEL), const),
            pl.BlockSpec((D_MLSTM, D_MODEL), const),
            pl.BlockSpec((1, D_MODEL), const),
            pl.BlockSpec((1, D_MODEL), const),
            pl.BlockSpec((D_MODEL, V7X_LANES), const),
            pl.BlockSpec((1, V7X_LANES), const),
        ],
        out_specs=[
            pl.BlockSpec((tm, D_MODEL), lambda i: (i, 0)),
            pl.BlockSpec((nsub, V7X_SUBLANES, TOK_TILE), lambda i: (i, 0, 0)),
            pl.BlockSpec((tm, V7X_LANES), lambda i: (i, 0)),
            pl.BlockSpec((nsub, V7X_SUBLANES, V7X_LANES), lambda i: (i, 0, 0)),
            pl.BlockSpec((V7X_SUBLANES, V7X_LANES), const),
        ],
        out_shape=[
            jax.ShapeDtypeStruct((t, D_MODEL), F32),
            jax.ShapeDtypeStruct((nt, V7X_SUBLANES, TOK_TILE), F32),
            jax.ShapeDtypeStruct((t, V7X_LANES), F32),
            jax.ShapeDtypeStruct((nt, V7X_SUBLANES, V7X_LANES), jnp.int32),
            jax.ShapeDtypeStruct((V7X_SUBLANES, V7X_LANES), jnp.int32),
        ],
        scratch_shapes=[pltpu.VMEM((1, V7X_LANES), F32)],
        compiler_params=_compiler_params(("arbitrary",)),
        name="outproj_router",
    )(attn2d, hm2d, x2d, wo_a, wo_m, ln_g, ln_b, w_r, b_r)


def _run_copy(src, dst, start_src, start_dst, n, sem):
    n = pl.multiple_of(n, RUN_ALIGN)
    return pltpu.make_async_copy(
        src.at[pl.ds(pl.multiple_of(start_src, RUN_ALIGN), n)],
        dst.at[pl.ds(pl.multiple_of(start_dst, RUN_ALIGN), n)],
        sem)


def _dispatch_kernel(cnt_sm, slot_sm, dst_sm, tail_sm, post_ref, h_ref, xs_hbm, ybuf, zbuf, sem, zsem):
    i = pl.program_id(0)
    nt = pl.num_programs(0)
    cur = i % 2

    n_blocks = xs_hbm.shape[0] // ROW_BLOCK

    def zero_fill(wait):
        def go(cp):
            if wait:
                cp.wait()
            else:
                cp.start()

        def tail(e, carry):
            n = tail_sm[N_EXPERTS + e]

            @pl.when(n > 0)
            def _():
                go(_run_copy(zbuf, xs_hbm, 0, tail_sm[e], n, zsem))
            return carry

        def block(j, carry):
            go(_run_copy(zbuf, xs_hbm, 0, j * ROW_BLOCK, ROW_BLOCK, zsem))
            return carry

        lax.fori_loop(0, N_EXPERTS, tail, 0)
        lax.fori_loop(tail_sm[2 * N_EXPERTS], n_blocks - 1, block, 0)

    @pl.when(i == 0)
    def _():
        for b in range(2):
            ybuf[b, ROWS_LOCAL:ROWS_LOCAL + RUN_ALIGN, :] = jnp.zeros((RUN_ALIGN, D_MODEL), F32)
        zbuf[...] = jnp.zeros_like(zbuf)
        zero_fill(False)

    hb = h_ref[...].astype(BF16)
    pos = post_ref[0]
    chunk = 256
    for rc in range(ROWS_LOCAL // chunk):
        rows = (lax.broadcasted_iota(jnp.int32, (chunk, TOK_TILE), 0) + rc * chunk).astype(F32)
        p = jnp.zeros((chunk, TOK_TILE), F32)
        for k in range(TOP_K):
            p = jnp.where(rows == pos[k:k + 1, :], 1.0, p)
        ybuf[cur, rc * chunk:(rc + 1) * chunk, :] = _dot(p.astype(BF16), hb)

    def total_rows(tile):
        return cnt_sm[nt * N_EXPERTS + tile]

    @pl.when(i > 0)
    def _():
        n_prev = total_rows(i - 1)
        _run_copy(ybuf.at[1 - cur], xs_hbm, 0, 0, n_prev, sem).wait()

    for e in range(N_EXPERTS):
        k = i * N_EXPERTS + e
        _run_copy(ybuf.at[cur], xs_hbm, slot_sm[k], dst_sm[k], cnt_sm[k], sem).start(priority=e % 2)

    @pl.when(i == nt - 1)
    def _():
        _run_copy(ybuf.at[cur], xs_hbm, 0, 0, total_rows(i), sem).wait()
        zero_fill(True)
        last = _run_copy(zbuf, xs_hbm, 0, (n_blocks - 1) * ROW_BLOCK, ROW_BLOCK, zsem)
        last.start()
        last.wait()


def _dispatch(cnt, slot, dst, tail, post, h2d, n_rows):
    t = h2d.shape[0]
    nt = t // TOK_TILE
    return pl.pallas_call(
        _dispatch_kernel,
        grid_spec=pltpu.PrefetchScalarGridSpec(
            num_scalar_prefetch=4,
            grid=(nt,),
            in_specs=[
                pl.BlockSpec((1, V7X_SUBLANES, TOK_TILE), lambda i, *_: (i, 0, 0)),
                pl.BlockSpec((TOK_TILE, D_MODEL), lambda i, *_: (i, 0)),
            ],
            out_specs=pl.BlockSpec(memory_space=pl.ANY),
            scratch_shapes=[
                pltpu.VMEM((2, ROWS_LOCAL + RUN_ALIGN, D_MODEL), F32),
                pltpu.VMEM((ROW_BLOCK, D_MODEL), F32),
                pltpu.SemaphoreType.DMA(()),
                pltpu.SemaphoreType.DMA(()),
            ],
        ),
        out_shape=jax.ShapeDtypeStruct((n_rows, D_MODEL), F32),
        compiler_params=_compiler_params(("arbitrary",)),
        name="moe_dispatch",
    )(cnt, slot, dst, tail, post, h2d)


def _expert_kernel(bexp_sm, nused_sm, nvalid_sm, xs_ref, wg_ref, bg_ref, wu_ref, bu_ref, wd_ref, bd_ref, ys_ref,
                   hbuf, wg_b, wu_b, wd_b):
    j = pl.program_id(0)
    used = j < nused_sm[0]
    new_expert = jnp.logical_or(j == 0, bexp_sm[j] != bexp_sm[jnp.maximum(j, 1) - 1])

    @pl.when(jnp.logical_and(used, new_expert))
    def _():
        def cast(i, carry):
            sl = pl.ds(pl.multiple_of(i * 128, 128), 128)
            wg_b[sl, :] = wg_ref[0, sl, :].astype(BF16)
            wu_b[sl, :] = wu_ref[0, sl, :].astype(BF16)
            wd_b[sl, :] = wd_ref[0, sl, :].astype(BF16)
            return carry
        lax.fori_loop(0, D_MODEL // 128, cast, 0)

    def mlp(rows):
        xb = xs_ref[0:rows, :].astype(BF16)
        step = 512
        for n in range(0, D_MODEL, step):
            g = _dot(xb, wg_b[:, n:n + step]) + bg_ref[0, :, n:n + step]
            u = _dot(xb, wu_b[:, n:n + step]) + bu_ref[0, :, n:n + step]
            g = jnp.minimum(g, SWIGLU_LIMIT)
            u = jnp.clip(u, -SWIGLU_LIMIT, SWIGLU_LIMIT)
            hbuf[0:rows, n:n + step] = (g * jax.nn.sigmoid(SWIGLU_ALPHA * g) * (u + 1.0)).astype(BF16)
        ys_ref[0:rows, :] = _dot(hbuf[0:rows, :], wd_b[...]) + bd_ref[0]

    half = ROW_BLOCK // 2
    upper_rows = nvalid_sm[j] > half

    @pl.when(jnp.logical_and(used, upper_rows))
    def _():
        mlp(ROW_BLOCK)

    @pl.when(jnp.logical_and(used, jnp.logical_not(upper_rows)))
    def _():
        mlp(half)
        ys_ref[half:ROW_BLOCK, :] = jnp.zeros((ROW_BLOCK - half, D_MODEL), F32)


def _experts(bexp, nused, nvalid, xs, wg, bg, wu, bu, wd, bd):
    n_rows = xs.shape[0]
    nb = n_rows // ROW_BLOCK

    def row_map(j, be, nu, nv):
        return (jnp.minimum(j, nu[0] - 1), 0)

    def w_map(j, be, nu, nv):
        return (be[jnp.minimum(j, nu[0] - 1)], 0, 0)

    wspec = pl.BlockSpec((1, D_MODEL, D_MODEL), w_map)
    bspec = pl.BlockSpec((1, 1, D_MODEL), w_map)
    return pl.pallas_call(
        _expert_kernel,
        grid_spec=pltpu.PrefetchScalarGridSpec(
            num_scalar_prefetch=3,
            grid=(nb,),
            in_specs=[pl.BlockSpec((ROW_BLOCK, D_MODEL), row_map), wspec, bspec, wspec, bspec, wspec, bspec],
            out_specs=pl.BlockSpec((ROW_BLOCK, D_MODEL), row_map),
            scratch_shapes=[pltpu.VMEM((ROW_BLOCK, D_MODEL), BF16)]
            + [pltpu.VMEM((D_MODEL, D_MODEL), BF16) for _ in range(3)],
        ),
        out_shape=jax.ShapeDtypeStruct((n_rows, D_MODEL), F32),
        input_output_aliases={3: 0},
        compiler_params=_compiler_params(("arbitrary",)),
        name="moe_experts",
    )(bexp, nused, nvalid, xs, wg, bg, wu, bu, wd, bd)


COMBINE_TILES = 2


def _combine_kernel(cnt_sm, slot_sm, dst_sm, col_ref, h_ref, g_ref, b_ref, ys_hbm, o_ref, ybuf, sbuf, sem):
    i = pl.program_id(0)
    ns = pl.num_programs(0)
    nt = ns * COMBINE_TILES
    cur = i % 2

    def fetch(step, buf):
        for s in range(COMBINE_TILES):
            for e in range(N_EXPERTS):
                k = (step * COMBINE_TILES + s) * N_EXPERTS + e
                _run_copy(ys_hbm, ybuf.at[buf, s], dst_sm[k], slot_sm[k], cnt_sm[k],
                          sem.at[buf]).start(priority=e % 2)

    @pl.when(i == 0)
    def _():
        ybuf[...] = jnp.zeros_like(ybuf)
        fetch(0, 0)

    @pl.when(i + 1 < ns)
    def _():
        fetch(i + 1, 1 - cur)

    for s in range(COMBINE_TILES):
        _run_copy(ys_hbm, ybuf.at[cur, s], 0, 0, cnt_sm[nt * N_EXPERTS + i * COMBINE_TILES + s], sem.at[cur]).wait()

    chunk = 256
    for s in range(COMBINE_TILES):
        rs = slice(s * TOK_TILE, (s + 1) * TOK_TILE)
        col = col_ref[rs, :]
        for rc in range(ROWS_LOCAL // chunk):
            rows = (lax.broadcasted_iota(jnp.int32, (TOK_TILE, chunk), 1) + rc * chunk).astype(F32)
            sel = jnp.zeros((TOK_TILE, chunk), F32)
            for k in range(TOP_K):
                sel = jnp.where(rows == col[:, k:k + 1], col[:, TOP_K + k:TOP_K + k + 1], sel)
            sbuf[s, :, rc * chunk:(rc + 1) * chunk] = sel.astype(BF16)
    for s in range(COMBINE_TILES):
        rs = slice(s * TOK_TILE, (s + 1) * TOK_TILE)
        for n in range(0, D_MODEL, chunk):
            moe = _dot(sbuf[s], ybuf[cur, s, 0:ROWS_LOCAL, n:n + chunk].astype(BF16))
            o_ref[rs, n:n + chunk] = DEEPNORM_ALPHA * h_ref[rs, n:n + chunk] + moe
    for s in range(COMBINE_TILES):
        rs = slice(s * TOK_TILE, (s + 1) * TOK_TILE)
        o_ref[rs, :] = _layer_norm(o_ref[rs, :], g_ref[...], b_ref[...])


def _combine(cnt, slot, dst, col, h2d, ln_g, ln_b, ys):
    t = h2d.shape[0]
    tm = TOK_TILE * COMBINE_TILES
    return pl.pallas_call(
        _combine_kernel,
        grid_spec=pltpu.PrefetchScalarGridSpec(
            num_scalar_prefetch=3,
            grid=(t // tm,),
            in_specs=[
                pl.BlockSpec((tm, V7X_LANES), lambda i, *_: (i, 0)),
                pl.BlockSpec((tm, D_MODEL), lambda i, *_: (i, 0)),
                pl.BlockSpec((1, D_MODEL), lambda i, *_: (0, 0)),
                pl.BlockSpec((1, D_MODEL), lambda i, *_: (0, 0)),
                pl.BlockSpec(memory_space=pl.ANY),
            ],
            out_specs=pl.BlockSpec((tm, D_MODEL), lambda i, *_: (i, 0)),
            scratch_shapes=[
                pltpu.VMEM((2, COMBINE_TILES, ROWS_LOCAL + RUN_ALIGN, D_MODEL), F32),
                pltpu.VMEM((COMBINE_TILES, TOK_TILE, ROWS_LOCAL), BF16),
                pltpu.SemaphoreType.DMA((2,)),
            ],
        ),
        out_shape=jax.ShapeDtypeStruct((t, D_MODEL), F32),
        compiler_params=_compiler_params(("arbitrary",)),
        name="moe_combine",
    )(cnt, slot, dst, col, h2d, ln_g, ln_b, ys)


def _pad_lanes(a, width=V7X_LANES):
    return jnp.pad(a, ((0, 0), (0, width - a.shape[1])))


def _layer(h3d, w_in, conv_w, conv_b, w_mq, w_mk, b_igate, b_fgate, mnorm_g, w_out,
           ln1_g, ln1_b, w_router, b_router, w_gate, b_gate, w_up, b_up, w_down, b_down, ln2_g, ln2_b):
    batch, seq, _ = h3d.shape
    t = batch * seq
    x2d = h3d.reshape(t, D_MODEL)

    w_main = w_in[:, :PROJ_MAIN].astype(BF16)
    w_gates = _pad_lanes(w_in[:, PROJ_MAIN:]).astype(BF16)
    proj, gcol, grow = _inproj(x2d, w_main, w_gates, batch, seq)
    proj3d = proj.reshape(batch, seq, PROJ_MAIN)
    attn = _attention(proj3d)
    gate_bias = jnp.concatenate([b_igate, b_fgate]).astype(F32)
    bias_col = _pad_lanes(gate_bias[None, :])
    bias_row = jnp.broadcast_to(gate_bias[:, None], (V7X_SUBLANES, seq))
    hm = _mlstm(proj3d, gcol.reshape(batch, seq, V7X_LANES), grow, conv_w, conv_b[None, :],
                w_mq.astype(BF16), w_mk.astype(BF16), bias_col, bias_row, mnorm_g[None, :])

    wo = w_out.astype(BF16)
    h2d, post, col, tab, tot = _outproj_router(
        attn.reshape(t, D_ATTN), hm.reshape(t, D_MLSTM), x2d, wo[:D_ATTN], wo[D_ATTN:],
        ln1_g[None, :], ln1_b[None, :], _pad_lanes(w_router).astype(BF16), _pad_lanes(b_router[None, :]))

    nt = t // TOK_TILE
    total = tot[0, :N_EXPERTS]
    region = (total + ROW_BLOCK - 1) // ROW_BLOCK * ROW_BLOCK
    region_end = jnp.cumsum(region)
    region_start = region_end - region
    n_rows = (t * TOP_K + nt * N_EXPERTS * (RUN_ALIGN - 1)) // ROW_BLOCK * ROW_BLOCK + (N_EXPERTS + 1) * ROW_BLOCK
    nb = n_rows // ROW_BLOCK
    runs = tab[:, 0, :N_EXPERTS]
    empty = runs == 0
    cnt = jnp.where(empty, RUN_ALIGN, runs)
    cnt = jnp.concatenate([cnt.reshape(-1), jnp.sum(cnt, axis=1)])
    slot = jnp.where(empty, ROWS_LOCAL, tab[:, 2, :N_EXPERTS]).reshape(-1)
    dst = jnp.where(empty, n_rows - ROW_BLOCK, tab[:, 1, :N_EXPERTS] + region_start[None, :]).reshape(-1)
    tail = jnp.concatenate([region_start + total, region - total, region_end[-1:] // ROW_BLOCK]).astype(jnp.int32)
    block_row = jnp.arange(nb, dtype=jnp.int32) * ROW_BLOCK
    bexp = jnp.minimum(jnp.sum(region_end[None, :] <= block_row[:, None], axis=1), N_EXPERTS - 1).astype(jnp.int32)
    nused = (region_end[-1:] // ROW_BLOCK).astype(jnp.int32)
    own = bexp[:, None] == jnp.arange(N_EXPERTS, dtype=jnp.int32)[None, :]
    used_end = jnp.sum(jnp.where(own, (region_start + total)[None, :], 0), axis=1)
    nvalid = jnp.clip(used_end - block_row, 0, ROW_BLOCK).astype(jnp.int32)

    xs = _dispatch(cnt, slot, dst, tail, post, h2d, n_rows)
    ys = _experts(bexp, nused, nvalid, xs, w_gate, b_gate[:, None, :], w_up, b_up[:, None, :], w_down, b_down[:, None, :])
    out = _combine(cnt, slot, dst, col, h2d, ln2_g[None, :], ln2_b[None, :], ys)
    return out.reshape(batch, seq, D_MODEL)


def kernel(x, w_in, conv_w, conv_b, w_mq, w_mk, b_igate, b_fgate, mnorm_g, w_out, ln1_g, ln1_b, w_router, b_router, w_gate, b_gate, w_up, b_up, w_down, b_down, ln2_g, ln2_b):
    h = x
    for l in range(w_in.shape[0]):
        h = _layer(h, w_in[l], conv_w[l], conv_b[l], w_mq[l], w_mk[l], b_igate[l], b_fgate[l], mnorm_g[l],
                   w_out[l], ln1_g[l], ln1_b[l], w_router[l], b_router[l], w_gate[l], b_gate[l], w_up[l],
                   b_up[l], w_down[l], b_down[l], ln2_g[l], ln2_b[l])
    return h
```

```python
import functools
import math

import jax
import jax.numpy as jnp
from jax import lax
from jax.experimental import pallas as pl
from jax.experimental.pallas import tpu as pltpu

F32 = jnp.float32
BF16 = jnp.bfloat16
NEG_INF = float("-inf")

V7X_LANES = 128
V7X_SUBLANES = 8
V7X_VMEM_LIMIT_BYTES = 56 * 1024 * 1024

D_MODEL = 1024
D_ATTN = 512
HEAD_DIM_A = 64
D_MLSTM = 512
N_HEADS_M = 4
HEAD_DIM_M = 128
CONV_WIDTH = 4
CHUNK = 128
DILATED_BRANCHES = ((128, 1), (512, 4), (2048, 16))
N_EXPERTS = 32
TOP_K = 4
SWIGLU_LIMIT = 7.0
SWIGLU_ALPHA = 1.702
DEEPNORM_ALPHA = 2.0 ** 0.25
LN_EPS = 1e-5
RMS_EPS = 1e-6

PROJ_MAIN = 3 * D_ATTN + 3 * D_MLSTM
TOK_TILE = 256
RUN_ALIGN = V7X_SUBLANES
ROWS_LOCAL = TOK_TILE * TOP_K + N_EXPERTS * RUN_ALIGN
ROW_BLOCK = 512
ROUTER_SUBTILES = 4


def _dot(a, b):
    return jnp.dot(a, b, preferred_element_type=F32)


def _dot_nt(a, b):
    return lax.dot_general(a, b, (((1,), (1,)), ((), ())), preferred_element_type=F32)


def _split3(x):
    hi = x.astype(BF16)
    r1 = x - hi.astype(F32)
    mid = r1.astype(BF16)
    lo = (r1 - mid.astype(F32)).astype(BF16)
    return hi, mid, lo


def _compiler_params(sem):
    return pltpu.CompilerParams(dimension_semantics=sem, vmem_limit_bytes=V7X_VMEM_LIMIT_BYTES)


def _inproj_kernel(x_ref, w_ref, wg_ref, proj_ref, gcol_ref, grow_ref):
    xb = x_ref[...].astype(BF16)
    step = 512
    for n in range(0, PROJ_MAIN, step):
        proj_ref[:, n:n + step] = _dot(xb, w_ref[:, n:n + step])
    g = _dot(xb, wg_ref[...])
    gcol_ref[...] = g
    grow_ref[0] = g.T[:V7X_SUBLANES, :]


def _inproj(x2d, w_main, w_gates, batch, seq):
    t = x2d.shape[0]
    tm = 512
    per_b = seq // tm
    return pl.pallas_call(
        _inproj_kernel,
        grid=(t // tm,),
        in_specs=[
            pl.BlockSpec((tm, D_MODEL), lambda i: (i, 0)),
            pl.BlockSpec((D_MODEL, PROJ_MAIN), lambda i: (0, 0)),
            pl.BlockSpec((D_MODEL, V7X_LANES), lambda i: (0, 0)),
        ],
        out_specs=[
            pl.BlockSpec((tm, PROJ_MAIN), lambda i: (i, 0)),
            pl.BlockSpec((tm, V7X_LANES), lambda i: (i, 0)),
            pl.BlockSpec((1, V7X_SUBLANES, tm), lambda i: (i // per_b, 0, i % per_b)),
        ],
        out_shape=[
            jax.ShapeDtypeStruct((t, PROJ_MAIN), F32),
            jax.ShapeDtypeStruct((t, V7X_LANES), F32),
            jax.ShapeDtypeStruct((batch, V7X_SUBLANES, seq), F32),
        ],
        compiler_params=_compiler_params(("arbitrary",)),
        name="inproj",
    )(x2d, w_main, w_gates)


ATTN_GROUP = 8


def _attn_kernel(q_ref, k_ref, v_ref, o_ref, q0s, q1s, bias_b, bias_f, sg, pg,
                 o0, o1, o2, l0, l1, l2, m0, m1, m2, x0, x1, x2, *, seq):
    obufs, lbufs, mbufs, xbufs = (o0, o1, o2), (l0, l1, l2), (m0, m1, m2), (x0, x1, x2)
    two = 2 * CHUNK
    head0 = lax.broadcasted_iota(jnp.int32, (CHUNK, V7X_LANES), 1) < HEAD_DIM_A
    qscale = HEAD_DIM_A ** -0.5 * math.log2(math.e)

    def prep(i, carry):
        sl = pl.ds(pl.multiple_of(i * two, two), two)
        h0 = lax.broadcasted_iota(jnp.int32, (two, V7X_LANES), 1) < HEAD_DIM_A
        q = q_ref[0, sl, :] * qscale
        q0s[sl, :] = jnp.where(h0, q, 0.0)
        q1s[sl, :] = jnp.where(h0, 0.0, q)
        return carry

    lax.fori_loop(0, seq // two, prep, 0)

    qi = lax.broadcasted_iota(jnp.int32, (two, two), 0) % CHUNK
    kj = lax.broadcasted_iota(jnp.int32, (two, two), 1)
    bias_b[...] = jnp.where((kj >= qi) & (kj <= qi + CHUNK), 0.0, NEG_INF)
    qf = lax.broadcasted_iota(jnp.int32, (two, CHUNK), 0) % CHUNK
    kf = lax.broadcasted_iota(jnp.int32, (two, CHUNK), 1)
    bias_f[...] = jnp.where(kf <= qf, 0.0, NEG_INF)

    def run_group(c, dil, starts, has_prev):
        assert len(starts) <= ATTN_GROUP
        nk = two if has_prev else CHUNK
        bias_ref = bias_b if has_prev else bias_f

        def rows(s0):
            return pl.ds(s0, CHUNK) if dil == 1 else pl.ds(s0, CHUNK, stride=dil)

        def keys(ref, st):
            if has_prev:
                return jnp.concatenate([ref[0, rows(st - dil * CHUNK), :], ref[0, rows(st), :]], axis=0).astype(BF16)
            return ref[0, rows(st), :].astype(BF16)

        def both_heads(x):
            return jnp.where(head0, x[:CHUNK], x[CHUNK:])

        for j, st in enumerate(starts):
            q2 = jnp.concatenate([q0s[rows(st), :], q1s[rows(st), :]], axis=0).astype(BF16)
            sg[j, :, 0:nk] = _dot_nt(q2, keys(k_ref, st)) + bias_ref[...]
        for j, st in enumerate(starts):
            s = sg[j, :, 0:nk]
            m = jnp.max(s, axis=1, keepdims=True)
            pg[j, :, 0:nk] = jnp.exp2(s - m).astype(BF16)
            ma = jnp.broadcast_to(m[:CHUNK], (CHUNK, V7X_LANES))
            mb = jnp.broadcast_to(m[CHUNK:], (CHUNK, V7X_LANES))
            mbufs[c][rows(st), :] = jnp.where(head0, ma, mb)
            xbufs[c][rows(st), :] = jnp.where(head0, mb, ma)
        khead0 = lax.broadcasted_iota(jnp.int32, (nk, V7X_LANES), 1) < HEAD_DIM_A
        for j, st in enumerate(starts):
            if has_prev:
                vv = jnp.concatenate([v_ref[0, rows(st - dil * CHUNK), :], v_ref[0, rows(st), :]], axis=0)
            else:
                vv = v_ref[0, rows(st), :]
            oa = _dot(pg[j, 0:CHUNK, 0:nk], jnp.where(khead0, vv, 1.0).astype(BF16))
            ob = _dot(pg[j, CHUNK:two, 0:nk], jnp.where(khead0, 1.0, vv).astype(BF16))
            obufs[c][rows(st), :] = jnp.where(head0, oa, ob)
            lbufs[c][rows(st), :] = jnp.where(head0, ob, oa)

    for c, (window, dil) in enumerate(DILATED_BRANCHES):
        assert window // dil == CHUNK
        nb = seq // (dil * CHUNK)
        span = dil * CHUNK
        if dil == 1:
            run_group(c, dil, [0], False)
            per = 5
            assert (nb - 1) % per == 0

            def band1(g, carry, c=c, dil=dil, per=per, span=span):
                base = span + g * (per * span)
                run_group(c, dil, [base + span * j for j in range(per)], True)
                return carry

            lax.fori_loop(0, (nb - 1) // per, band1, 0, unroll=True)
        elif nb > 1:
            run_group(c, dil, list(range(dil)), False)

            def band2(g, carry, c=c, dil=dil, nb=nb, span=span):
                starts = [2 * g + rr + span * b for rr in range(2) for b in range(1, nb)]
                run_group(c, dil, starts, True)
                return carry

            lax.fori_loop(0, dil // 2, band2, 0, unroll=True)
        else:
            def firsts(g, carry, c=c, dil=dil):
                run_group(c, dil, [ATTN_GROUP * g + j for j in range(ATTN_GROUP)], False)
                return carry

            lax.fori_loop(0, dil // ATTN_GROUP, firsts, 0)

    def combine(i, carry):
        sl = pl.ds(pl.multiple_of(i * 256, 256), 256)
        ma, mb, mc = m0[sl, :], m1[sl, :], m2[sl, :]
        mx = jnp.maximum(jnp.maximum(ma, mb), mc)
        wa, wb, wc = jnp.exp2(ma - mx), jnp.exp2(mb - mx), jnp.exp2(mc - mx)
        xa, xb, xc = x0[sl, :], x1[sl, :], x2[sl, :]
        xm = jnp.maximum(jnp.maximum(xa, xb), xc)
        den = (jnp.exp2(xa - xm) * l0[sl, :] + jnp.exp2(xb - xm) * l1[sl, :] + jnp.exp2(xc - xm) * l2[sl, :])
        den = pltpu.roll(den, HEAD_DIM_A, axis=1)
        out = (wa * o0[sl, :] + wb * o1[sl, :] + wc * o2[sl, :]) / den
        o_ref[0, sl, :] = out.astype(o_ref.dtype)
        return carry

    lax.fori_loop(0, seq // 256, combine, 0)


def _attention(proj3d):
    batch, seq, _ = proj3d.shape
    nblk = D_ATTN // V7X_LANES
    blk = (1, seq, V7X_LANES)
    scratch = ([pltpu.VMEM((seq, V7X_LANES), F32), pltpu.VMEM((seq, V7X_LANES), F32),
                pltpu.VMEM((2 * CHUNK, 2 * CHUNK), F32), pltpu.VMEM((2 * CHUNK, CHUNK), F32),
                pltpu.VMEM((ATTN_GROUP, 2 * CHUNK, 2 * CHUNK), F32),
                pltpu.VMEM((ATTN_GROUP, 2 * CHUNK, 2 * CHUNK), BF16)]
               + [pltpu.VMEM((seq, V7X_LANES), F32) for _ in range(12)])
    return pl.pallas_call(
        functools.partial(_attn_kernel, seq=seq),
        grid=(batch, nblk),
        in_specs=[
            pl.BlockSpec(blk, lambda b, g: (b, 0, g)),
            pl.BlockSpec(blk, lambda b, g: (b, 0, nblk + g)),
            pl.BlockSpec(blk, lambda b, g: (b, 0, 2 * nblk + g)),
        ],
        out_specs=pl.BlockSpec(blk, lambda b, g: (b, 0, g)),
        out_shape=jax.ShapeDtypeStruct((batch, seq, D_ATTN), BF16),
        scratch_shapes=scratch,
        compiler_params=_compiler_params(("arbitrary", "arbitrary")),
        name="dilated_attention",
    )(proj3d, proj3d, proj3d)


def _log_sigmoid(x):
    return jnp.minimum(x, 0.0) - jnp.log1p(jnp.exp(-jnp.abs(x)))


MLSTM_HEADS_PER_STEP = 2


def _mlstm_kernel(xm_ref, vm_ref, om_ref, gcol_ref, grow_ref, cw_ref, cb_ref, wq_ref, wk_ref,
                  bcol_ref, brow_ref, mg_ref, o_ref, *scratch, seq):
    nh = MLSTM_HEADS_PER_STEP
    per_head = len(scratch) // nh
    nchunk = seq // CHUNK
    ri = lax.broadcasted_iota(jnp.int32, (CHUNK, CHUNK), 0)
    ci = lax.broadcasted_iota(jnp.int32, (CHUNK, CHUNK), 1)
    tril = ri >= ci
    tril_b = jnp.where(tril, 1.0, 0.0).astype(BF16)
    triu_b = jnp.where(ri <= ci, 1.0, 0.0).astype(BF16)
    pad = V7X_SUBLANES
    qscale = HEAD_DIM_M ** -0.5

    class Head:
        def __init__(self, hh):
            (self.xpad, self.q_s, self.k_s, self.icol_s, self.lfcol_s, self.gcol_s,
             self.ri_s, self.rf_s, self.gf_s, self.mi_s, self.mo_s, self.gl_s) = scratch[hh * per_head:(hh + 1) * per_head]
            self.h = pl.program_id(1) * nh + hh
            self.cols = slice(hh * V7X_LANES, (hh + 1) * V7X_LANES)
            self.wq = wq_ref[hh]
            self.wk = wk_ref[hh]
            self.mg = mg_ref[:, self.cols]

    heads = [Head(hh) for hh in range(nh)]

    for hd in heads:
        hd.xpad[0:pad, :] = jnp.zeros((pad, V7X_LANES), F32)
        hd.xpad[pad:pad + seq, :] = xm_ref[0, :, hd.cols]

    def conv_step(i, carry):
        base = pl.multiple_of(i * 256, 256)
        rows = pl.ds(base, 256)
        gc = gcol_ref[0, rows, :] + bcol_ref[...]
        lane256 = lax.broadcasted_iota(jnp.int32, (256, V7X_LANES), 1)
        for hd in heads:
            y = jnp.broadcast_to(cb_ref[:, hd.cols], (256, V7X_LANES))
            xw = hd.xpad[pl.ds(base, 256 + pad), :]
            for j in range(CONV_WIDTH):
                off = pad - (CONV_WIDTH - 1) + j
                y = y + cw_ref[j:j + 1, hd.cols] * xw[off:off + 256, :]
            xc = (y * jax.nn.sigmoid(y)).astype(BF16)
            hd.q_s[rows, :] = _dot(xc, hd.wq) * qscale
            hd.k_s[rows, :] = _dot(xc, hd.wk)
            ic = jnp.sum(jnp.where(lane256 == hd.h, gc, 0.0), axis=1, keepdims=True)
            fc = jnp.sum(jnp.where(lane256 == hd.h + N_HEADS_M, gc, 0.0), axis=1, keepdims=True)
            hd.icol_s[rows, :] = jnp.broadcast_to(ic, (256, V7X_LANES))
            hd.lfcol_s[rows, :] = jnp.broadcast_to(_log_sigmoid(fc), (256, V7X_LANES))
        return carry

    lax.fori_loop(0, seq // 256, conv_step, 0)

    gr = grow_ref[0] + brow_ref[...]
    sub = lax.broadcasted_iota(jnp.int32, (V7X_SUBLANES, seq), 0)
    for hd in heads:
        irow = jnp.sum(jnp.where(sub == hd.h, gr, 0.0), axis=0, keepdims=True)
        lfrow = _log_sigmoid(jnp.sum(jnp.where(sub == hd.h + N_HEADS_M, gr, 0.0), axis=0, keepdims=True))
        for c in range(nchunk):
            hd.ri_s[c:c + 1, :] = irow[:, c * CHUNK:(c + 1) * CHUNK]
            hd.rf_s[c:c + 1, :] = lfrow[:, c * CHUNK:(c + 1) * CHUNK]

    def cum_step(c, carry):
        sl = pl.ds(pl.multiple_of(c * CHUNK, CHUNK), CHUNK)
        for hd in heads:
            hd.gcol_s[sl, :] = sum(_dot(tril_b, part) for part in _split3(hd.lfcol_s[sl, :]))
        return carry

    lax.fori_loop(0, nchunk, cum_step, 0, unroll=2)

    crow = lax.broadcasted_iota(jnp.int32, (nchunk, V7X_LANES), 0)
    for hd in heads:
        irows = hd.ri_s[...]
        grows = sum(_dot(part, triu_b) for part in _split3(hd.rf_s[...]))
        hd.gf_s[...] = grows
        g_last = grows[:, CHUNK - 1:CHUNK]
        a_max = jnp.max(g_last - grows + irows, axis=1, keepdims=True)
        m = jnp.zeros((1, 1), F32)
        m_in = jnp.zeros((nchunk, V7X_LANES), F32)
        m_out = jnp.zeros((nchunk, V7X_LANES), F32)
        for c in range(nchunk):
            m_in = jnp.where(crow == c, m, m_in)
            m = jnp.maximum(g_last[c:c + 1, :] + m, a_max[c:c + 1, :])
            m_out = jnp.where(crow == c, m, m_out)
        hd.mi_s[...] = m_in
        hd.mo_s[...] = m_out
        hd.gl_s[...] = jnp.broadcast_to(g_last, (nchunk, V7X_LANES))

    def chunk_head(hd, c, cmat, nrow):
        sl = pl.ds(pl.multiple_of(c * CHUNK, CHUNK), CHUNK)
        one = pl.ds(c, 1)
        qc = hd.q_s[sl, :]
        kc = hd.k_s[sl, :]
        vc = vm_ref[0, sl, hd.cols].astype(BF16)
        icol = hd.icol_s[sl, :]
        gcol = hd.gcol_s[sl, :]
        irow_c = hd.ri_s[one, :]
        grow_c = hd.gf_s[one, :]
        m = hd.mi_s[one, :]
        m_new = hd.mo_s[one, :]
        g_last = hd.gl_s[one, :]

        d = gcol - grow_c + irow_c
        d = jnp.where(tril, d, NEG_INF)
        inter = gcol + m
        m_t = jnp.maximum(inter, jnp.max(d, axis=1, keepdims=True))
        w_intra = jnp.exp(d - m_t)
        w_inter = jnp.exp(inter - m_t)
        qb = qc.astype(BF16)
        kb = kc.astype(BF16)
        qk = _dot_nt(qb, kb) * w_intra
        num = _dot(qk.astype(BF16), vc) + w_inter * _dot(qb, cmat.astype(BF16))
        den = jnp.sum(qk, axis=1, keepdims=True) + w_inter * jnp.sum(qc * nrow, axis=1, keepdims=True)
        hc = num / jnp.maximum(jnp.abs(den), jnp.exp(-m_t))

        a_col = g_last - gcol + icol
        decay = jnp.exp(g_last + m - m_new)
        wk_col = jnp.exp(a_col - m_new)
        kw = wk_col * kc
        c_new = decay * cmat + _dot(kw.T.astype(BF16), vc)
        n_new = decay * nrow + jnp.sum(kw, axis=0, keepdims=True)

        hn = hc * lax.rsqrt(jnp.mean(hc * hc, axis=1, keepdims=True) + RMS_EPS) * hd.mg
        hn = jax.nn.sigmoid(om_ref[0, sl, hd.cols]) * hn
        o_ref[0, sl, hd.cols] = hn.astype(o_ref.dtype)
        return c_new, n_new

    def chunk_step(c, carry):
        return tuple(chunk_head(hd, c, *carry[i]) for i, hd in enumerate(heads))

    init = tuple((jnp.zeros((HEAD_DIM_M, HEAD_DIM_M), F32), jnp.zeros((1, HEAD_DIM_M), F32)) for _ in heads)
    lax.fori_loop(0, nchunk, chunk_step, init, unroll=8)


def _mlstm(proj3d, gcol3d, grow3d, conv_w, conv_b, wq, wk, bias_col, bias_row, mnorm):
    batch, seq, _ = proj3d.shape
    nh = MLSTM_HEADS_PER_STEP
    width = nh * V7X_LANES
    blk = (1, seq, width)
    a0 = 3 * D_ATTN // width
    ng = N_HEADS_M // nh
    per_head = ([pltpu.VMEM((seq + V7X_SUBLANES, V7X_LANES), F32)]
                + [pltpu.VMEM((seq, V7X_LANES), F32) for _ in range(5)]
                + [pltpu.VMEM((seq // CHUNK, V7X_LANES), F32) for _ in range(6)])
    return pl.pallas_call(
        functools.partial(_mlstm_kernel, seq=seq),
        grid=(batch, ng),
        in_specs=[
            pl.BlockSpec(blk, lambda b, g: (b, 0, a0 + g)),
            pl.BlockSpec(blk, lambda b, g: (b, 0, a0 + ng + g)),
            pl.BlockSpec(blk, lambda b, g: (b, 0, a0 + 2 * ng + g)),
            pl.BlockSpec((1, seq, V7X_LANES), lambda b, g: (b, 0, 0)),
            pl.BlockSpec((1, V7X_SUBLANES, seq), lambda b, g: (b, 0, 0)),
            pl.BlockSpec((CONV_WIDTH, width), lambda b, g: (0, g)),
            pl.BlockSpec((1, width), lambda b, g: (0, g)),
            pl.BlockSpec((nh, HEAD_DIM_M, HEAD_DIM_M), lambda b, g: (g, 0, 0)),
            pl.BlockSpec((nh, HEAD_DIM_M, HEAD_DIM_M), lambda b, g: (g, 0, 0)),
            pl.BlockSpec((1, V7X_LANES), lambda b, g: (0, 0)),
            pl.BlockSpec((V7X_SUBLANES, seq), lambda b, g: (0, 0)),
            pl.BlockSpec((1, width), lambda b, g: (0, g)),
        ],
        out_specs=pl.BlockSpec(blk, lambda b, g: (b, 0, g)),
        out_shape=jax.ShapeDtypeStruct((batch, seq, D_MLSTM), BF16),
        scratch_shapes=per_head * nh,
        compiler_params=_compiler_params(("arbitrary", "arbitrary")),
        name="mlstm",
    )(proj3d, proj3d, proj3d, gcol3d, grow3d, conv_w, conv_b, wq, wk, bias_col, bias_row, mnorm)


def _layer_norm(z, g, b):
    mu = jnp.mean(z, axis=1, keepdims=True)
    zc = z - mu
    var = jnp.mean(zc * zc, axis=1, keepdims=True)
    return zc * lax.rsqrt(var + LN_EPS) * g + b


def _router_kernel(attn_ref, hm_ref, x_ref, woa_ref, wom_ref, g_ref, b_ref, wr_ref, br_ref,
                   h_ref, post_ref, col_ref, tab_ref, tot_ref, carry, *, nsub):
    i = pl.program_id(0)
    tm = TOK_TILE

    @pl.when(i == 0)
    def _():
        carry[...] = jnp.zeros_like(carry)

    ne = N_EXPERTS
    ex = lax.broadcasted_iota(jnp.int32, (ne, tm), 0).astype(F32)
    ri = lax.broadcasted_iota(jnp.int32, (tm, tm), 0)
    ci = lax.broadcasted_iota(jnp.int32, (tm, tm), 1)
    earlier = jnp.where(ri < ci, 1.0, 0.0).astype(BF16)
    er = lax.broadcasted_iota(jnp.int32, (ne, ne), 0)
    ec = lax.broadcasted_iota(jnp.int32, (ne, ne), 1)
    lower = jnp.where(ec < er, 1.0, 0.0).astype(BF16)
    diag = (lax.broadcasted_iota(jnp.int32, (ne, V7X_LANES), 0)
            == lax.broadcasted_iota(jnp.int32, (ne, V7X_LANES), 1))
    sub = lax.broadcasted_iota(jnp.int32, (V7X_SUBLANES, V7X_LANES), 0)
    base = carry[...]

    def to_lanes(colvec):
        return jnp.sum(jnp.where(diag, colvec, 0.0), axis=0, keepdims=True)

    for s in range(nsub):
        rs = slice(s * tm, (s + 1) * tm)
        h_ref[rs, :] = (DEEPNORM_ALPHA * x_ref[rs, :] + _dot(attn_ref[rs, :], woa_ref[...])
                        + _dot(hm_ref[rs, :], wom_ref[...]))
    for s in range(nsub):
        rs = slice(s * tm, (s + 1) * tm)
        hval = _layer_norm(h_ref[rs, :], g_ref[...], b_ref[...])
        h_ref[rs, :] = hval

        logits = _dot(hval.astype(BF16), wr_ref[...]) + br_ref[...]
        lt = logits.T[:ne, :]

        sel, vals = [], []
        for _ in range(TOP_K):
            mx = jnp.max(lt, axis=0, keepdims=True)
            idx = jnp.min(jnp.where(lt == mx, ex, float(ne)), axis=0, keepdims=True)
            hit = ex == idx
            lt = jnp.where(hit, NEG_INF, lt)
            sel.append(hit)
            vals.append(mx)
        exps = [jnp.exp(v - vals[0]) for v in vals]
        den = exps[0] + exps[1] + exps[2] + exps[3]
        gates = [e / den for e in exps]

        onehot = jnp.zeros((ne, tm), F32)
        for hit in sel:
            onehot = jnp.where(hit, 1.0, onehot)
        rank = _dot(onehot.astype(BF16), earlier)
        cnt = jnp.broadcast_to(jnp.sum(onehot, axis=1, keepdims=True), (ne, V7X_LANES))
        cnt_al = jnp.floor((cnt + (RUN_ALIGN - 1)) * (1.0 / RUN_ALIGN))
        slot = _dot(lower, cnt_al.astype(BF16)) * float(RUN_ALIGN)
        cnt_al = cnt_al * float(RUN_ALIGN)
        where_row = slot[:, 0:1] + rank

        rows = [jnp.sum(jnp.where(sel[k], where_row, 0.0), axis=0, keepdims=True) for k in range(TOP_K)]
        info = jnp.concatenate(rows + gates, axis=0)
        post_ref[s] = info
        col_ref[rs, :] = jnp.concatenate([info, jnp.zeros((V7X_LANES - 2 * TOP_K, tm), F32)], axis=0).T

        cnt_row = to_lanes(cnt_al)
        tab = jnp.zeros((V7X_SUBLANES, V7X_LANES), F32)
        tab = jnp.where(sub == 0, cnt_row, tab)
        tab = jnp.where(sub == 1, base, tab)
        tab = jnp.where(sub == 2, to_lanes(slot), tab)
        tab_ref[s] = tab.astype(jnp.int32)
        base = base + cnt_row

    carry[...] = base
    tot_ref[...] = jnp.broadcast_to(base, (V7X_SUBLANES, V7X_LANES)).astype(jnp.int32)


def _outproj_router(attn2d, hm2d, x2d, wo_a, wo_m, ln_g, ln_b, w_r, b_r):
    t = x2d.shape[0]
    nsub = ROUTER_SUBTILES
    tm = TOK_TILE * nsub
    nt = t // TOK_TILE
    const = lambda i: (0, 0)
    return pl.pallas_call(
        functools.partial(_router_kernel, nsub=nsub),
        grid=(t // tm,),
        in_specs=[
            pl.BlockSpec((tm, D_ATTN), lambda i: (i, 0)),
            pl.BlockSpec((tm, D_MLSTM), lambda i: (i, 0)),
            pl.BlockSpec((tm, D_MODEL), lambda i: (i, 0)),
            pl.BlockSpec((D_ATTN, D_MODEL), const),
            pl.BlockSpec((D_MLSTM, D_MODEL), const),
            pl.BlockSpec((1, D_MODEL), const),
            pl.BlockSpec((1, D_MODEL), const),
            pl.BlockSpec((D_MODEL, V7X_LANES), const),
            pl.BlockSpec((1, V7X_LANES), const),
        ],
        out_specs=[
            pl.BlockSpec((tm, D_MODEL), lambda i: (i, 0)),
            pl.BlockSpec((nsub, V7X_SUBLANES, TOK_TILE), lambda i: (i, 0, 0)),
            pl.BlockSpec((tm, V7X_LANES), lambda i: (i, 0)),
            pl.BlockSpec((nsub, V7X_SUBLANES, V7X_LANES), lambda i: (i, 0, 0)),
            pl.BlockSpec((V7X_SUBLANES, V7X_LANES), const),
        ],
        out_shape=[
            jax.ShapeDtypeStruct((t, D_MODEL), F32),
            jax.ShapeDtypeStruct((nt, V7X_SUBLANES, TOK_TILE), F32),
            jax.ShapeDtypeStruct((t, V7X_LANES), F32),
            jax.ShapeDtypeStruct((nt, V7X_SUBLANES, V7X_LANES), jnp.int32),
            jax.ShapeDtypeStruct((V7X_SUBLANES, V7X_LANES), jnp.int32),
        ],
        scratch_shapes=[pltpu.VMEM((1, V7X_LANES), F32)],
        compiler_params=_compiler_params(("arbitrary",)),
        name="outproj_router",
    )(attn2d, hm2d, x2d, wo_a, wo_m, ln_g, ln_b, w_r, b_r)


def _groups(rows):
    if isinstance(rows, int):
        assert rows % RUN_ALIGN == 0
        return rows // RUN_ALIGN
    return rows >> (RUN_ALIGN.bit_length() - 1)


def _as_groups(x):
    return x.reshape(x.shape[0] // RUN_ALIGN, RUN_ALIGN, x.shape[1])


def _run_copy(src, dst, start_src, start_dst, n, sem):
    g = _groups(n)
    return pltpu.make_async_copy(src.at[pl.ds(_groups(start_src), g)], dst.at[pl.ds(_groups(start_dst), g)], sem)


def _dispatch_kernel(cnt_sm, slot_sm, dst_sm, tail_sm, post_ref, h_ref, xs_hbm, ybuf, zbuf, sem, zsem):
    i = pl.program_id(0)
    nt = pl.num_programs(0)
    cur = i % 2

    n_blocks = xs_hbm.shape[0] * RUN_ALIGN // ROW_BLOCK

    def zero_fill(wait):
        def go(cp):
            if wait:
                cp.wait()
            else:
                cp.start()

        def tail(e, carry):
            n = tail_sm[N_EXPERTS + e]

            @pl.when(n > 0)
            def _():
                go(_run_copy(zbuf, xs_hbm, 0, tail_sm[e], n, zsem))
            return carry

        def block(j, carry):
            go(_run_copy(zbuf, xs_hbm, 0, j * ROW_BLOCK, ROW_BLOCK, zsem))
            return carry

        lax.fori_loop(0, N_EXPERTS, tail, 0)
        lax.fori_loop(tail_sm[2 * N_EXPERTS], n_blocks - 1, block, 0)

    @pl.when(i == 0)
    def _():
        for b in range(2):
            ybuf[b, _groups(ROWS_LOCAL):_groups(ROWS_LOCAL) + 1] = jnp.zeros((1, RUN_ALIGN, D_MODEL), BF16)
        zbuf[...] = jnp.zeros_like(zbuf)
        zero_fill(False)

    hb = h_ref[...].astype(BF16)
    pos = post_ref[0]
    chunk = 256
    for rc in range(ROWS_LOCAL // chunk):
        rows = (lax.broadcasted_iota(jnp.int32, (chunk, TOK_TILE), 0) + rc * chunk).astype(F32)
        p = jnp.zeros((chunk, TOK_TILE), F32)
        for k in range(TOP_K):
            p = jnp.where(rows == pos[k:k + 1, :], 1.0, p)
        ybuf[cur, _groups(rc * chunk):_groups((rc + 1) * chunk)] = _as_groups(_dot(p.astype(BF16), hb).astype(BF16))

    def total_rows(tile):
        return cnt_sm[nt * N_EXPERTS + tile]

    @pl.when(i > 0)
    def _():
        n_prev = total_rows(i - 1)
        _run_copy(ybuf.at[1 - cur], xs_hbm, 0, 0, n_prev, sem).wait()

    for e in range(N_EXPERTS):
        k = i * N_EXPERTS + e
        _run_copy(ybuf.at[cur], xs_hbm, slot_sm[k], dst_sm[k], cnt_sm[k], sem).start()

    @pl.when(i == nt - 1)
    def _():
        _run_copy(ybuf.at[cur], xs_hbm, 0, 0, total_rows(i), sem).wait()
        zero_fill(True)
        last = _run_copy(zbuf, xs_hbm, 0, (n_blocks - 1) * ROW_BLOCK, ROW_BLOCK, zsem)
        last.start()
        last.wait()


def _dispatch(cnt, slot, dst, tail, post, h2d, n_rows):
    t = h2d.shape[0]
    nt = t // TOK_TILE
    return pl.pallas_call(
        _dispatch_kernel,
        grid_spec=pltpu.PrefetchScalarGridSpec(
            num_scalar_prefetch=4,
            grid=(nt,),
            in_specs=[
                pl.BlockSpec((1, V7X_SUBLANES, TOK_TILE), lambda i, *_: (i, 0, 0)),
                pl.BlockSpec((TOK_TILE, D_MODEL), lambda i, *_: (i, 0)),
            ],
            out_specs=pl.BlockSpec(memory_space=pl.ANY),
            scratch_shapes=[
                pltpu.VMEM((2, _groups(ROWS_LOCAL) + 1, RUN_ALIGN, D_MODEL), BF16),
                pltpu.VMEM((_groups(ROW_BLOCK), RUN_ALIGN, D_MODEL), BF16),
                pltpu.SemaphoreType.DMA(()),
                pltpu.SemaphoreType.DMA(()),
            ],
        ),
        out_shape=jax.ShapeDtypeStruct((_groups(n_rows), RUN_ALIGN, D_MODEL), BF16),
        compiler_params=_compiler_params(("arbitrary",)),
        name="moe_dispatch",
    )(cnt, slot, dst, tail, post, h2d)


def _expert_kernel(bexp_sm, nused_sm, nvalid_sm, xs_ref, wg_ref, bg_ref, wu_ref, bu_ref, wd_ref, bd_ref, ys_ref,
                   hbuf, wg_b, wu_b, wd_b):
    j = pl.program_id(0)
    used = j < nused_sm[0]
    new_expert = jnp.logical_or(j == 0, bexp_sm[j] != bexp_sm[jnp.maximum(j, 1) - 1])

    @pl.when(jnp.logical_and(used, new_expert))
    def _():
        def cast(i, carry):
            sl = pl.ds(pl.multiple_of(i * 128, 128), 128)
            wg_b[sl, :] = wg_ref[0, sl, :].astype(BF16)
            wu_b[sl, :] = wu_ref[0, sl, :].astype(BF16)
            wd_b[sl, :] = wd_ref[0, sl, :].astype(BF16)
            return carry
        lax.fori_loop(0, D_MODEL // 128, cast, 0)

    def mlp(rows):
        xb = xs_ref[0:_groups(rows)].reshape(rows, D_MODEL)
        step = 512
        for n in range(0, D_MODEL, step):
            g = _dot(xb, wg_b[:, n:n + step]) + bg_ref[0, :, n:n + step]
            u = _dot(xb, wu_b[:, n:n + step]) + bu_ref[0, :, n:n + step]
            g = jnp.minimum(g, SWIGLU_LIMIT)
            u = jnp.clip(u, -SWIGLU_LIMIT, SWIGLU_LIMIT)
            hbuf[0:rows, n:n + step] = (g * jax.nn.sigmoid(SWIGLU_ALPHA * g) * (u + 1.0)).astype(BF16)
        ys_ref[0:_groups(rows)] = _as_groups((_dot(hbuf[0:rows, :], wd_b[...]) + bd_ref[0]).astype(BF16))

    half = ROW_BLOCK // 2
    upper_rows = nvalid_sm[j] > half

    @pl.when(jnp.logical_and(used, upper_rows))
    def _():
        mlp(ROW_BLOCK)

    @pl.when(jnp.logical_and(used, jnp.logical_not(upper_rows)))
    def _():
        mlp(half)
        ys_ref[_groups(half):_groups(ROW_BLOCK)] = jnp.zeros((_groups(ROW_BLOCK - half), RUN_ALIGN, D_MODEL), BF16)


def _experts(bexp, nused, nvalid, xs, wg, bg, wu, bu, wd, bd):
    nb = xs.shape[0] * RUN_ALIGN // ROW_BLOCK
    row_blk = (_groups(ROW_BLOCK), RUN_ALIGN, D_MODEL)

    def row_map(j, be, nu, nv):
        return (jnp.minimum(j, nu[0] - 1), 0, 0)

    def w_map(j, be, nu, nv):
        return (be[jnp.minimum(j, nu[0] - 1)], 0, 0)

    wspec = pl.BlockSpec((1, D_MODEL, D_MODEL), w_map)
    bspec = pl.BlockSpec((1, 1, D_MODEL), w_map)
    return pl.pallas_call(
        _expert_kernel,
        grid_spec=pltpu.PrefetchScalarGridSpec(
            num_scalar_prefetch=3,
            grid=(nb,),
            in_specs=[pl.BlockSpec(row_blk, row_map), wspec, bspec, wspec, bspec, wspec, bspec],
            out_specs=pl.BlockSpec(row_blk, row_map),
            scratch_shapes=[pltpu.VMEM((ROW_BLOCK, D_MODEL), BF16)]
            + [pltpu.VMEM((D_MODEL, D_MODEL), BF16) for _ in range(3)],
        ),
        out_shape=jax.ShapeDtypeStruct(xs.shape, xs.dtype),
        input_output_aliases={3: 0},
        compiler_params=_compiler_params(("arbitrary",)),
        name="moe_experts",
    )(bexp, nused, nvalid, xs, wg, bg, wu, bu, wd, bd)


COMBINE_TILES = 2


def _combine_kernel(cnt_sm, slot_sm, dst_sm, col_ref, h_ref, g_ref, b_ref, ys_hbm, o_ref, ybuf, sbuf, sem):
    i = pl.program_id(0)
    ns = pl.num_programs(0)
    nt = ns * COMBINE_TILES
    cur = i % 2

    def fetch(step, buf):
        for s in range(COMBINE_TILES):
            for e in range(N_EXPERTS):
                k = (step * COMBINE_TILES + s) * N_EXPERTS + e
                _run_copy(ys_hbm, ybuf.at[buf, s], dst_sm[k], slot_sm[k], cnt_sm[k], sem.at[buf]).start()

    @pl.when(i == 0)
    def _():
        ybuf[...] = jnp.zeros_like(ybuf)
        fetch(0, 0)

    @pl.when(i + 1 < ns)
    def _():
        fetch(i + 1, 1 - cur)

    for s in range(COMBINE_TILES):
        _run_copy(ys_hbm, ybuf.at[cur, s], 0, 0, cnt_sm[nt * N_EXPERTS + i * COMBINE_TILES + s], sem.at[cur]).wait()

    chunk = 256
    for s in range(COMBINE_TILES):
        rs = slice(s * TOK_TILE, (s + 1) * TOK_TILE)
        col = col_ref[rs, :]
        for rc in range(ROWS_LOCAL // chunk):
            rows = (lax.broadcasted_iota(jnp.int32, (TOK_TILE, chunk), 1) + rc * chunk).astype(F32)
            sel = jnp.zeros((TOK_TILE, chunk), F32)
            for k in range(TOP_K):
                sel = jnp.where(rows == col[:, k:k + 1], col[:, TOP_K + k:TOP_K + k + 1], sel)
            sbuf[s, :, rc * chunk:(rc + 1) * chunk] = sel.astype(BF16)
    for s in range(COMBINE_TILES):
        rs = slice(s * TOK_TILE, (s + 1) * TOK_TILE)
        for n in range(0, D_MODEL, chunk):
            rows_n = ybuf[cur, s, 0:_groups(ROWS_LOCAL), :, n:n + chunk].reshape(ROWS_LOCAL, chunk)
            moe = _dot(sbuf[s], rows_n)
            o_ref[rs, n:n + chunk] = DEEPNORM_ALPHA * h_ref[rs, n:n + chunk] + moe
    for s in range(COMBINE_TILES):
        rs = slice(s * TOK_TILE, (s + 1) * TOK_TILE)
        o_ref[rs, :] = _layer_norm(o_ref[rs, :], g_ref[...], b_ref[...])


def _combine(cnt, slot, dst, col, h2d, ln_g, ln_b, ys):
    t = h2d.shape[0]
    tm = TOK_TILE * COMBINE_TILES
    return pl.pallas_call(
        _combine_kernel,
        grid_spec=pltpu.PrefetchScalarGridSpec(
            num_scalar_prefetch=3,
            grid=(t // tm,),
            in_specs=[
                pl.BlockSpec((tm, V7X_LANES), lambda i, *_: (i, 0)),
                pl.BlockSpec((tm, D_MODEL), lambda i, *_: (i, 0)),
                pl.BlockSpec((1, D_MODEL), lambda i, *_: (0, 0)),
                pl.BlockSpec((1, D_MODEL), lambda i, *_: (0, 0)),
                pl.BlockSpec(memory_space=pl.ANY),
            ],
            out_specs=pl.BlockSpec((tm, D_MODEL), lambda i, *_: (i, 0)),
            scratch_shapes=[
                pltpu.VMEM((2, COMBINE_TILES, _groups(ROWS_LOCAL) + 1, RUN_ALIGN, D_MODEL), BF16),
                pltpu.VMEM((COMBINE_TILES, TOK_TILE, ROWS_LOCAL), BF16),
                pltpu.SemaphoreType.DMA((2,)),
            ],
        ),
        out_shape=jax.ShapeDtypeStruct((t, D_MODEL), F32),
        compiler_params=_compiler_params(("arbitrary",)),
        name="moe_combine",
    )(cnt, slot, dst, col, h2d, ln_g, ln_b, ys)


def _pad_lanes(a, width=V7X_LANES):
    return jnp.pad(a, ((0, 0), (0, width - a.shape[1])))


def _layer(h3d, w_in, conv_w, conv_b, w_mq, w_mk, b_igate, b_fgate, mnorm_g, w_out,
           ln1_g, ln1_b, w_router, b_router, w_gate, b_gate, w_up, b_up, w_down, b_down, ln2_g, ln2_b):
    batch, seq, _ = h3d.shape
    t = batch * seq
    x2d = h3d.reshape(t, D_MODEL)

    w_main = w_in[:, :PROJ_MAIN].astype(BF16)
    w_gates = _pad_lanes(w_in[:, PROJ_MAIN:]).astype(BF16)
    proj, gcol, grow = _inproj(x2d, w_main, w_gates, batch, seq)
    proj3d = proj.reshape(batch, seq, PROJ_MAIN)
    attn = _attention(proj3d)
    gate_bias = jnp.concatenate([b_igate, b_fgate]).astype(F32)
    bias_col = _pad_lanes(gate_bias[None, :])
    bias_row = jnp.broadcast_to(gate_bias[:, None], (V7X_SUBLANES, seq))
    hm = _mlstm(proj3d, gcol.reshape(batch, seq, V7X_LANES), grow, conv_w, conv_b[None, :],
                w_mq.astype(BF16), w_mk.astype(BF16), bias_col, bias_row, mnorm_g[None, :])

    wo = w_out.astype(BF16)
    h2d, post, col, tab, tot = _outproj_router(
        attn.reshape(t, D_ATTN), hm.reshape(t, D_MLSTM), x2d, wo[:D_ATTN], wo[D_ATTN:],
        ln1_g[None, :], ln1_b[None, :], _pad_lanes(w_router).astype(BF16), _pad_lanes(b_router[None, :]))

    nt = t // TOK_TILE
    total = tot[0, :N_EXPERTS]
    region = (total + ROW_BLOCK - 1) // ROW_BLOCK * ROW_BLOCK
    region_end = jnp.cumsum(region)
    region_start = region_end - region
    n_rows = (t * TOP_K + nt * N_EXPERTS * (RUN_ALIGN - 1)) // ROW_BLOCK * ROW_BLOCK + (N_EXPERTS + 1) * ROW_BLOCK
    nb = n_rows // ROW_BLOCK
    runs = tab[:, 0, :N_EXPERTS]
    empty = runs == 0
    cnt = jnp.where(empty, RUN_ALIGN, runs)
    cnt = jnp.concatenate([cnt.reshape(-1), jnp.sum(cnt, axis=1)])
    slot = jnp.where(empty, ROWS_LOCAL, tab[:, 2, :N_EXPERTS]).reshape(-1)
    dst = jnp.where(empty, n_rows - ROW_BLOCK, tab[:, 1, :N_EXPERTS] + region_start[None, :]).reshape(-1)
    tail = jnp.concatenate([region_start + total, region - total, region_end[-1:] // ROW_BLOCK]).astype(jnp.int32)
    block_row = jnp.arange(nb, dtype=jnp.int32) * ROW_BLOCK
    bexp = jnp.minimum(jnp.sum(region_end[None, :] <= block_row[:, None], axis=1), N_EXPERTS - 1).astype(jnp.int32)
    nused = (region_end[-1:] // ROW_BLOCK).astype(jnp.int32)
    own = bexp[:, None] == jnp.arange(N_EXPERTS, dtype=jnp.int32)[None, :]
    used_end = jnp.sum(jnp.where(own, (region_start + total)[None, :], 0), axis=1)
    nvalid = jnp.clip(used_end - block_row, 0, ROW_BLOCK).astype(jnp.int32)

    xs = _dispatch(cnt, slot, dst, tail, post, h2d, n_rows)
    ys = _experts(bexp, nused, nvalid, xs, w_gate, b_gate[:, None, :], w_up, b_up[:, None, :], w_down, b_down[:, None, :])
    out = _combine(cnt, slot, dst, col, h2d, ln2_g[None, :], ln2_b[None, :], ys)
    return out.reshape(batch, seq, D_MODEL)


def kernel(x, w_in, conv_w, conv_b, w_mq, w_mk, b_igate, b_fgate, mnorm_g, w_out, ln1_g, ln1_b, w_router, b_router, w_gate, b_gate, w_up, b_up, w_down, b_down, ln2_g, ln2_b):
    h = x
    for l in range(w_in.shape[0]):
        h = _layer(h, w_in[l], conv_w[l], conv_b[l], w_mq[l], w_mk[l], b_igate[l], b_fgate[l], mnorm_g[l],
                   w_out[l], ln1_g[l], ln1_b[l], w_router[l], b_router[l], w_gate[l], b_gate[l], w_up[l],
                   b_up[l], w_down[l], b_down[l], ln2_g[l], ln2_b[l])
    return h
```

```python
import functools
import math

import jax
import jax.numpy as jnp
from jax import lax
from jax.experimental import pallas as pl
from jax.experimental.pallas import tpu as pltpu

F32 = jnp.float32
BF16 = jnp.bfloat16
NEG_INF = float("-inf")

V7X_LANES = 128
V7X_SUBLANES = 8
V7X_VMEM_LIMIT_BYTES = 56 * 1024 * 1024

D_MODEL = 1024
D_ATTN = 512
HEAD_DIM_A = 64
D_MLSTM = 512
N_HEADS_M = 4
HEAD_DIM_M = 128
CONV_WIDTH = 4
CHUNK = 128
DILATED_BRANCHES = ((128, 1), (512, 4), (2048, 16))
N_EXPERTS = 32
TOP_K = 4
SWIGLU_LIMIT = 7.0
SWIGLU_ALPHA = 1.702
DEEPNORM_ALPHA = 2.0 ** 0.25
LN_EPS = 1e-5
RMS_EPS = 1e-6

PROJ_MAIN = 3 * D_ATTN + 3 * D_MLSTM
TOK_TILE = 256
RUN_ALIGN = 4
ROWS_LOCAL = -(-(TOK_TILE * TOP_K + N_EXPERTS * (RUN_ALIGN - 1)) // V7X_LANES) * V7X_LANES
ROW_BLOCK = 512
ROUTER_SUBTILES = 4


def _dot(a, b):
    return jnp.dot(a, b, preferred_element_type=F32)


def _dot_nt(a, b):
    return lax.dot_general(a, b, (((1,), (1,)), ((), ())), preferred_element_type=F32)


def _split3(x):
    hi = x.astype(BF16)
    r1 = x - hi.astype(F32)
    mid = r1.astype(BF16)
    lo = (r1 - mid.astype(F32)).astype(BF16)
    return hi, mid, lo


def _compiler_params(sem):
    return pltpu.CompilerParams(dimension_semantics=sem, vmem_limit_bytes=V7X_VMEM_LIMIT_BYTES)


def _inproj_kernel(x_ref, w_ref, wg_ref, proj_ref, gcol_ref, grow_ref):
    xb = x_ref[...].astype(BF16)
    step = 512
    for n in range(0, PROJ_MAIN, step):
        proj_ref[:, n:n + step] = _dot(xb, w_ref[:, n:n + step])
    g = _dot(xb, wg_ref[...])
    gcol_ref[...] = g
    grow_ref[0] = g.T[:V7X_SUBLANES, :]


def _inproj(x2d, w_main, w_gates, batch, seq):
    t = x2d.shape[0]
    tm = 512
    per_b = seq // tm
    return pl.pallas_call(
        _inproj_kernel,
        grid=(t // tm,),
        in_specs=[
            pl.BlockSpec((tm, D_MODEL), lambda i: (i, 0)),
            pl.BlockSpec((D_MODEL, PROJ_MAIN), lambda i: (0, 0)),
            pl.BlockSpec((D_MODEL, V7X_LANES), lambda i: (0, 0)),
        ],
        out_specs=[
            pl.BlockSpec((tm, PROJ_MAIN), lambda i: (i, 0)),
            pl.BlockSpec((tm, V7X_LANES), lambda i: (i, 0)),
            pl.BlockSpec((1, V7X_SUBLANES, tm), lambda i: (i // per_b, 0, i % per_b)),
        ],
        out_shape=[
            jax.ShapeDtypeStruct((t, PROJ_MAIN), F32),
            jax.ShapeDtypeStruct((t, V7X_LANES), F32),
            jax.ShapeDtypeStruct((batch, V7X_SUBLANES, seq), F32),
        ],
        compiler_params=_compiler_params(("arbitrary",)),
        name="inproj",
    )(x2d, w_main, w_gates)


ATTN_GROUP = 8


def _attn_kernel(q_ref, k_ref, v_ref, o_ref, q0s, q1s, bias_b, bias_f, sg, pg,
                 o0, o1, o2, l0, l1, l2, m0, m1, m2, x0, x1, x2, *, seq):
    obufs, lbufs, mbufs, xbufs = (o0, o1, o2), (l0, l1, l2), (m0, m1, m2), (x0, x1, x2)
    two = 2 * CHUNK
    head0 = lax.broadcasted_iota(jnp.int32, (CHUNK, V7X_LANES), 1) < HEAD_DIM_A
    qscale = HEAD_DIM_A ** -0.5 * math.log2(math.e)

    def prep(i, carry):
        sl = pl.ds(pl.multiple_of(i * two, two), two)
        h0 = lax.broadcasted_iota(jnp.int32, (two, V7X_LANES), 1) < HEAD_DIM_A
        q = q_ref[0, sl, :] * qscale
        q0s[sl, :] = jnp.where(h0, q, 0.0)
        q1s[sl, :] = jnp.where(h0, 0.0, q)
        return carry

    lax.fori_loop(0, seq // two, prep, 0)

    qi = lax.broadcasted_iota(jnp.int32, (two, two), 0) % CHUNK
    kj = lax.broadcasted_iota(jnp.int32, (two, two), 1)
    bias_b[...] = jnp.where((kj >= qi) & (kj <= qi + CHUNK), 0.0, NEG_INF)
    qf = lax.broadcasted_iota(jnp.int32, (two, CHUNK), 0) % CHUNK
    kf = lax.broadcasted_iota(jnp.int32, (two, CHUNK), 1)
    bias_f[...] = jnp.where(kf <= qf, 0.0, NEG_INF)

    def run_group(c, dil, starts, has_prev):
        assert len(starts) <= ATTN_GROUP
        nk = two if has_prev else CHUNK
        bias_ref = bias_b if has_prev else bias_f

        def rows(s0):
            return pl.ds(s0, CHUNK) if dil == 1 else pl.ds(s0, CHUNK, stride=dil)

        def keys(ref, st):
            if has_prev:
                return jnp.concatenate([ref[0, rows(st - dil * CHUNK), :], ref[0, rows(st), :]], axis=0).astype(BF16)
            return ref[0, rows(st), :].astype(BF16)

        def both_heads(x):
            return jnp.where(head0, x[:CHUNK], x[CHUNK:])

        for j, st in enumerate(starts):
            q2 = jnp.concatenate([q0s[rows(st), :], q1s[rows(st), :]], axis=0).astype(BF16)
            sg[j, :, 0:nk] = _dot_nt(q2, keys(k_ref, st)) + bias_ref[...]
        for j, st in enumerate(starts):
            s = sg[j, :, 0:nk]
            m = jnp.max(s, axis=1, keepdims=True)
            pg[j, :, 0:nk] = jnp.exp2(s - m).astype(BF16)
            ma = jnp.broadcast_to(m[:CHUNK], (CHUNK, V7X_LANES))
            mb = jnp.broadcast_to(m[CHUNK:], (CHUNK, V7X_LANES))
            mbufs[c][rows(st), :] = jnp.where(head0, ma, mb)
            xbufs[c][rows(st), :] = jnp.where(head0, mb, ma)
        khead0 = lax.broadcasted_iota(jnp.int32, (nk, V7X_LANES), 1) < HEAD_DIM_A
        for j, st in enumerate(starts):
            if has_prev:
                vv = jnp.concatenate([v_ref[0, rows(st - dil * CHUNK), :], v_ref[0, rows(st), :]], axis=0)
            else:
                vv = v_ref[0, rows(st), :]
            oa = _dot(pg[j, 0:CHUNK, 0:nk], jnp.where(khead0, vv, 1.0).astype(BF16))
            ob = _dot(pg[j, CHUNK:two, 0:nk], jnp.where(khead0, 1.0, vv).astype(BF16))
            obufs[c][rows(st), :] = jnp.where(head0, oa, ob)
            lbufs[c][rows(st), :] = jnp.where(head0, ob, oa)

    for c, (window, dil) in enumerate(DILATED_BRANCHES):
        assert window // dil == CHUNK
        nb = seq // (dil * CHUNK)
        span = dil * CHUNK
        if dil == 1:
            run_group(c, dil, [0], False)
            per = 5
            assert (nb - 1) % per == 0

            def band1(g, carry, c=c, dil=dil, per=per, span=span):
                base = span + g * (per * span)
                run_group(c, dil, [base + span * j for j in range(per)], True)
                return carry

            lax.fori_loop(0, (nb - 1) // per, band1, 0, unroll=True)
        elif nb > 1:
            run_group(c, dil, list(range(dil)), False)

            def band2(g, carry, c=c, dil=dil, nb=nb, span=span):
                starts = [2 * g + rr + span * b for rr in range(2) for b in range(1, nb)]
                run_group(c, dil, starts, True)
                return carry

            lax.fori_loop(0, dil // 2, band2, 0, unroll=True)
        else:
            def firsts(g, carry, c=c, dil=dil):
                run_group(c, dil, [ATTN_GROUP * g + j for j in range(ATTN_GROUP)], False)
                return carry

            lax.fori_loop(0, dil // ATTN_GROUP, firsts, 0)

    def combine(i, carry):
        sl = pl.ds(pl.multiple_of(i * 256, 256), 256)
        ma, mb, mc = m0[sl, :], m1[sl, :], m2[sl, :]
        mx = jnp.maximum(jnp.maximum(ma, mb), mc)
        wa, wb, wc = jnp.exp2(ma - mx), jnp.exp2(mb - mx), jnp.exp2(mc - mx)
        xa, xb, xc = x0[sl, :], x1[sl, :], x2[sl, :]
        xm = jnp.maximum(jnp.maximum(xa, xb), xc)
        den = (jnp.exp2(xa - xm) * l0[sl, :] + jnp.exp2(xb - xm) * l1[sl, :] + jnp.exp2(xc - xm) * l2[sl, :])
        den = pltpu.roll(den, HEAD_DIM_A, axis=1)
        out = (wa * o0[sl, :] + wb * o1[sl, :] + wc * o2[sl, :]) / den
        o_ref[0, sl, :] = out.astype(o_ref.dtype)
        return carry

    lax.fori_loop(0, seq // 256, combine, 0)


def _attention(proj3d):
    batch, seq, _ = proj3d.shape
    nblk = D_ATTN // V7X_LANES
    blk = (1, seq, V7X_LANES)
    scratch = ([pltpu.VMEM((seq, V7X_LANES), F32), pltpu.VMEM((seq, V7X_LANES), F32),
                pltpu.VMEM((2 * CHUNK, 2 * CHUNK), F32), pltpu.VMEM((2 * CHUNK, CHUNK), F32),
                pltpu.VMEM((ATTN_GROUP, 2 * CHUNK, 2 * CHUNK), F32),
                pltpu.VMEM((ATTN_GROUP, 2 * CHUNK, 2 * CHUNK), BF16)]
               + [pltpu.VMEM((seq, V7X_LANES), F32) for _ in range(12)])
    return pl.pallas_call(
        functools.partial(_attn_kernel, seq=seq),
        grid=(batch, nblk),
        in_specs=[
            pl.BlockSpec(blk, lambda b, g: (b, 0, g)),
            pl.BlockSpec(blk, lambda b, g: (b, 0, nblk + g)),
            pl.BlockSpec(blk, lambda b, g: (b, 0, 2 * nblk + g)),
        ],
        out_specs=pl.BlockSpec(blk, lambda b, g: (b, 0, g)),
        out_shape=jax.ShapeDtypeStruct((batch, seq, D_ATTN), BF16),
        scratch_shapes=scratch,
        compiler_params=_compiler_params(("arbitrary", "arbitrary")),
        name="dilated_attention",
    )(proj3d, proj3d, proj3d)


def _log_sigmoid(x):
    return jnp.minimum(x, 0.0) - jnp.log1p(jnp.exp(-jnp.abs(x)))


MLSTM_HEADS_PER_STEP = 2


def _mlstm_kernel(xm_ref, vm_ref, om_ref, gcol_ref, grow_ref, cw_ref, cb_ref, wq_ref, wk_ref,
                  bcol_ref, brow_ref, mg_ref, o_ref, *scratch, seq):
    nh = MLSTM_HEADS_PER_STEP
    per_head = len(scratch) // nh
    nchunk = seq // CHUNK
    ri = lax.broadcasted_iota(jnp.int32, (CHUNK, CHUNK), 0)
    ci = lax.broadcasted_iota(jnp.int32, (CHUNK, CHUNK), 1)
    tril = ri >= ci
    tril_b = jnp.where(tril, 1.0, 0.0).astype(BF16)
    triu_b = jnp.where(ri <= ci, 1.0, 0.0).astype(BF16)
    pad = V7X_SUBLANES
    qscale = HEAD_DIM_M ** -0.5

    class Head:
        def __init__(self, hh):
            (self.xpad, self.q_s, self.k_s, self.icol_s, self.lfcol_s, self.gcol_s,
             self.ri_s, self.rf_s, self.gf_s, self.mi_s, self.mo_s, self.gl_s) = scratch[hh * per_head:(hh + 1) * per_head]
            self.h = pl.program_id(1) * nh + hh
            self.cols = slice(hh * V7X_LANES, (hh + 1) * V7X_LANES)
            self.wq = wq_ref[hh]
            self.wk = wk_ref[hh]
            self.mg = mg_ref[:, self.cols]

    heads = [Head(hh) for hh in range(nh)]

    for hd in heads:
        hd.xpad[0:pad, :] = jnp.zeros((pad, V7X_LANES), F32)
        hd.xpad[pad:pad + seq, :] = xm_ref[0, :, hd.cols]

    def conv_step(i, carry):
        base = pl.multiple_of(i * 256, 256)
        rows = pl.ds(base, 256)
        gc = gcol_ref[0, rows, :] + bcol_ref[...]
        lane256 = lax.broadcasted_iota(jnp.int32, (256, V7X_LANES), 1)
        for hd in heads:
            y = jnp.broadcast_to(cb_ref[:, hd.cols], (256, V7X_LANES))
            xw = hd.xpad[pl.ds(base, 256 + pad), :]
            for j in range(CONV_WIDTH):
                off = pad - (CONV_WIDTH - 1) + j
                y = y + cw_ref[j:j + 1, hd.cols] * xw[off:off + 256, :]
            xc = (y * jax.nn.sigmoid(y)).astype(BF16)
            hd.q_s[rows, :] = _dot(xc, hd.wq) * qscale
            hd.k_s[rows, :] = _dot(xc, hd.wk)
            ic = jnp.sum(jnp.where(lane256 == hd.h, gc, 0.0), axis=1, keepdims=True)
            fc = jnp.sum(jnp.where(lane256 == hd.h + N_HEADS_M, gc, 0.0), axis=1, keepdims=True)
            hd.icol_s[rows, :] = jnp.broadcast_to(ic, (256, V7X_LANES))
            hd.lfcol_s[rows, :] = jnp.broadcast_to(_log_sigmoid(fc), (256, V7X_LANES))
        return carry

    lax.fori_loop(0, seq // 256, conv_step, 0)

    gr = grow_ref[0] + brow_ref[...]
    sub = lax.broadcasted_iota(jnp.int32, (V7X_SUBLANES, seq), 0)
    for hd in heads:
        irow = jnp.sum(jnp.where(sub == hd.h, gr, 0.0), axis=0, keepdims=True)
        lfrow = _log_sigmoid(jnp.sum(jnp.where(sub == hd.h + N_HEADS_M, gr, 0.0), axis=0, keepdims=True))
        for c in range(nchunk):
            hd.ri_s[c:c + 1, :] = irow[:, c * CHUNK:(c + 1) * CHUNK]
            hd.rf_s[c:c + 1, :] = lfrow[:, c * CHUNK:(c + 1) * CHUNK]

    def cum_step(c, carry):
        sl = pl.ds(pl.multiple_of(c * CHUNK, CHUNK), CHUNK)
        for hd in heads:
            hd.gcol_s[sl, :] = sum(_dot(tril_b, part) for part in _split3(hd.lfcol_s[sl, :]))
        return carry

    lax.fori_loop(0, nchunk, cum_step, 0, unroll=2)

    crow = lax.broadcasted_iota(jnp.int32, (nchunk, V7X_LANES), 0)
    for hd in heads:
        irows = hd.ri_s[...]
        grows = sum(_dot(part, triu_b) for part in _split3(hd.rf_s[...]))
        hd.gf_s[...] = grows
        g_last = grows[:, CHUNK - 1:CHUNK]
        a_max = jnp.max(g_last - grows + irows, axis=1, keepdims=True)
        m = jnp.zeros((1, 1), F32)
        m_in = jnp.zeros((nchunk, V7X_LANES), F32)
        m_out = jnp.zeros((nchunk, V7X_LANES), F32)
        for c in range(nchunk):
            m_in = jnp.where(crow == c, m, m_in)
            m = jnp.maximum(g_last[c:c + 1, :] + m, a_max[c:c + 1, :])
            m_out = jnp.where(crow == c, m, m_out)
        hd.mi_s[...] = m_in
        hd.mo_s[...] = m_out
        hd.gl_s[...] = jnp.broadcast_to(g_last, (nchunk, V7X_LANES))

    def chunk_head(hd, c, cmat, nrow):
        sl = pl.ds(pl.multiple_of(c * CHUNK, CHUNK), CHUNK)
        one = pl.ds(c, 1)
        qc = hd.q_s[sl, :]
        kc = hd.k_s[sl, :]
        vc = vm_ref[0, sl, hd.cols].astype(BF16)
        icol = hd.icol_s[sl, :]
        gcol = hd.gcol_s[sl, :]
        irow_c = hd.ri_s[one, :]
        grow_c = hd.gf_s[one, :]
        m = hd.mi_s[one, :]
        m_new = hd.mo_s[one, :]
        g_last = hd.gl_s[one, :]

        d = gcol - grow_c + irow_c
        d = jnp.where(tril, d, NEG_INF)
        inter = gcol + m
        m_t = jnp.maximum(inter, jnp.max(d, axis=1, keepdims=True))
        w_intra = jnp.exp(d - m_t)
        w_inter = jnp.exp(inter - m_t)
        qb = qc.astype(BF16)
        kb = kc.astype(BF16)
        qk = _dot_nt(qb, kb) * w_intra
        num = _dot(qk.astype(BF16), vc) + w_inter * _dot(qb, cmat.astype(BF16))
        den = jnp.sum(qk, axis=1, keepdims=True) + w_inter * jnp.sum(qc * nrow, axis=1, keepdims=True)
        hc = num / jnp.maximum(jnp.abs(den), jnp.exp(-m_t))

        a_col = g_last - gcol + icol
        decay = jnp.exp(g_last + m - m_new)
        wk_col = jnp.exp(a_col - m_new)
        kw = wk_col * kc
        c_new = decay * cmat + _dot(kw.T.astype(BF16), vc)
        n_new = decay * nrow + jnp.sum(kw, axis=0, keepdims=True)

        hn = hc * lax.rsqrt(jnp.mean(hc * hc, axis=1, keepdims=True) + RMS_EPS) * hd.mg
        hn = jax.nn.sigmoid(om_ref[0, sl, hd.cols]) * hn
        o_ref[0, sl, hd.cols] = hn.astype(o_ref.dtype)
        return c_new, n_new

    def chunk_step(c, carry):
        return tuple(chunk_head(hd, c, *carry[i]) for i, hd in enumerate(heads))

    init = tuple((jnp.zeros((HEAD_DIM_M, HEAD_DIM_M), F32), jnp.zeros((1, HEAD_DIM_M), F32)) for _ in heads)
    lax.fori_loop(0, nchunk, chunk_step, init, unroll=8)


def _mlstm(proj3d, gcol3d, grow3d, conv_w, conv_b, wq, wk, bias_col, bias_row, mnorm):
    batch, seq, _ = proj3d.shape
    nh = MLSTM_HEADS_PER_STEP
    width = nh * V7X_LANES
    blk = (1, seq, width)
    a0 = 3 * D_ATTN // width
    ng = N_HEADS_M // nh
    per_head = ([pltpu.VMEM((seq + V7X_SUBLANES, V7X_LANES), F32)]
                + [pltpu.VMEM((seq, V7X_LANES), F32) for _ in range(5)]
                + [pltpu.VMEM((seq // CHUNK, V7X_LANES), F32) for _ in range(6)])
    return pl.pallas_call(
        functools.partial(_mlstm_kernel, seq=seq),
        grid=(batch, ng),
        in_specs=[
            pl.BlockSpec(blk, lambda b, g: (b, 0, a0 + g)),
            pl.BlockSpec(blk, lambda b, g: (b, 0, a0 + ng + g)),
            pl.BlockSpec(blk, lambda b, g: (b, 0, a0 + 2 * ng + g)),
            pl.BlockSpec((1, seq, V7X_LANES), lambda b, g: (b, 0, 0)),
            pl.BlockSpec((1, V7X_SUBLANES, seq), lambda b, g: (b, 0, 0)),
            pl.BlockSpec((CONV_WIDTH, width), lambda b, g: (0, g)),
            pl.BlockSpec((1, width), lambda b, g: (0, g)),
            pl.BlockSpec((nh, HEAD_DIM_M, HEAD_DIM_M), lambda b, g: (g, 0, 0)),
            pl.BlockSpec((nh, HEAD_DIM_M, HEAD_DIM_M), lambda b, g: (g, 0, 0)),
            pl.BlockSpec((1, V7X_LANES), lambda b, g: (0, 0)),
            pl.BlockSpec((V7X_SUBLANES, seq), lambda b, g: (0, 0)),
            pl.BlockSpec((1, width), lambda b, g: (0, g)),
        ],
        out_specs=pl.BlockSpec(blk, lambda b, g: (b, 0, g)),
        out_shape=jax.ShapeDtypeStruct((batch, seq, D_MLSTM), BF16),
        scratch_shapes=per_head * nh,
        compiler_params=_compiler_params(("arbitrary", "arbitrary")),
        name="mlstm",
    )(proj3d, proj3d, proj3d, gcol3d, grow3d, conv_w, conv_b, wq, wk, bias_col, bias_row, mnorm)


def _layer_norm(z, g, b):
    mu = jnp.mean(z, axis=1, keepdims=True)
    zc = z - mu
    var = jnp.mean(zc * zc, axis=1, keepdims=True)
    return zc * lax.rsqrt(var + LN_EPS) * g + b


def _router_kernel(attn_ref, hm_ref, x_ref, woa_ref, wom_ref, g_ref, b_ref, wr_ref, br_ref,
                   h_ref, post_ref, col_ref, tab_ref, tot_ref, carry, *, nsub):
    i = pl.program_id(0)
    tm = TOK_TILE

    @pl.when(i == 0)
    def _():
        carry[...] = jnp.zeros_like(carry)

    ne = N_EXPERTS
    ex = lax.broadcasted_iota(jnp.int32, (ne, tm), 0).astype(F32)
    ri = lax.broadcasted_iota(jnp.int32, (tm, tm), 0)
    ci = lax.broadcasted_iota(jnp.int32, (tm, tm), 1)
    earlier = jnp.where(ri < ci, 1.0, 0.0).astype(BF16)
    er = lax.broadcasted_iota(jnp.int32, (ne, ne), 0)
    ec = lax.broadcasted_iota(jnp.int32, (ne, ne), 1)
    lower = jnp.where(ec < er, 1.0, 0.0).astype(BF16)
    diag = (lax.broadcasted_iota(jnp.int32, (ne, V7X_LANES), 0)
            == lax.broadcasted_iota(jnp.int32, (ne, V7X_LANES), 1))
    sub = lax.broadcasted_iota(jnp.int32, (V7X_SUBLANES, V7X_LANES), 0)
    base = carry[...]

    def to_lanes(colvec):
        return jnp.sum(jnp.where(diag, colvec, 0.0), axis=0, keepdims=True)

    for s in range(nsub):
        rs = slice(s * tm, (s + 1) * tm)
        h_ref[rs, :] = (DEEPNORM_ALPHA * x_ref[rs, :] + _dot(attn_ref[rs, :], woa_ref[...])
                        + _dot(hm_ref[rs, :], wom_ref[...]))
    for s in range(nsub):
        rs = slice(s * tm, (s + 1) * tm)
        hval = _layer_norm(h_ref[rs, :], g_ref[...], b_ref[...])
        h_ref[rs, :] = hval

        logits = _dot(hval.astype(BF16), wr_ref[...]) + br_ref[...]
        lt = logits.T[:ne, :]

        sel, vals = [], []
        for _ in range(TOP_K):
            mx = jnp.max(lt, axis=0, keepdims=True)
            idx = jnp.min(jnp.where(lt == mx, ex, float(ne)), axis=0, keepdims=True)
            hit = ex == idx
            lt = jnp.where(hit, NEG_INF, lt)
            sel.append(hit)
            vals.append(mx)
        exps = [jnp.exp(v - vals[0]) for v in vals]
        den = exps[0] + exps[1] + exps[2] + exps[3]
        gates = [e / den for e in exps]

        onehot = jnp.zeros((ne, tm), F32)
        for hit in sel:
            onehot = jnp.where(hit, 1.0, onehot)
        rank = _dot(onehot.astype(BF16), earlier)
        cnt = jnp.broadcast_to(jnp.sum(onehot, axis=1, keepdims=True), (ne, V7X_LANES))
        cnt_al = jnp.floor((cnt + (RUN_ALIGN - 1)) * (1.0 / RUN_ALIGN))
        slot = _dot(lower, cnt_al.astype(BF16)) * float(RUN_ALIGN)
        cnt_al = cnt_al * float(RUN_ALIGN)
        where_row = slot[:, 0:1] + rank

        rows = [jnp.sum(jnp.where(sel[k], where_row, 0.0), axis=0, keepdims=True) for k in range(TOP_K)]
        info = jnp.concatenate(rows + gates, axis=0)
        post_ref[s] = info
        col_ref[rs, :] = jnp.concatenate([info, jnp.zeros((V7X_LANES - 2 * TOP_K, tm), F32)], axis=0).T

        cnt_row = to_lanes(cnt_al)
        tab = jnp.zeros((V7X_SUBLANES, V7X_LANES), F32)
        tab = jnp.where(sub == 0, cnt_row, tab)
        tab = jnp.where(sub == 1, base, tab)
        tab = jnp.where(sub == 2, to_lanes(slot), tab)
        tab_ref[s] = tab.astype(jnp.int32)
        base = base + cnt_row

    carry[...] = base
    tot_ref[...] = jnp.broadcast_to(base, (V7X_SUBLANES, V7X_LANES)).astype(jnp.int32)


def _outproj_router(attn2d, hm2d, x2d, wo_a, wo_m, ln_g, ln_b, w_r, b_r):
    t = x2d.shape[0]
    nsub = ROUTER_SUBTILES
    tm = TOK_TILE * nsub
    nt = t // TOK_TILE
    const = lambda i: (0, 0)
    return pl.pallas_call(
        functools.partial(_router_kernel, nsub=nsub),
        grid=(t // tm,),
        in_specs=[
            pl.BlockSpec((tm, D_ATTN), lambda i: (i, 0)),
            pl.BlockSpec((tm, D_MLSTM), lambda i: (i, 0)),
            pl.BlockSpec((tm, D_MODEL), lambda i: (i, 0)),
            pl.BlockSpec((D_ATTN, D_MODEL), const),
            pl.BlockSpec((D_MLSTM, D_MODEL), const),
            pl.BlockSpec((1, D_MODEL), const),
            pl.BlockSpec((1, D_MODEL), const),
            pl.BlockSpec((D_MODEL, V7X_LANES), const),
            pl.BlockSpec((1, V7X_LANES), const),
        ],
        out_specs=[
            pl.BlockSpec((tm, D_MODEL), lambda i: (i, 0)),
            pl.BlockSpec((nsub, V7X_SUBLANES, TOK_TILE), lambda i: (i, 0, 0)),
            pl.BlockSpec((tm, V7X_LANES), lambda i: (i, 0)),
            pl.BlockSpec((nsub, V7X_SUBLANES, V7X_LANES), lambda i: (i, 0, 0)),
            pl.BlockSpec((V7X_SUBLANES, V7X_LANES), const),
        ],
        out_shape=[
            jax.ShapeDtypeStruct((t, D_MODEL), F32),
            jax.ShapeDtypeStruct((nt, V7X_SUBLANES, TOK_TILE), F32),
            jax.ShapeDtypeStruct((t, V7X_LANES), F32),
            jax.ShapeDtypeStruct((nt, V7X_SUBLANES, V7X_LANES), jnp.int32),
            jax.ShapeDtypeStruct((V7X_SUBLANES, V7X_LANES), jnp.int32),
        ],
        scratch_shapes=[pltpu.VMEM((1, V7X_LANES), F32)],
        compiler_params=_compiler_params(("arbitrary",)),
        name="outproj_router",
    )(attn2d, hm2d, x2d, wo_a, wo_m, ln_g, ln_b, w_r, b_r)


def _groups(rows):
    if isinstance(rows, int):
        assert rows % RUN_ALIGN == 0
        return rows // RUN_ALIGN
    return rows >> (RUN_ALIGN.bit_length() - 1)


def _as_groups(x):
    return x.reshape(x.shape[0] // RUN_ALIGN, RUN_ALIGN, x.shape[1])


def _run_copy(src, dst, start_src, start_dst, n, sem):
    g = _groups(n)
    return pltpu.make_async_copy(src.at[pl.ds(_groups(start_src), g)], dst.at[pl.ds(_groups(start_dst), g)], sem)


def _dispatch_kernel(cnt_sm, slot_sm, dst_sm, tail_sm, post_ref, h_ref, xs_hbm, ybuf, zbuf, sem, zsem):
    i = pl.program_id(0)
    nt = pl.num_programs(0)
    cur = i % 2

    n_blocks = xs_hbm.shape[0] * RUN_ALIGN // ROW_BLOCK

    def zero_fill(wait):
        def go(cp):
            if wait:
                cp.wait()
            else:
                cp.start()

        def tail(e, carry):
            n = tail_sm[N_EXPERTS + e]

            @pl.when(n > 0)
            def _():
                go(_run_copy(zbuf, xs_hbm, 0, tail_sm[e], n, zsem))
            return carry

        def block(j, carry):
            go(_run_copy(zbuf, xs_hbm, 0, j * ROW_BLOCK, ROW_BLOCK, zsem))
            return carry

        lax.fori_loop(0, N_EXPERTS, tail, 0)
        lax.fori_loop(tail_sm[2 * N_EXPERTS], n_blocks - 1, block, 0)

    @pl.when(i == 0)
    def _():
        for b in range(2):
            ybuf[b, _groups(ROWS_LOCAL):_groups(ROWS_LOCAL) + 1] = jnp.zeros((1, RUN_ALIGN, D_MODEL), BF16)
        zbuf[...] = jnp.zeros_like(zbuf)
        zero_fill(False)

    hb = h_ref[...].astype(BF16)
    pos = post_ref[0]
    for r0 in range(0, ROWS_LOCAL, 256):
        chunk = min(256, ROWS_LOCAL - r0)
        rows = (lax.broadcasted_iota(jnp.int32, (chunk, TOK_TILE), 0) + r0).astype(F32)
        p = jnp.zeros((chunk, TOK_TILE), F32)
        for k in range(TOP_K):
            p = jnp.where(rows == pos[k:k + 1, :], 1.0, p)
        ybuf[cur, _groups(r0):_groups(r0 + chunk)] = _as_groups(_dot(p.astype(BF16), hb).astype(BF16))

    def total_rows(tile):
        return cnt_sm[nt * N_EXPERTS + tile]

    @pl.when(i > 0)
    def _():
        n_prev = total_rows(i - 1)
        _run_copy(ybuf.at[1 - cur], xs_hbm, 0, 0, n_prev, sem).wait()

    for e in range(N_EXPERTS):
        k = i * N_EXPERTS + e
        _run_copy(ybuf.at[cur], xs_hbm, slot_sm[k], dst_sm[k], cnt_sm[k], sem).start()

    @pl.when(i == nt - 1)
    def _():
        _run_copy(ybuf.at[cur], xs_hbm, 0, 0, total_rows(i), sem).wait()
        zero_fill(True)
        last = _run_copy(zbuf, xs_hbm, 0, (n_blocks - 1) * ROW_BLOCK, ROW_BLOCK, zsem)
        last.start()
        last.wait()


def _dispatch(cnt, slot, dst, tail, post, h2d, n_rows):
    t = h2d.shape[0]
    nt = t // TOK_TILE
    return pl.pallas_call(
        _dispatch_kernel,
        grid_spec=pltpu.PrefetchScalarGridSpec(
            num_scalar_prefetch=4,
            grid=(nt,),
            in_specs=[
                pl.BlockSpec((1, V7X_SUBLANES, TOK_TILE), lambda i, *_: (i, 0, 0)),
                pl.BlockSpec((TOK_TILE, D_MODEL), lambda i, *_: (i, 0)),
            ],
            out_specs=pl.BlockSpec(memory_space=pl.ANY),
            scratch_shapes=[
                pltpu.VMEM((2, _groups(ROWS_LOCAL) + 1, RUN_ALIGN, D_MODEL), BF16),
                pltpu.VMEM((_groups(ROW_BLOCK), RUN_ALIGN, D_MODEL), BF16),
                pltpu.SemaphoreType.DMA(()),
                pltpu.SemaphoreType.DMA(()),
            ],
        ),
        out_shape=jax.ShapeDtypeStruct((_groups(n_rows), RUN_ALIGN, D_MODEL), BF16),
        compiler_params=_compiler_params(("arbitrary",)),
        name="moe_dispatch",
    )(cnt, slot, dst, tail, post, h2d)


def _expert_kernel(bexp_sm, nused_sm, nvalid_sm, xs_ref, wg_ref, bg_ref, wu_ref, bu_ref, wd_ref, bd_ref, ys_ref,
                   hbuf, wg_b, wu_b, wd_b):
    j = pl.program_id(0)
    used = j < nused_sm[0]
    new_expert = jnp.logical_or(j == 0, bexp_sm[j] != bexp_sm[jnp.maximum(j, 1) - 1])

    @pl.when(jnp.logical_and(used, new_expert))
    def _():
        def cast(i, carry):
            sl = pl.ds(pl.multiple_of(i * 128, 128), 128)
            wg_b[sl, :] = wg_ref[0, sl, :].astype(BF16)
            wu_b[sl, :] = wu_ref[0, sl, :].astype(BF16)
            wd_b[sl, :] = wd_ref[0, sl, :].astype(BF16)
            return carry
        lax.fori_loop(0, D_MODEL // 128, cast, 0)

    def mlp(rows):
        xb = xs_ref[0:_groups(rows)].reshape(rows, D_MODEL)
        step = 512
        for n in range(0, D_MODEL, step):
            g = _dot(xb, wg_b[:, n:n + step]) + bg_ref[0, :, n:n + step]
            u = _dot(xb, wu_b[:, n:n + step]) + bu_ref[0, :, n:n + step]
            g = jnp.minimum(g, SWIGLU_LIMIT)
            u = jnp.clip(u, -SWIGLU_LIMIT, SWIGLU_LIMIT)
            hbuf[0:rows, n:n + step] = (g * jax.nn.sigmoid(SWIGLU_ALPHA * g) * (u + 1.0)).astype(BF16)
        ys_ref[0:_groups(rows)] = _as_groups((_dot(hbuf[0:rows, :], wd_b[...]) + bd_ref[0]).astype(BF16))

    half = ROW_BLOCK // 2
    upper_rows = nvalid_sm[j] > half

    @pl.when(jnp.logical_and(used, upper_rows))
    def _():
        mlp(ROW_BLOCK)

    @pl.when(jnp.logical_and(used, jnp.logical_not(upper_rows)))
    def _():
        mlp(half)
        ys_ref[_groups(half):_groups(ROW_BLOCK)] = jnp.zeros((_groups(ROW_BLOCK - half), RUN_ALIGN, D_MODEL), BF16)


def _experts(bexp, nused, nvalid, xs, wg, bg, wu, bu, wd, bd):
    nb = xs.shape[0] * RUN_ALIGN // ROW_BLOCK
    row_blk = (_groups(ROW_BLOCK), RUN_ALIGN, D_MODEL)

    def row_map(j, be, nu, nv):
        return (jnp.minimum(j, nu[0] - 1), 0, 0)

    def w_map(j, be, nu, nv):
        return (be[jnp.minimum(j, nu[0] - 1)], 0, 0)

    wspec = pl.BlockSpec((1, D_MODEL, D_MODEL), w_map)
    bspec = pl.BlockSpec((1, 1, D_MODEL), w_map)
    return pl.pallas_call(
        _expert_kernel,
        grid_spec=pltpu.PrefetchScalarGridSpec(
            num_scalar_prefetch=3,
            grid=(nb,),
            in_specs=[pl.BlockSpec(row_blk, row_map), wspec, bspec, wspec, bspec, wspec, bspec],
            out_specs=pl.BlockSpec(row_blk, row_map),
            scratch_shapes=[pltpu.VMEM((ROW_BLOCK, D_MODEL), BF16)]
            + [pltpu.VMEM((D_MODEL, D_MODEL), BF16) for _ in range(3)],
        ),
        out_shape=jax.ShapeDtypeStruct(xs.shape, xs.dtype),
        input_output_aliases={3: 0},
        compiler_params=_compiler_params(("arbitrary",)),
        name="moe_experts",
    )(bexp, nused, nvalid, xs, wg, bg, wu, bu, wd, bd)


COMBINE_TILES = 2


def _combine_kernel(cnt_sm, slot_sm, dst_sm, col_ref, h_ref, g_ref, b_ref, ys_hbm, o_ref, ybuf, sbuf, sem):
    i = pl.program_id(0)
    ns = pl.num_programs(0)
    nt = ns * COMBINE_TILES
    cur = i % 2

    def fetch(step, buf):
        for s in range(COMBINE_TILES):
            for e in range(N_EXPERTS):
                k = (step * COMBINE_TILES + s) * N_EXPERTS + e
                _run_copy(ys_hbm, ybuf.at[buf, s], dst_sm[k], slot_sm[k], cnt_sm[k], sem.at[buf]).start()

    @pl.when(i == 0)
    def _():
        ybuf[...] = jnp.zeros_like(ybuf)
        fetch(0, 0)

    @pl.when(i + 1 < ns)
    def _():
        fetch(i + 1, 1 - cur)

    for s in range(COMBINE_TILES):
        _run_copy(ys_hbm, ybuf.at[cur, s], 0, 0, cnt_sm[nt * N_EXPERTS + i * COMBINE_TILES + s], sem.at[cur]).wait()

    chunk = 256
    for s in range(COMBINE_TILES):
        rs = slice(s * TOK_TILE, (s + 1) * TOK_TILE)
        col = col_ref[rs, :]
        for r0 in range(0, ROWS_LOCAL, chunk):
            width = min(chunk, ROWS_LOCAL - r0)
            rows = (lax.broadcasted_iota(jnp.int32, (TOK_TILE, width), 1) + r0).astype(F32)
            sel = jnp.zeros((TOK_TILE, width), F32)
            for k in range(TOP_K):
                sel = jnp.where(rows == col[:, k:k + 1], col[:, TOP_K + k:TOP_K + k + 1], sel)
            sbuf[s, :, r0:r0 + width] = sel.astype(BF16)
    for s in range(COMBINE_TILES):
        rs = slice(s * TOK_TILE, (s + 1) * TOK_TILE)
        for n in range(0, D_MODEL, chunk):
            rows_n = ybuf[cur, s, 0:_groups(ROWS_LOCAL), :, n:n + chunk].reshape(ROWS_LOCAL, chunk)
            moe = _dot(sbuf[s], rows_n)
            o_ref[rs, n:n + chunk] = DEEPNORM_ALPHA * h_ref[rs, n:n + chunk] + moe
    for s in range(COMBINE_TILES):
        rs = slice(s * TOK_TILE, (s + 1) * TOK_TILE)
        o_ref[rs, :] = _layer_norm(o_ref[rs, :], g_ref[...], b_ref[...])


def _combine(cnt, slot, dst, col, h2d, ln_g, ln_b, ys):
    t = h2d.shape[0]
    tm = TOK_TILE * COMBINE_TILES
    return pl.pallas_call(
        _combine_kernel,
        grid_spec=pltpu.PrefetchScalarGridSpec(
            num_scalar_prefetch=3,
            grid=(t // tm,),
            in_specs=[
                pl.BlockSpec((tm, V7X_LANES), lambda i, *_: (i, 0)),
                pl.BlockSpec((tm, D_MODEL), lambda i, *_: (i, 0)),
                pl.BlockSpec((1, D_MODEL), lambda i, *_: (0, 0)),
                pl.BlockSpec((1, D_MODEL), lambda i, *_: (0, 0)),
                pl.BlockSpec(memory_space=pl.ANY),
            ],
            out_specs=pl.BlockSpec((tm, D_MODEL), lambda i, *_: (i, 0)),
            scratch_shapes=[
                pltpu.VMEM((2, COMBINE_TILES, _groups(ROWS_LOCAL) + 1, RUN_ALIGN, D_MODEL), BF16),
                pltpu.VMEM((COMBINE_TILES, TOK_TILE, ROWS_LOCAL), BF16),
                pltpu.SemaphoreType.DMA((2,)),
            ],
        ),
        out_shape=jax.ShapeDtypeStruct((t, D_MODEL), F32),
        compiler_params=_compiler_params(("arbitrary",)),
        name="moe_combine",
    )(cnt, slot, dst, col, h2d, ln_g, ln_b, ys)


def _pad_lanes(a, width=V7X_LANES):
    return jnp.pad(a, ((0, 0), (0, width - a.shape[1])))


def _layer(h3d, w_in, conv_w, conv_b, w_mq, w_mk, b_igate, b_fgate, mnorm_g, w_out,
           ln1_g, ln1_b, w_router, b_router, w_gate, b_gate, w_up, b_up, w_down, b_down, ln2_g, ln2_b):
    batch, seq, _ = h3d.shape
    t = batch * seq
    x2d = h3d.reshape(t, D_MODEL)

    w_main = w_in[:, :PROJ_MAIN].astype(BF16)
    w_gates = _pad_lanes(w_in[:, PROJ_MAIN:]).astype(BF16)
    proj, gcol, grow = _inproj(x2d, w_main, w_gates, batch, seq)
    proj3d = proj.reshape(batch, seq, PROJ_MAIN)
    attn = _attention(proj3d)
    gate_bias = jnp.concatenate([b_igate, b_fgate]).astype(F32)
    bias_col = _pad_lanes(gate_bias[None, :])
    bias_row = jnp.broadcast_to(gate_bias[:, None], (V7X_SUBLANES, seq))
    hm = _mlstm(proj3d, gcol.reshape(batch, seq, V7X_LANES), grow, conv_w, conv_b[None, :],
                w_mq.astype(BF16), w_mk.astype(BF16), bias_col, bias_row, mnorm_g[None, :])

    wo = w_out.astype(BF16)
    h2d, post, col, tab, tot = _outproj_router(
        attn.reshape(t, D_ATTN), hm.reshape(t, D_MLSTM), x2d, wo[:D_ATTN], wo[D_ATTN:],
        ln1_g[None, :], ln1_b[None, :], _pad_lanes(w_router).astype(BF16), _pad_lanes(b_router[None, :]))

    nt = t // TOK_TILE
    total = tot[0, :N_EXPERTS]
    region = (total + ROW_BLOCK - 1) // ROW_BLOCK * ROW_BLOCK
    region_end = jnp.cumsum(region)
    region_start = region_end - region
    n_rows = (t * TOP_K + nt * N_EXPERTS * (RUN_ALIGN - 1)) // ROW_BLOCK * ROW_BLOCK + (N_EXPERTS + 1) * ROW_BLOCK
    nb = n_rows // ROW_BLOCK
    runs = tab[:, 0, :N_EXPERTS]
    empty = runs == 0
    cnt = jnp.where(empty, RUN_ALIGN, runs)
    cnt = jnp.concatenate([cnt.reshape(-1), jnp.sum(cnt, axis=1)])
    slot = jnp.where(empty, ROWS_LOCAL, tab[:, 2, :N_EXPERTS]).reshape(-1)
    dst = jnp.where(empty, n_rows - ROW_BLOCK, tab[:, 1, :N_EXPERTS] + region_start[None, :]).reshape(-1)
    tail = jnp.concatenate([region_start + total, region - total, region_end[-1:] // ROW_BLOCK]).astype(jnp.int32)
    block_row = jnp.arange(nb, dtype=jnp.int32) * ROW_BLOCK
    bexp = jnp.minimum(jnp.sum(region_end[None, :] <= block_row[:, None], axis=1), N_EXPERTS - 1).astype(jnp.int32)
    nused = (region_end[-1:] // ROW_BLOCK).astype(jnp.int32)
    own = bexp[:, None] == jnp.arange(N_EXPERTS, dtype=jnp.int32)[None, :]
    used_end = jnp.sum(jnp.where(own, (region_start + total)[None, :], 0), axis=1)
    nvalid = jnp.clip(used_end - block_row, 0, ROW_BLOCK).astype(jnp.int32)

    xs = _dispatch(cnt, slot, dst, tail, post, h2d, n_rows)
    ys = _experts(bexp, nused, nvalid, xs, w_gate, b_gate[:, None, :], w_up, b_up[:, None, :], w_down, b_down[:, None, :])
    out = _combine(cnt, slot, dst, col, h2d, ln2_g[None, :], ln2_b[None, :], ys)
    return out.reshape(batch, seq, D_MODEL)


def kernel(x, w_in, conv_w, conv_b, w_mq, w_mk, b_igate, b_fgate, mnorm_g, w_out, ln1_g, ln1_b, w_router, b_router, w_gate, b_gate, w_up, b_up, w_down, b_down, ln2_g, ln2_b):
    h = x
    for l in range(w_in.shape[0]):
        h = _layer(h, w_in[l], conv_w[l], conv_b[l], w_mq[l], w_mk[l], b_igate[l], b_fgate[l], mnorm_g[l],
                   w_out[l], ln1_g[l], ln1_b[l], w_router[l], b_router[l], w_gate[l], b_gate[l], w_up[l],
                   b_up[l], w_down[l], b_down[l], ln2_g[l], ln2_b[l])
    return h
```

```python
import functools
import math

import jax
import jax.numpy as jnp
from jax import lax
from jax.experimental import pallas as pl
from jax.experimental.pallas import tpu as pltpu

F32 = jnp.float32
BF16 = jnp.bfloat16
NEG_INF = float("-inf")

V7X_LANES = 128
V7X_SUBLANES = 8
V7X_VMEM_LIMIT_BYTES = 56 * 1024 * 1024

D_MODEL = 1024
D_ATTN = 512
HEAD_DIM_A = 64
D_MLSTM = 512
N_HEADS_M = 4
HEAD_DIM_M = 128
CONV_WIDTH = 4
CHUNK = 128
DILATED_BRANCHES = ((128, 1), (512, 4), (2048, 16))
N_EXPERTS = 32
TOP_K = 4
SWIGLU_LIMIT = 7.0
SWIGLU_ALPHA = 1.702
DEEPNORM_ALPHA = 2.0 ** 0.25
LN_EPS = 1e-5
RMS_EPS = 1e-6

PROJ_MAIN = 3 * D_ATTN + 3 * D_MLSTM
TOK_TILE = 256
RUN_ALIGN = 4
ROWS_LOCAL = -(-(TOK_TILE * TOP_K + N_EXPERTS * (RUN_ALIGN - 1)) // V7X_LANES) * V7X_LANES
ROW_BLOCK = 512
ROUTER_SUBTILES = 4


def _dot(a, b):
    return jnp.dot(a, b, preferred_element_type=F32)


def _dot_nt(a, b):
    return lax.dot_general(a, b, (((1,), (1,)), ((), ())), preferred_element_type=F32)


def _split3(x):
    hi = x.astype(BF16)
    r1 = x - hi.astype(F32)
    mid = r1.astype(BF16)
    lo = (r1 - mid.astype(F32)).astype(BF16)
    return hi, mid, lo


def _compiler_params(sem):
    return pltpu.CompilerParams(dimension_semantics=sem, vmem_limit_bytes=V7X_VMEM_LIMIT_BYTES)


def _inproj_kernel(x_ref, w_ref, wg_ref, proj_ref, gcol_ref, grow_ref):
    xb = x_ref[...].astype(BF16)
    step = 512
    for n in range(0, PROJ_MAIN, step):
        proj_ref[:, n:n + step] = _dot(xb, w_ref[:, n:n + step])
    g = _dot(xb, wg_ref[...])
    gcol_ref[...] = g
    grow_ref[0] = g.T[:V7X_SUBLANES, :]


def _inproj(x2d, w_main, w_gates, batch, seq):
    t = x2d.shape[0]
    tm = 512
    per_b = seq // tm
    return pl.pallas_call(
        _inproj_kernel,
        grid=(t // tm,),
        in_specs=[
            pl.BlockSpec((tm, D_MODEL), lambda i: (i, 0)),
            pl.BlockSpec((D_MODEL, PROJ_MAIN), lambda i: (0, 0)),
            pl.BlockSpec((D_MODEL, V7X_LANES), lambda i: (0, 0)),
        ],
        out_specs=[
            pl.BlockSpec((tm, PROJ_MAIN), lambda i: (i, 0)),
            pl.BlockSpec((tm, V7X_LANES), lambda i: (i, 0)),
            pl.BlockSpec((1, V7X_SUBLANES, tm), lambda i: (i // per_b, 0, i % per_b)),
        ],
        out_shape=[
            jax.ShapeDtypeStruct((t, PROJ_MAIN), F32),
            jax.ShapeDtypeStruct((t, V7X_LANES), F32),
            jax.ShapeDtypeStruct((batch, V7X_SUBLANES, seq), F32),
        ],
        compiler_params=_compiler_params(("arbitrary",)),
        name="inproj",
    )(x2d, w_main, w_gates)


ATTN_GROUP = 8


def _attn_kernel(q_ref, k_ref, v_ref, o_ref, q0s, q1s, bias_b, bias_f, sg, pg,
                 o0, o1, o2, l0, l1, l2, m0, m1, m2, x0, x1, x2, *, seq):
    obufs, lbufs, mbufs, xbufs = (o0, o1, o2), (l0, l1, l2), (m0, m1, m2), (x0, x1, x2)
    two = 2 * CHUNK
    head0 = lax.broadcasted_iota(jnp.int32, (CHUNK, V7X_LANES), 1) < HEAD_DIM_A
    qscale = HEAD_DIM_A ** -0.5 * math.log2(math.e)

    def prep(i, carry):
        sl = pl.ds(pl.multiple_of(i * two, two), two)
        h0 = lax.broadcasted_iota(jnp.int32, (two, V7X_LANES), 1) < HEAD_DIM_A
        q = q_ref[0, sl, :] * qscale
        q0s[sl, :] = jnp.where(h0, q, 0.0)
        q1s[sl, :] = jnp.where(h0, 0.0, q)
        return carry

    lax.fori_loop(0, seq // two, prep, 0)

    qi = lax.broadcasted_iota(jnp.int32, (two, two), 0) % CHUNK
    kj = lax.broadcasted_iota(jnp.int32, (two, two), 1)
    bias_b[...] = jnp.where((kj >= qi) & (kj <= qi + CHUNK), 0.0, NEG_INF)
    qf = lax.broadcasted_iota(jnp.int32, (two, CHUNK), 0) % CHUNK
    kf = lax.broadcasted_iota(jnp.int32, (two, CHUNK), 1)
    bias_f[...] = jnp.where(kf <= qf, 0.0, NEG_INF)

    def run_group(c, dil, starts, has_prev):
        assert len(starts) <= ATTN_GROUP
        nk = two if has_prev else CHUNK
        bias_ref = bias_b if has_prev else bias_f

        def rows(s0):
            return pl.ds(s0, CHUNK) if dil == 1 else pl.ds(s0, CHUNK, stride=dil)

        def keys(ref, st):
            if has_prev:
                return jnp.concatenate([ref[0, rows(st - dil * CHUNK), :], ref[0, rows(st), :]], axis=0).astype(BF16)
            return ref[0, rows(st), :].astype(BF16)

        def both_heads(x):
            return jnp.where(head0, x[:CHUNK], x[CHUNK:])

        for j, st in enumerate(starts):
            q2 = jnp.concatenate([q0s[rows(st), :], q1s[rows(st), :]], axis=0).astype(BF16)
            sg[j, :, 0:nk] = _dot_nt(q2, keys(k_ref, st)) + bias_ref[...]
        for j, st in enumerate(starts):
            s = sg[j, :, 0:nk]
            m = jnp.max(s, axis=1, keepdims=True)
            pg[j, :, 0:nk] = jnp.exp2(s - m).astype(BF16)
            ma = jnp.broadcast_to(m[:CHUNK], (CHUNK, V7X_LANES))
            mb = jnp.broadcast_to(m[CHUNK:], (CHUNK, V7X_LANES))
            mbufs[c][rows(st), :] = jnp.where(head0, ma, mb)
            xbufs[c][rows(st), :] = jnp.where(head0, mb, ma)
        khead0 = lax.broadcasted_iota(jnp.int32, (nk, V7X_LANES), 1) < HEAD_DIM_A
        for j, st in enumerate(starts):
            if has_prev:
                vv = jnp.concatenate([v_ref[0, rows(st - dil * CHUNK), :], v_ref[0, rows(st), :]], axis=0)
            else:
                vv = v_ref[0, rows(st), :]
            oa = _dot(pg[j, 0:CHUNK, 0:nk], jnp.where(khead0, vv, 1.0).astype(BF16))
            ob = _dot(pg[j, CHUNK:two, 0:nk], jnp.where(khead0, 1.0, vv).astype(BF16))
            obufs[c][rows(st), :] = jnp.where(head0, oa, ob)
            lbufs[c][rows(st), :] = jnp.where(head0, ob, oa)

    for c, (window, dil) in enumerate(DILATED_BRANCHES):
        assert window // dil == CHUNK
        nb = seq // (dil * CHUNK)
        span = dil * CHUNK
        if dil == 1:
            run_group(c, dil, [0], False)
            per = 5
            assert (nb - 1) % per == 0

            def band1(g, carry, c=c, dil=dil, per=per, span=span):
                base = span + g * (per * span)
                run_group(c, dil, [base + span * j for j in range(per)], True)
                return carry

            lax.fori_loop(0, (nb - 1) // per, band1, 0, unroll=True)
        elif nb > 1:
            run_group(c, dil, list(range(dil)), False)

            def band2(g, carry, c=c, dil=dil, nb=nb, span=span):
                starts = [2 * g + rr + span * b for rr in range(2) for b in range(1, nb)]
                run_group(c, dil, starts, True)
                return carry

            lax.fori_loop(0, dil // 2, band2, 0, unroll=True)
        else:
            def firsts(g, carry, c=c, dil=dil):
                run_group(c, dil, [ATTN_GROUP * g + j for j in range(ATTN_GROUP)], False)
                return carry

            lax.fori_loop(0, dil // ATTN_GROUP, firsts, 0)

    def combine(i, carry):
        sl = pl.ds(pl.multiple_of(i * 256, 256), 256)
        ma, mb, mc = m0[sl, :], m1[sl, :], m2[sl, :]
        mx = jnp.maximum(jnp.maximum(ma, mb), mc)
        wa, wb, wc = jnp.exp2(ma - mx), jnp.exp2(mb - mx), jnp.exp2(mc - mx)
        xa, xb, xc = x0[sl, :], x1[sl, :], x2[sl, :]
        xm = jnp.maximum(jnp.maximum(xa, xb), xc)
        den = (jnp.exp2(xa - xm) * l0[sl, :] + jnp.exp2(xb - xm) * l1[sl, :] + jnp.exp2(xc - xm) * l2[sl, :])
        den = pltpu.roll(den, HEAD_DIM_A, axis=1)
        out = (wa * o0[sl, :] + wb * o1[sl, :] + wc * o2[sl, :]) / den
        o_ref[0, sl, :] = out.astype(o_ref.dtype)
        return carry

    lax.fori_loop(0, seq // 256, combine, 0)


def _attention(proj3d):
    batch, seq, _ = proj3d.shape
    nblk = D_ATTN // V7X_LANES
    blk = (1, seq, V7X_LANES)
    scratch = ([pltpu.VMEM((seq, V7X_LANES), F32), pltpu.VMEM((seq, V7X_LANES), F32),
                pltpu.VMEM((2 * CHUNK, 2 * CHUNK), F32), pltpu.VMEM((2 * CHUNK, CHUNK), F32),
                pltpu.VMEM((ATTN_GROUP, 2 * CHUNK, 2 * CHUNK), F32),
                pltpu.VMEM((ATTN_GROUP, 2 * CHUNK, 2 * CHUNK), BF16)]
               + [pltpu.VMEM((seq, V7X_LANES), F32) for _ in range(12)])
    return pl.pallas_call(
        functools.partial(_attn_kernel, seq=seq),
        grid=(batch, nblk),
        in_specs=[
            pl.BlockSpec(blk, lambda b, g: (b, 0, g)),
            pl.BlockSpec(blk, lambda b, g: (b, 0, nblk + g)),
            pl.BlockSpec(blk, lambda b, g: (b, 0, 2 * nblk + g)),
        ],
        out_specs=pl.BlockSpec(blk, lambda b, g: (b, 0, g)),
        out_shape=jax.ShapeDtypeStruct((batch, seq, D_ATTN), BF16),
        scratch_shapes=scratch,
        compiler_params=_compiler_params(("arbitrary", "arbitrary")),
        name="dilated_attention",
    )(proj3d, proj3d, proj3d)


def _log_sigmoid(x):
    return jnp.minimum(x, 0.0) - jnp.log1p(jnp.exp(-jnp.abs(x)))


MLSTM_HEADS_PER_STEP = 2


def _mlstm_kernel(xm_ref, vm_ref, om_ref, gcol_ref, grow_ref, cw_ref, cb_ref, wq_ref, wk_ref,
                  bcol_ref, brow_ref, mg_ref, o_ref, *scratch, seq):
    nh = MLSTM_HEADS_PER_STEP
    per_head = len(scratch) // nh
    nchunk = seq // CHUNK
    ri = lax.broadcasted_iota(jnp.int32, (CHUNK, CHUNK), 0)
    ci = lax.broadcasted_iota(jnp.int32, (CHUNK, CHUNK), 1)
    tril = ri >= ci
    tril_b = jnp.where(tril, 1.0, 0.0).astype(BF16)
    triu_b = jnp.where(ri <= ci, 1.0, 0.0).astype(BF16)
    pad = V7X_SUBLANES
    qscale = HEAD_DIM_M ** -0.5

    class Head:
        def __init__(self, hh):
            (self.xpad, self.q_s, self.k_s, self.icol_s, self.lfcol_s, self.gcol_s,
             self.ri_s, self.rf_s, self.gf_s, self.mi_s, self.mo_s, self.gl_s) = scratch[hh * per_head:(hh + 1) * per_head]
            self.h = pl.program_id(1) * nh + hh
            self.cols = slice(hh * V7X_LANES, (hh + 1) * V7X_LANES)
            self.wq = wq_ref[hh]
            self.wk = wk_ref[hh]
            self.mg = mg_ref[:, self.cols]

    heads = [Head(hh) for hh in range(nh)]

    for hd in heads:
        hd.xpad[0:pad, :] = jnp.zeros((pad, V7X_LANES), F32)
        hd.xpad[pad:pad + seq, :] = xm_ref[0, :, hd.cols]

    def conv_step(i, carry):
        base = pl.multiple_of(i * 256, 256)
        rows = pl.ds(base, 256)
        gc = gcol_ref[0, rows, :] + bcol_ref[...]
        lane256 = lax.broadcasted_iota(jnp.int32, (256, V7X_LANES), 1)
        for hd in heads:
            y = jnp.broadcast_to(cb_ref[:, hd.cols], (256, V7X_LANES))
            xw = hd.xpad[pl.ds(base, 256 + pad), :]
            for j in range(CONV_WIDTH):
                off = pad - (CONV_WIDTH - 1) + j
                y = y + cw_ref[j:j + 1, hd.cols] * xw[off:off + 256, :]
            xc = (y * jax.nn.sigmoid(y)).astype(BF16)
            hd.q_s[rows, :] = _dot(xc, hd.wq) * qscale
            hd.k_s[rows, :] = _dot(xc, hd.wk)
            ic = jnp.sum(jnp.where(lane256 == hd.h, gc, 0.0), axis=1, keepdims=True)
            fc = jnp.sum(jnp.where(lane256 == hd.h + N_HEADS_M, gc, 0.0), axis=1, keepdims=True)
            hd.icol_s[rows, :] = jnp.broadcast_to(ic, (256, V7X_LANES))
            hd.lfcol_s[rows, :] = jnp.broadcast_to(_log_sigmoid(fc), (256, V7X_LANES))
        return carry

    lax.fori_loop(0, seq // 256, conv_step, 0)

    gr = grow_ref[0] + brow_ref[...]
    sub = lax.broadcasted_iota(jnp.int32, (V7X_SUBLANES, seq), 0)
    for hd in heads:
        irow = jnp.sum(jnp.where(sub == hd.h, gr, 0.0), axis=0, keepdims=True)
        lfrow = _log_sigmoid(jnp.sum(jnp.where(sub == hd.h + N_HEADS_M, gr, 0.0), axis=0, keepdims=True))
        for c in range(nchunk):
            hd.ri_s[c:c + 1, :] = irow[:, c * CHUNK:(c + 1) * CHUNK]
            hd.rf_s[c:c + 1, :] = lfrow[:, c * CHUNK:(c + 1) * CHUNK]

    def cum_step(c, carry):
        sl = pl.ds(pl.multiple_of(c * CHUNK, CHUNK), CHUNK)
        for hd in heads:
            hd.gcol_s[sl, :] = sum(_dot(tril_b, part) for part in _split3(hd.lfcol_s[sl, :]))
        return carry

    lax.fori_loop(0, nchunk, cum_step, 0, unroll=2)

    crow = lax.broadcasted_iota(jnp.int32, (nchunk, V7X_LANES), 0)
    for hd in heads:
        irows = hd.ri_s[...]
        grows = sum(_dot(part, triu_b) for part in _split3(hd.rf_s[...]))
        hd.gf_s[...] = grows
        g_last = grows[:, CHUNK - 1:CHUNK]
        a_max = jnp.max(g_last - grows + irows, axis=1, keepdims=True)
        m = jnp.zeros((1, 1), F32)
        m_in = jnp.zeros((nchunk, V7X_LANES), F32)
        m_out = jnp.zeros((nchunk, V7X_LANES), F32)
        for c in range(nchunk):
            m_in = jnp.where(crow == c, m, m_in)
            m = jnp.maximum(g_last[c:c + 1, :] + m, a_max[c:c + 1, :])
            m_out = jnp.where(crow == c, m, m_out)
        hd.mi_s[...] = m_in
        hd.mo_s[...] = m_out
        hd.gl_s[...] = jnp.broadcast_to(g_last, (nchunk, V7X_LANES))

    def chunk_head(hd, c, cmat, nrow):
        sl = pl.ds(pl.multiple_of(c * CHUNK, CHUNK), CHUNK)
        one = pl.ds(c, 1)
        qc = hd.q_s[sl, :]
        kc = hd.k_s[sl, :]
        vc = vm_ref[0, sl, hd.cols].astype(BF16)
        icol = hd.icol_s[sl, :]
        gcol = hd.gcol_s[sl, :]
        irow_c = hd.ri_s[one, :]
        grow_c = hd.gf_s[one, :]
        m = hd.mi_s[one, :]
        m_new = hd.mo_s[one, :]
        g_last = hd.gl_s[one, :]

        d = gcol - grow_c + irow_c
        d = jnp.where(tril, d, NEG_INF)
        inter = gcol + m
        m_t = jnp.maximum(inter, jnp.max(d, axis=1, keepdims=True))
        w_intra = jnp.exp(d - m_t)
        w_inter = jnp.exp(inter - m_t)
        qb = qc.astype(BF16)
        kb = kc.astype(BF16)
        qk = _dot_nt(qb, kb) * w_intra
        num = _dot(qk.astype(BF16), vc) + w_inter * _dot(qb, cmat.astype(BF16))
        den = jnp.sum(qk, axis=1, keepdims=True) + w_inter * jnp.sum(qc * nrow, axis=1, keepdims=True)
        hc = num / jnp.maximum(jnp.abs(den), jnp.exp(-m_t))

        a_col = g_last - gcol + icol
        decay = jnp.exp(g_last + m - m_new)
        wk_col = jnp.exp(a_col - m_new)
        kw = wk_col * kc
        c_new = decay * cmat + _dot(kw.T.astype(BF16), vc)
        n_new = decay * nrow + jnp.sum(kw, axis=0, keepdims=True)

        hn = hc * lax.rsqrt(jnp.mean(hc * hc, axis=1, keepdims=True) + RMS_EPS) * hd.mg
        hn = jax.nn.sigmoid(om_ref[0, sl, hd.cols]) * hn
        o_ref[0, sl, hd.cols] = hn.astype(o_ref.dtype)
        return c_new, n_new

    def chunk_step(c, carry):
        return tuple(chunk_head(hd, c, *carry[i]) for i, hd in enumerate(heads))

    init = tuple((jnp.zeros((HEAD_DIM_M, HEAD_DIM_M), F32), jnp.zeros((1, HEAD_DIM_M), F32)) for _ in heads)
    lax.fori_loop(0, nchunk, chunk_step, init, unroll=8)


def _mlstm(proj3d, gcol3d, grow3d, conv_w, conv_b, wq, wk, bias_col, bias_row, mnorm):
    batch, seq, _ = proj3d.shape
    nh = MLSTM_HEADS_PER_STEP
    width = nh * V7X_LANES
    blk = (1, seq, width)
    a0 = 3 * D_ATTN // width
    ng = N_HEADS_M // nh
    per_head = ([pltpu.VMEM((seq + V7X_SUBLANES, V7X_LANES), F32)]
                + [pltpu.VMEM((seq, V7X_LANES), F32) for _ in range(5)]
                + [pltpu.VMEM((seq // CHUNK, V7X_LANES), F32) for _ in range(6)])
    return pl.pallas_call(
        functools.partial(_mlstm_kernel, seq=seq),
        grid=(batch, ng),
        in_specs=[
            pl.BlockSpec(blk, lambda b, g: (b, 0, a0 + g)),
            pl.BlockSpec(blk, lambda b, g: (b, 0, a0 + ng + g)),
            pl.BlockSpec(blk, lambda b, g: (b, 0, a0 + 2 * ng + g)),
            pl.BlockSpec((1, seq, V7X_LANES), lambda b, g: (b, 0, 0)),
            pl.BlockSpec((1, V7X_SUBLANES, seq), lambda b, g: (b, 0, 0)),
            pl.BlockSpec((CONV_WIDTH, width), lambda b, g: (0, g)),
            pl.BlockSpec((1, width), lambda b, g: (0, g)),
            pl.BlockSpec((nh, HEAD_DIM_M, HEAD_DIM_M), lambda b, g: (g, 0, 0)),
            pl.BlockSpec((nh, HEAD_DIM_M, HEAD_DIM_M), lambda b, g: (g, 0, 0)),
            pl.BlockSpec((1, V7X_LANES), lambda b, g: (0, 0)),
            pl.BlockSpec((V7X_SUBLANES, seq), lambda b, g: (0, 0)),
            pl.BlockSpec((1, width), lambda b, g: (0, g)),
        ],
        out_specs=pl.BlockSpec(blk, lambda b, g: (b, 0, g)),
        out_shape=jax.ShapeDtypeStruct((batch, seq, D_MLSTM), BF16),
        scratch_shapes=per_head * nh,
        compiler_params=_compiler_params(("arbitrary", "arbitrary")),
        name="mlstm",
    )(proj3d, proj3d, proj3d, gcol3d, grow3d, conv_w, conv_b, wq, wk, bias_col, bias_row, mnorm)


def _layer_norm(z, g, b):
    mu = jnp.mean(z, axis=1, keepdims=True)
    zc = z - mu
    var = jnp.mean(zc * zc, axis=1, keepdims=True)
    return zc * lax.rsqrt(var + LN_EPS) * g + b


def _router_kernel(attn_ref, hm_ref, x_ref, woa_ref, wom_ref, g_ref, b_ref, wr_ref, br_ref,
                   h_ref, post_ref, col_ref, tab_ref, tot_ref, carry, *, nsub):
    i = pl.program_id(0)
    tm = TOK_TILE

    @pl.when(i == 0)
    def _():
        carry[...] = jnp.zeros_like(carry)

    ne = N_EXPERTS
    ex = lax.broadcasted_iota(jnp.int32, (ne, tm), 0).astype(F32)
    ri = lax.broadcasted_iota(jnp.int32, (tm, tm), 0)
    ci = lax.broadcasted_iota(jnp.int32, (tm, tm), 1)
    earlier = jnp.where(ri < ci, 1.0, 0.0).astype(BF16)
    er = lax.broadcasted_iota(jnp.int32, (ne, ne), 0)
    ec = lax.broadcasted_iota(jnp.int32, (ne, ne), 1)
    lower = jnp.where(ec < er, 1.0, 0.0).astype(BF16)
    diag = (lax.broadcasted_iota(jnp.int32, (ne, V7X_LANES), 0)
            == lax.broadcasted_iota(jnp.int32, (ne, V7X_LANES), 1))
    sub = lax.broadcasted_iota(jnp.int32, (V7X_SUBLANES, V7X_LANES), 0)
    base = carry[...]

    def to_lanes(colvec):
        return jnp.sum(jnp.where(diag, colvec, 0.0), axis=0, keepdims=True)

    for s in range(nsub):
        rs = slice(s * tm, (s + 1) * tm)
        h_ref[rs, :] = (DEEPNORM_ALPHA * x_ref[rs, :] + _dot(attn_ref[rs, :], woa_ref[...])
                        + _dot(hm_ref[rs, :], wom_ref[...]))
    for s in range(nsub):
        rs = slice(s * tm, (s + 1) * tm)
        hval = _layer_norm(h_ref[rs, :], g_ref[...], b_ref[...])
        h_ref[rs, :] = hval

        logits = _dot(hval.astype(BF16), wr_ref[...]) + br_ref[...]
        lt = logits.T[:ne, :]

        sel, vals = [], []
        for _ in range(TOP_K):
            mx = jnp.max(lt, axis=0, keepdims=True)
            idx = jnp.min(jnp.where(lt == mx, ex, float(ne)), axis=0, keepdims=True)
            hit = ex == idx
            lt = jnp.where(hit, NEG_INF, lt)
            sel.append(hit)
            vals.append(mx)
        exps = [jnp.exp(v - vals[0]) for v in vals]
        den = exps[0] + exps[1] + exps[2] + exps[3]
        gates = [e / den for e in exps]

        onehot = jnp.zeros((ne, tm), F32)
        for hit in sel:
            onehot = jnp.where(hit, 1.0, onehot)
        rank = _dot(onehot.astype(BF16), earlier)
        cnt = jnp.broadcast_to(jnp.sum(onehot, axis=1, keepdims=True), (ne, V7X_LANES))
        cnt_al = jnp.floor((cnt + (RUN_ALIGN - 1)) * (1.0 / RUN_ALIGN))
        slot = _dot(lower, cnt_al.astype(BF16)) * float(RUN_ALIGN)
        cnt_al = cnt_al * float(RUN_ALIGN)
        where_row = slot[:, 0:1] + rank

        rows = [jnp.sum(jnp.where(sel[k], where_row, 0.0), axis=0, keepdims=True) for k in range(TOP_K)]
        info = jnp.concatenate(rows + gates, axis=0)
        post_ref[s] = info
        col_ref[rs, :] = jnp.concatenate([info, jnp.zeros((V7X_LANES - 2 * TOP_K, tm), F32)], axis=0).T

        cnt_row = to_lanes(cnt_al)
        tab = jnp.zeros((V7X_SUBLANES, V7X_LANES), F32)
        tab = jnp.where(sub == 0, cnt_row, tab)
        tab = jnp.where(sub == 1, base, tab)
        tab = jnp.where(sub == 2, to_lanes(slot), tab)
        tab_ref[s] = tab.astype(jnp.int32)
        base = base + cnt_row

    carry[...] = base
    tot_ref[...] = jnp.broadcast_to(base, (V7X_SUBLANES, V7X_LANES)).astype(jnp.int32)


def _outproj_router(attn2d, hm2d, x2d, wo_a, wo_m, ln_g, ln_b, w_r, b_r):
    t = x2d.shape[0]
    nsub = ROUTER_SUBTILES
    tm = TOK_TILE * nsub
    nt = t // TOK_TILE
    const = lambda i: (0, 0)
    return pl.pallas_call(
        functools.partial(_router_kernel, nsub=nsub),
        grid=(t // tm,),
        in_specs=[
            pl.BlockSpec((tm, D_ATTN), lambda i: (i, 0)),
            pl.BlockSpec((tm, D_MLSTM), lambda i: (i, 0)),
            pl.BlockSpec((tm, D_MODEL), lambda i: (i, 0)),
            pl.BlockSpec((D_ATTN, D_MODEL), const),
            pl.BlockSpec((D_MLSTM, D_MODEL), const),
            pl.BlockSpec((1, D_MODEL), const),
            pl.BlockSpec((1, D_MODEL), const),
            pl.BlockSpec((D_MODEL, V7X_LANES), const),
            pl.BlockSpec((1, V7X_LANES), const),
        ],
        out_specs=[
            pl.BlockSpec((tm, D_MODEL), lambda i: (i, 0)),
            pl.BlockSpec((nsub, V7X_SUBLANES, TOK_TILE), lambda i: (i, 0, 0)),
            pl.BlockSpec((tm, V7X_LANES), lambda i: (i, 0)),
            pl.BlockSpec((nsub, V7X_SUBLANES, V7X_LANES), lambda i: (i, 0, 0)),
            pl.BlockSpec((V7X_SUBLANES, V7X_LANES), const),
        ],
        out_shape=[
            jax.ShapeDtypeStruct((t, D_MODEL), F32),
            jax.ShapeDtypeStruct((nt, V7X_SUBLANES, TOK_TILE), F32),
            jax.ShapeDtypeStruct((t, V7X_LANES), F32),
            jax.ShapeDtypeStruct((nt, V7X_SUBLANES, V7X_LANES), jnp.int32),
            jax.ShapeDtypeStruct((V7X_SUBLANES, V7X_LANES), jnp.int32),
        ],
        scratch_shapes=[pltpu.VMEM((1, V7X_LANES), F32)],
        compiler_params=_compiler_params(("arbitrary",)),
        name="outproj_router",
    )(attn2d, hm2d, x2d, wo_a, wo_m, ln_g, ln_b, w_r, b_r)


def _groups(rows):
    if isinstance(rows, int):
        assert rows % RUN_ALIGN == 0
        return rows // RUN_ALIGN
    return rows >> (RUN_ALIGN.bit_length() - 1)


def _as_groups(x):
    return x.reshape(x.shape[0] // RUN_ALIGN, RUN_ALIGN, x.shape[1])


def _run_copy(src, dst, start_src, start_dst, n, sem):
    g = _groups(n)
    return pltpu.make_async_copy(src.at[pl.ds(_groups(start_src), g)], dst.at[pl.ds(_groups(start_dst), g)], sem)


DISPATCH_TILES = 2


def _dispatch_kernel(cnt_sm, slot_sm, dst_sm, tail_sm, post_ref, h_ref, xs_hbm, ybuf, zbuf, sem, zsem):
    i = pl.program_id(0)
    ns = pl.num_programs(0)
    nt = ns * DISPATCH_TILES
    cur = i % 2

    n_blocks = xs_hbm.shape[0] * RUN_ALIGN // ROW_BLOCK

    def zero_fill(wait):
        def go(cp):
            if wait:
                cp.wait()
            else:
                cp.start()

        def tail(e, carry):
            n = tail_sm[N_EXPERTS + e]

            @pl.when(n > 0)
            def _():
                go(_run_copy(zbuf, xs_hbm, 0, tail_sm[e], n, zsem))
            return carry

        def block(j, carry):
            go(_run_copy(zbuf, xs_hbm, 0, j * ROW_BLOCK, ROW_BLOCK, zsem))
            return carry

        lax.fori_loop(0, N_EXPERTS, tail, 0)
        lax.fori_loop(tail_sm[2 * N_EXPERTS], n_blocks - 1, block, 0)

    @pl.when(i == 0)
    def _():
        for b in range(2):
            for s in range(DISPATCH_TILES):
                ybuf[b, s, _groups(ROWS_LOCAL):_groups(ROWS_LOCAL) + 1] = jnp.zeros((1, RUN_ALIGN, D_MODEL), BF16)
        zbuf[...] = jnp.zeros_like(zbuf)
        zero_fill(False)

    for s in range(DISPATCH_TILES):
        hb = h_ref[s * TOK_TILE:(s + 1) * TOK_TILE, :].astype(BF16)
        pos = post_ref[s]
        for r0 in range(0, ROWS_LOCAL, 256):
            chunk = min(256, ROWS_LOCAL - r0)
            rows = (lax.broadcasted_iota(jnp.int32, (chunk, TOK_TILE), 0) + r0).astype(F32)
            p = jnp.zeros((chunk, TOK_TILE), F32)
            for k in range(TOP_K):
                p = jnp.where(rows == pos[k:k + 1, :], 1.0, p)
            ybuf[cur, s, _groups(r0):_groups(r0 + chunk)] = _as_groups(_dot(p.astype(BF16), hb).astype(BF16))

    def wait_step(step, buf):
        for s in range(DISPATCH_TILES):
            total = cnt_sm[nt * N_EXPERTS + step * DISPATCH_TILES + s]
            _run_copy(ybuf.at[buf, s], xs_hbm, 0, 0, total, sem).wait()

    @pl.when(i > 0)
    def _():
        wait_step(i - 1, 1 - cur)

    for s in range(DISPATCH_TILES):
        for e in range(N_EXPERTS):
            k = (i * DISPATCH_TILES + s) * N_EXPERTS + e
            _run_copy(ybuf.at[cur, s], xs_hbm, slot_sm[k], dst_sm[k], cnt_sm[k], sem).start()

    @pl.when(i == ns - 1)
    def _():
        wait_step(i, cur)
        zero_fill(True)
        last = _run_copy(zbuf, xs_hbm, 0, (n_blocks - 1) * ROW_BLOCK, ROW_BLOCK, zsem)
        last.start()
        last.wait()


def _dispatch(cnt, slot, dst, tail, post, h2d, n_rows):
    t = h2d.shape[0]
    tm = TOK_TILE * DISPATCH_TILES
    return pl.pallas_call(
        _dispatch_kernel,
        grid_spec=pltpu.PrefetchScalarGridSpec(
            num_scalar_prefetch=4,
            grid=(t // tm,),
            in_specs=[
                pl.BlockSpec((DISPATCH_TILES, V7X_SUBLANES, TOK_TILE), lambda i, *_: (i, 0, 0)),
                pl.BlockSpec((tm, D_MODEL), lambda i, *_: (i, 0)),
            ],
            out_specs=pl.BlockSpec(memory_space=pl.ANY),
            scratch_shapes=[
                pltpu.VMEM((2, DISPATCH_TILES, _groups(ROWS_LOCAL) + 1, RUN_ALIGN, D_MODEL), BF16),
                pltpu.VMEM((_groups(ROW_BLOCK), RUN_ALIGN, D_MODEL), BF16),
                pltpu.SemaphoreType.DMA(()),
                pltpu.SemaphoreType.DMA(()),
            ],
        ),
        out_shape=jax.ShapeDtypeStruct((_groups(n_rows), RUN_ALIGN, D_MODEL), BF16),
        compiler_params=_compiler_params(("arbitrary",)),
        name="moe_dispatch",
    )(cnt, slot, dst, tail, post, h2d)


def _expert_kernel(bexp_sm, nused_sm, nvalid_sm, xs_ref, wg_ref, bg_ref, wu_ref, bu_ref, wd_ref, bd_ref, ys_ref,
                   hbuf, wg_b, wu_b, wd_b):
    j = pl.program_id(0)
    used = j < nused_sm[0]
    new_expert = jnp.logical_or(j == 0, bexp_sm[j] != bexp_sm[jnp.maximum(j, 1) - 1])

    @pl.when(jnp.logical_and(used, new_expert))
    def _():
        def cast(i, carry):
            sl = pl.ds(pl.multiple_of(i * 128, 128), 128)
            wg_b[sl, :] = wg_ref[0, sl, :].astype(BF16)
            wu_b[sl, :] = wu_ref[0, sl, :].astype(BF16)
            wd_b[sl, :] = wd_ref[0, sl, :].astype(BF16)
            return carry
        lax.fori_loop(0, D_MODEL // 128, cast, 0)

    def mlp(rows):
        xb = xs_ref[0:_groups(rows)].reshape(rows, D_MODEL)
        step = 512
        for n in range(0, D_MODEL, step):
            g = _dot(xb, wg_b[:, n:n + step]) + bg_ref[0, :, n:n + step]
            u = _dot(xb, wu_b[:, n:n + step]) + bu_ref[0, :, n:n + step]
            g = jnp.minimum(g, SWIGLU_LIMIT)
            u = jnp.clip(u, -SWIGLU_LIMIT, SWIGLU_LIMIT)
            hbuf[0:rows, n:n + step] = (g * jax.nn.sigmoid(SWIGLU_ALPHA * g) * (u + 1.0)).astype(BF16)
        ys_ref[0:_groups(rows)] = _as_groups((_dot(hbuf[0:rows, :], wd_b[...]) + bd_ref[0]).astype(BF16))

    half = ROW_BLOCK // 2
    upper_rows = nvalid_sm[j] > half

    @pl.when(jnp.logical_and(used, upper_rows))
    def _():
        mlp(ROW_BLOCK)

    @pl.when(jnp.logical_and(used, jnp.logical_not(upper_rows)))
    def _():
        mlp(half)
        ys_ref[_groups(half):_groups(ROW_BLOCK)] = jnp.zeros((_groups(ROW_BLOCK - half), RUN_ALIGN, D_MODEL), BF16)


def _experts(bexp, nused, nvalid, xs, wg, bg, wu, bu, wd, bd):
    nb = xs.shape[0] * RUN_ALIGN // ROW_BLOCK
    row_blk = (_groups(ROW_BLOCK), RUN_ALIGN, D_MODEL)

    def row_map(j, be, nu, nv):
        return (jnp.minimum(j, nu[0] - 1), 0, 0)

    def w_map(j, be, nu, nv):
        return (be[jnp.minimum(j, nu[0] - 1)], 0, 0)

    wspec = pl.BlockSpec((1, D_MODEL, D_MODEL), w_map)
    bspec = pl.BlockSpec((1, 1, D_MODEL), w_map)
    return pl.pallas_call(
        _expert_kernel,
        grid_spec=pltpu.PrefetchScalarGridSpec(
            num_scalar_prefetch=3,
            grid=(nb,),
            in_specs=[pl.BlockSpec(row_blk, row_map), wspec, bspec, wspec, bspec, wspec, bspec],
            out_specs=pl.BlockSpec(row_blk, row_map),
            scratch_shapes=[pltpu.VMEM((ROW_BLOCK, D_MODEL), BF16)]
            + [pltpu.VMEM((D_MODEL, D_MODEL), BF16) for _ in range(3)],
        ),
        out_shape=jax.ShapeDtypeStruct(xs.shape, xs.dtype),
        input_output_aliases={3: 0},
        compiler_params=_compiler_params(("arbitrary",)),
        name="moe_experts",
    )(bexp, nused, nvalid, xs, wg, bg, wu, bu, wd, bd)


COMBINE_TILES = 4


def _combine_kernel(cnt_sm, slot_sm, dst_sm, col_ref, h_ref, g_ref, b_ref, ys_hbm, o_ref, ybuf, sbuf, sem):
    i = pl.program_id(0)
    ns = pl.num_programs(0)
    nt = ns * COMBINE_TILES
    cur = i % 2

    def fetch(step, buf):
        for s in range(COMBINE_TILES):
            for e in range(N_EXPERTS):
                k = (step * COMBINE_TILES + s) * N_EXPERTS + e
                _run_copy(ys_hbm, ybuf.at[buf, s], dst_sm[k], slot_sm[k], cnt_sm[k], sem.at[buf]).start()

    @pl.when(i == 0)
    def _():
        ybuf[...] = jnp.zeros_like(ybuf)
        fetch(0, 0)

    @pl.when(i + 1 < ns)
    def _():
        fetch(i + 1, 1 - cur)

    for s in range(COMBINE_TILES):
        _run_copy(ys_hbm, ybuf.at[cur, s], 0, 0, cnt_sm[nt * N_EXPERTS + i * COMBINE_TILES + s], sem.at[cur]).wait()

    chunk = 256
    for s in range(COMBINE_TILES):
        rs = slice(s * TOK_TILE, (s + 1) * TOK_TILE)
        col = col_ref[rs, :]
        for r0 in range(0, ROWS_LOCAL, chunk):
            width = min(chunk, ROWS_LOCAL - r0)
            rows = (lax.broadcasted_iota(jnp.int32, (TOK_TILE, width), 1) + r0).astype(F32)
            sel = jnp.zeros((TOK_TILE, width), F32)
            for k in range(TOP_K):
                sel = jnp.where(rows == col[:, k:k + 1], col[:, TOP_K + k:TOP_K + k + 1], sel)
            sbuf[s, :, r0:r0 + width] = sel.astype(BF16)
    for s in range(COMBINE_TILES):
        rs = slice(s * TOK_TILE, (s + 1) * TOK_TILE)
        for n in range(0, D_MODEL, chunk):
            rows_n = ybuf[cur, s, 0:_groups(ROWS_LOCAL), :, n:n + chunk].reshape(ROWS_LOCAL, chunk)
            moe = _dot(sbuf[s], rows_n)
            o_ref[rs, n:n + chunk] = DEEPNORM_ALPHA * h_ref[rs, n:n + chunk] + moe
    for s in range(COMBINE_TILES):
        rs = slice(s * TOK_TILE, (s + 1) * TOK_TILE)
        o_ref[rs, :] = _layer_norm(o_ref[rs, :], g_ref[...], b_ref[...])


def _combine(cnt, slot, dst, col, h2d, ln_g, ln_b, ys):
    t = h2d.shape[0]
    tm = TOK_TILE * COMBINE_TILES
    return pl.pallas_call(
        _combine_kernel,
        grid_spec=pltpu.PrefetchScalarGridSpec(
            num_scalar_prefetch=3,
            grid=(t // tm,),
            in_specs=[
                pl.BlockSpec((tm, V7X_LANES), lambda i, *_: (i, 0)),
                pl.BlockSpec((tm, D_MODEL), lambda i, *_: (i, 0)),
                pl.BlockSpec((1, D_MODEL), lambda i, *_: (0, 0)),
                pl.BlockSpec((1, D_MODEL), lambda i, *_: (0, 0)),
                pl.BlockSpec(memory_space=pl.ANY),
            ],
            out_specs=pl.BlockSpec((tm, D_MODEL), lambda i, *_: (i, 0)),
            scratch_shapes=[
                pltpu.VMEM((2, COMBINE_TILES, _groups(ROWS_LOCAL) + 1, RUN_ALIGN, D_MODEL), BF16),
                pltpu.VMEM((COMBINE_TILES, TOK_TILE, ROWS_LOCAL), BF16),
                pltpu.SemaphoreType.DMA((2,)),
            ],
        ),
        out_shape=jax.ShapeDtypeStruct((t, D_MODEL), F32),
        compiler_params=_compiler_params(("arbitrary",)),
        name="moe_combine",
    )(cnt, slot, dst, col, h2d, ln_g, ln_b, ys)


def _pad_lanes(a, width=V7X_LANES):
    return jnp.pad(a, ((0, 0), (0, width - a.shape[1])))


def _layer(h3d, w_in, conv_w, conv_b, w_mq, w_mk, b_igate, b_fgate, mnorm_g, w_out,
           ln1_g, ln1_b, w_router, b_router, w_gate, b_gate, w_up, b_up, w_down, b_down, ln2_g, ln2_b):
    batch, seq, _ = h3d.shape
    t = batch * seq
    x2d = h3d.reshape(t, D_MODEL)

    w_main = w_in[:, :PROJ_MAIN].astype(BF16)
    w_gates = _pad_lanes(w_in[:, PROJ_MAIN:]).astype(BF16)
    proj, gcol, grow = _inproj(x2d, w_main, w_gates, batch, seq)
    proj3d = proj.reshape(batch, seq, PROJ_MAIN)
    attn = _attention(proj3d)
    gate_bias = jnp.concatenate([b_igate, b_fgate]).astype(F32)
    bias_col = _pad_lanes(gate_bias[None, :])
    bias_row = jnp.broadcast_to(gate_bias[:, None], (V7X_SUBLANES, seq))
    hm = _mlstm(proj3d, gcol.reshape(batch, seq, V7X_LANES), grow, conv_w, conv_b[None, :],
                w_mq.astype(BF16), w_mk.astype(BF16), bias_col, bias_row, mnorm_g[None, :])

    wo = w_out.astype(BF16)
    h2d, post, col, tab, tot = _outproj_router(
        attn.reshape(t, D_ATTN), hm.reshape(t, D_MLSTM), x2d, wo[:D_ATTN], wo[D_ATTN:],
        ln1_g[None, :], ln1_b[None, :], _pad_lanes(w_router).astype(BF16), _pad_lanes(b_router[None, :]))

    nt = t // TOK_TILE
    total = tot[0, :N_EXPERTS]
    region = (total + ROW_BLOCK - 1) // ROW_BLOCK * ROW_BLOCK
    region_end = jnp.cumsum(region)
    region_start = region_end - region
    n_rows = (t * TOP_K + nt * N_EXPERTS * (RUN_ALIGN - 1)) // ROW_BLOCK * ROW_BLOCK + (N_EXPERTS + 1) * ROW_BLOCK
    nb = n_rows // ROW_BLOCK
    runs = tab[:, 0, :N_EXPERTS]
    empty = runs == 0
    cnt = jnp.where(empty, RUN_ALIGN, runs)
    cnt = jnp.concatenate([cnt.reshape(-1), jnp.sum(cnt, axis=1)])
    slot = jnp.where(empty, ROWS_LOCAL, tab[:, 2, :N_EXPERTS]).reshape(-1)
    dst = jnp.where(empty, n_rows - ROW_BLOCK, tab[:, 1, :N_EXPERTS] + region_start[None, :]).reshape(-1)
    tail = jnp.concatenate([region_start + total, region - total, region_end[-1:] // ROW_BLOCK]).astype(jnp.int32)
    block_row = jnp.arange(nb, dtype=jnp.int32) * ROW_BLOCK
    bexp = jnp.minimum(jnp.sum(region_end[None, :] <= block_row[:, None], axis=1), N_EXPERTS - 1).astype(jnp.int32)
    nused = (region_end[-1:] // ROW_BLOCK).astype(jnp.int32)
    own = bexp[:, None] == jnp.arange(N_EXPERTS, dtype=jnp.int32)[None, :]
    used_end = jnp.sum(jnp.where(own, (region_start + total)[None, :], 0), axis=1)
    nvalid = jnp.clip(used_end - block_row, 0, ROW_BLOCK).astype(jnp.int32)

    xs = _dispatch(cnt, slot, dst, tail, post, h2d, n_rows)
    ys = _experts(bexp, nused, nvalid, xs, w_gate, b_gate[:, None, :], w_up, b_up[:, None, :], w_down, b_down[:, None, :])
    out = _combine(cnt, slot, dst, col, h2d, ln2_g[None, :], ln2_b[None, :], ys)
    return out.reshape(batch, seq, D_MODEL)


def kernel(x, w_in, conv_w, conv_b, w_mq, w_mk, b_igate, b_fgate, mnorm_g, w_out, ln1_g, ln1_b, w_router, b_router, w_gate, b_gate, w_up, b_up, w_down, b_down, ln2_g, ln2_b):
    h = x
    for l in range(w_in.shape[0]):
        h = _layer(h, w_in[l], conv_w[l], conv_b[l], w_mq[l], w_mk[l], b_igate[l], b_fgate[l], mnorm_g[l],
                   w_out[l], ln1_g[l], ln1_b[l], w_router[l], b_router[l], w_gate[l], b_gate[l], w_up[l],
                   b_up[l], w_down[l], b_down[l], ln2_g[l], ln2_b[l])
    return h
```

```python
import functools
import math

import jax
import jax.numpy as jnp
from jax import lax
from jax.experimental import pallas as pl
from jax.experimental.pallas import tpu as pltpu

F32 = jnp.float32
BF16 = jnp.bfloat16
NEG_INF = float("-inf")

V7X_LANES = 128
V7X_SUBLANES = 8
V7X_VMEM_LIMIT_BYTES = 56 * 1024 * 1024

D_MODEL = 1024
D_ATTN = 512
HEAD_DIM_A = 64
D_MLSTM = 512
N_HEADS_M = 4
HEAD_DIM_M = 128
CONV_WIDTH = 4
CHUNK = 128
DILATED_BRANCHES = ((128, 1), (512, 4), (2048, 16))
N_EXPERTS = 32
TOP_K = 4
SWIGLU_LIMIT = 7.0
SWIGLU_ALPHA = 1.702
DEEPNORM_ALPHA = 2.0 ** 0.25
LN_EPS = 1e-5
RMS_EPS = 1e-6

PROJ_MAIN = 3 * D_ATTN + 3 * D_MLSTM
TOK_TILE = 256
RUN_ALIGN = 4
ROWS_LOCAL = -(-(TOK_TILE * TOP_K + N_EXPERTS * (RUN_ALIGN - 1)) // V7X_LANES) * V7X_LANES
ROW_BLOCK = 512
ROUTER_SUBTILES = 4


def _dot(a, b):
    return jnp.dot(a, b, preferred_element_type=F32)


def _dot_nt(a, b):
    return lax.dot_general(a, b, (((1,), (1,)), ((), ())), preferred_element_type=F32)


def _split3(x):
    hi = x.astype(BF16)
    r1 = x - hi.astype(F32)
    mid = r1.astype(BF16)
    lo = (r1 - mid.astype(F32)).astype(BF16)
    return hi, mid, lo


def _compiler_params(sem):
    return pltpu.CompilerParams(dimension_semantics=sem, vmem_limit_bytes=V7X_VMEM_LIMIT_BYTES)


def _inproj_kernel(x_ref, w_ref, wg_ref, proj_ref, gcol_ref, grow_ref):
    xb = x_ref[...].astype(BF16)
    step = 512
    for n in range(0, PROJ_MAIN, step):
        proj_ref[:, n:n + step] = _dot(xb, w_ref[:, n:n + step])
    g = _dot(xb, wg_ref[...])
    gcol_ref[...] = g
    grow_ref[0] = g.T[:V7X_SUBLANES, :]


def _inproj(x2d, w_main, w_gates, batch, seq):
    t = x2d.shape[0]
    tm = 512
    per_b = seq // tm
    return pl.pallas_call(
        _inproj_kernel,
        grid=(t // tm,),
        in_specs=[
            pl.BlockSpec((tm, D_MODEL), lambda i: (i, 0)),
            pl.BlockSpec((D_MODEL, PROJ_MAIN), lambda i: (0, 0)),
            pl.BlockSpec((D_MODEL, V7X_LANES), lambda i: (0, 0)),
        ],
        out_specs=[
            pl.BlockSpec((tm, PROJ_MAIN), lambda i: (i, 0)),
            pl.BlockSpec((tm, V7X_LANES), lambda i: (i, 0)),
            pl.BlockSpec((1, V7X_SUBLANES, tm), lambda i: (i // per_b, 0, i % per_b)),
        ],
        out_shape=[
            jax.ShapeDtypeStruct((t, PROJ_MAIN), F32),
            jax.ShapeDtypeStruct((t, V7X_LANES), F32),
            jax.ShapeDtypeStruct((batch, V7X_SUBLANES, seq), F32),
        ],
        compiler_params=_compiler_params(("arbitrary",)),
        name="inproj",
    )(x2d, w_main, w_gates)


ATTN_GROUP = 8


def _attn_kernel(q_ref, k_ref, v_ref, o_ref, q0s, q1s, bias_b, bias_f, sg, pg,
                 o0, o1, o2, l0, l1, l2, m0, m1, m2, x0, x1, x2, *, seq):
    obufs, lbufs, mbufs, xbufs = (o0, o1, o2), (l0, l1, l2), (m0, m1, m2), (x0, x1, x2)
    two = 2 * CHUNK
    head0 = lax.broadcasted_iota(jnp.int32, (CHUNK, V7X_LANES), 1) < HEAD_DIM_A
    qscale = HEAD_DIM_A ** -0.5 * math.log2(math.e)

    def prep(i, carry):
        sl = pl.ds(pl.multiple_of(i * two, two), two)
        h0 = lax.broadcasted_iota(jnp.int32, (two, V7X_LANES), 1) < HEAD_DIM_A
        q = q_ref[0, sl, :] * qscale
        q0s[sl, :] = jnp.where(h0, q, 0.0)
        q1s[sl, :] = jnp.where(h0, 0.0, q)
        return carry

    lax.fori_loop(0, seq // two, prep, 0)

    qi = lax.broadcasted_iota(jnp.int32, (two, two), 0) % CHUNK
    kj = lax.broadcasted_iota(jnp.int32, (two, two), 1)
    bias_b[...] = jnp.where((kj >= qi) & (kj <= qi + CHUNK), 0.0, NEG_INF)
    qf = lax.broadcasted_iota(jnp.int32, (two, CHUNK), 0) % CHUNK
    kf = lax.broadcasted_iota(jnp.int32, (two, CHUNK), 1)
    bias_f[...] = jnp.where(kf <= qf, 0.0, NEG_INF)

    def run_group(c, dil, starts, has_prev):
        assert len(starts) <= ATTN_GROUP
        nk = two if has_prev else CHUNK
        bias_ref = bias_b if has_prev else bias_f

        def rows(s0):
            return pl.ds(s0, CHUNK) if dil == 1 else pl.ds(s0, CHUNK, stride=dil)

        def keys(ref, st):
            if has_prev:
                return jnp.concatenate([ref[0, rows(st - dil * CHUNK), :], ref[0, rows(st), :]], axis=0).astype(BF16)
            return ref[0, rows(st), :].astype(BF16)

        def both_heads(x):
            return jnp.where(head0, x[:CHUNK], x[CHUNK:])

        for j, st in enumerate(starts):
            q2 = jnp.concatenate([q0s[rows(st), :], q1s[rows(st), :]], axis=0).astype(BF16)
            sg[j, :, 0:nk] = _dot_nt(q2, keys(k_ref, st)) + bias_ref[...]
        for j, st in enumerate(starts):
            s = sg[j, :, 0:nk]
            m = jnp.max(s, axis=1, keepdims=True)
            pg[j, :, 0:nk] = jnp.exp2(s - m).astype(BF16)
            ma = jnp.broadcast_to(m[:CHUNK], (CHUNK, V7X_LANES))
            mb = jnp.broadcast_to(m[CHUNK:], (CHUNK, V7X_LANES))
            mbufs[c][rows(st), :] = jnp.where(head0, ma, mb)
            xbufs[c][rows(st), :] = jnp.where(head0, mb, ma)
        khead0 = lax.broadcasted_iota(jnp.int32, (nk, V7X_LANES), 1) < HEAD_DIM_A
        for j, st in enumerate(starts):
            if has_prev:
                vv = jnp.concatenate([v_ref[0, rows(st - dil * CHUNK), :], v_ref[0, rows(st), :]], axis=0)
            else:
                vv = v_ref[0, rows(st), :]
            oa = _dot(pg[j, 0:CHUNK, 0:nk], jnp.where(khead0, vv, 1.0).astype(BF16))
            ob = _dot(pg[j, CHUNK:two, 0:nk], jnp.where(khead0, 1.0, vv).astype(BF16))
            obufs[c][rows(st), :] = jnp.where(head0, oa, ob)
            lbufs[c][rows(st), :] = jnp.where(head0, ob, oa)

    for c, (window, dil) in enumerate(DILATED_BRANCHES):
        assert window // dil == CHUNK
        nb = seq // (dil * CHUNK)
        span = dil * CHUNK
        if dil == 1:
            run_group(c, dil, [0], False)
            per = 5
            assert (nb - 1) % per == 0

            def band1(g, carry, c=c, dil=dil, per=per, span=span):
                base = span + g * (per * span)
                run_group(c, dil, [base + span * j for j in range(per)], True)
                return carry

            lax.fori_loop(0, (nb - 1) // per, band1, 0, unroll=True)
        elif nb > 1:
            run_group(c, dil, list(range(dil)), False)

            def band2(g, carry, c=c, dil=dil, nb=nb, span=span):
                starts = [2 * g + rr + span * b for rr in range(2) for b in range(1, nb)]
                run_group(c, dil, starts, True)
                return carry

            lax.fori_loop(0, dil // 2, band2, 0, unroll=True)
        else:
            def firsts(g, carry, c=c, dil=dil):
                run_group(c, dil, [ATTN_GROUP * g + j for j in range(ATTN_GROUP)], False)
                return carry

            lax.fori_loop(0, dil // ATTN_GROUP, firsts, 0)

    def combine(i, carry):
        sl = pl.ds(pl.multiple_of(i * 256, 256), 256)
        ma, mb, mc = m0[sl, :], m1[sl, :], m2[sl, :]
        mx = jnp.maximum(jnp.maximum(ma, mb), mc)
        wa, wb, wc = jnp.exp2(ma - mx), jnp.exp2(mb - mx), jnp.exp2(mc - mx)
        xa, xb, xc = x0[sl, :], x1[sl, :], x2[sl, :]
        xm = jnp.maximum(jnp.maximum(xa, xb), xc)
        den = (jnp.exp2(xa - xm) * l0[sl, :] + jnp.exp2(xb - xm) * l1[sl, :] + jnp.exp2(xc - xm) * l2[sl, :])
        den = pltpu.roll(den, HEAD_DIM_A, axis=1)
        out = (wa * o0[sl, :] + wb * o1[sl, :] + wc * o2[sl, :]) / den
        o_ref[0, sl, :] = out.astype(o_ref.dtype)
        return carry

    lax.fori_loop(0, seq // 256, combine, 0)


def _attention(proj3d):
    batch, seq, _ = proj3d.shape
    nblk = D_ATTN // V7X_LANES
    blk = (1, seq, V7X_LANES)
    scratch = ([pltpu.VMEM((seq, V7X_LANES), F32), pltpu.VMEM((seq, V7X_LANES), F32),
                pltpu.VMEM((2 * CHUNK, 2 * CHUNK), F32), pltpu.VMEM((2 * CHUNK, CHUNK), F32),
                pltpu.VMEM((ATTN_GROUP, 2 * CHUNK, 2 * CHUNK), F32),
                pltpu.VMEM((ATTN_GROUP, 2 * CHUNK, 2 * CHUNK), BF16)]
               + [pltpu.VMEM((seq, V7X_LANES), F32) for _ in range(12)])
    return pl.pallas_call(
        functools.partial(_attn_kernel, seq=seq),
        grid=(batch, nblk),
        in_specs=[
            pl.BlockSpec(blk, lambda b, g: (b, 0, g)),
            pl.BlockSpec(blk, lambda b, g: (b, 0, nblk + g)),
            pl.BlockSpec(blk, lambda b, g: (b, 0, 2 * nblk + g)),
        ],
        out_specs=pl.BlockSpec(blk, lambda b, g: (b, 0, g)),
        out_shape=jax.ShapeDtypeStruct((batch, seq, D_ATTN), BF16),
        scratch_shapes=scratch,
        compiler_params=_compiler_params(("arbitrary", "arbitrary")),
        name="dilated_attention",
    )(proj3d, proj3d, proj3d)


def _log_sigmoid(x):
    return jnp.minimum(x, 0.0) - jnp.log1p(jnp.exp(-jnp.abs(x)))


MLSTM_HEADS_PER_STEP = 2


def _mlstm_kernel(xm_ref, vm_ref, om_ref, gcol_ref, grow_ref, cw_ref, cb_ref, wq_ref, wk_ref,
                  bcol_ref, brow_ref, mg_ref, o_ref, *scratch, seq):
    nh = MLSTM_HEADS_PER_STEP
    per_head = len(scratch) // nh
    nchunk = seq // CHUNK
    ri = lax.broadcasted_iota(jnp.int32, (CHUNK, CHUNK), 0)
    ci = lax.broadcasted_iota(jnp.int32, (CHUNK, CHUNK), 1)
    tril = ri >= ci
    tril_b = jnp.where(tril, 1.0, 0.0).astype(BF16)
    triu_b = jnp.where(ri <= ci, 1.0, 0.0).astype(BF16)
    pad = V7X_SUBLANES
    qscale = HEAD_DIM_M ** -0.5

    class Head:
        def __init__(self, hh):
            (self.xpad, self.q_s, self.k_s, self.icol_s, self.lfcol_s, self.gcol_s,
             self.ri_s, self.rf_s, self.gf_s, self.mi_s, self.mo_s, self.gl_s) = scratch[hh * per_head:(hh + 1) * per_head]
            self.h = pl.program_id(1) * nh + hh
            self.cols = slice(hh * V7X_LANES, (hh + 1) * V7X_LANES)
            self.wq = wq_ref[hh]
            self.wk = wk_ref[hh]
            self.mg = mg_ref[:, self.cols]

    heads = [Head(hh) for hh in range(nh)]

    for hd in heads:
        hd.xpad[0:pad, :] = jnp.zeros((pad, V7X_LANES), F32)
        hd.xpad[pad:pad + seq, :] = xm_ref[0, :, hd.cols]

    def conv_step(i, carry):
        base = pl.multiple_of(i * 256, 256)
        rows = pl.ds(base, 256)
        gc = gcol_ref[0, rows, :] + bcol_ref[...]
        lane256 = lax.broadcasted_iota(jnp.int32, (256, V7X_LANES), 1)
        for hd in heads:
            y = jnp.broadcast_to(cb_ref[:, hd.cols], (256, V7X_LANES))
            xw = hd.xpad[pl.ds(base, 256 + pad), :]
            for j in range(CONV_WIDTH):
                off = pad - (CONV_WIDTH - 1) + j
                y = y + cw_ref[j:j + 1, hd.cols] * xw[off:off + 256, :]
            xc = (y * jax.nn.sigmoid(y)).astype(BF16)
            hd.q_s[rows, :] = _dot(xc, hd.wq) * qscale
            hd.k_s[rows, :] = _dot(xc, hd.wk)
            ic = jnp.sum(jnp.where(lane256 == hd.h, gc, 0.0), axis=1, keepdims=True)
            fc = jnp.sum(jnp.where(lane256 == hd.h + N_HEADS_M, gc, 0.0), axis=1, keepdims=True)
            hd.icol_s[rows, :] = jnp.broadcast_to(ic, (256, V7X_LANES))
            hd.lfcol_s[rows, :] = jnp.broadcast_to(_log_sigmoid(fc), (256, V7X_LANES))
        return carry

    lax.fori_loop(0, seq // 256, conv_step, 0)

    gr = grow_ref[0] + brow_ref[...]
    sub = lax.broadcasted_iota(jnp.int32, (V7X_SUBLANES, seq), 0)
    for hd in heads:
        irow = jnp.sum(jnp.where(sub == hd.h, gr, 0.0), axis=0, keepdims=True)
        lfrow = _log_sigmoid(jnp.sum(jnp.where(sub == hd.h + N_HEADS_M, gr, 0.0), axis=0, keepdims=True))
        for c in range(nchunk):
            hd.ri_s[c:c + 1, :] = irow[:, c * CHUNK:(c + 1) * CHUNK]
            hd.rf_s[c:c + 1, :] = lfrow[:, c * CHUNK:(c + 1) * CHUNK]

    def cum_step(c, carry):
        sl = pl.ds(pl.multiple_of(c * CHUNK, CHUNK), CHUNK)
        for hd in heads:
            hd.gcol_s[sl, :] = sum(_dot(tril_b, part) for part in _split3(hd.lfcol_s[sl, :]))
        return carry

    lax.fori_loop(0, nchunk, cum_step, 0, unroll=True)

    crow = lax.broadcasted_iota(jnp.int32, (nchunk, V7X_LANES), 0)
    for hd in heads:
        irows = hd.ri_s[...]
        grows = sum(_dot(part, triu_b) for part in _split3(hd.rf_s[...]))
        hd.gf_s[...] = grows
        g_last = grows[:, CHUNK - 1:CHUNK]
        a_max = jnp.max(g_last - grows + irows, axis=1, keepdims=True)
        m = jnp.zeros((1, 1), F32)
        m_in = jnp.zeros((nchunk, V7X_LANES), F32)
        m_out = jnp.zeros((nchunk, V7X_LANES), F32)
        for c in range(nchunk):
            m_in = jnp.where(crow == c, m, m_in)
            m = jnp.maximum(g_last[c:c + 1, :] + m, a_max[c:c + 1, :])
            m_out = jnp.where(crow == c, m, m_out)
        hd.mi_s[...] = m_in
        hd.mo_s[...] = m_out
        hd.gl_s[...] = jnp.broadcast_to(g_last, (nchunk, V7X_LANES))

    def chunk_head(hd, c, cmat, nrow):
        sl = pl.ds(pl.multiple_of(c * CHUNK, CHUNK), CHUNK)
        one = pl.ds(c, 1)
        qc = hd.q_s[sl, :]
        kc = hd.k_s[sl, :]
        vc = vm_ref[0, sl, hd.cols].astype(BF16)
        icol = hd.icol_s[sl, :]
        gcol = hd.gcol_s[sl, :]
        irow_c = hd.ri_s[one, :]
        grow_c = hd.gf_s[one, :]
        m = hd.mi_s[one, :]
        m_new = hd.mo_s[one, :]
        g_last = hd.gl_s[one, :]

        d = gcol - grow_c + irow_c
        d = jnp.where(tril, d, NEG_INF)
        inter = gcol + m
        m_t = jnp.maximum(inter, jnp.max(d, axis=1, keepdims=True))
        w_intra = jnp.exp(d - m_t)
        w_inter = jnp.exp(inter - m_t)
        qb = qc.astype(BF16)
        kb = kc.astype(BF16)
        qk = _dot_nt(qb, kb) * w_intra
        num = _dot(qk.astype(BF16), vc) + w_inter * _dot(qb, cmat.astype(BF16))
        den = jnp.sum(qk, axis=1, keepdims=True) + w_inter * jnp.sum(qc * nrow, axis=1, keepdims=True)
        hc = num / jnp.maximum(jnp.abs(den), jnp.exp(-m_t))

        a_col = g_last - gcol + icol
        decay = jnp.exp(g_last + m - m_new)
        wk_col = jnp.exp(a_col - m_new)
        kw = wk_col * kc
        c_new = decay * cmat + _dot(kw.T.astype(BF16), vc)
        n_new = decay * nrow + jnp.sum(kw, axis=0, keepdims=True)

        hn = hc * lax.rsqrt(jnp.mean(hc * hc, axis=1, keepdims=True) + RMS_EPS) * hd.mg
        hn = jax.nn.sigmoid(om_ref[0, sl, hd.cols]) * hn
        o_ref[0, sl, hd.cols] = hn.astype(o_ref.dtype)
        return c_new, n_new

    def chunk_step(c, carry):
        return tuple(chunk_head(hd, c, *carry[i]) for i, hd in enumerate(heads))

    init = tuple((jnp.zeros((HEAD_DIM_M, HEAD_DIM_M), F32), jnp.zeros((1, HEAD_DIM_M), F32)) for _ in heads)
    lax.fori_loop(0, nchunk, chunk_step, init, unroll=8)


def _mlstm(proj3d, gcol3d, grow3d, conv_w, conv_b, wq, wk, bias_col, bias_row, mnorm):
    batch, seq, _ = proj3d.shape
    nh = MLSTM_HEADS_PER_STEP
    width = nh * V7X_LANES
    blk = (1, seq, width)
    a0 = 3 * D_ATTN // width
    ng = N_HEADS_M // nh
    per_head = ([pltpu.VMEM((seq + V7X_SUBLANES, V7X_LANES), F32)]
                + [pltpu.VMEM((seq, V7X_LANES), F32) for _ in range(5)]
                + [pltpu.VMEM((seq // CHUNK, V7X_LANES), F32) for _ in range(6)])
    return pl.pallas_call(
        functools.partial(_mlstm_kernel, seq=seq),
        grid=(batch, ng),
        in_specs=[
            pl.BlockSpec(blk, lambda b, g: (b, 0, a0 + g)),
            pl.BlockSpec(blk, lambda b, g: (b, 0, a0 + ng + g)),
            pl.BlockSpec(blk, lambda b, g: (b, 0, a0 + 2 * ng + g)),
            pl.BlockSpec((1, seq, V7X_LANES), lambda b, g: (b, 0, 0)),
            pl.BlockSpec((1, V7X_SUBLANES, seq), lambda b, g: (b, 0, 0)),
            pl.BlockSpec((CONV_WIDTH, width), lambda b, g: (0, g)),
            pl.BlockSpec((1, width), lambda b, g: (0, g)),
            pl.BlockSpec((nh, HEAD_DIM_M, HEAD_DIM_M), lambda b, g: (g, 0, 0)),
            pl.BlockSpec((nh, HEAD_DIM_M, HEAD_DIM_M), lambda b, g: (g, 0, 0)),
            pl.BlockSpec((1, V7X_LANES), lambda b, g: (0, 0)),
            pl.BlockSpec((V7X_SUBLANES, seq), lambda b, g: (0, 0)),
            pl.BlockSpec((1, width), lambda b, g: (0, g)),
        ],
        out_specs=pl.BlockSpec(blk, lambda b, g: (b, 0, g)),
        out_shape=jax.ShapeDtypeStruct((batch, seq, D_MLSTM), BF16),
        scratch_shapes=per_head * nh,
        compiler_params=_compiler_params(("arbitrary", "arbitrary")),
        name="mlstm",
    )(proj3d, proj3d, proj3d, gcol3d, grow3d, conv_w, conv_b, wq, wk, bias_col, bias_row, mnorm)


def _layer_norm(z, g, b):
    mu = jnp.mean(z, axis=1, keepdims=True)
    zc = z - mu
    var = jnp.mean(zc * zc, axis=1, keepdims=True)
    return zc * lax.rsqrt(var + LN_EPS) * g + b


def _router_kernel(attn_ref, hm_ref, x_ref, woa_ref, wom_ref, g_ref, b_ref, wr_ref, br_ref,
                   h_ref, post_ref, col_ref, tab_ref, tot_ref, carry, *, nsub):
    i = pl.program_id(0)
    tm = TOK_TILE

    @pl.when(i == 0)
    def _():
        carry[...] = jnp.zeros_like(carry)

    ne = N_EXPERTS
    ex = lax.broadcasted_iota(jnp.int32, (ne, tm), 0).astype(F32)
    ri = lax.broadcasted_iota(jnp.int32, (tm, tm), 0)
    ci = lax.broadcasted_iota(jnp.int32, (tm, tm), 1)
    earlier = jnp.where(ri < ci, 1.0, 0.0).astype(BF16)
    er = lax.broadcasted_iota(jnp.int32, (ne, ne), 0)
    ec = lax.broadcasted_iota(jnp.int32, (ne, ne), 1)
    lower = jnp.where(ec < er, 1.0, 0.0).astype(BF16)
    diag = (lax.broadcasted_iota(jnp.int32, (ne, V7X_LANES), 0)
            == lax.broadcasted_iota(jnp.int32, (ne, V7X_LANES), 1))
    sub = lax.broadcasted_iota(jnp.int32, (V7X_SUBLANES, V7X_LANES), 0)
    base = carry[...]

    def to_lanes(colvec):
        return jnp.sum(jnp.where(diag, colvec, 0.0), axis=0, keepdims=True)

    for s in range(nsub):
        rs = slice(s * tm, (s + 1) * tm)
        h_ref[rs, :] = (DEEPNORM_ALPHA * x_ref[rs, :] + _dot(attn_ref[rs, :], woa_ref[...])
                        + _dot(hm_ref[rs, :], wom_ref[...]))
    for s in range(nsub):
        rs = slice(s * tm, (s + 1) * tm)
        hval = _layer_norm(h_ref[rs, :], g_ref[...], b_ref[...])
        h_ref[rs, :] = hval

        logits = _dot(hval.astype(BF16), wr_ref[...]) + br_ref[...]
        lt = logits.T[:ne, :]

        sel, vals = [], []
        for _ in range(TOP_K):
            mx = jnp.max(lt, axis=0, keepdims=True)
            idx = jnp.min(jnp.where(lt == mx, ex, float(ne)), axis=0, keepdims=True)
            hit = ex == idx
            lt = jnp.where(hit, NEG_INF, lt)
            sel.append(hit)
            vals.append(mx)
        exps = [jnp.exp(v - vals[0]) for v in vals]
        den = exps[0] + exps[1] + exps[2] + exps[3]
        gates = [e / den for e in exps]

        onehot = jnp.zeros((ne, tm), F32)
        for hit in sel:
            onehot = jnp.where(hit, 1.0, onehot)
        rank = _dot(onehot.astype(BF16), earlier)
        cnt = jnp.broadcast_to(jnp.sum(onehot, axis=1, keepdims=True), (ne, V7X_LANES))
        cnt_al = jnp.floor((cnt + (RUN_ALIGN - 1)) * (1.0 / RUN_ALIGN))
        slot = _dot(lower, cnt_al.astype(BF16)) * float(RUN_ALIGN)
        cnt_al = cnt_al * float(RUN_ALIGN)
        where_row = slot[:, 0:1] + rank

        rows = [jnp.sum(jnp.where(sel[k], where_row, 0.0), axis=0, keepdims=True) for k in range(TOP_K)]
        info = jnp.concatenate(rows + gates, axis=0)
        post_ref[s] = info
        col_ref[rs, :] = jnp.concatenate([info, jnp.zeros((V7X_LANES - 2 * TOP_K, tm), F32)], axis=0).T

        cnt_row = to_lanes(cnt_al)
        tab = jnp.zeros((V7X_SUBLANES, V7X_LANES), F32)
        tab = jnp.where(sub == 0, cnt_row, tab)
        tab = jnp.where(sub == 1, base, tab)
        tab = jnp.where(sub == 2, to_lanes(slot), tab)
        tab_ref[s] = tab.astype(jnp.int32)
        base = base + cnt_row

    carry[...] = base
    tot_ref[...] = jnp.broadcast_to(base, (V7X_SUBLANES, V7X_LANES)).astype(jnp.int32)


def _outproj_router(attn2d, hm2d, x2d, wo_a, wo_m, ln_g, ln_b, w_r, b_r):
    t = x2d.shape[0]
    nsub = ROUTER_SUBTILES
    tm = TOK_TILE * nsub
    nt = t // TOK_TILE
    const = lambda i: (0, 0)
    return pl.pallas_call(
        functools.partial(_router_kernel, nsub=nsub),
        grid=(t // tm,),
        in_specs=[
            pl.BlockSpec((tm, D_ATTN), lambda i: (i, 0)),
            pl.BlockSpec((tm, D_MLSTM), lambda i: (i, 0)),
            pl.BlockSpec((tm, D_MODEL), lambda i: (i, 0)),
            pl.BlockSpec((D_ATTN, D_MODEL), const),
            pl.BlockSpec((D_MLSTM, D_MODEL), const),
            pl.BlockSpec((1, D_MODEL), const),
            pl.BlockSpec((1, D_MODEL), const),
            pl.BlockSpec((D_MODEL, V7X_LANES), const),
            pl.BlockSpec((1, V7X_LANES), const),
        ],
        out_specs=[
            pl.BlockSpec((tm, D_MODEL), lambda i: (i, 0)),
            pl.BlockSpec((nsub, V7X_SUBLANES, TOK_TILE), lambda i: (i, 0, 0)),
            pl.BlockSpec((tm, V7X_LANES), lambda i: (i, 0)),
            pl.BlockSpec((nsub, V7X_SUBLANES, V7X_LANES), lambda i: (i, 0, 0)),
            pl.BlockSpec((V7X_SUBLANES, V7X_LANES), const),
        ],
        out_shape=[
            jax.ShapeDtypeStruct((t, D_MODEL), F32),
            jax.ShapeDtypeStruct((nt, V7X_SUBLANES, TOK_TILE), F32),
            jax.ShapeDtypeStruct((t, V7X_LANES), F32),
            jax.ShapeDtypeStruct((nt, V7X_SUBLANES, V7X_LANES), jnp.int32),
            jax.ShapeDtypeStruct((V7X_SUBLANES, V7X_LANES), jnp.int32),
        ],
        scratch_shapes=[pltpu.VMEM((1, V7X_LANES), F32)],
        compiler_params=_compiler_params(("arbitrary",)),
        name="outproj_router",
    )(attn2d, hm2d, x2d, wo_a, wo_m, ln_g, ln_b, w_r, b_r)


def _groups(rows):
    if isinstance(rows, int):
        assert rows % RUN_ALIGN == 0
        return rows // RUN_ALIGN
    return rows >> (RUN_ALIGN.bit_length() - 1)


def _as_groups(x):
    return x.reshape(x.shape[0] // RUN_ALIGN, RUN_ALIGN, x.shape[1])


def _run_copy(src, dst, start_src, start_dst, n, sem):
    g = _groups(n)
    return pltpu.make_async_copy(src.at[pl.ds(_groups(start_src), g)], dst.at[pl.ds(_groups(start_dst), g)], sem)


DISPATCH_TILES = 4


def _dispatch_kernel(cnt_sm, slot_sm, dst_sm, tail_sm, post_ref, h_ref, xs_hbm, ybuf, zbuf, sem, zsem):
    i = pl.program_id(0)
    ns = pl.num_programs(0)
    nt = ns * DISPATCH_TILES
    cur = i % 2

    n_blocks = xs_hbm.shape[0] * RUN_ALIGN // ROW_BLOCK

    def zero_fill(wait):
        def go(cp):
            if wait:
                cp.wait()
            else:
                cp.start()

        def tail(e, carry):
            n = tail_sm[N_EXPERTS + e]

            @pl.when(n > 0)
            def _():
                go(_run_copy(zbuf, xs_hbm, 0, tail_sm[e], n, zsem))
            return carry

        def block(j, carry):
            go(_run_copy(zbuf, xs_hbm, 0, j * ROW_BLOCK, ROW_BLOCK, zsem))
            return carry

        lax.fori_loop(0, N_EXPERTS, tail, 0)
        lax.fori_loop(tail_sm[2 * N_EXPERTS], n_blocks - 1, block, 0)

    @pl.when(i == 0)
    def _():
        for b in range(2):
            for s in range(DISPATCH_TILES):
                ybuf[b, s, _groups(ROWS_LOCAL):_groups(ROWS_LOCAL) + 1] = jnp.zeros((1, RUN_ALIGN, D_MODEL), BF16)
        zbuf[...] = jnp.zeros_like(zbuf)
        zero_fill(False)

    for s in range(DISPATCH_TILES):
        hb = h_ref[s * TOK_TILE:(s + 1) * TOK_TILE, :].astype(BF16)
        pos = post_ref[s]
        for r0 in range(0, ROWS_LOCAL, 256):
            chunk = min(256, ROWS_LOCAL - r0)
            rows = (lax.broadcasted_iota(jnp.int32, (chunk, TOK_TILE), 0) + r0).astype(F32)
            p = jnp.zeros((chunk, TOK_TILE), F32)
            for k in range(TOP_K):
                p = jnp.where(rows == pos[k:k + 1, :], 1.0, p)
            ybuf[cur, s, _groups(r0):_groups(r0 + chunk)] = _as_groups(_dot(p.astype(BF16), hb).astype(BF16))

    def wait_step(step, buf):
        for s in range(DISPATCH_TILES):
            total = cnt_sm[nt * N_EXPERTS + step * DISPATCH_TILES + s]
            _run_copy(ybuf.at[buf, s], xs_hbm, 0, 0, total, sem).wait()

    @pl.when(i > 0)
    def _():
        wait_step(i - 1, 1 - cur)

    for s in range(DISPATCH_TILES):
        for e in range(N_EXPERTS):
            k = (i * DISPATCH_TILES + s) * N_EXPERTS + e
            _run_copy(ybuf.at[cur, s], xs_hbm, slot_sm[k], dst_sm[k], cnt_sm[k], sem).start()

    @pl.when(i == ns - 1)
    def _():
        wait_step(i, cur)
        zero_fill(True)
        last = _run_copy(zbuf, xs_hbm, 0, (n_blocks - 1) * ROW_BLOCK, ROW_BLOCK, zsem)
        last.start()
        last.wait()


def _dispatch(cnt, slot, dst, tail, post, h2d, n_rows):
    t = h2d.shape[0]
    tm = TOK_TILE * DISPATCH_TILES
    return pl.pallas_call(
        _dispatch_kernel,
        grid_spec=pltpu.PrefetchScalarGridSpec(
            num_scalar_prefetch=4,
            grid=(t // tm,),
            in_specs=[
                pl.BlockSpec((DISPATCH_TILES, V7X_SUBLANES, TOK_TILE), lambda i, *_: (i, 0, 0)),
                pl.BlockSpec((tm, D_MODEL), lambda i, *_: (i, 0)),
            ],
            out_specs=pl.BlockSpec(memory_space=pl.ANY),
            scratch_shapes=[
                pltpu.VMEM((2, DISPATCH_TILES, _groups(ROWS_LOCAL) + 1, RUN_ALIGN, D_MODEL), BF16),
                pltpu.VMEM((_groups(ROW_BLOCK), RUN_ALIGN, D_MODEL), BF16),
                pltpu.SemaphoreType.DMA(()),
                pltpu.SemaphoreType.DMA(()),
            ],
        ),
        out_shape=jax.ShapeDtypeStruct((_groups(n_rows), RUN_ALIGN, D_MODEL), BF16),
        compiler_params=_compiler_params(("arbitrary",)),
        name="moe_dispatch",
    )(cnt, slot, dst, tail, post, h2d)


def _expert_kernel(bexp_sm, nused_sm, nvalid_sm, xs_ref, wg_ref, bg_ref, wu_ref, bu_ref, wd_ref, bd_ref, ys_ref,
                   hbuf, wg_b, wu_b, wd_b):
    j = pl.program_id(0)
    used = j < nused_sm[0]
    new_expert = jnp.logical_or(j == 0, bexp_sm[j] != bexp_sm[jnp.maximum(j, 1) - 1])

    @pl.when(jnp.logical_and(used, new_expert))
    def _():
        def cast(i, carry):
            sl = pl.ds(pl.multiple_of(i * 128, 128), 128)
            wg_b[sl, :] = wg_ref[0, sl, :].astype(BF16)
            wu_b[sl, :] = wu_ref[0, sl, :].astype(BF16)
            wd_b[sl, :] = wd_ref[0, sl, :].astype(BF16)
            return carry
        lax.fori_loop(0, D_MODEL // 128, cast, 0)

    def mlp(rows):
        xb = xs_ref[0:_groups(rows)].reshape(rows, D_MODEL)
        step = 512
        for n in range(0, D_MODEL, step):
            g = _dot(xb, wg_b[:, n:n + step]) + bg_ref[0, :, n:n + step]
            u = _dot(xb, wu_b[:, n:n + step]) + bu_ref[0, :, n:n + step]
            g = jnp.minimum(g, SWIGLU_LIMIT)
            u = jnp.clip(u, -SWIGLU_LIMIT, SWIGLU_LIMIT)
            hbuf[0:rows, n:n + step] = (g * jax.nn.sigmoid(SWIGLU_ALPHA * g) * (u + 1.0)).astype(BF16)
        ys_ref[0:_groups(rows)] = _as_groups((_dot(hbuf[0:rows, :], wd_b[...]) + bd_ref[0]).astype(BF16))

    half = ROW_BLOCK // 2
    upper_rows = nvalid_sm[j] > half

    @pl.when(jnp.logical_and(used, upper_rows))
    def _():
        mlp(ROW_BLOCK)

    @pl.when(jnp.logical_and(used, jnp.logical_not(upper_rows)))
    def _():
        mlp(half)
        ys_ref[_groups(half):_groups(ROW_BLOCK)] = jnp.zeros((_groups(ROW_BLOCK - half), RUN_ALIGN, D_MODEL), BF16)


def _experts(bexp, nused, nvalid, xs, wg, bg, wu, bu, wd, bd):
    nb = xs.shape[0] * RUN_ALIGN // ROW_BLOCK
    row_blk = (_groups(ROW_BLOCK), RUN_ALIGN, D_MODEL)

    def row_map(j, be, nu, nv):
        return (jnp.minimum(j, nu[0] - 1), 0, 0)

    def w_map(j, be, nu, nv):
        return (be[jnp.minimum(j, nu[0] - 1)], 0, 0)

    wspec = pl.BlockSpec((1, D_MODEL, D_MODEL), w_map)
    bspec = pl.BlockSpec((1, 1, D_MODEL), w_map)
    return pl.pallas_call(
        _expert_kernel,
        grid_spec=pltpu.PrefetchScalarGridSpec(
            num_scalar_prefetch=3,
            grid=(nb,),
            in_specs=[pl.BlockSpec(row_blk, row_map), wspec, bspec, wspec, bspec, wspec, bspec],
            out_specs=pl.BlockSpec(row_blk, row_map),
            scratch_shapes=[pltpu.VMEM((ROW_BLOCK, D_MODEL), BF16)]
            + [pltpu.VMEM((D_MODEL, D_MODEL), BF16) for _ in range(3)],
        ),
        out_shape=jax.ShapeDtypeStruct(xs.shape, xs.dtype),
        input_output_aliases={3: 0},
        compiler_params=_compiler_params(("arbitrary",)),
        name="moe_experts",
    )(bexp, nused, nvalid, xs, wg, bg, wu, bu, wd, bd)


COMBINE_TILES = 4


def _combine_kernel(cnt_sm, slot_sm, dst_sm, col_ref, h_ref, g_ref, b_ref, ys_hbm, o_ref, ybuf, sbuf, sem):
    i = pl.program_id(0)
    ns = pl.num_programs(0)
    nt = ns * COMBINE_TILES
    cur = i % 2

    def fetch(step, buf):
        for s in range(COMBINE_TILES):
            for e in range(N_EXPERTS):
                k = (step * COMBINE_TILES + s) * N_EXPERTS + e
                _run_copy(ys_hbm, ybuf.at[buf, s], dst_sm[k], slot_sm[k], cnt_sm[k], sem.at[buf]).start()

    @pl.when(i == 0)
    def _():
        ybuf[...] = jnp.zeros_like(ybuf)
        fetch(0, 0)

    @pl.when(i + 1 < ns)
    def _():
        fetch(i + 1, 1 - cur)

    for s in range(COMBINE_TILES):
        _run_copy(ys_hbm, ybuf.at[cur, s], 0, 0, cnt_sm[nt * N_EXPERTS + i * COMBINE_TILES + s], sem.at[cur]).wait()

    chunk = 256
    for s in range(COMBINE_TILES):
        rs = slice(s * TOK_TILE, (s + 1) * TOK_TILE)
        col = col_ref[rs, :]
        for r0 in range(0, ROWS_LOCAL, chunk):
            width = min(chunk, ROWS_LOCAL - r0)
            rows = (lax.broadcasted_iota(jnp.int32, (TOK_TILE, width), 1) + r0).astype(F32)
            sel = jnp.zeros((TOK_TILE, width), F32)
            for k in range(TOP_K):
                sel = jnp.where(rows == col[:, k:k + 1], col[:, TOP_K + k:TOP_K + k + 1], sel)
            sbuf[s, :, r0:r0 + width] = sel.astype(BF16)
    for s in range(COMBINE_TILES):
        rs = slice(s * TOK_TILE, (s + 1) * TOK_TILE)
        for n in range(0, D_MODEL, chunk):
            rows_n = ybuf[cur, s, 0:_groups(ROWS_LOCAL), :, n:n + chunk].reshape(ROWS_LOCAL, chunk)
            moe = _dot(sbuf[s], rows_n)
            o_ref[rs, n:n + chunk] = DEEPNORM_ALPHA * h_ref[rs, n:n + chunk] + moe
    for s in range(COMBINE_TILES):
        rs = slice(s * TOK_TILE, (s + 1) * TOK_TILE)
        o_ref[rs, :] = _layer_norm(o_ref[rs, :], g_ref[...], b_ref[...])


def _combine(cnt, slot, dst, col, h2d, ln_g, ln_b, ys):
    t = h2d.shape[0]
    tm = TOK_TILE * COMBINE_TILES
    return pl.pallas_call(
        _combine_kernel,
        grid_spec=pltpu.PrefetchScalarGridSpec(
            num_scalar_prefetch=3,
            grid=(t // tm,),
            in_specs=[
                pl.BlockSpec((tm, V7X_LANES), lambda i, *_: (i, 0)),
                pl.BlockSpec((tm, D_MODEL), lambda i, *_: (i, 0)),
                pl.BlockSpec((1, D_MODEL), lambda i, *_: (0, 0)),
                pl.BlockSpec((1, D_MODEL), lambda i, *_: (0, 0)),
                pl.BlockSpec(memory_space=pl.ANY),
            ],
            out_specs=pl.BlockSpec((tm, D_MODEL), lambda i, *_: (i, 0)),
            scratch_shapes=[
                pltpu.VMEM((2, COMBINE_TILES, _groups(ROWS_LOCAL) + 1, RUN_ALIGN, D_MODEL), BF16),
                pltpu.VMEM((COMBINE_TILES, TOK_TILE, ROWS_LOCAL), BF16),
                pltpu.SemaphoreType.DMA((2,)),
            ],
        ),
        out_shape=jax.ShapeDtypeStruct((t, D_MODEL), F32),
        compiler_params=_compiler_params(("arbitrary",)),
        name="moe_combine",
    )(cnt, slot, dst, col, h2d, ln_g, ln_b, ys)


def _pad_lanes(a, width=V7X_LANES):
    return jnp.pad(a, ((0, 0), (0, width - a.shape[1])))


def _layer(h3d, w_in, conv_w, conv_b, w_mq, w_mk, b_igate, b_fgate, mnorm_g, w_out,
           ln1_g, ln1_b, w_router, b_router, w_gate, b_gate, w_up, b_up, w_down, b_down, ln2_g, ln2_b):
    batch, seq, _ = h3d.shape
    t = batch * seq
    x2d = h3d.reshape(t, D_MODEL)

    w_main = w_in[:, :PROJ_MAIN].astype(BF16)
    w_gates = _pad_lanes(w_in[:, PROJ_MAIN:]).astype(BF16)
    proj, gcol, grow = _inproj(x2d, w_main, w_gates, batch, seq)
    proj3d = proj.reshape(batch, seq, PROJ_MAIN)
    attn = _attention(proj3d)
    gate_bias = jnp.concatenate([b_igate, b_fgate]).astype(F32)
    bias_col = _pad_lanes(gate_bias[None, :])
    bias_row = jnp.broadcast_to(gate_bias[:, None], (V7X_SUBLANES, seq))
    hm = _mlstm(proj3d, gcol.reshape(batch, seq, V7X_LANES), grow, conv_w, conv_b[None, :],
                w_mq.astype(BF16), w_mk.astype(BF16), bias_col, bias_row, mnorm_g[None, :])

    wo = w_out.astype(BF16)
    h2d, post, col, tab, tot = _outproj_router(
        attn.reshape(t, D_ATTN), hm.reshape(t, D_MLSTM), x2d, wo[:D_ATTN], wo[D_ATTN:],
        ln1_g[None, :], ln1_b[None, :], _pad_lanes(w_router).astype(BF16), _pad_lanes(b_router[None, :]))

    nt = t // TOK_TILE
    total = tot[0, :N_EXPERTS]
    region = (total + ROW_BLOCK - 1) // ROW_BLOCK * ROW_BLOCK
    region_end = jnp.cumsum(region)
    region_start = region_end - region
    n_rows = (t * TOP_K + nt * N_EXPERTS * (RUN_ALIGN - 1)) // ROW_BLOCK * ROW_BLOCK + (N_EXPERTS + 1) * ROW_BLOCK
    nb = n_rows // ROW_BLOCK
    runs = tab[:, 0, :N_EXPERTS]
    empty = runs == 0
    cnt = jnp.where(empty, RUN_ALIGN, runs)
    cnt = jnp.concatenate([cnt.reshape(-1), jnp.sum(cnt, axis=1)])
    slot = jnp.where(empty, ROWS_LOCAL, tab[:, 2, :N_EXPERTS]).reshape(-1)
    dst = jnp.where(empty, n_rows - ROW_BLOCK, tab[:, 1, :N_EXPERTS] + region_start[None, :]).reshape(-1)
    tail = jnp.concatenate([region_start + total, region - total, region_end[-1:] // ROW_BLOCK]).astype(jnp.int32)
    block_row = jnp.arange(nb, dtype=jnp.int32) * ROW_BLOCK
    bexp = jnp.minimum(jnp.sum(region_end[None, :] <= block_row[:, None], axis=1), N_EXPERTS - 1).astype(jnp.int32)
    nused = (region_end[-1:] // ROW_BLOCK).astype(jnp.int32)
    own = bexp[:, None] == jnp.arange(N_EXPERTS, dtype=jnp.int32)[None, :]
    used_end = jnp.sum(jnp.where(own, (region_start + total)[None, :], 0), axis=1)
    nvalid = jnp.clip(used_end - block_row, 0, ROW_BLOCK).astype(jnp.int32)

    xs = _dispatch(cnt, slot, dst, tail, post, h2d, n_rows)
    ys = _experts(bexp, nused, nvalid, xs, w_gate, b_gate[:, None, :], w_up, b_up[:, None, :], w_down, b_down[:, None, :])
    out = _combine(cnt, slot, dst, col, h2d, ln2_g[None, :], ln2_b[None, :], ys)
    return out.reshape(batch, seq, D_MODEL)


def kernel(x, w_in, conv_w, conv_b, w_mq, w_mk, b_igate, b_fgate, mnorm_g, w_out, ln1_g, ln1_b, w_router, b_router, w_gate, b_gate, w_up, b_up, w_down, b_down, ln2_g, ln2_b):
    h = x
    for l in range(w_in.shape[0]):
        h = _layer(h, w_in[l], conv_w[l], conv_b[l], w_mq[l], w_mk[l], b_igate[l], b_fgate[l], mnorm_g[l],
                   w_out[l], ln1_g[l], ln1_b[l], w_router[l], b_router[l], w_gate[l], b_gate[l], w_up[l],
                   b_up[l], w_down[l], b_down[l], ln2_g[l], ln2_b[l])
    return h
```

```python
import functools
import math

import jax
import jax.numpy as jnp
from jax import lax
from jax.experimental import pallas as pl
from jax.experimental.pallas import tpu as pltpu

F32 = jnp.float32
BF16 = jnp.bfloat16
NEG_INF = float("-inf")

V7X_LANES = 128
V7X_SUBLANES = 8
V7X_VMEM_LIMIT_BYTES = 56 * 1024 * 1024

D_MODEL = 1024
D_ATTN = 512
HEAD_DIM_A = 64
D_MLSTM = 512
N_HEADS_M = 4
HEAD_DIM_M = 128
CONV_WIDTH = 4
CHUNK = 128
DILATED_BRANCHES = ((128, 1), (512, 4), (2048, 16))
N_EXPERTS = 32
TOP_K = 4
SWIGLU_LIMIT = 7.0
SWIGLU_ALPHA = 1.702
DEEPNORM_ALPHA = 2.0 ** 0.25
LN_EPS = 1e-5
RMS_EPS = 1e-6

PROJ_MAIN = 3 * D_ATTN + 3 * D_MLSTM
TOK_TILE = 256
RUN_ALIGN = 4
ROWS_LOCAL = -(-(TOK_TILE * TOP_K + N_EXPERTS * (RUN_ALIGN - 1)) // V7X_LANES) * V7X_LANES
ROW_CHUNK = 256
ROW_BLOCK = 512
ROUTER_SUBTILES = 4


def _dot(a, b):
    return jnp.dot(a, b, preferred_element_type=F32)


def _dot_nt(a, b):
    return lax.dot_general(a, b, (((1,), (1,)), ((), ())), preferred_element_type=F32)


def _split3(x):
    hi = x.astype(BF16)
    r1 = x - hi.astype(F32)
    mid = r1.astype(BF16)
    lo = (r1 - mid.astype(F32)).astype(BF16)
    return hi, mid, lo


def _compiler_params(sem):
    return pltpu.CompilerParams(dimension_semantics=sem, vmem_limit_bytes=V7X_VMEM_LIMIT_BYTES)


def _inproj_kernel(x_ref, w_ref, wg_ref, proj_ref, gcol_ref, grow_ref):
    xb = x_ref[...].astype(BF16)
    step = 512
    for n in range(0, PROJ_MAIN, step):
        proj_ref[:, n:n + step] = _dot(xb, w_ref[:, n:n + step])
    g = _dot(xb, wg_ref[...])
    gcol_ref[...] = g
    grow_ref[0] = g.T[:V7X_SUBLANES, :]


def _inproj(x2d, w_main, w_gates, batch, seq):
    t = x2d.shape[0]
    tm = 512
    per_b = seq // tm
    return pl.pallas_call(
        _inproj_kernel,
        grid=(t // tm,),
        in_specs=[
            pl.BlockSpec((tm, D_MODEL), lambda i: (i, 0)),
            pl.BlockSpec((D_MODEL, PROJ_MAIN), lambda i: (0, 0)),
            pl.BlockSpec((D_MODEL, V7X_LANES), lambda i: (0, 0)),
        ],
        out_specs=[
            pl.BlockSpec((tm, PROJ_MAIN), lambda i: (i, 0)),
            pl.BlockSpec((tm, V7X_LANES), lambda i: (i, 0)),
            pl.BlockSpec((1, V7X_SUBLANES, tm), lambda i: (i // per_b, 0, i % per_b)),
        ],
        out_shape=[
            jax.ShapeDtypeStruct((t, PROJ_MAIN), F32),
            jax.ShapeDtypeStruct((t, V7X_LANES), F32),
            jax.ShapeDtypeStruct((batch, V7X_SUBLANES, seq), F32),
        ],
        compiler_params=_compiler_params(("arbitrary",)),
        name="inproj",
    )(x2d, w_main, w_gates)


ATTN_GROUP = 8


def _attn_kernel(q_ref, k_ref, v_ref, o_ref, q0s, q1s, bias_b, bias_f, sg, pg,
                 o0, o1, o2, l0, l1, l2, m0, m1, m2, x0, x1, x2, *, seq):
    obufs, lbufs, mbufs, xbufs = (o0, o1, o2), (l0, l1, l2), (m0, m1, m2), (x0, x1, x2)
    two = 2 * CHUNK
    head0 = lax.broadcasted_iota(jnp.int32, (CHUNK, V7X_LANES), 1) < HEAD_DIM_A
    qscale = HEAD_DIM_A ** -0.5 * math.log2(math.e)

    def prep(i, carry):
        sl = pl.ds(pl.multiple_of(i * two, two), two)
        h0 = lax.broadcasted_iota(jnp.int32, (two, V7X_LANES), 1) < HEAD_DIM_A
        q = q_ref[0, sl, :] * qscale
        q0s[sl, :] = jnp.where(h0, q, 0.0)
        q1s[sl, :] = jnp.where(h0, 0.0, q)
        return carry

    lax.fori_loop(0, seq // two, prep, 0)

    qi = lax.broadcasted_iota(jnp.int32, (two, two), 0) % CHUNK
    kj = lax.broadcasted_iota(jnp.int32, (two, two), 1)
    bias_b[...] = jnp.where((kj >= qi) & (kj <= qi + CHUNK), 0.0, NEG_INF)
    qf = lax.broadcasted_iota(jnp.int32, (two, CHUNK), 0) % CHUNK
    kf = lax.broadcasted_iota(jnp.int32, (two, CHUNK), 1)
    bias_f[...] = jnp.where(kf <= qf, 0.0, NEG_INF)

    def run_group(c, dil, starts, has_prev):
        assert len(starts) <= ATTN_GROUP
        nk = two if has_prev else CHUNK
        bias_ref = bias_b if has_prev else bias_f

        def rows(s0):
            return pl.ds(s0, CHUNK) if dil == 1 else pl.ds(s0, CHUNK, stride=dil)

        def keys(ref, st):
            if has_prev:
                return jnp.concatenate([ref[0, rows(st - dil * CHUNK), :], ref[0, rows(st), :]], axis=0).astype(BF16)
            return ref[0, rows(st), :].astype(BF16)

        def both_heads(x):
            return jnp.where(head0, x[:CHUNK], x[CHUNK:])

        for j, st in enumerate(starts):
            q2 = jnp.concatenate([q0s[rows(st), :], q1s[rows(st), :]], axis=0).astype(BF16)
            sg[j, :, 0:nk] = _dot_nt(q2, keys(k_ref, st)) + bias_ref[...]
        for j, st in enumerate(starts):
            s = sg[j, :, 0:nk]
            m = jnp.max(s, axis=1, keepdims=True)
            pg[j, :, 0:nk] = jnp.exp2(s - m).astype(BF16)
            ma = jnp.broadcast_to(m[:CHUNK], (CHUNK, V7X_LANES))
            mb = jnp.broadcast_to(m[CHUNK:], (CHUNK, V7X_LANES))
            mbufs[c][rows(st), :] = jnp.where(head0, ma, mb)
            xbufs[c][rows(st), :] = jnp.where(head0, mb, ma)
        khead0 = lax.broadcasted_iota(jnp.int32, (nk, V7X_LANES), 1) < HEAD_DIM_A
        for j, st in enumerate(starts):
            if has_prev:
                vv = jnp.concatenate([v_ref[0, rows(st - dil * CHUNK), :], v_ref[0, rows(st), :]], axis=0)
            else:
                vv = v_ref[0, rows(st), :]
            oa = _dot(pg[j, 0:CHUNK, 0:nk], jnp.where(khead0, vv, 1.0).astype(BF16))
            ob = _dot(pg[j, CHUNK:two, 0:nk], jnp.where(khead0, 1.0, vv).astype(BF16))
            obufs[c][rows(st), :] = jnp.where(head0, oa, ob)
            lbufs[c][rows(st), :] = jnp.where(head0, ob, oa)

    for c, (window, dil) in enumerate(DILATED_BRANCHES):
        assert window // dil == CHUNK
        nb = seq // (dil * CHUNK)
        span = dil * CHUNK
        if dil == 1:
            run_group(c, dil, [0], False)
            per = 5
            assert (nb - 1) % per == 0

            def band1(g, carry, c=c, dil=dil, per=per, span=span):
                base = span + g * (per * span)
                run_group(c, dil, [base + span * j for j in range(per)], True)
                return carry

            lax.fori_loop(0, (nb - 1) // per, band1, 0, unroll=True)
        elif nb > 1:
            run_group(c, dil, list(range(dil)), False)

            def band2(g, carry, c=c, dil=dil, nb=nb, span=span):
                starts = [2 * g + rr + span * b for rr in range(2) for b in range(1, nb)]
                run_group(c, dil, starts, True)
                return carry

            lax.fori_loop(0, dil // 2, band2, 0, unroll=True)
        else:
            def firsts(g, carry, c=c, dil=dil):
                run_group(c, dil, [ATTN_GROUP * g + j for j in range(ATTN_GROUP)], False)
                return carry

            lax.fori_loop(0, dil // ATTN_GROUP, firsts, 0)

    def combine(i, carry):
        sl = pl.ds(pl.multiple_of(i * ROW_CHUNK, ROW_CHUNK), ROW_CHUNK)
        ma, mb, mc = m0[sl, :], m1[sl, :], m2[sl, :]
        mx = jnp.maximum(jnp.maximum(ma, mb), mc)
        wa, wb, wc = jnp.exp2(ma - mx), jnp.exp2(mb - mx), jnp.exp2(mc - mx)
        xa, xb, xc = x0[sl, :], x1[sl, :], x2[sl, :]
        xm = jnp.maximum(jnp.maximum(xa, xb), xc)
        den = (jnp.exp2(xa - xm) * l0[sl, :] + jnp.exp2(xb - xm) * l1[sl, :] + jnp.exp2(xc - xm) * l2[sl, :])
        den = pltpu.roll(den, HEAD_DIM_A, axis=1)
        out = (wa * o0[sl, :] + wb * o1[sl, :] + wc * o2[sl, :]) / den
        o_ref[0, sl, :] = out.astype(o_ref.dtype)
        return carry

    lax.fori_loop(0, seq // ROW_CHUNK, combine, 0)


def _attention(proj3d):
    batch, seq, _ = proj3d.shape
    nblk = D_ATTN // V7X_LANES
    blk = (1, seq, V7X_LANES)
    scratch = ([pltpu.VMEM((seq, V7X_LANES), F32), pltpu.VMEM((seq, V7X_LANES), F32),
                pltpu.VMEM((2 * CHUNK, 2 * CHUNK), F32), pltpu.VMEM((2 * CHUNK, CHUNK), F32),
                pltpu.VMEM((ATTN_GROUP, 2 * CHUNK, 2 * CHUNK), F32),
                pltpu.VMEM((ATTN_GROUP, 2 * CHUNK, 2 * CHUNK), BF16)]
               + [pltpu.VMEM((seq, V7X_LANES), F32) for _ in range(12)])
    return pl.pallas_call(
        functools.partial(_attn_kernel, seq=seq),
        grid=(batch, nblk),
        in_specs=[
            pl.BlockSpec(blk, lambda b, g: (b, 0, g)),
            pl.BlockSpec(blk, lambda b, g: (b, 0, nblk + g)),
            pl.BlockSpec(blk, lambda b, g: (b, 0, 2 * nblk + g)),
        ],
        out_specs=pl.BlockSpec(blk, lambda b, g: (b, 0, g)),
        out_shape=jax.ShapeDtypeStruct((batch, seq, D_ATTN), BF16),
        scratch_shapes=scratch,
        compiler_params=_compiler_params(("arbitrary", "arbitrary")),
        name="dilated_attention",
    )(proj3d, proj3d, proj3d)


def _log_sigmoid(x):
    return jnp.minimum(x, 0.0) - jnp.log1p(jnp.exp(-jnp.abs(x)))


MLSTM_HEADS_PER_STEP = 2


def _mlstm_kernel(xm_ref, vm_ref, om_ref, gcol_ref, grow_ref, cw_ref, cb_ref, wq_ref, wk_ref,
                  bcol_ref, brow_ref, mg_ref, o_ref, *scratch, seq):
    nh = MLSTM_HEADS_PER_STEP
    per_head = len(scratch) // nh
    nchunk = seq // CHUNK
    ri = lax.broadcasted_iota(jnp.int32, (CHUNK, CHUNK), 0)
    ci = lax.broadcasted_iota(jnp.int32, (CHUNK, CHUNK), 1)
    tril = ri >= ci
    tril_b = jnp.where(tril, 1.0, 0.0).astype(BF16)
    triu_b = jnp.where(ri <= ci, 1.0, 0.0).astype(BF16)
    pad = V7X_SUBLANES
    qscale = HEAD_DIM_M ** -0.5

    class Head:
        def __init__(self, hh):
            (self.xpad, self.q_s, self.k_s, self.icol_s, self.lfcol_s, self.gcol_s,
             self.ri_s, self.rf_s, self.gf_s, self.mi_s, self.mo_s, self.gl_s) = scratch[hh * per_head:(hh + 1) * per_head]
            self.h = pl.program_id(1) * nh + hh
            self.cols = slice(hh * V7X_LANES, (hh + 1) * V7X_LANES)
            self.wq = wq_ref[hh]
            self.wk = wk_ref[hh]
            self.mg = mg_ref[:, self.cols]

    heads = [Head(hh) for hh in range(nh)]

    for hd in heads:
        hd.xpad[0:pad, :] = jnp.zeros((pad, V7X_LANES), F32)
        hd.xpad[pad:pad + seq, :] = xm_ref[0, :, hd.cols]

    rc = ROW_CHUNK

    def conv_step(i, carry):
        base = pl.multiple_of(i * rc, rc)
        rows = pl.ds(base, rc)
        gc = gcol_ref[0, rows, :] + bcol_ref[...]
        lane_id = lax.broadcasted_iota(jnp.int32, (rc, V7X_LANES), 1)
        for hd in heads:
            y = jnp.broadcast_to(cb_ref[:, hd.cols], (rc, V7X_LANES))
            xw = hd.xpad[pl.ds(base, rc + pad), :]
            for j in range(CONV_WIDTH):
                off = pad - (CONV_WIDTH - 1) + j
                y = y + cw_ref[j:j + 1, hd.cols] * xw[off:off + rc, :]
            xc = (y * jax.nn.sigmoid(y)).astype(BF16)
            hd.q_s[rows, :] = _dot(xc, hd.wq) * qscale
            hd.k_s[rows, :] = _dot(xc, hd.wk)
            ic = jnp.sum(jnp.where(lane_id == hd.h, gc, 0.0), axis=1, keepdims=True)
            fc = jnp.sum(jnp.where(lane_id == hd.h + N_HEADS_M, gc, 0.0), axis=1, keepdims=True)
            hd.icol_s[rows, :] = jnp.broadcast_to(ic, (rc, V7X_LANES))
            hd.lfcol_s[rows, :] = jnp.broadcast_to(_log_sigmoid(fc), (rc, V7X_LANES))
        return carry

    lax.fori_loop(0, seq // rc, conv_step, 0)

    gr = grow_ref[0] + brow_ref[...]
    sub = lax.broadcasted_iota(jnp.int32, (V7X_SUBLANES, seq), 0)
    for hd in heads:
        irow = jnp.sum(jnp.where(sub == hd.h, gr, 0.0), axis=0, keepdims=True)
        lfrow = _log_sigmoid(jnp.sum(jnp.where(sub == hd.h + N_HEADS_M, gr, 0.0), axis=0, keepdims=True))
        for c in range(nchunk):
            hd.ri_s[c:c + 1, :] = irow[:, c * CHUNK:(c + 1) * CHUNK]
            hd.rf_s[c:c + 1, :] = lfrow[:, c * CHUNK:(c + 1) * CHUNK]

    def cum_step(c, carry):
        sl = pl.ds(pl.multiple_of(c * CHUNK, CHUNK), CHUNK)
        for hd in heads:
            hd.gcol_s[sl, :] = sum(_dot(tril_b, part) for part in _split3(hd.lfcol_s[sl, :]))
        return carry

    lax.fori_loop(0, nchunk, cum_step, 0, unroll=True)

    crow = lax.broadcasted_iota(jnp.int32, (nchunk, V7X_LANES), 0)
    for hd in heads:
        irows = hd.ri_s[...]
        grows = sum(_dot(part, triu_b) for part in _split3(hd.rf_s[...]))
        hd.gf_s[...] = grows
        g_last = grows[:, CHUNK - 1:CHUNK]
        a_max = jnp.max(g_last - grows + irows, axis=1, keepdims=True)
        m = jnp.zeros((1, 1), F32)
        m_in = jnp.zeros((nchunk, V7X_LANES), F32)
        m_out = jnp.zeros((nchunk, V7X_LANES), F32)
        for c in range(nchunk):
            m_in = jnp.where(crow == c, m, m_in)
            m = jnp.maximum(g_last[c:c + 1, :] + m, a_max[c:c + 1, :])
            m_out = jnp.where(crow == c, m, m_out)
        hd.mi_s[...] = m_in
        hd.mo_s[...] = m_out
        hd.gl_s[...] = jnp.broadcast_to(g_last, (nchunk, V7X_LANES))

    def chunk_head(hd, c, cmat, nrow):
        sl = pl.ds(pl.multiple_of(c * CHUNK, CHUNK), CHUNK)
        one = pl.ds(c, 1)
        qc = hd.q_s[sl, :]
        kc = hd.k_s[sl, :]
        vc = vm_ref[0, sl, hd.cols].astype(BF16)
        icol = hd.icol_s[sl, :]
        gcol = hd.gcol_s[sl, :]
        irow_c = hd.ri_s[one, :]
        grow_c = hd.gf_s[one, :]
        m = hd.mi_s[one, :]
        m_new = hd.mo_s[one, :]
        g_last = hd.gl_s[one, :]

        d = gcol - grow_c + irow_c
        d = jnp.where(tril, d, NEG_INF)
        inter = gcol + m
        m_t = jnp.maximum(inter, jnp.max(d, axis=1, keepdims=True))
        w_intra = jnp.exp(d - m_t)
        w_inter = jnp.exp(inter - m_t)
        qb = qc.astype(BF16)
        kb = kc.astype(BF16)
        qk = _dot_nt(qb, kb) * w_intra
        num = _dot(qk.astype(BF16), vc) + w_inter * _dot(qb, cmat.astype(BF16))
        den = jnp.sum(qk, axis=1, keepdims=True) + w_inter * jnp.sum(qc * nrow, axis=1, keepdims=True)
        hc = num / jnp.maximum(jnp.abs(den), jnp.exp(-m_t))

        a_col = g_last - gcol + icol
        decay = jnp.exp(g_last + m - m_new)
        wk_col = jnp.exp(a_col - m_new)
        kw = wk_col * kc
        c_new = decay * cmat + _dot(kw.T.astype(BF16), vc)
        n_new = decay * nrow + jnp.sum(kw, axis=0, keepdims=True)

        hn = hc * lax.rsqrt(jnp.mean(hc * hc, axis=1, keepdims=True) + RMS_EPS) * hd.mg
        hn = jax.nn.sigmoid(om_ref[0, sl, hd.cols]) * hn
        o_ref[0, sl, hd.cols] = hn.astype(o_ref.dtype)
        return c_new, n_new

    def chunk_step(c, carry):
        return tuple(chunk_head(hd, c, *carry[i]) for i, hd in enumerate(heads))

    init = tuple((jnp.zeros((HEAD_DIM_M, HEAD_DIM_M), F32), jnp.zeros((1, HEAD_DIM_M), F32)) for _ in heads)
    lax.fori_loop(0, nchunk, chunk_step, init, unroll=8)


def _mlstm(proj3d, gcol3d, grow3d, conv_w, conv_b, wq, wk, bias_col, bias_row, mnorm):
    batch, seq, _ = proj3d.shape
    nh = MLSTM_HEADS_PER_STEP
    width = nh * V7X_LANES
    blk = (1, seq, width)
    a0 = 3 * D_ATTN // width
    ng = N_HEADS_M // nh
    per_head = ([pltpu.VMEM((seq + V7X_SUBLANES, V7X_LANES), F32)]
                + [pltpu.VMEM((seq, V7X_LANES), F32) for _ in range(5)]
                + [pltpu.VMEM((seq // CHUNK, V7X_LANES), F32) for _ in range(6)])
    return pl.pallas_call(
        functools.partial(_mlstm_kernel, seq=seq),
        grid=(batch, ng),
        in_specs=[
            pl.BlockSpec(blk, lambda b, g: (b, 0, a0 + g)),
            pl.BlockSpec(blk, lambda b, g: (b, 0, a0 + ng + g)),
            pl.BlockSpec(blk, lambda b, g: (b, 0, a0 + 2 * ng + g)),
            pl.BlockSpec((1, seq, V7X_LANES), lambda b, g: (b, 0, 0)),
            pl.BlockSpec((1, V7X_SUBLANES, seq), lambda b, g: (b, 0, 0)),
            pl.BlockSpec((CONV_WIDTH, width), lambda b, g: (0, g)),
            pl.BlockSpec((1, width), lambda b, g: (0, g)),
            pl.BlockSpec((nh, HEAD_DIM_M, HEAD_DIM_M), lambda b, g: (g, 0, 0)),
            pl.BlockSpec((nh, HEAD_DIM_M, HEAD_DIM_M), lambda b, g: (g, 0, 0)),
            pl.BlockSpec((1, V7X_LANES), lambda b, g: (0, 0)),
            pl.BlockSpec((V7X_SUBLANES, seq), lambda b, g: (0, 0)),
            pl.BlockSpec((1, width), lambda b, g: (0, g)),
        ],
        out_specs=pl.BlockSpec(blk, lambda b, g: (b, 0, g)),
        out_shape=jax.ShapeDtypeStruct((batch, seq, D_MLSTM), BF16),
        scratch_shapes=per_head * nh,
        compiler_params=_compiler_params(("arbitrary", "arbitrary")),
        name="mlstm",
    )(proj3d, proj3d, proj3d, gcol3d, grow3d, conv_w, conv_b, wq, wk, bias_col, bias_row, mnorm)


def _layer_norm(z, g, b):
    mu = jnp.mean(z, axis=1, keepdims=True)
    zc = z - mu
    var = jnp.mean(zc * zc, axis=1, keepdims=True)
    return zc * lax.rsqrt(var + LN_EPS) * g + b


def _router_kernel(attn_ref, hm_ref, x_ref, woa_ref, wom_ref, g_ref, b_ref, wr_ref, br_ref,
                   h_ref, post_ref, col_ref, tab_ref, tot_ref, carry, *, nsub):
    i = pl.program_id(0)
    tm = TOK_TILE

    @pl.when(i == 0)
    def _():
        carry[...] = jnp.zeros_like(carry)

    ne = N_EXPERTS
    ex = lax.broadcasted_iota(jnp.int32, (ne, tm), 0).astype(F32)
    ri = lax.broadcasted_iota(jnp.int32, (tm, tm), 0)
    ci = lax.broadcasted_iota(jnp.int32, (tm, tm), 1)
    earlier = jnp.where(ri < ci, 1.0, 0.0).astype(BF16)
    er = lax.broadcasted_iota(jnp.int32, (ne, ne), 0)
    ec = lax.broadcasted_iota(jnp.int32, (ne, ne), 1)
    lower = jnp.where(ec < er, 1.0, 0.0).astype(BF16)
    diag = (lax.broadcasted_iota(jnp.int32, (ne, V7X_LANES), 0)
            == lax.broadcasted_iota(jnp.int32, (ne, V7X_LANES), 1))
    sub = lax.broadcasted_iota(jnp.int32, (V7X_SUBLANES, V7X_LANES), 0)
    base = carry[...]

    def to_lanes(colvec):
        return jnp.sum(jnp.where(diag, colvec, 0.0), axis=0, keepdims=True)

    for s in range(nsub):
        rs = slice(s * tm, (s + 1) * tm)
        h_ref[rs, :] = (DEEPNORM_ALPHA * x_ref[rs, :] + _dot(attn_ref[rs, :], woa_ref[...])
                        + _dot(hm_ref[rs, :], wom_ref[...]))
    for s in range(nsub):
        rs = slice(s * tm, (s + 1) * tm)
        hval = _layer_norm(h_ref[rs, :], g_ref[...], b_ref[...])
        h_ref[rs, :] = hval

        logits = _dot(hval.astype(BF16), wr_ref[...]) + br_ref[...]
        lt = logits.T[:ne, :]

        sel, vals = [], []
        for _ in range(TOP_K):
            mx = jnp.max(lt, axis=0, keepdims=True)
            idx = jnp.min(jnp.where(lt == mx, ex, float(ne)), axis=0, keepdims=True)
            hit = ex == idx
            lt = jnp.where(hit, NEG_INF, lt)
            sel.append(hit)
            vals.append(mx)
        exps = [jnp.exp(v - vals[0]) for v in vals]
        den = exps[0] + exps[1] + exps[2] + exps[3]
        gates = [e / den for e in exps]

        onehot = jnp.zeros((ne, tm), F32)
        for hit in sel:
            onehot = jnp.where(hit, 1.0, onehot)
        rank = _dot(onehot.astype(BF16), earlier)
        cnt = jnp.broadcast_to(jnp.sum(onehot, axis=1, keepdims=True), (ne, V7X_LANES))
        cnt_al = jnp.floor((cnt + (RUN_ALIGN - 1)) * (1.0 / RUN_ALIGN))
        slot = _dot(lower, cnt_al.astype(BF16)) * float(RUN_ALIGN)
        cnt_al = cnt_al * float(RUN_ALIGN)
        where_row = slot[:, 0:1] + rank

        rows = [jnp.sum(jnp.where(sel[k], where_row, 0.0), axis=0, keepdims=True) for k in range(TOP_K)]
        info = jnp.concatenate(rows + gates, axis=0)
        post_ref[s] = info
        col_ref[rs, :] = jnp.concatenate([info, jnp.zeros((V7X_LANES - 2 * TOP_K, tm), F32)], axis=0).T

        cnt_row = to_lanes(cnt_al)
        tab = jnp.zeros((V7X_SUBLANES, V7X_LANES), F32)
        tab = jnp.where(sub == 0, cnt_row, tab)
        tab = jnp.where(sub == 1, base, tab)
        tab = jnp.where(sub == 2, to_lanes(slot), tab)
        tab_ref[s] = tab.astype(jnp.int32)
        base = base + cnt_row

    carry[...] = base
    tot_ref[...] = jnp.broadcast_to(base, (V7X_SUBLANES, V7X_LANES)).astype(jnp.int32)


def _outproj_router(attn2d, hm2d, x2d, wo_a, wo_m, ln_g, ln_b, w_r, b_r):
    t = x2d.shape[0]
    nsub = ROUTER_SUBTILES
    tm = TOK_TILE * nsub
    nt = t // TOK_TILE
    const = lambda i: (0, 0)
    return pl.pallas_call(
        functools.partial(_router_kernel, nsub=nsub),
        grid=(t // tm,),
        in_specs=[
            pl.BlockSpec((tm, D_ATTN), lambda i: (i, 0)),
            pl.BlockSpec((tm, D_MLSTM), lambda i: (i, 0)),
            pl.BlockSpec((tm, D_MODEL), lambda i: (i, 0)),
            pl.BlockSpec((D_ATTN, D_MODEL), const),
            pl.BlockSpec((D_MLSTM, D_MODEL), const),
            pl.BlockSpec((1, D_MODEL), const),
            pl.BlockSpec((1, D_MODEL), const),
            pl.BlockSpec((D_MODEL, V7X_LANES), const),
            pl.BlockSpec((1, V7X_LANES), const),
        ],
        out_specs=[
            pl.BlockSpec((tm, D_MODEL), lambda i: (i, 0)),
            pl.BlockSpec((nsub, V7X_SUBLANES, TOK_TILE), lambda i: (i, 0, 0)),
            pl.BlockSpec((tm, V7X_LANES), lambda i: (i, 0)),
            pl.BlockSpec((nsub, V7X_SUBLANES, V7X_LANES), lambda i: (i, 0, 0)),
            pl.BlockSpec((V7X_SUBLANES, V7X_LANES), const),
        ],
        out_shape=[
            jax.ShapeDtypeStruct((t, D_MODEL), F32),
            jax.ShapeDtypeStruct((nt, V7X_SUBLANES, TOK_TILE), F32),
            jax.ShapeDtypeStruct((t, V7X_LANES), F32),
            jax.ShapeDtypeStruct((nt, V7X_SUBLANES, V7X_LANES), jnp.int32),
            jax.ShapeDtypeStruct((V7X_SUBLANES, V7X_LANES), jnp.int32),
        ],
        scratch_shapes=[pltpu.VMEM((1, V7X_LANES), F32)],
        compiler_params=_compiler_params(("arbitrary",)),
        name="outproj_router",
    )(attn2d, hm2d, x2d, wo_a, wo_m, ln_g, ln_b, w_r, b_r)


def _groups(rows):
    if isinstance(rows, int):
        assert rows % RUN_ALIGN == 0
        return rows // RUN_ALIGN
    return rows >> (RUN_ALIGN.bit_length() - 1)


def _as_groups(x):
    return x.reshape(x.shape[0] // RUN_ALIGN, RUN_ALIGN, x.shape[1])


def _run_copy(src, dst, start_src, start_dst, n, sem):
    g = _groups(n)
    return pltpu.make_async_copy(src.at[pl.ds(_groups(start_src), g)], dst.at[pl.ds(_groups(start_dst), g)], sem)


DISPATCH_TILES = 4


def _dispatch_kernel(cnt_sm, slot_sm, dst_sm, tail_sm, post_ref, h_ref, xs_hbm, ybuf, zbuf, sem, zsem):
    i = pl.program_id(0)
    ns = pl.num_programs(0)
    nt = ns * DISPATCH_TILES
    cur = i % 2

    n_blocks = xs_hbm.shape[0] * RUN_ALIGN // ROW_BLOCK

    def zero_fill(wait):
        def go(cp):
            if wait:
                cp.wait()
            else:
                cp.start()

        def tail(e, carry):
            n = tail_sm[N_EXPERTS + e]

            @pl.when(n > 0)
            def _():
                go(_run_copy(zbuf, xs_hbm, 0, tail_sm[e], n, zsem))
            return carry

        def block(j, carry):
            go(_run_copy(zbuf, xs_hbm, 0, j * ROW_BLOCK, ROW_BLOCK, zsem))
            return carry

        lax.fori_loop(0, N_EXPERTS, tail, 0)
        lax.fori_loop(tail_sm[2 * N_EXPERTS], n_blocks - 1, block, 0)

    @pl.when(i == 0)
    def _():
        for b in range(2):
            for s in range(DISPATCH_TILES):
                ybuf[b, s, _groups(ROWS_LOCAL):_groups(ROWS_LOCAL) + 1] = jnp.zeros((1, RUN_ALIGN, D_MODEL), BF16)
        zbuf[...] = jnp.zeros_like(zbuf)
        zero_fill(False)

    for s in range(DISPATCH_TILES):
        hb = h_ref[s * TOK_TILE:(s + 1) * TOK_TILE, :].astype(BF16)
        pos = post_ref[s]
        for r0 in range(0, ROWS_LOCAL, ROW_CHUNK):
            chunk = min(ROW_CHUNK, ROWS_LOCAL - r0)
            rows = (lax.broadcasted_iota(jnp.int32, (chunk, TOK_TILE), 0) + r0).astype(F32)
            p = jnp.zeros((chunk, TOK_TILE), F32)
            for k in range(TOP_K):
                p = jnp.where(rows == pos[k:k + 1, :], 1.0, p)
            ybuf[cur, s, _groups(r0):_groups(r0 + chunk)] = _as_groups(_dot(p.astype(BF16), hb).astype(BF16))

    def wait_step(step, buf):
        for s in range(DISPATCH_TILES):
            total = cnt_sm[nt * N_EXPERTS + step * DISPATCH_TILES + s]
            _run_copy(ybuf.at[buf, s], xs_hbm, 0, 0, total, sem).wait()

    @pl.when(i > 0)
    def _():
        wait_step(i - 1, 1 - cur)

    for s in range(DISPATCH_TILES):
        for e in range(N_EXPERTS):
            k = (i * DISPATCH_TILES + s) * N_EXPERTS + e
            _run_copy(ybuf.at[cur, s], xs_hbm, slot_sm[k], dst_sm[k], cnt_sm[k], sem).start()

    @pl.when(i == ns - 1)
    def _():
        wait_step(i, cur)
        zero_fill(True)
        last = _run_copy(zbuf, xs_hbm, 0, (n_blocks - 1) * ROW_BLOCK, ROW_BLOCK, zsem)
        last.start()
        last.wait()


def _dispatch(cnt, slot, dst, tail, post, h2d, n_rows):
    t = h2d.shape[0]
    tm = TOK_TILE * DISPATCH_TILES
    return pl.pallas_call(
        _dispatch_kernel,
        grid_spec=pltpu.PrefetchScalarGridSpec(
            num_scalar_prefetch=4,
            grid=(t // tm,),
            in_specs=[
                pl.BlockSpec((DISPATCH_TILES, V7X_SUBLANES, TOK_TILE), lambda i, *_: (i, 0, 0)),
                pl.BlockSpec((tm, D_MODEL), lambda i, *_: (i, 0)),
            ],
            out_specs=pl.BlockSpec(memory_space=pl.ANY),
            scratch_shapes=[
                pltpu.VMEM((2, DISPATCH_TILES, _groups(ROWS_LOCAL) + 1, RUN_ALIGN, D_MODEL), BF16),
                pltpu.VMEM((_groups(ROW_BLOCK), RUN_ALIGN, D_MODEL), BF16),
                pltpu.SemaphoreType.DMA(()),
                pltpu.SemaphoreType.DMA(()),
            ],
        ),
        out_shape=jax.ShapeDtypeStruct((_groups(n_rows), RUN_ALIGN, D_MODEL), BF16),
        compiler_params=_compiler_params(("arbitrary",)),
        name="moe_dispatch",
    )(cnt, slot, dst, tail, post, h2d)


def _expert_kernel(bexp_sm, nused_sm, nvalid_sm, xs_ref, wg_ref, bg_ref, wu_ref, bu_ref, wd_ref, bd_ref, ys_ref,
                   hbuf, wg_b, wu_b, wd_b):
    j = pl.program_id(0)
    used = j < nused_sm[0]
    new_expert = jnp.logical_or(j == 0, bexp_sm[j] != bexp_sm[jnp.maximum(j, 1) - 1])

    @pl.when(jnp.logical_and(used, new_expert))
    def _():
        def cast(i, carry):
            sl = pl.ds(pl.multiple_of(i * 128, 128), 128)
            wg_b[sl, :] = wg_ref[0, sl, :].astype(BF16)
            wu_b[sl, :] = wu_ref[0, sl, :].astype(BF16)
            wd_b[sl, :] = wd_ref[0, sl, :].astype(BF16)
            return carry
        lax.fori_loop(0, D_MODEL // 128, cast, 0)

    def mlp(rows):
        xb = xs_ref[0:_groups(rows)].reshape(rows, D_MODEL)
        step = 512
        for n in range(0, D_MODEL, step):
            g = _dot(xb, wg_b[:, n:n + step]) + bg_ref[0, :, n:n + step]
            u = _dot(xb, wu_b[:, n:n + step]) + bu_ref[0, :, n:n + step]
            g = jnp.minimum(g, SWIGLU_LIMIT)
            u = jnp.clip(u, -SWIGLU_LIMIT, SWIGLU_LIMIT)
            hbuf[0:rows, n:n + step] = (g * jax.nn.sigmoid(SWIGLU_ALPHA * g) * (u + 1.0)).astype(BF16)
        ys_ref[0:_groups(rows)] = _as_groups((_dot(hbuf[0:rows, :], wd_b[...]) + bd_ref[0]).astype(BF16))

    half = ROW_BLOCK // 2
    upper_rows = nvalid_sm[j] > half

    @pl.when(jnp.logical_and(used, upper_rows))
    def _():
        mlp(ROW_BLOCK)

    @pl.when(jnp.logical_and(used, jnp.logical_not(upper_rows)))
    def _():
        mlp(half)
        ys_ref[_groups(half):_groups(ROW_BLOCK)] = jnp.zeros((_groups(ROW_BLOCK - half), RUN_ALIGN, D_MODEL), BF16)


def _experts(bexp, nused, nvalid, xs, wg, bg, wu, bu, wd, bd):
    nb = xs.shape[0] * RUN_ALIGN // ROW_BLOCK
    row_blk = (_groups(ROW_BLOCK), RUN_ALIGN, D_MODEL)

    def row_map(j, be, nu, nv):
        return (jnp.minimum(j, nu[0] - 1), 0, 0)

    def w_map(j, be, nu, nv):
        return (be[jnp.minimum(j, nu[0] - 1)], 0, 0)

    wspec = pl.BlockSpec((1, D_MODEL, D_MODEL), w_map)
    bspec = pl.BlockSpec((1, 1, D_MODEL), w_map)
    return pl.pallas_call(
        _expert_kernel,
        grid_spec=pltpu.PrefetchScalarGridSpec(
            num_scalar_prefetch=3,
            grid=(nb,),
            in_specs=[pl.BlockSpec(row_blk, row_map), wspec, bspec, wspec, bspec, wspec, bspec],
            out_specs=pl.BlockSpec(row_blk, row_map),
            scratch_shapes=[pltpu.VMEM((ROW_BLOCK, D_MODEL), BF16)]
            + [pltpu.VMEM((D_MODEL, D_MODEL), BF16) for _ in range(3)],
        ),
        out_shape=jax.ShapeDtypeStruct(xs.shape, xs.dtype),
        input_output_aliases={3: 0},
        compiler_params=_compiler_params(("arbitrary",)),
        name="moe_experts",
    )(bexp, nused, nvalid, xs, wg, bg, wu, bu, wd, bd)


COMBINE_TILES = 4


def _combine_kernel(cnt_sm, slot_sm, dst_sm, col_ref, h_ref, g_ref, b_ref, ys_hbm, o_ref, ybuf, sbuf, sem):
    i = pl.program_id(0)
    ns = pl.num_programs(0)
    nt = ns * COMBINE_TILES
    cur = i % 2

    def fetch(step, buf):
        for s in range(COMBINE_TILES):
            for e in range(N_EXPERTS):
                k = (step * COMBINE_TILES + s) * N_EXPERTS + e
                _run_copy(ys_hbm, ybuf.at[buf, s], dst_sm[k], slot_sm[k], cnt_sm[k], sem.at[buf]).start()

    @pl.when(i == 0)
    def _():
        ybuf[...] = jnp.zeros_like(ybuf)
        fetch(0, 0)

    @pl.when(i + 1 < ns)
    def _():
        fetch(i + 1, 1 - cur)

    for s in range(COMBINE_TILES):
        _run_copy(ys_hbm, ybuf.at[cur, s], 0, 0, cnt_sm[nt * N_EXPERTS + i * COMBINE_TILES + s], sem.at[cur]).wait()

    chunk = ROW_CHUNK
    for s in range(COMBINE_TILES):
        rs = slice(s * TOK_TILE, (s + 1) * TOK_TILE)
        col = col_ref[rs, :]
        for r0 in range(0, ROWS_LOCAL, chunk):
            width = min(chunk, ROWS_LOCAL - r0)
            rows = (lax.broadcasted_iota(jnp.int32, (TOK_TILE, width), 1) + r0).astype(F32)
            sel = jnp.zeros((TOK_TILE, width), F32)
            for k in range(TOP_K):
                sel = jnp.where(rows == col[:, k:k + 1], col[:, TOP_K + k:TOP_K + k + 1], sel)
            sbuf[s, :, r0:r0 + width] = sel.astype(BF16)
    for s in range(COMBINE_TILES):
        rs = slice(s * TOK_TILE, (s + 1) * TOK_TILE)
        for n in range(0, D_MODEL, chunk):
            rows_n = ybuf[cur, s, 0:_groups(ROWS_LOCAL), :, n:n + chunk].reshape(ROWS_LOCAL, chunk)
            moe = _dot(sbuf[s], rows_n)
            o_ref[rs, n:n + chunk] = DEEPNORM_ALPHA * h_ref[rs, n:n + chunk] + moe
    for s in range(COMBINE_TILES):
        rs = slice(s * TOK_TILE, (s + 1) * TOK_TILE)
        o_ref[rs, :] = _layer_norm(o_ref[rs, :], g_ref[...], b_ref[...])


def _combine(cnt, slot, dst, col, h2d, ln_g, ln_b, ys):
    t = h2d.shape[0]
    tm = TOK_TILE * COMBINE_TILES
    return pl.pallas_call(
        _combine_kernel,
        grid_spec=pltpu.PrefetchScalarGridSpec(
            num_scalar_prefetch=3,
            grid=(t // tm,),
            in_specs=[
                pl.BlockSpec((tm, V7X_LANES), lambda i, *_: (i, 0)),
                pl.BlockSpec((tm, D_MODEL), lambda i, *_: (i, 0)),
                pl.BlockSpec((1, D_MODEL), lambda i, *_: (0, 0)),
                pl.BlockSpec((1, D_MODEL), lambda i, *_: (0, 0)),
                pl.BlockSpec(memory_space=pl.ANY),
            ],
            out_specs=pl.BlockSpec((tm, D_MODEL), lambda i, *_: (i, 0)),
            scratch_shapes=[
                pltpu.VMEM((2, COMBINE_TILES, _groups(ROWS_LOCAL) + 1, RUN_ALIGN, D_MODEL), BF16),
                pltpu.VMEM((COMBINE_TILES, TOK_TILE, ROWS_LOCAL), BF16),
                pltpu.SemaphoreType.DMA((2,)),
            ],
        ),
        out_shape=jax.ShapeDtypeStruct((t, D_MODEL), F32),
        compiler_params=_compiler_params(("arbitrary",)),
        name="moe_combine",
    )(cnt, slot, dst, col, h2d, ln_g, ln_b, ys)


def _pad_lanes(a, width=V7X_LANES):
    return jnp.pad(a, ((0, 0), (0, width - a.shape[1])))


def _layer(h3d, w_in, conv_w, conv_b, w_mq, w_mk, b_igate, b_fgate, mnorm_g, w_out,
           ln1_g, ln1_b, w_router, b_router, w_gate, b_gate, w_up, b_up, w_down, b_down, ln2_g, ln2_b):
    batch, seq, _ = h3d.shape
    t = batch * seq
    x2d = h3d.reshape(t, D_MODEL)

    w_main = w_in[:, :PROJ_MAIN].astype(BF16)
    w_gates = _pad_lanes(w_in[:, PROJ_MAIN:]).astype(BF16)
    proj, gcol, grow = _inproj(x2d, w_main, w_gates, batch, seq)
    proj3d = proj.reshape(batch, seq, PROJ_MAIN)
    attn = _attention(proj3d)
    gate_bias = jnp.concatenate([b_igate, b_fgate]).astype(F32)
    bias_col = _pad_lanes(gate_bias[None, :])
    bias_row = jnp.broadcast_to(gate_bias[:, None], (V7X_SUBLANES, seq))
    hm = _mlstm(proj3d, gcol.reshape(batch, seq, V7X_LANES), grow, conv_w, conv_b[None, :],
                w_mq.astype(BF16), w_mk.astype(BF16), bias_col, bias_row, mnorm_g[None, :])

    wo = w_out.astype(BF16)
    h2d, post, col, tab, tot = _outproj_router(
        attn.reshape(t, D_ATTN), hm.reshape(t, D_MLSTM), x2d, wo[:D_ATTN], wo[D_ATTN:],
        ln1_g[None, :], ln1_b[None, :], _pad_lanes(w_router).astype(BF16), _pad_lanes(b_router[None, :]))

    nt = t // TOK_TILE
    total = tot[0, :N_EXPERTS]
    region = (total + ROW_BLOCK - 1) // ROW_BLOCK * ROW_BLOCK
    region_end = jnp.cumsum(region)
    region_start = region_end - region
    n_rows = (t * TOP_K + nt * N_EXPERTS * (RUN_ALIGN - 1)) // ROW_BLOCK * ROW_BLOCK + (N_EXPERTS + 1) * ROW_BLOCK
    nb = n_rows // ROW_BLOCK
    runs = tab[:, 0, :N_EXPERTS]
    empty = runs == 0
    cnt = jnp.where(empty, RUN_ALIGN, runs)
    cnt = jnp.concatenate([cnt.reshape(-1), jnp.sum(cnt, axis=1)])
    slot = jnp.where(empty, ROWS_LOCAL, tab[:, 2, :N_EXPERTS]).reshape(-1)
    dst = jnp.where(empty, n_rows - ROW_BLOCK, tab[:, 1, :N_EXPERTS] + region_start[None, :]).reshape(-1)
    tail = jnp.concatenate([region_start + total, region - total, region_end[-1:] // ROW_BLOCK]).astype(jnp.int32)
    block_row = jnp.arange(nb, dtype=jnp.int32) * ROW_BLOCK
    bexp = jnp.minimum(jnp.sum(region_end[None, :] <= block_row[:, None], axis=1), N_EXPERTS - 1).astype(jnp.int32)
    nused = (region_end[-1:] // ROW_BLOCK).astype(jnp.int32)
    own = bexp[:, None] == jnp.arange(N_EXPERTS, dtype=jnp.int32)[None, :]
    used_end = jnp.sum(jnp.where(own, (region_start + total)[None, :], 0), axis=1)
    nvalid = jnp.clip(used_end - block_row, 0, ROW_BLOCK).astype(jnp.int32)

    xs = _dispatch(cnt, slot, dst, tail, post, h2d, n_rows)
    ys = _experts(bexp, nused, nvalid, xs, w_gate, b_gate[:, None, :], w_up, b_up[:, None, :], w_down, b_down[:, None, :])
    out = _combine(cnt, slot, dst, col, h2d, ln2_g[None, :], ln2_b[None, :], ys)
    return out.reshape(batch, seq, D_MODEL)


def kernel(x, w_in, conv_w, conv_b, w_mq, w_mk, b_igate, b_fgate, mnorm_g, w_out, ln1_g, ln1_b, w_router, b_router, w_gate, b_gate, w_up, b_up, w_down, b_down, ln2_g, ln2_b):
    h = x
    for l in range(w_in.shape[0]):
        h = _layer(h, w_in[l], conv_w[l], conv_b[l], w_mq[l], w_mk[l], b_igate[l], b_fgate[l], mnorm_g[l],
                   w_out[l], ln1_g[l], ln1_b[l], w_router[l], b_router[l], w_gate[l], b_gate[l], w_up[l],
                   b_up[l], w_down[l], b_down[l], ln2_g[l], ln2_b[l])
    return h
```

```python
import functools
import math

import jax
import jax.numpy as jnp
from jax import lax
from jax.experimental import pallas as pl
from jax.experimental.pallas import tpu as pltpu

F32 = jnp.float32
BF16 = jnp.bfloat16
NEG_INF = float("-inf")

V7X_LANES = 128
V7X_SUBLANES = 8
V7X_VMEM_LIMIT_BYTES = 56 * 1024 * 1024

D_MODEL = 1024
D_ATTN = 512
HEAD_DIM_A = 64
D_MLSTM = 512
N_HEADS_M = 4
HEAD_DIM_M = 128
CONV_WIDTH = 4
CHUNK = 128
DILATED_BRANCHES = ((128, 1), (512, 4), (2048, 16))
N_EXPERTS = 32
TOP_K = 4
SWIGLU_LIMIT = 7.0
SWIGLU_ALPHA = 1.702
DEEPNORM_ALPHA = 2.0 ** 0.25
LN_EPS = 1e-5
RMS_EPS = 1e-6

PROJ_MAIN = 3 * D_ATTN + 3 * D_MLSTM
TOK_TILE = 256
RUN_ALIGN = 4
ROWS_LOCAL = -(-(TOK_TILE * TOP_K + N_EXPERTS * (RUN_ALIGN - 1)) // V7X_LANES) * V7X_LANES
ROW_CHUNK = 256
ROW_BLOCK = 1024
ROW_PATHS = 4
ROUTER_SUBTILES = 4


def _dot(a, b):
    return jnp.dot(a, b, preferred_element_type=F32)


def _dot_nt(a, b):
    return lax.dot_general(a, b, (((1,), (1,)), ((), ())), preferred_element_type=F32)


def _split3(x):
    hi = x.astype(BF16)
    r1 = x - hi.astype(F32)
    mid = r1.astype(BF16)
    lo = (r1 - mid.astype(F32)).astype(BF16)
    return hi, mid, lo


def _compiler_params(sem):
    return pltpu.CompilerParams(dimension_semantics=sem, vmem_limit_bytes=V7X_VMEM_LIMIT_BYTES)


def _inproj_kernel(x_ref, w_ref, wg_ref, proj_ref, gcol_ref, grow_ref):
    xb = x_ref[...].astype(BF16)
    step = 512
    for n in range(0, PROJ_MAIN, step):
        proj_ref[:, n:n + step] = _dot(xb, w_ref[:, n:n + step])
    g = _dot(xb, wg_ref[...])
    gcol_ref[...] = g
    grow_ref[0] = g.T[:V7X_SUBLANES, :]


def _inproj(x2d, w_main, w_gates, batch, seq):
    t = x2d.shape[0]
    tm = 512
    per_b = seq // tm
    return pl.pallas_call(
        _inproj_kernel,
        grid=(t // tm,),
        in_specs=[
            pl.BlockSpec((tm, D_MODEL), lambda i: (i, 0)),
            pl.BlockSpec((D_MODEL, PROJ_MAIN), lambda i: (0, 0)),
            pl.BlockSpec((D_MODEL, V7X_LANES), lambda i: (0, 0)),
        ],
        out_specs=[
            pl.BlockSpec((tm, PROJ_MAIN), lambda i: (i, 0)),
            pl.BlockSpec((tm, V7X_LANES), lambda i: (i, 0)),
            pl.BlockSpec((1, V7X_SUBLANES, tm), lambda i: (i // per_b, 0, i % per_b)),
        ],
        out_shape=[
            jax.ShapeDtypeStruct((t, PROJ_MAIN), F32),
            jax.ShapeDtypeStruct((t, V7X_LANES), F32),
            jax.ShapeDtypeStruct((batch, V7X_SUBLANES, seq), F32),
        ],
        compiler_params=_compiler_params(("arbitrary",)),
        name="inproj",
    )(x2d, w_main, w_gates)


ATTN_GROUP = 8


def _attn_kernel(q_ref, k_ref, v_ref, o_ref, q0s, q1s, bias_b, bias_f, sg, pg,
                 o0, o1, o2, l0, l1, l2, m0, m1, m2, x0, x1, x2, *, seq):
    obufs, lbufs, mbufs, xbufs = (o0, o1, o2), (l0, l1, l2), (m0, m1, m2), (x0, x1, x2)
    two = 2 * CHUNK
    head0 = lax.broadcasted_iota(jnp.int32, (CHUNK, V7X_LANES), 1) < HEAD_DIM_A
    qscale = HEAD_DIM_A ** -0.5 * math.log2(math.e)

    def prep(i, carry):
        sl = pl.ds(pl.multiple_of(i * two, two), two)
        h0 = lax.broadcasted_iota(jnp.int32, (two, V7X_LANES), 1) < HEAD_DIM_A
        q = q_ref[0, sl, :] * qscale
        q0s[sl, :] = jnp.where(h0, q, 0.0)
        q1s[sl, :] = jnp.where(h0, 0.0, q)
        return carry

    lax.fori_loop(0, seq // two, prep, 0)

    qi = lax.broadcasted_iota(jnp.int32, (two, two), 0) % CHUNK
    kj = lax.broadcasted_iota(jnp.int32, (two, two), 1)
    bias_b[...] = jnp.where((kj >= qi) & (kj <= qi + CHUNK), 0.0, NEG_INF)
    qf = lax.broadcasted_iota(jnp.int32, (two, CHUNK), 0) % CHUNK
    kf = lax.broadcasted_iota(jnp.int32, (two, CHUNK), 1)
    bias_f[...] = jnp.where(kf <= qf, 0.0, NEG_INF)

    def run_group(c, dil, starts, has_prev):
        assert len(starts) <= ATTN_GROUP
        nk = two if has_prev else CHUNK
        bias_ref = bias_b if has_prev else bias_f

        def rows(s0):
            return pl.ds(s0, CHUNK) if dil == 1 else pl.ds(s0, CHUNK, stride=dil)

        def keys(ref, st):
            if has_prev:
                return jnp.concatenate([ref[0, rows(st - dil * CHUNK), :], ref[0, rows(st), :]], axis=0).astype(BF16)
            return ref[0, rows(st), :].astype(BF16)

        def both_heads(x):
            return jnp.where(head0, x[:CHUNK], x[CHUNK:])

        for j, st in enumerate(starts):
            q2 = jnp.concatenate([q0s[rows(st), :], q1s[rows(st), :]], axis=0).astype(BF16)
            sg[j, :, 0:nk] = _dot_nt(q2, keys(k_ref, st)) + bias_ref[...]
        for j, st in enumerate(starts):
            s = sg[j, :, 0:nk]
            m = jnp.max(s, axis=1, keepdims=True)
            pg[j, :, 0:nk] = jnp.exp2(s - m).astype(BF16)
            ma = jnp.broadcast_to(m[:CHUNK], (CHUNK, V7X_LANES))
            mb = jnp.broadcast_to(m[CHUNK:], (CHUNK, V7X_LANES))
            mbufs[c][rows(st), :] = jnp.where(head0, ma, mb)
            xbufs[c][rows(st), :] = jnp.where(head0, mb, ma)
        khead0 = lax.broadcasted_iota(jnp.int32, (nk, V7X_LANES), 1) < HEAD_DIM_A
        for j, st in enumerate(starts):
            if has_prev:
                vv = jnp.concatenate([v_ref[0, rows(st - dil * CHUNK), :], v_ref[0, rows(st), :]], axis=0)
            else:
                vv = v_ref[0, rows(st), :]
            oa = _dot(pg[j, 0:CHUNK, 0:nk], jnp.where(khead0, vv, 1.0).astype(BF16))
            ob = _dot(pg[j, CHUNK:two, 0:nk], jnp.where(khead0, 1.0, vv).astype(BF16))
            obufs[c][rows(st), :] = jnp.where(head0, oa, ob)
            lbufs[c][rows(st), :] = jnp.where(head0, ob, oa)

    for c, (window, dil) in enumerate(DILATED_BRANCHES):
        assert window // dil == CHUNK
        nb = seq // (dil * CHUNK)
        span = dil * CHUNK
        if dil == 1:
            run_group(c, dil, [0], False)
            per = 5
            assert (nb - 1) % per == 0

            def band1(g, carry, c=c, dil=dil, per=per, span=span):
                base = span + g * (per * span)
                run_group(c, dil, [base + span * j for j in range(per)], True)
                return carry

            lax.fori_loop(0, (nb - 1) // per, band1, 0, unroll=True)
        elif nb > 1:
            run_group(c, dil, list(range(dil)), False)

            def band2(g, carry, c=c, dil=dil, nb=nb, span=span):
                starts = [2 * g + rr + span * b for rr in range(2) for b in range(1, nb)]
                run_group(c, dil, starts, True)
                return carry

            lax.fori_loop(0, dil // 2, band2, 0, unroll=True)
        else:
            def firsts(g, carry, c=c, dil=dil):
                run_group(c, dil, [ATTN_GROUP * g + j for j in range(ATTN_GROUP)], False)
                return carry

            lax.fori_loop(0, dil // ATTN_GROUP, firsts, 0)

    def combine(i, carry):
        sl = pl.ds(pl.multiple_of(i * ROW_CHUNK, ROW_CHUNK), ROW_CHUNK)
        ma, mb, mc = m0[sl, :], m1[sl, :], m2[sl, :]
        mx = jnp.maximum(jnp.maximum(ma, mb), mc)
        wa, wb, wc = jnp.exp2(ma - mx), jnp.exp2(mb - mx), jnp.exp2(mc - mx)
        xa, xb, xc = x0[sl, :], x1[sl, :], x2[sl, :]
        xm = jnp.maximum(jnp.maximum(xa, xb), xc)
        den = (jnp.exp2(xa - xm) * l0[sl, :] + jnp.exp2(xb - xm) * l1[sl, :] + jnp.exp2(xc - xm) * l2[sl, :])
        den = pltpu.roll(den, HEAD_DIM_A, axis=1)
        out = (wa * o0[sl, :] + wb * o1[sl, :] + wc * o2[sl, :]) / den
        o_ref[0, sl, :] = out.astype(o_ref.dtype)
        return carry

    lax.fori_loop(0, seq // ROW_CHUNK, combine, 0)


def _attention(proj3d):
    batch, seq, _ = proj3d.shape
    nblk = D_ATTN // V7X_LANES
    blk = (1, seq, V7X_LANES)
    scratch = ([pltpu.VMEM((seq, V7X_LANES), F32), pltpu.VMEM((seq, V7X_LANES), F32),
                pltpu.VMEM((2 * CHUNK, 2 * CHUNK), F32), pltpu.VMEM((2 * CHUNK, CHUNK), F32),
                pltpu.VMEM((ATTN_GROUP, 2 * CHUNK, 2 * CHUNK), F32),
                pltpu.VMEM((ATTN_GROUP, 2 * CHUNK, 2 * CHUNK), BF16)]
               + [pltpu.VMEM((seq, V7X_LANES), F32) for _ in range(12)])
    return pl.pallas_call(
        functools.partial(_attn_kernel, seq=seq),
        grid=(batch, nblk),
        in_specs=[
            pl.BlockSpec(blk, lambda b, g: (b, 0, g)),
            pl.BlockSpec(blk, lambda b, g: (b, 0, nblk + g)),
            pl.BlockSpec(blk, lambda b, g: (b, 0, 2 * nblk + g)),
        ],
        out_specs=pl.BlockSpec(blk, lambda b, g: (b, 0, g)),
        out_shape=jax.ShapeDtypeStruct((batch, seq, D_ATTN), BF16),
        scratch_shapes=scratch,
        compiler_params=_compiler_params(("arbitrary", "arbitrary")),
        name="dilated_attention",
    )(proj3d, proj3d, proj3d)


def _log_sigmoid(x):
    return jnp.minimum(x, 0.0) - jnp.log1p(jnp.exp(-jnp.abs(x)))


MLSTM_HEADS_PER_STEP = 2


def _mlstm_kernel(xm_ref, vm_ref, om_ref, gcol_ref, grow_ref, cw_ref, cb_ref, wq_ref, wk_ref,
                  bcol_ref, brow_ref, mg_ref, o_ref, *scratch, seq):
    nh = MLSTM_HEADS_PER_STEP
    per_head = len(scratch) // nh
    nchunk = seq // CHUNK
    ri = lax.broadcasted_iota(jnp.int32, (CHUNK, CHUNK), 0)
    ci = lax.broadcasted_iota(jnp.int32, (CHUNK, CHUNK), 1)
    tril = ri >= ci
    tril_b = jnp.where(tril, 1.0, 0.0).astype(BF16)
    triu_b = jnp.where(ri <= ci, 1.0, 0.0).astype(BF16)
    pad = V7X_SUBLANES
    qscale = HEAD_DIM_M ** -0.5

    class Head:
        def __init__(self, hh):
            (self.xpad, self.q_s, self.k_s, self.icol_s, self.lfcol_s, self.gcol_s,
             self.ri_s, self.rf_s, self.gf_s, self.mi_s, self.mo_s, self.gl_s) = scratch[hh * per_head:(hh + 1) * per_head]
            self.h = pl.program_id(1) * nh + hh
            self.cols = slice(hh * V7X_LANES, (hh + 1) * V7X_LANES)
            self.wq = wq_ref[hh]
            self.wk = wk_ref[hh]
            self.mg = mg_ref[:, self.cols]

    heads = [Head(hh) for hh in range(nh)]

    for hd in heads:
        hd.xpad[0:pad, :] = jnp.zeros((pad, V7X_LANES), F32)
        hd.xpad[pad:pad + seq, :] = xm_ref[0, :, hd.cols]

    rc = ROW_CHUNK

    def conv_step(i, carry):
        base = pl.multiple_of(i * rc, rc)
        rows = pl.ds(base, rc)
        gc = gcol_ref[0, rows, :] + bcol_ref[...]
        lane_id = lax.broadcasted_iota(jnp.int32, (rc, V7X_LANES), 1)
        for hd in heads:
            y = jnp.broadcast_to(cb_ref[:, hd.cols], (rc, V7X_LANES))
            xw = hd.xpad[pl.ds(base, rc + pad), :]
            for j in range(CONV_WIDTH):
                off = pad - (CONV_WIDTH - 1) + j
                y = y + cw_ref[j:j + 1, hd.cols] * xw[off:off + rc, :]
            xc = (y * jax.nn.sigmoid(y)).astype(BF16)
            hd.q_s[rows, :] = _dot(xc, hd.wq) * qscale
            hd.k_s[rows, :] = _dot(xc, hd.wk)
            ic = jnp.sum(jnp.where(lane_id == hd.h, gc, 0.0), axis=1, keepdims=True)
            fc = jnp.sum(jnp.where(lane_id == hd.h + N_HEADS_M, gc, 0.0), axis=1, keepdims=True)
            hd.icol_s[rows, :] = jnp.broadcast_to(ic, (rc, V7X_LANES))
            hd.lfcol_s[rows, :] = jnp.broadcast_to(_log_sigmoid(fc), (rc, V7X_LANES))
        return carry

    lax.fori_loop(0, seq // rc, conv_step, 0)

    gr = grow_ref[0] + brow_ref[...]
    sub = lax.broadcasted_iota(jnp.int32, (V7X_SUBLANES, seq), 0)
    for hd in heads:
        irow = jnp.sum(jnp.where(sub == hd.h, gr, 0.0), axis=0, keepdims=True)
        lfrow = _log_sigmoid(jnp.sum(jnp.where(sub == hd.h + N_HEADS_M, gr, 0.0), axis=0, keepdims=True))
        for c in range(nchunk):
            hd.ri_s[c:c + 1, :] = irow[:, c * CHUNK:(c + 1) * CHUNK]
            hd.rf_s[c:c + 1, :] = lfrow[:, c * CHUNK:(c + 1) * CHUNK]

    def cum_step(c, carry):
        sl = pl.ds(pl.multiple_of(c * CHUNK, CHUNK), CHUNK)
        for hd in heads:
            hd.gcol_s[sl, :] = sum(_dot(tril_b, part) for part in _split3(hd.lfcol_s[sl, :]))
        return carry

    lax.fori_loop(0, nchunk, cum_step, 0, unroll=True)

    crow = lax.broadcasted_iota(jnp.int32, (nchunk, V7X_LANES), 0)
    for hd in heads:
        irows = hd.ri_s[...]
        grows = sum(_dot(part, triu_b) for part in _split3(hd.rf_s[...]))
        hd.gf_s[...] = grows
        g_last = grows[:, CHUNK - 1:CHUNK]
        a_max = jnp.max(g_last - grows + irows, axis=1, keepdims=True)
        m = jnp.zeros((1, 1), F32)
        m_in = jnp.zeros((nchunk, V7X_LANES), F32)
        m_out = jnp.zeros((nchunk, V7X_LANES), F32)
        for c in range(nchunk):
            m_in = jnp.where(crow == c, m, m_in)
            m = jnp.maximum(g_last[c:c + 1, :] + m, a_max[c:c + 1, :])
            m_out = jnp.where(crow == c, m, m_out)
        hd.mi_s[...] = m_in
        hd.mo_s[...] = m_out
        hd.gl_s[...] = jnp.broadcast_to(g_last, (nchunk, V7X_LANES))

    def chunk_head(hd, c, cmat, nrow):
        sl = pl.ds(pl.multiple_of(c * CHUNK, CHUNK), CHUNK)
        one = pl.ds(c, 1)
        qc = hd.q_s[sl, :]
        kc = hd.k_s[sl, :]
        vc = vm_ref[0, sl, hd.cols].astype(BF16)
        icol = hd.icol_s[sl, :]
        gcol = hd.gcol_s[sl, :]
        irow_c = hd.ri_s[one, :]
        grow_c = hd.gf_s[one, :]
        m = hd.mi_s[one, :]
        m_new = hd.mo_s[one, :]
        g_last = hd.gl_s[one, :]

        d = gcol - grow_c + irow_c
        d = jnp.where(tril, d, NEG_INF)
        inter = gcol + m
        m_t = jnp.maximum(inter, jnp.max(d, axis=1, keepdims=True))
        w_intra = jnp.exp(d - m_t)
        w_inter = jnp.exp(inter - m_t)
        qb = qc.astype(BF16)
        kb = kc.astype(BF16)
        qk = _dot_nt(qb, kb) * w_intra
        num = _dot(qk.astype(BF16), vc) + w_inter * _dot(qb, cmat.astype(BF16))
        den = jnp.sum(qk, axis=1, keepdims=True) + w_inter * jnp.sum(qc * nrow, axis=1, keepdims=True)
        hc = num / jnp.maximum(jnp.abs(den), jnp.exp(-m_t))

        a_col = g_last - gcol + icol
        decay = jnp.exp(g_last + m - m_new)
        wk_col = jnp.exp(a_col - m_new)
        kw = wk_col * kc
        c_new = decay * cmat + _dot(kw.T.astype(BF16), vc)
        n_new = decay * nrow + jnp.sum(kw, axis=0, keepdims=True)

        hn = hc * lax.rsqrt(jnp.mean(hc * hc, axis=1, keepdims=True) + RMS_EPS) * hd.mg
        hn = jax.nn.sigmoid(om_ref[0, sl, hd.cols]) * hn
        o_ref[0, sl, hd.cols] = hn.astype(o_ref.dtype)
        return c_new, n_new

    def chunk_step(c, carry):
        return tuple(chunk_head(hd, c, *carry[i]) for i, hd in enumerate(heads))

    init = tuple((jnp.zeros((HEAD_DIM_M, HEAD_DIM_M), F32), jnp.zeros((1, HEAD_DIM_M), F32)) for _ in heads)
    lax.fori_loop(0, nchunk, chunk_step, init, unroll=8)


def _mlstm(proj3d, gcol3d, grow3d, conv_w, conv_b, wq, wk, bias_col, bias_row, mnorm):
    batch, seq, _ = proj3d.shape
    nh = MLSTM_HEADS_PER_STEP
    width = nh * V7X_LANES
    blk = (1, seq, width)
    a0 = 3 * D_ATTN // width
    ng = N_HEADS_M // nh
    per_head = ([pltpu.VMEM((seq + V7X_SUBLANES, V7X_LANES), F32)]
                + [pltpu.VMEM((seq, V7X_LANES), F32) for _ in range(5)]
                + [pltpu.VMEM((seq // CHUNK, V7X_LANES), F32) for _ in range(6)])
    return pl.pallas_call(
        functools.partial(_mlstm_kernel, seq=seq),
        grid=(batch, ng),
        in_specs=[
            pl.BlockSpec(blk, lambda b, g: (b, 0, a0 + g)),
            pl.BlockSpec(blk, lambda b, g: (b, 0, a0 + ng + g)),
            pl.BlockSpec(blk, lambda b, g: (b, 0, a0 + 2 * ng + g)),
            pl.BlockSpec((1, seq, V7X_LANES), lambda b, g: (b, 0, 0)),
            pl.BlockSpec((1, V7X_SUBLANES, seq), lambda b, g: (b, 0, 0)),
            pl.BlockSpec((CONV_WIDTH, width), lambda b, g: (0, g)),
            pl.BlockSpec((1, width), lambda b, g: (0, g)),
            pl.BlockSpec((nh, HEAD_DIM_M, HEAD_DIM_M), lambda b, g: (g, 0, 0)),
            pl.BlockSpec((nh, HEAD_DIM_M, HEAD_DIM_M), lambda b, g: (g, 0, 0)),
            pl.BlockSpec((1, V7X_LANES), lambda b, g: (0, 0)),
            pl.BlockSpec((V7X_SUBLANES, seq), lambda b, g: (0, 0)),
            pl.BlockSpec((1, width), lambda b, g: (0, g)),
        ],
        out_specs=pl.BlockSpec(blk, lambda b, g: (b, 0, g)),
        out_shape=jax.ShapeDtypeStruct((batch, seq, D_MLSTM), BF16),
        scratch_shapes=per_head * nh,
        compiler_params=_compiler_params(("arbitrary", "arbitrary")),
        name="mlstm",
    )(proj3d, proj3d, proj3d, gcol3d, grow3d, conv_w, conv_b, wq, wk, bias_col, bias_row, mnorm)


def _layer_norm(z, g, b):
    mu = jnp.mean(z, axis=1, keepdims=True)
    zc = z - mu
    var = jnp.mean(zc * zc, axis=1, keepdims=True)
    return zc * lax.rsqrt(var + LN_EPS) * g + b


def _router_kernel(attn_ref, hm_ref, x_ref, woa_ref, wom_ref, g_ref, b_ref, wr_ref, br_ref,
                   h_ref, post_ref, col_ref, tab_ref, tot_ref, carry, *, nsub):
    i = pl.program_id(0)
    tm = TOK_TILE

    @pl.when(i == 0)
    def _():
        carry[...] = jnp.zeros_like(carry)

    ne = N_EXPERTS
    ex = lax.broadcasted_iota(jnp.int32, (ne, tm), 0).astype(F32)
    ri = lax.broadcasted_iota(jnp.int32, (tm, tm), 0)
    ci = lax.broadcasted_iota(jnp.int32, (tm, tm), 1)
    earlier = jnp.where(ri < ci, 1.0, 0.0).astype(BF16)
    er = lax.broadcasted_iota(jnp.int32, (ne, ne), 0)
    ec = lax.broadcasted_iota(jnp.int32, (ne, ne), 1)
    lower = jnp.where(ec < er, 1.0, 0.0).astype(BF16)
    diag = (lax.broadcasted_iota(jnp.int32, (ne, V7X_LANES), 0)
            == lax.broadcasted_iota(jnp.int32, (ne, V7X_LANES), 1))
    sub = lax.broadcasted_iota(jnp.int32, (V7X_SUBLANES, V7X_LANES), 0)
    base = carry[...]

    def to_lanes(colvec):
        return jnp.sum(jnp.where(diag, colvec, 0.0), axis=0, keepdims=True)

    for s in range(nsub):
        rs = slice(s * tm, (s + 1) * tm)
        h_ref[rs, :] = (DEEPNORM_ALPHA * x_ref[rs, :] + _dot(attn_ref[rs, :], woa_ref[...])
                        + _dot(hm_ref[rs, :], wom_ref[...]))
    for s in range(nsub):
        rs = slice(s * tm, (s + 1) * tm)
        hval = _layer_norm(h_ref[rs, :], g_ref[...], b_ref[...])
        h_ref[rs, :] = hval

        logits = _dot(hval.astype(BF16), wr_ref[...]) + br_ref[...]
        lt = logits.T[:ne, :]

        sel, vals = [], []
        for _ in range(TOP_K):
            mx = jnp.max(lt, axis=0, keepdims=True)
            idx = jnp.min(jnp.where(lt == mx, ex, float(ne)), axis=0, keepdims=True)
            hit = ex == idx
            lt = jnp.where(hit, NEG_INF, lt)
            sel.append(hit)
            vals.append(mx)
        exps = [jnp.exp(v - vals[0]) for v in vals]
        den = exps[0] + exps[1] + exps[2] + exps[3]
        gates = [e / den for e in exps]

        onehot = jnp.zeros((ne, tm), F32)
        for hit in sel:
            onehot = jnp.where(hit, 1.0, onehot)
        rank = _dot(onehot.astype(BF16), earlier)
        cnt = jnp.broadcast_to(jnp.sum(onehot, axis=1, keepdims=True), (ne, V7X_LANES))
        cnt_al = jnp.floor((cnt + (RUN_ALIGN - 1)) * (1.0 / RUN_ALIGN))
        slot = _dot(lower, cnt_al.astype(BF16)) * float(RUN_ALIGN)
        cnt_al = cnt_al * float(RUN_ALIGN)
        where_row = slot[:, 0:1] + rank

        rows = [jnp.sum(jnp.where(sel[k], where_row, 0.0), axis=0, keepdims=True) for k in range(TOP_K)]
        info = jnp.concatenate(rows + gates, axis=0)
        post_ref[s] = info
        col_ref[rs, :] = jnp.concatenate([info, jnp.zeros((V7X_LANES - 2 * TOP_K, tm), F32)], axis=0).T

        cnt_row = to_lanes(cnt_al)
        tab = jnp.zeros((V7X_SUBLANES, V7X_LANES), F32)
        tab = jnp.where(sub == 0, cnt_row, tab)
        tab = jnp.where(sub == 1, base, tab)
        tab = jnp.where(sub == 2, to_lanes(slot), tab)
        tab_ref[s] = tab.astype(jnp.int32)
        base = base + cnt_row

    carry[...] = base
    tot_ref[...] = jnp.broadcast_to(base, (V7X_SUBLANES, V7X_LANES)).astype(jnp.int32)


def _outproj_router(attn2d, hm2d, x2d, wo_a, wo_m, ln_g, ln_b, w_r, b_r):
    t = x2d.shape[0]
    nsub = ROUTER_SUBTILES
    tm = TOK_TILE * nsub
    nt = t // TOK_TILE
    const = lambda i: (0, 0)
    return pl.pallas_call(
        functools.partial(_router_kernel, nsub=nsub),
        grid=(t // tm,),
        in_specs=[
            pl.BlockSpec((tm, D_ATTN), lambda i: (i, 0)),
            pl.BlockSpec((tm, D_MLSTM), lambda i: (i, 0)),
            pl.BlockSpec((tm, D_MODEL), lambda i: (i, 0)),
            pl.BlockSpec((D_ATTN, D_MODEL), const),
            pl.BlockSpec((D_MLSTM, D_MODEL), const),
            pl.BlockSpec((1, D_MODEL), const),
            pl.BlockSpec((1, D_MODEL), const),
            pl.BlockSpec((D_MODEL, V7X_LANES), const),
            pl.BlockSpec((1, V7X_LANES), const),
        ],
        out_specs=[
            pl.BlockSpec((tm, D_MODEL), lambda i: (i, 0)),
            pl.BlockSpec((nsub, V7X_SUBLANES, TOK_TILE), lambda i: (i, 0, 0)),
            pl.BlockSpec((tm, V7X_LANES), lambda i: (i, 0)),
            pl.BlockSpec((nsub, V7X_SUBLANES, V7X_LANES), lambda i: (i, 0, 0)),
            pl.BlockSpec((V7X_SUBLANES, V7X_LANES), const),
        ],
        out_shape=[
            jax.ShapeDtypeStruct((t, D_MODEL), F32),
            jax.ShapeDtypeStruct((nt, V7X_SUBLANES, TOK_TILE), F32),
            jax.ShapeDtypeStruct((t, V7X_LANES), F32),
            jax.ShapeDtypeStruct((nt, V7X_SUBLANES, V7X_LANES), jnp.int32),
            jax.ShapeDtypeStruct((V7X_SUBLANES, V7X_LANES), jnp.int32),
        ],
        scratch_shapes=[pltpu.VMEM((1, V7X_LANES), F32)],
        compiler_params=_compiler_params(("arbitrary",)),
        name="outproj_router",
    )(attn2d, hm2d, x2d, wo_a, wo_m, ln_g, ln_b, w_r, b_r)


def _groups(rows):
    if isinstance(rows, int):
        assert rows % RUN_ALIGN == 0
        return rows // RUN_ALIGN
    return rows >> (RUN_ALIGN.bit_length() - 1)


def _as_groups(x):
    return x.reshape(x.shape[0] // RUN_ALIGN, RUN_ALIGN, x.shape[1])


def _run_copy(src, dst, start_src, start_dst, n, sem):
    g = _groups(n)
    return pltpu.make_async_copy(src.at[pl.ds(_groups(start_src), g)], dst.at[pl.ds(_groups(start_dst), g)], sem)


DISPATCH_TILES = 4


def _dispatch_kernel(cnt_sm, slot_sm, dst_sm, tail_sm, post_ref, h_ref, xs_hbm, ybuf, zbuf, sem, zsem):
    i = pl.program_id(0)
    ns = pl.num_programs(0)
    nt = ns * DISPATCH_TILES
    cur = i % 2

    n_blocks = xs_hbm.shape[0] * RUN_ALIGN // ROW_BLOCK

    def zero_fill(wait):
        def go(cp):
            if wait:
                cp.wait()
            else:
                cp.start()

        def tail(e, carry):
            n = tail_sm[N_EXPERTS + e]

            @pl.when(n > 0)
            def _():
                go(_run_copy(zbuf, xs_hbm, 0, tail_sm[e], n, zsem))
            return carry

        def block(j, carry):
            go(_run_copy(zbuf, xs_hbm, 0, j * ROW_BLOCK, ROW_BLOCK, zsem))
            return carry

        lax.fori_loop(0, N_EXPERTS, tail, 0)
        lax.fori_loop(tail_sm[2 * N_EXPERTS], n_blocks - 1, block, 0)

    @pl.when(i == 0)
    def _():
        for b in range(2):
            for s in range(DISPATCH_TILES):
                ybuf[b, s, _groups(ROWS_LOCAL):_groups(ROWS_LOCAL) + 1] = jnp.zeros((1, RUN_ALIGN, D_MODEL), BF16)
        zbuf[...] = jnp.zeros_like(zbuf)
        zero_fill(False)

    for s in range(DISPATCH_TILES):
        hb = h_ref[s * TOK_TILE:(s + 1) * TOK_TILE, :].astype(BF16)
        pos = post_ref[s]
        for r0 in range(0, ROWS_LOCAL, ROW_CHUNK):
            chunk = min(ROW_CHUNK, ROWS_LOCAL - r0)
            rows = (lax.broadcasted_iota(jnp.int32, (chunk, TOK_TILE), 0) + r0).astype(F32)
            p = jnp.zeros((chunk, TOK_TILE), F32)
            for k in range(TOP_K):
                p = jnp.where(rows == pos[k:k + 1, :], 1.0, p)
            ybuf[cur, s, _groups(r0):_groups(r0 + chunk)] = _as_groups(_dot(p.astype(BF16), hb).astype(BF16))

    def wait_step(step, buf):
        for s in range(DISPATCH_TILES):
            total = cnt_sm[nt * N_EXPERTS + step * DISPATCH_TILES + s]
            _run_copy(ybuf.at[buf, s], xs_hbm, 0, 0, total, sem).wait()

    @pl.when(i > 0)
    def _():
        wait_step(i - 1, 1 - cur)

    for s in range(DISPATCH_TILES):
        for e in range(N_EXPERTS):
            k = (i * DISPATCH_TILES + s) * N_EXPERTS + e
            _run_copy(ybuf.at[cur, s], xs_hbm, slot_sm[k], dst_sm[k], cnt_sm[k], sem).start()

    @pl.when(i == ns - 1)
    def _():
        wait_step(i, cur)
        zero_fill(True)
        last = _run_copy(zbuf, xs_hbm, 0, (n_blocks - 1) * ROW_BLOCK, ROW_BLOCK, zsem)
        last.start()
        last.wait()


def _dispatch(cnt, slot, dst, tail, post, h2d, n_rows):
    t = h2d.shape[0]
    tm = TOK_TILE * DISPATCH_TILES
    return pl.pallas_call(
        _dispatch_kernel,
        grid_spec=pltpu.PrefetchScalarGridSpec(
            num_scalar_prefetch=4,
            grid=(t // tm,),
            in_specs=[
                pl.BlockSpec((DISPATCH_TILES, V7X_SUBLANES, TOK_TILE), lambda i, *_: (i, 0, 0)),
                pl.BlockSpec((tm, D_MODEL), lambda i, *_: (i, 0)),
            ],
            out_specs=pl.BlockSpec(memory_space=pl.ANY),
            scratch_shapes=[
                pltpu.VMEM((2, DISPATCH_TILES, _groups(ROWS_LOCAL) + 1, RUN_ALIGN, D_MODEL), BF16),
                pltpu.VMEM((_groups(ROW_BLOCK), RUN_ALIGN, D_MODEL), BF16),
                pltpu.SemaphoreType.DMA(()),
                pltpu.SemaphoreType.DMA(()),
            ],
        ),
        out_shape=jax.ShapeDtypeStruct((_groups(n_rows), RUN_ALIGN, D_MODEL), BF16),
        compiler_params=_compiler_params(("arbitrary",)),
        name="moe_dispatch",
    )(cnt, slot, dst, tail, post, h2d)


def _expert_kernel(bexp_sm, nused_sm, nvalid_sm, xs_ref, wg_ref, bg_ref, wu_ref, bu_ref, wd_ref, bd_ref, ys_ref,
                   hbuf, wg_b, wu_b, wd_b):
    j = pl.program_id(0)
    used = j < nused_sm[0]
    new_expert = jnp.logical_or(j == 0, bexp_sm[j] != bexp_sm[jnp.maximum(j, 1) - 1])

    @pl.when(jnp.logical_and(used, new_expert))
    def _():
        def cast(i, carry):
            sl = pl.ds(pl.multiple_of(i * 128, 128), 128)
            wg_b[sl, :] = wg_ref[0, sl, :].astype(BF16)
            wu_b[sl, :] = wu_ref[0, sl, :].astype(BF16)
            wd_b[sl, :] = wd_ref[0, sl, :].astype(BF16)
            return carry
        lax.fori_loop(0, D_MODEL // 128, cast, 0)

    def mlp(rows):
        xb = xs_ref[0:_groups(rows)].reshape(rows, D_MODEL)
        step = 512
        for n in range(0, D_MODEL, step):
            g = _dot(xb, wg_b[:, n:n + step]) + bg_ref[0, :, n:n + step]
            u = _dot(xb, wu_b[:, n:n + step]) + bu_ref[0, :, n:n + step]
            g = jnp.minimum(g, SWIGLU_LIMIT)
            u = jnp.clip(u, -SWIGLU_LIMIT, SWIGLU_LIMIT)
            hbuf[0:rows, n:n + step] = (g * jax.nn.sigmoid(SWIGLU_ALPHA * g) * (u + 1.0)).astype(BF16)
        ys_ref[0:_groups(rows)] = _as_groups((_dot(hbuf[0:rows, :], wd_b[...]) + bd_ref[0]).astype(BF16))

    step_rows = ROW_BLOCK // ROW_PATHS
    nvalid = nvalid_sm[j]
    for k in range(1, ROW_PATHS + 1):
        rows = k * step_rows
        fits = nvalid <= rows if k == 1 else jnp.logical_and(nvalid > rows - step_rows, nvalid <= rows)

        @pl.when(jnp.logical_and(used, fits))
        def _(rows=rows):
            mlp(rows)
            if rows < ROW_BLOCK:
                ys_ref[_groups(rows):_groups(ROW_BLOCK)] = jnp.zeros(
                    (_groups(ROW_BLOCK - rows), RUN_ALIGN, D_MODEL), BF16)


def _experts(bexp, nused, nvalid, xs, wg, bg, wu, bu, wd, bd):
    nb = xs.shape[0] * RUN_ALIGN // ROW_BLOCK
    row_blk = (_groups(ROW_BLOCK), RUN_ALIGN, D_MODEL)

    def row_map(j, be, nu, nv):
        return (jnp.minimum(j, nu[0] - 1), 0, 0)

    def w_map(j, be, nu, nv):
        return (be[jnp.minimum(j, nu[0] - 1)], 0, 0)

    wspec = pl.BlockSpec((1, D_MODEL, D_MODEL), w_map)
    bspec = pl.BlockSpec((1, 1, D_MODEL), w_map)
    return pl.pallas_call(
        _expert_kernel,
        grid_spec=pltpu.PrefetchScalarGridSpec(
            num_scalar_prefetch=3,
            grid=(nb,),
            in_specs=[pl.BlockSpec(row_blk, row_map), wspec, bspec, wspec, bspec, wspec, bspec],
            out_specs=pl.BlockSpec(row_blk, row_map),
            scratch_shapes=[pltpu.VMEM((ROW_BLOCK, D_MODEL), BF16)]
            + [pltpu.VMEM((D_MODEL, D_MODEL), BF16) for _ in range(3)],
        ),
        out_shape=jax.ShapeDtypeStruct(xs.shape, xs.dtype),
        input_output_aliases={3: 0},
        compiler_params=_compiler_params(("arbitrary",)),
        name="moe_experts",
    )(bexp, nused, nvalid, xs, wg, bg, wu, bu, wd, bd)


COMBINE_TILES = 4


def _combine_kernel(cnt_sm, slot_sm, dst_sm, col_ref, h_ref, g_ref, b_ref, ys_hbm, o_ref, ybuf, sbuf, sem):
    i = pl.program_id(0)
    ns = pl.num_programs(0)
    nt = ns * COMBINE_TILES
    cur = i % 2

    def fetch(step, buf):
        for s in range(COMBINE_TILES):
            for e in range(N_EXPERTS):
                k = (step * COMBINE_TILES + s) * N_EXPERTS + e
                _run_copy(ys_hbm, ybuf.at[buf, s], dst_sm[k], slot_sm[k], cnt_sm[k], sem.at[buf]).start()

    @pl.when(i == 0)
    def _():
        ybuf[...] = jnp.zeros_like(ybuf)
        fetch(0, 0)

    @pl.when(i + 1 < ns)
    def _():
        fetch(i + 1, 1 - cur)

    for s in range(COMBINE_TILES):
        _run_copy(ys_hbm, ybuf.at[cur, s], 0, 0, cnt_sm[nt * N_EXPERTS + i * COMBINE_TILES + s], sem.at[cur]).wait()

    chunk = ROW_CHUNK
    for s in range(COMBINE_TILES):
        rs = slice(s * TOK_TILE, (s + 1) * TOK_TILE)
        col = col_ref[rs, :]
        for r0 in range(0, ROWS_LOCAL, chunk):
            width = min(chunk, ROWS_LOCAL - r0)
            rows = (lax.broadcasted_iota(jnp.int32, (TOK_TILE, width), 1) + r0).astype(F32)
            sel = jnp.zeros((TOK_TILE, width), F32)
            for k in range(TOP_K):
                sel = jnp.where(rows == col[:, k:k + 1], col[:, TOP_K + k:TOP_K + k + 1], sel)
            sbuf[s, :, r0:r0 + width] = sel.astype(BF16)
    for s in range(COMBINE_TILES):
        rs = slice(s * TOK_TILE, (s + 1) * TOK_TILE)
        for n in range(0, D_MODEL, chunk):
            rows_n = ybuf[cur, s, 0:_groups(ROWS_LOCAL), :, n:n + chunk].reshape(ROWS_LOCAL, chunk)
            moe = _dot(sbuf[s], rows_n)
            o_ref[rs, n:n + chunk] = DEEPNORM_ALPHA * h_ref[rs, n:n + chunk] + moe
    for s in range(COMBINE_TILES):
        rs = slice(s * TOK_TILE, (s + 1) * TOK_TILE)
        o_ref[rs, :] = _layer_norm(o_ref[rs, :], g_ref[...], b_ref[...])


def _combine(cnt, slot, dst, col, h2d, ln_g, ln_b, ys):
    t = h2d.shape[0]
    tm = TOK_TILE * COMBINE_TILES
    return pl.pallas_call(
        _combine_kernel,
        grid_spec=pltpu.PrefetchScalarGridSpec(
            num_scalar_prefetch=3,
            grid=(t // tm,),
            in_specs=[
                pl.BlockSpec((tm, V7X_LANES), lambda i, *_: (i, 0)),
                pl.BlockSpec((tm, D_MODEL), lambda i, *_: (i, 0)),
                pl.BlockSpec((1, D_MODEL), lambda i, *_: (0, 0)),
                pl.BlockSpec((1, D_MODEL), lambda i, *_: (0, 0)),
                pl.BlockSpec(memory_space=pl.ANY),
            ],
            out_specs=pl.BlockSpec((tm, D_MODEL), lambda i, *_: (i, 0)),
            scratch_shapes=[
                pltpu.VMEM((2, COMBINE_TILES, _groups(ROWS_LOCAL) + 1, RUN_ALIGN, D_MODEL), BF16),
                pltpu.VMEM((COMBINE_TILES, TOK_TILE, ROWS_LOCAL), BF16),
                pltpu.SemaphoreType.DMA((2,)),
            ],
        ),
        out_shape=jax.ShapeDtypeStruct((t, D_MODEL), F32),
        compiler_params=_compiler_params(("arbitrary",)),
        name="moe_combine",
    )(cnt, slot, dst, col, h2d, ln_g, ln_b, ys)


def _pad_lanes(a, width=V7X_LANES):
    return jnp.pad(a, ((0, 0), (0, width - a.shape[1])))


def _layer(h3d, w_in, conv_w, conv_b, w_mq, w_mk, b_igate, b_fgate, mnorm_g, w_out,
           ln1_g, ln1_b, w_router, b_router, w_gate, b_gate, w_up, b_up, w_down, b_down, ln2_g, ln2_b):
    batch, seq, _ = h3d.shape
    t = batch * seq
    x2d = h3d.reshape(t, D_MODEL)

    w_main = w_in[:, :PROJ_MAIN].astype(BF16)
    w_gates = _pad_lanes(w_in[:, PROJ_MAIN:]).astype(BF16)
    proj, gcol, grow = _inproj(x2d, w_main, w_gates, batch, seq)
    proj3d = proj.reshape(batch, seq, PROJ_MAIN)
    attn = _attention(proj3d)
    gate_bias = jnp.concatenate([b_igate, b_fgate]).astype(F32)
    bias_col = _pad_lanes(gate_bias[None, :])
    bias_row = jnp.broadcast_to(gate_bias[:, None], (V7X_SUBLANES, seq))
    hm = _mlstm(proj3d, gcol.reshape(batch, seq, V7X_LANES), grow, conv_w, conv_b[None, :],
                w_mq.astype(BF16), w_mk.astype(BF16), bias_col, bias_row, mnorm_g[None, :])

    wo = w_out.astype(BF16)
    h2d, post, col, tab, tot = _outproj_router(
        attn.reshape(t, D_ATTN), hm.reshape(t, D_MLSTM), x2d, wo[:D_ATTN], wo[D_ATTN:],
        ln1_g[None, :], ln1_b[None, :], _pad_lanes(w_router).astype(BF16), _pad_lanes(b_router[None, :]))

    nt = t // TOK_TILE
    total = tot[0, :N_EXPERTS]
    region = (total + ROW_BLOCK - 1) // ROW_BLOCK * ROW_BLOCK
    region_end = jnp.cumsum(region)
    region_start = region_end - region
    n_rows = (t * TOP_K + nt * N_EXPERTS * (RUN_ALIGN - 1)) // ROW_BLOCK * ROW_BLOCK + (N_EXPERTS + 1) * ROW_BLOCK
    nb = n_rows // ROW_BLOCK
    runs = tab[:, 0, :N_EXPERTS]
    empty = runs == 0
    cnt = jnp.where(empty, RUN_ALIGN, runs)
    cnt = jnp.concatenate([cnt.reshape(-1), jnp.sum(cnt, axis=1)])
    slot = jnp.where(empty, ROWS_LOCAL, tab[:, 2, :N_EXPERTS]).reshape(-1)
    dst = jnp.where(empty, n_rows - ROW_BLOCK, tab[:, 1, :N_EXPERTS] + region_start[None, :]).reshape(-1)
    tail = jnp.concatenate([region_start + total, region - total, region_end[-1:] // ROW_BLOCK]).astype(jnp.int32)
    block_row = jnp.arange(nb, dtype=jnp.int32) * ROW_BLOCK
    bexp = jnp.minimum(jnp.sum(region_end[None, :] <= block_row[:, None], axis=1), N_EXPERTS - 1).astype(jnp.int32)
    nused = (region_end[-1:] // ROW_BLOCK).astype(jnp.int32)
    own = bexp[:, None] == jnp.arange(N_EXPERTS, dtype=jnp.int32)[None, :]
    used_end = jnp.sum(jnp.where(own, (region_start + total)[None, :], 0), axis=1)
    nvalid = jnp.clip(used_end - block_row, 0, ROW_BLOCK).astype(jnp.int32)

    xs = _dispatch(cnt, slot, dst, tail, post, h2d, n_rows)
    ys = _experts(bexp, nused, nvalid, xs, w_gate, b_gate[:, None, :], w_up, b_up[:, None, :], w_down, b_down[:, None, :])
    out = _combine(cnt, slot, dst, col, h2d, ln2_g[None, :], ln2_b[None, :], ys)
    return out.reshape(batch, seq, D_MODEL)


def kernel(x, w_in, conv_w, conv_b, w_mq, w_mk, b_igate, b_fgate, mnorm_g, w_out, ln1_g, ln1_b, w_router, b_router, w_gate, b_gate, w_up, b_up, w_down, b_down, ln2_g, ln2_b):
    h = x
    for l in range(w_in.shape[0]):
        h = _layer(h, w_in[l], conv_w[l], conv_b[l], w_mq[l], w_mk[l], b_igate[l], b_fgate[l], mnorm_g[l],
                   w_out[l], ln1_g[l], ln1_b[l], w_router[l], b_router[l], w_gate[l], b_gate[l], w_up[l],
                   b_up[l], w_down[l], b_down[l], ln2_g[l], ln2_b[l])
    return h
```
